```python
import math
import jax, jax.numpy as jnp
from jax import lax
import numpy as np

D_MODEL = 2048
BATCH = 8
SEQ = 2048
DEPTH = 4

GRID_W = 64
CTX_LEN = 256

GDN_DK = 128
GDN_DV = 128
GDN_WIDTH = D_MODEL // 2
GDN_HEADS = GDN_WIDTH // GDN_DV
GDN_CHUNK = 64
SHORT_CONV = 5
RWKV_HEAD = 64
RWKV_WIDTH = D_MODEL // 2
RWKV_HEADS = RWKV_WIDTH // RWKV_HEAD
RWKV_LORA = 64
RWKV_GN_EPS = 64e-5
ATT_HEAD = 64
ATT_Q_HEADS = D_MODEL // ATT_HEAD
ATT_KV_HEADS = ATT_Q_HEADS // 8
ATT_GROUP = ATT_Q_HEADS // ATT_KV_HEADS
ATT_WIDTH = ATT_Q_HEADS * ATT_HEAD
WINDOW = 128
ATT_BLOCK = 128
ROPE_BASE = 10000.0
ROPE_FREQS = ATT_HEAD // 4
NEG_INF = -1e30
N_EVEN = (DEPTH + 1) // 2
N_ODD = DEPTH // 2
DEEPNORM_ALPHA = (2 * DEPTH) ** 0.25
DEEPNORM_BETA = (8 * DEPTH) ** -0.25
LN_EPS = 1e-5
GDN_IN = 4 * GDN_WIDTH + 4 * GDN_HEADS
RWKV_IN = 4 * RWKV_WIDTH + 2 * RWKV_LORA
EVEN_IN = GDN_IN + RWKV_IN
EVEN_MIX = GDN_WIDTH + RWKV_WIDTH
ODD_IN = 2 * ATT_WIDTH + 2 * ATT_KV_HEADS * ATT_HEAD

kernel_name = 'hybrid_gdn_rwkv7_swa_flow_block'

F32 = jnp.float32


def layer_norm(x, g, b):
    xf = x.astype(F32)
    mu = xf.mean(-1, keepdims=True)
    var = jnp.square(xf - mu).mean(-1, keepdims=True)
    return ((xf - mu) * lax.rsqrt(var + LN_EPS)).astype(x.dtype) * g + b


def l2_normalize(x):
    xf = x.astype(F32)
    return xf * lax.rsqrt(jnp.sum(xf * xf, -1, keepdims=True) + 1e-6)


def depthwise_conv(u, w):
    pad = w.shape[0] // 2
    return lax.conv_general_dilated(u, w[:, None, :].astype(u.dtype), window_strides=(1,), padding=[(pad, pad)],
                                    dimension_numbers=('NWC', 'WIO', 'NWC'), feature_group_count=u.shape[-1])


def token_shift(u, mix):
    prev = jnp.pad(u, ((0, 0), (1, 0), (0, 0)))[:, :-1]
    nxt = jnp.pad(u, ((0, 0), (0, 1), (0, 0)))[:, 1:]
    return u + mix[0] * (prev - u) + mix[1] * (nxt - u)


def gated_delta_chunked(q, k, v, g, beta, s0):
    bsz, t, h, dk = q.shape
    dv = v.shape[-1]
    n = t // GDN_CHUNK

    def chunks(a):
        a = a.reshape(bsz, n, GDN_CHUNK, h, *a.shape[3:])
        return jnp.moveaxis(a, (1, 3), (0, 2))

    qc, kc, vc, gc, bc = (chunks(a) for a in (q, k, v, g, beta))
    gcum = jnp.cumsum(gc, -1)
    idx = jnp.arange(GDN_CHUNK)
    incl = idx[:, None] >= idx[None, :]
    strict = idx[:, None] > idx[None, :]
    diff = gcum[..., :, None] - gcum[..., None, :]
    decay = jnp.where(incl, jnp.exp(jnp.where(incl, diff, 0.0)), 0.0)
    kb = kc * bc[..., None]
    lmat = jnp.where(strict, jnp.einsum('nbhid,nbhjd->nbhij', kb, kc) * decay, 0.0)
    rhs = jnp.concatenate([vc * bc[..., None], kb * jnp.exp(gcum)[..., None]], -1)
    sol = lax.linalg.triangular_solve(lmat + jnp.eye(GDN_CHUNK, dtype=lmat.dtype), rhs,
                                      left_side=True, lower=True, unit_diagonal=True)
    u_c, w_c = sol[..., :dv], sol[..., dv:]
    attn = jnp.where(incl, jnp.einsum('nbhid,nbhjd->nbhij', qc, kc) * decay, 0.0)
    qg = qc * jnp.exp(gcum)[..., None]
    kg = kc * jnp.exp(gcum[..., -1:] - gcum)[..., None]
    glast = jnp.exp(gcum[..., -1])

    def step(s, inp):
        u_n, w_n, a_n, qg_n, kg_n, gl_n = inp
        v_new = u_n - jnp.einsum('bhcd,bhde->bhce', w_n, s)
        o = jnp.einsum('bhcd,bhde->bhce', qg_n, s) + jnp.einsum('bhij,bhje->bhie', a_n, v_new)
        s = s * gl_n[..., None, None] + jnp.einsum('bhcd,bhce->bhde', kg_n, v_new)
        return s, o

    s, o = lax.scan(step, s0, (u_c, w_c, attn, qg, kg, glast))
    o = jnp.moveaxis(o, (0, 2), (1, 3)).reshape(bsz, t, h, dv)
    return o, s


def gdn_branch(u, s0, conv_w, a_log, dt_bias, norm_g):
    bsz, t, _ = u.shape
    wd = GDN_WIDTH
    qkv = jax.nn.silu(depthwise_conv(u[..., :3 * wd], conv_w))
    z = u[..., 3 * wd:4 * wd]
    beta = jax.nn.sigmoid(u[..., 4 * wd:4 * wd + 2 * GDN_HEADS].astype(F32)).reshape(bsz, t, 2, GDN_HEADS)
    g = -jnp.exp(a_log.astype(F32)) * jax.nn.softplus(
        u[..., 4 * wd + 2 * GDN_HEADS:].astype(F32).reshape(bsz, t, 2, GDN_HEADS) + dt_bias.astype(F32))
    q = l2_normalize(qkv[..., :wd].reshape(bsz, t, GDN_HEADS, GDN_DK)) * (GDN_DK ** -0.5)
    k = l2_normalize(qkv[..., wd:2 * wd].reshape(bsz, t, GDN_HEADS, GDN_DK))
    v = qkv[..., 2 * wd:].reshape(bsz, t, GDN_HEADS, GDN_DV).astype(F32)
    o_f, s_f = gated_delta_chunked(q, k, v, g[:, :, 0], beta[:, :, 0], s0[0])
    rev = lambda a: a[:, ::-1]
    o_b, s_b = gated_delta_chunked(rev(q), rev(k), rev(v), rev(g[:, :, 1]), rev(beta[:, :, 1]), s0[1])
    o = o_f + rev(o_b)
    o = o * lax.rsqrt(jnp.mean(o * o, -1, keepdims=True) + 1e-6) * norm_g.astype(F32)
    o = o.astype(u.dtype) * jax.nn.silu(z).reshape(bsz, t, GDN_HEADS, GDN_DV)
    return o.reshape(bsz, t, wd), jnp.stack([s_f, s_b])


def rwkv_step(s, inp):
    r_t, w_t, k_t, v_t, kk_t, b_t = inp
    sa = -jnp.einsum('bhvk,bhk->bhv', s, kk_t)
    s = s * w_t[:, :, None, :] + sa[..., None] * b_t[:, :, None, :] + v_t[..., None] * k_t[:, :, None, :]
    return s, jnp.einsum('bhvk,bhk->bhv', s, r_t)


def rwkv_branch(u, s0, mix, w0, w_up, a0, a_up, k_k, k_a, r_k, gn_g, gn_b):
    bsz, t, _ = u.shape
    dt = u.dtype
    wd, nh, hn = RWKV_WIDTH, RWKV_HEADS, RWKV_HEAD
    u = token_shift(u, mix).astype(F32)
    r, k, v, gate = (u[..., i * wd:(i + 1) * wd] for i in range(4))
    xw = u[..., 4 * wd:4 * wd + RWKV_LORA]
    xa = u[..., 4 * wd + RWKV_LORA:]
    w_pre = w0[:, None, None, :] + jnp.einsum('btr,drc->dbtc', jnp.tanh(xw), w_up)
    decay = jnp.exp(-jnp.exp(-jax.nn.softplus(-w_pre) - 0.5))
    a = jax.nn.sigmoid(a0[:, None, None, :] + jnp.einsum('btr,drc->dbtc', xa, a_up))
    heads = lambda z: z.reshape(*z.shape[:-1], nh, hn)
    kk = l2_normalize(heads(k * k_k))
    k_dir = heads(k[None] * (1.0 + (a - 1.0) * k_a))
    b_dir = kk[None] * heads(a)
    dec_h = heads(decay)
    r_h, v_h = heads(r), heads(v)
    tm = lambda z: jnp.moveaxis(z, 1, 0)
    y = None
    finals = []
    for d in range(2):
        s_d, y_d = lax.scan(rwkv_step, s0[d], (tm(r_h), tm(dec_h[d]), tm(k_dir[d]), tm(v_h), tm(kk), tm(b_dir[d])),
                            reverse=(d == 1))
        y_d = jnp.moveaxis(y_d, 0, 1)
        y = y_d if y is None else y + y_d
        finals.append(s_d)
    mu = y.mean(-1, keepdims=True)
    var = jnp.square(y - mu).mean(-1, keepdims=True)
    y = (y - mu) * lax.rsqrt(var + RWKV_GN_EPS) * heads(gn_g) + heads(gn_b)
    bonus = jnp.sum(r_h[None] * k_dir * r_k, -1, keepdims=True).sum(0) * v_h
    out = ((y + bonus).reshape(bsz, t, wd) * jax.nn.silu(gate)).astype(dt)
    return out, jnp.stack(finals)


def even_mixer(h_ctx, h_lat, w_in, w_out, gdn_p, rwkv_p, need_ctx):
    def branch(h, s_gdn, s_rwkv):
        u = h @ w_in
        o_g, s_gdn = gdn_branch(u[..., :GDN_IN], s_gdn, *gdn_p)
        o_r, s_rwkv = rwkv_branch(u[..., GDN_IN:], s_rwkv, *rwkv_p)
        return jnp.concatenate([o_g, o_r], -1), s_gdn, s_rwkv

    bsz = h_lat.shape[0]
    s_gdn0 = jnp.zeros((2, bsz, GDN_HEADS, GDN_DK, GDN_DV), F32)
    s_rwkv0 = jnp.zeros((2, bsz, RWKV_HEADS, RWKV_HEAD, RWKV_HEAD), F32)
    o_ctx, s_gdn_c, s_rwkv_c = branch(h_ctx, s_gdn0, s_rwkv0)
    o_lat, _, _ = branch(h_lat, s_gdn_c, s_rwkv_c)
    y_lat = o_lat @ w_out
    y_ctx = o_ctx @ w_out if need_ctx else None
    return y_ctx, y_lat


def rope_tables(t):
    rows = t // GRID_W
    row = jnp.broadcast_to(jnp.arange(rows)[:, None], (rows, GRID_W)).reshape(-1)
    col = jnp.broadcast_to(jnp.arange(GRID_W)[None, :], (rows, GRID_W)).reshape(-1)
    inv = ROPE_BASE ** (-jnp.arange(ROPE_FREQS, dtype=F32) / ROPE_FREQS)
    ang = jnp.stack([row, col], -1).astype(F32)[..., None] * inv
    return jnp.cos(ang), jnp.sin(ang)


def apply_rope_2d(x, cos, sin):
    xf = x.astype(F32).reshape(*x.shape[:-1], 2, 2, ROPE_FREQS)
    x1, x2 = xf[..., 0, :], xf[..., 1, :]
    c, s = cos[:, None], sin[:, None]
    out = jnp.stack([x1 * c - x2 * s, x2 * c + x1 * s], -2)
    return out.reshape(x.shape).astype(x.dtype)


def sink_attend(scores, values, sink_hg):
    bsz, hk, grp, q_len, _ = scores[0].shape
    sink_col = jnp.broadcast_to(sink_hg[None, :, :, None, None], (bsz, hk, grp, q_len, 1))
    p = jax.nn.softmax(jnp.concatenate([*scores, sink_col], -1), axis=-1)
    out, off = None, 0
    for s, v in zip(scores, values):
        n = s.shape[-1]
        term = jnp.einsum('bhgqk,bkhd->bqhgd', p[..., off:off + n].astype(v.dtype), v)
        out = term if out is None else out + term
        off += n
    return out


def odd_mixer(h_ctx, h_lat, w_in, w_out, sink, need_ctx):
    bsz, t, _ = h_lat.shape
    hk, grp, hd = ATT_KV_HEADS, ATT_GROUP, ATT_HEAD
    nq, nkv = ATT_WIDTH, ATT_KV_HEADS * ATT_HEAD
    scale = hd ** -0.5
    sink_hg = sink.astype(F32).reshape(hk, grp)

    u = h_lat @ w_in
    cos, sin = rope_tables(t)
    q = apply_rope_2d(u[..., :nq].reshape(bsz, t, ATT_Q_HEADS, hd), cos, sin) * scale
    q = q.reshape(bsz, t, hk, grp, hd)
    k = apply_rope_2d(u[..., nq:nq + nkv].reshape(bsz, t, hk, hd), cos, sin)
    v = u[..., nq + nkv:nq + 2 * nkv].reshape(bsz, t, hk, hd)
    gate = u[..., nq + 2 * nkv:]

    kv_c = (h_ctx @ w_in[:, nq:nq + 2 * nkv]).reshape(bsz, -1, 2, hk, hd)
    k_c, v_c = kv_c[:, :, 0], kv_c[:, :, 1]

    nb = t // ATT_BLOCK
    span = ATT_BLOCK + 2 * WINDOW
    k_pad = jnp.pad(k, ((0, 0), (WINDOW, WINDOW), (0, 0), (0, 0)))
    v_pad = jnp.pad(v, ((0, 0), (WINDOW, WINDOW), (0, 0), (0, 0)))
    q_blocks = jnp.moveaxis(q.reshape(bsz, nb, ATT_BLOCK, hk, grp, hd), 1, 0)

    def attend_block(args):
        i, q_b = args
        start = i * ATT_BLOCK
        k_w = lax.dynamic_slice_in_dim(k_pad, start, span, axis=1)
        v_w = lax.dynamic_slice_in_dim(v_pad, start, span, axis=1)
        qpos = start + jnp.arange(ATT_BLOCK)
        kpos = start - WINDOW + jnp.arange(span)
        valid = (jnp.abs(qpos[:, None] - kpos[None, :]) <= WINDOW) & (kpos >= 0)[None, :] & (kpos < t)[None, :]
        s_loc = jnp.where(valid, jnp.einsum('bqhgd,bkhd->bhgqk', q_b, k_w).astype(F32), NEG_INF)
        s_ctx = jnp.einsum('bqhgd,bkhd->bhgqk', q_b, k_c).astype(F32)
        return sink_attend([s_loc, s_ctx], [v_w, v_c], sink_hg)

    o = lax.map(attend_block, (jnp.arange(nb), q_blocks))
    o = jnp.moveaxis(o, 0, 1).reshape(bsz, t, ATT_WIDTH)
    y_lat = (o * jax.nn.silu(gate)) @ w_out

    y_ctx = None
    if need_ctx:
        q_c = (h_ctx @ w_in[:, :nq]).reshape(bsz, -1, hk, grp, hd) * scale
        gate_c = h_ctx @ w_in[:, nq + 2 * nkv:]
        s_cc = jnp.einsum('bqhgd,bkhd->bhgqk', q_c, k_c).astype(F32)
        o_c = sink_attend([s_cc], [v_c], sink_hg).reshape(bsz, -1, ATT_WIDTH)
        y_ctx = (o_c * jax.nn.silu(gate_c)) @ w_out
    return y_ctx, y_lat


def setup_inputs(seed: int = 0) -> dict:
    key = jax.random.key(seed)
    ks = jax.random.split(key, 32)
    nrm = lambda k, shape, s: jax.random.normal(k, shape, F32) * s
    uni = lambda k, shape, lo, hi: jax.random.uniform(k, shape, F32, minval=lo, maxval=hi)
    dt = jnp.exp(uni(ks[12], (N_EVEN, 2, GDN_HEADS), math.log(1e-3), math.log(1e-1)))
    return {
        'x': nrm(ks[0], (BATCH, SEQ, D_MODEL), 1.0),
        'c': nrm(ks[1], (BATCH, D_MODEL), 1.0),
        'ctx': nrm(ks[2], (BATCH, CTX_LEN, D_MODEL), 1.0),
        'c_ctx': nrm(ks[3], (D_MODEL,), 1.0),
        'mod_w': nrm(ks[4], (DEPTH, D_MODEL, 3 * D_MODEL), 0.5 * D_MODEL ** -0.5),
        'mod_b': nrm(ks[5], (DEPTH, 3 * D_MODEL), 0.02),
        'ln_g': 1.0 + nrm(ks[6], (DEPTH, D_MODEL), 0.05),
        'ln_b': nrm(ks[7], (DEPTH, D_MODEL), 0.02),
        'ev_w_in': nrm(ks[8], (N_EVEN, D_MODEL, EVEN_IN), D_MODEL ** -0.5),
        'ev_w_out': nrm(ks[9], (N_EVEN, EVEN_MIX, D_MODEL), DEEPNORM_BETA * EVEN_MIX ** -0.5),
        'gdn_conv': nrm(ks[10], (N_EVEN, SHORT_CONV, 3 * GDN_WIDTH), SHORT_CONV ** -0.5),
        'gdn_a_log': jnp.log(uni(ks[11], (N_EVEN, 2, GDN_HEADS), 1.0, 16.0)),
        'gdn_dt_bias': dt + jnp.log(-jnp.expm1(-dt)),
        'gdn_norm_g': 1.0 + nrm(ks[13], (N_EVEN, GDN_DV), 0.05),
        'rwkv_mix': uni(ks[14], (N_EVEN, 2, RWKV_IN), 0.0, 0.5),
        'rwkv_w0': uni(ks[15], (N_EVEN, 2, RWKV_WIDTH), -6.5, -1.5),
        'rwkv_w_up': nrm(ks[16], (N_EVEN, 2, RWKV_LORA, RWKV_WIDTH), 0.1 * RWKV_LORA ** -0.5),
        'rwkv_a0': nrm(ks[17], (N_EVEN, 2, RWKV_WIDTH), 0.1),
        'rwkv_a_up': nrm(ks[18], (N_EVEN, 2, RWKV_LORA, RWKV_WIDTH), 0.1 * RWKV_LORA ** -0.5),
        'rwkv_k_k': 0.85 + nrm(ks[19], (N_EVEN, RWKV_WIDTH), 0.05),
        'rwkv_k_a': 1.0 + nrm(ks[20], (N_EVEN, RWKV_WIDTH), 0.05),
        'rwkv_r_k': nrm(ks[21], (N_EVEN, RWKV_HEADS, RWKV_HEAD), 0.1),
        'rwkv_gn_g': 1.0 + nrm(ks[22], (N_EVEN, RWKV_WIDTH), 0.05),
        'rwkv_gn_b': nrm(ks[23], (N_EVEN, RWKV_WIDTH), 0.02),
        'od_w_in': nrm(ks[24], (N_ODD, D_MODEL, ODD_IN), D_MODEL ** -0.5),
        'od_w_out': nrm(ks[25], (N_ODD, ATT_WIDTH, D_MODEL), DEEPNORM_BETA * ATT_WIDTH ** -0.5),
        'od_sink': nrm(ks[26], (N_ODD, ATT_Q_HEADS), 1.0),
    }


def reference(x, c, ctx, c_ctx, mod_w, mod_b, ln_g, ln_b, ev_w_in, ev_w_out, gdn_conv, gdn_a_log, gdn_dt_bias,
              gdn_norm_g, rwkv_mix, rwkv_w0, rwkv_w_up, rwkv_a0, rwkv_a_up, rwkv_k_k, rwkv_k_a, rwkv_r_k,
              rwkv_gn_g, rwkv_gn_b, od_w_in, od_w_out, od_sink):
    x_lat, x_ctx = x, ctx
    for layer in range(DEPTH):
        need_ctx = layer < DEPTH - 1
        i = layer // 2
        mod = jax.nn.silu(c) @ mod_w[layer] + mod_b[layer]
        mod_c = jax.nn.silu(c_ctx) @ mod_w[layer] + mod_b[layer]
        sh, sc, gt = jnp.split(mod[:, None, :], 3, axis=-1)
        sh_c, sc_c, gt_c = jnp.split(mod_c, 3, axis=-1)
        h_lat = x_lat * (1.0 + sc) + sh
        h_ctx = x_ctx * (1.0 + sc_c) + sh_c
        if layer % 2 == 0:
            gdn_p = (gdn_conv[i], gdn_a_log[i], gdn_dt_bias[i], gdn_norm_g[i])
            rwkv_p = (rwkv_mix[i], rwkv_w0[i], rwkv_w_up[i], rwkv_a0[i], rwkv_a_up[i], rwkv_k_k[i], rwkv_k_a[i],
                      rwkv_r_k[i], rwkv_gn_g[i], rwkv_gn_b[i])
            y_ctx, y_lat = even_mixer(h_ctx, h_lat, ev_w_in[i], ev_w_out[i], gdn_p, rwkv_p, need_ctx)
        else:
            y_ctx, y_lat = odd_mixer(h_ctx, h_lat, od_w_in[i], od_w_out[i], od_sink[i], need_ctx)
        x_lat = layer_norm(DEEPNORM_ALPHA * x_lat + gt * y_lat, ln_g[layer], ln_b[layer])
        if need_ctx:
            x_ctx = layer_norm(DEEPNORM_ALPHA * x_ctx + gt_c * y_ctx, ln_g[layer], ln_b[layer])
    return x_lat
```

```python
import functools
import math

import jax
import jax.numpy as jnp
from jax import lax
from jax.experimental import pallas as pl
from jax.experimental.pallas import tpu as pltpu

F32 = jnp.float32
BF16 = jnp.bfloat16

D_MODEL = 2048
DEPTH = 4
GRID_W = 64
GDN_DK = 128
GDN_DV = 128
GDN_WIDTH = D_MODEL // 2
GDN_HEADS = GDN_WIDTH // GDN_DV
GDN_CHUNK = 64
RWKV_HEAD = 64
RWKV_WIDTH = D_MODEL // 2
RWKV_HEADS = RWKV_WIDTH // RWKV_HEAD
RWKV_LORA = 64
RWKV_GN_EPS = 64e-5
ATT_HEAD = 64
ATT_Q_HEADS = D_MODEL // ATT_HEAD
ATT_KV_HEADS = ATT_Q_HEADS // 8
ATT_GROUP = ATT_Q_HEADS // ATT_KV_HEADS
ATT_WIDTH = ATT_Q_HEADS * ATT_HEAD
WINDOW = 128
ATT_BLOCK = 128
ROPE_BASE = 10000.0
ROPE_FREQS = ATT_HEAD // 4
NEG_INF = -1e30
DEEPNORM_ALPHA = (2 * DEPTH) ** 0.25
LN_EPS = 1e-5
GDN_IN = 4 * GDN_WIDTH + 4 * GDN_HEADS
RWKV_IN = 4 * RWKV_WIDTH + 2 * RWKV_LORA

V7X_VMEM_LIMIT_BYTES = 48 * 1024 * 1024


def _mm_kernel(a_ref, w_ref, o_ref):
    o_ref[...] = jnp.dot(a_ref[...].astype(BF16), w_ref[...].astype(BF16), preferred_element_type=F32)


def _matmul(a, w, tm=512, tn=512):
    m, k = a.shape
    n = w.shape[1]
    mp = -(-m // 8) * 8
    tm = min(tm, mp)
    mp = -(-mp // tm) * tm
    np_ = -(-n // tn) * tn
    if mp != m:
        a = jnp.pad(a, ((0, mp - m), (0, 0)))
    if np_ != n:
        w = jnp.pad(w, ((0, 0), (0, np_ - n)))
    out = pl.pallas_call(
        _mm_kernel,
        grid=(np_ // tn, mp // tm),
        in_specs=[pl.BlockSpec((tm, k), lambda j, i: (i, 0)), pl.BlockSpec((k, tn), lambda j, i: (0, j))],
        out_specs=pl.BlockSpec((tm, tn), lambda j, i: (i, j)),
        out_shape=jax.ShapeDtypeStruct((mp, np_), F32),
        compiler_params=pltpu.CompilerParams(
            dimension_semantics=("arbitrary", "arbitrary"), vmem_limit_bytes=V7X_VMEM_LIMIT_BYTES),
    )(a, w)
    return out[:m, :n]


def _mm(a, w):
    lead = a.shape[:-1]
    return _matmul(a.reshape(-1, a.shape[-1]), w).reshape(*lead, w.shape[1])


def layer_norm(x, g, b):
    mu = x.mean(-1, keepdims=True)
    var = jnp.square(x - mu).mean(-1, keepdims=True)
    return (x - mu) * lax.rsqrt(var + LN_EPS) * g + b


def l2_normalize(x):
    return x * lax.rsqrt(jnp.sum(x * x, -1, keepdims=True) + 1e-6)


def depthwise_conv(u, w):
    pad = w.shape[0] // 2
    return lax.conv_general_dilated(u, w[:, None, :], window_strides=(1,), padding=[(pad, pad)],
                                    dimension_numbers=('NWC', 'WIO', 'NWC'), feature_group_count=u.shape[-1])


def token_shift(u, mix):
    prev = jnp.pad(u, ((0, 0), (1, 0), (0, 0)))[:, :-1]
    nxt = jnp.pad(u, ((0, 0), (0, 1), (0, 0)))[:, 1:]
    return u + mix[0] * (prev - u) + mix[1] * (nxt - u)


def gated_delta_chunked(q, k, v, g, beta, s0):
    bsz, t, h, dk = q.shape
    dv = v.shape[-1]
    n = t // GDN_CHUNK

    def chunks(a):
        a = a.reshape(bsz, n, GDN_CHUNK, h, *a.shape[3:])
        return jnp.moveaxis(a, (1, 3), (0, 2))

    qc, kc, vc, gc, bc = (chunks(a) for a in (q, k, v, g, beta))
    gcum = jnp.cumsum(gc, -1)
    idx = jnp.arange(GDN_CHUNK)
    incl = idx[:, None] >= idx[None, :]
    strict = idx[:, None] > idx[None, :]
    diff = gcum[..., :, None] - gcum[..., None, :]
    decay = jnp.where(incl, jnp.exp(jnp.where(incl, diff, 0.0)), 0.0)
    kb = kc * bc[..., None]
    lmat = jnp.where(strict, jnp.einsum('nbhid,nbhjd->nbhij', kb, kc) * decay, 0.0)
    rhs = jnp.concatenate([vc * bc[..., None], kb * jnp.exp(gcum)[..., None]], -1)
    sol = lax.linalg.triangular_solve(lmat + jnp.eye(GDN_CHUNK, dtype=lmat.dtype), rhs,
                                      left_side=True, lower=True, unit_diagonal=True)
    u_c, w_c = sol[..., :dv], sol[..., dv:]
    attn = jnp.where(incl, jnp.einsum('nbhid,nbhjd->nbhij', qc, kc) * decay, 0.0)
    qg = qc * jnp.exp(gcum)[..., None]
    kg = kc * jnp.exp(gcum[..., -1:] - gcum)[..., None]
    glast = jnp.exp(gcum[..., -1])

    def step(s, inp):
        u_n, w_n, a_n, qg_n, kg_n, gl_n = inp
        v_new = u_n - jnp.einsum('bhcd,bhde->bhce', w_n, s)
        o = jnp.einsum('bhcd,bhde->bhce', qg_n, s) + jnp.einsum('bhij,bhje->bhie', a_n, v_new)
        s = s * gl_n[..., None, None] + jnp.einsum('bhcd,bhce->bhde', kg_n, v_new)
        return s, o

    s, o = lax.scan(step, s0, (u_c, w_c, attn, qg, kg, glast))
    o = jnp.moveaxis(o, (0, 2), (1, 3)).reshape(bsz, t, h, dv)
    return o, s


def gdn_branch(u, s0, conv_w, a_log, dt_bias, norm_g):
    bsz, t, _ = u.shape
    wd = GDN_WIDTH
    qkv = jax.nn.silu(depthwise_conv(u[..., :3 * wd], conv_w))
    z = u[..., 3 * wd:4 * wd]
    beta = jax.nn.sigmoid(u[..., 4 * wd:4 * wd + 2 * GDN_HEADS]).reshape(bsz, t, 2, GDN_HEADS)
    g = -jnp.exp(a_log) * jax.nn.softplus(
        u[..., 4 * wd + 2 * GDN_HEADS:].reshape(bsz, t, 2, GDN_HEADS) + dt_bias)
    q = l2_normalize(qkv[..., :wd].reshape(bsz, t, GDN_HEADS, GDN_DK)) * (GDN_DK ** -0.5)
    k = l2_normalize(qkv[..., wd:2 * wd].reshape(bsz, t, GDN_HEADS, GDN_DK))
    v = qkv[..., 2 * wd:].reshape(bsz, t, GDN_HEADS, GDN_DV)
    o_f, s_f = gated_delta_chunked(q, k, v, g[:, :, 0], beta[:, :, 0], s0[0])
    rev = lambda a: a[:, ::-1]
    o_b, s_b = gated_delta_chunked(rev(q), rev(k), rev(v), rev(g[:, :, 1]), rev(beta[:, :, 1]), s0[1])
    o = o_f + rev(o_b)
    o = o * lax.rsqrt(jnp.mean(o * o, -1, keepdims=True) + 1e-6) * norm_g
    o = o * jax.nn.silu(z).reshape(bsz, t, GDN_HEADS, GDN_DV)
    return o.reshape(bsz, t, wd), jnp.stack([s_f, s_b])


def rwkv_step(s, inp):
    r_t, w_t, k_t, v_t, kk_t, b_t = inp
    sa = -jnp.einsum('bhvk,bhk->bhv', s, kk_t)
    s = s * w_t[:, :, None, :] + sa[..., None] * b_t[:, :, None, :] + v_t[..., None] * k_t[:, :, None, :]
    return s, jnp.einsum('bhvk,bhk->bhv', s, r_t)


def rwkv_branch(u, s0, mix, w0, w_up, a0, a_up, k_k, k_a, r_k, gn_g, gn_b):
    bsz, t, _ = u.shape
    wd, nh, hn = RWKV_WIDTH, RWKV_HEADS, RWKV_HEAD
    u = token_shift(u, mix)
    r, k, v, gate = (u[..., i * wd:(i + 1) * wd] for i in range(4))
    xw = u[..., 4 * wd:4 * wd + RWKV_LORA]
    xa = u[..., 4 * wd + RWKV_LORA:]
    w_pre = w0[:, None, None, :] + jnp.einsum('btr,drc->dbtc', jnp.tanh(xw), w_up)
    decay = jnp.exp(-jnp.exp(-jax.nn.softplus(-w_pre) - 0.5))
    a = jax.nn.sigmoid(a0[:, None, None, :] + jnp.einsum('btr,drc->dbtc', xa, a_up))
    heads = lambda z: z.reshape(*z.shape[:-1], nh, hn)
    kk = l2_normalize(heads(k * k_k))
    k_dir = heads(k[None] * (1.0 + (a - 1.0) * k_a))
    b_dir = kk[None] * heads(a)
    dec_h = heads(decay)
    r_h, v_h = heads(r), heads(v)
    tm = lambda z: jnp.moveaxis(z, 1, 0)
    y = None
    finals = []
    for d in range(2):
        s_d, y_d = lax.scan(rwkv_step, s0[d], (tm(r_h), tm(dec_h[d]), tm(k_dir[d]), tm(v_h), tm(kk), tm(b_dir[d])),
                            reverse=(d == 1))
        y_d = jnp.moveaxis(y_d, 0, 1)
        y = y_d if y is None else y + y_d
        finals.append(s_d)
    mu = y.mean(-1, keepdims=True)
    var = jnp.square(y - mu).mean(-1, keepdims=True)
    y = (y - mu) * lax.rsqrt(var + RWKV_GN_EPS) * heads(gn_g) + heads(gn_b)
    bonus = jnp.sum(r_h[None] * k_dir * r_k, -1, keepdims=True).sum(0) * v_h
    out = (y + bonus).reshape(bsz, t, wd) * jax.nn.silu(gate)
    return out, jnp.stack(finals)


def even_mixer(h_ctx, h_lat, w_in, w_out, gdn_p, rwkv_p, need_ctx):
    def branch(h, s_gdn, s_rwkv):
        u = _mm(h, w_in)
        o_g, s_gdn = gdn_branch(u[..., :GDN_IN], s_gdn, *gdn_p)
        o_r, s_rwkv = rwkv_branch(u[..., GDN_IN:], s_rwkv, *rwkv_p)
        return jnp.concatenate([o_g, o_r], -1), s_gdn, s_rwkv

    bsz = h_lat.shape[0]
    s_gdn0 = jnp.zeros((2, bsz, GDN_HEADS, GDN_DK, GDN_DV), F32)
    s_rwkv0 = jnp.zeros((2, bsz, RWKV_HEADS, RWKV_HEAD, RWKV_HEAD), F32)
    o_ctx, s_gdn_c, s_rwkv_c = branch(h_ctx, s_gdn0, s_rwkv0)
    o_lat, _, _ = branch(h_lat, s_gdn_c, s_rwkv_c)
    y_lat = _mm(o_lat, w_out)
    y_ctx = _mm(o_ctx, w_out) if need_ctx else None
    return y_ctx, y_lat


def rope_tables(t):
    rows = t // GRID_W
    row = jnp.broadcast_to(jnp.arange(rows)[:, None], (rows, GRID_W)).reshape(-1)
    col = jnp.broadcast_to(jnp.arange(GRID_W)[None, :], (rows, GRID_W)).reshape(-1)
    inv = ROPE_BASE ** (-jnp.arange(ROPE_FREQS, dtype=F32) / ROPE_FREQS)
    ang = jnp.stack([row, col], -1).astype(F32)[..., None] * inv
    return jnp.cos(ang), jnp.sin(ang)


def apply_rope_2d(x, cos, sin):
    xf = x.reshape(*x.shape[:-1], 2, 2, ROPE_FREQS)
    x1, x2 = xf[..., 0, :], xf[..., 1, :]
    c, s = cos[:, None], sin[:, None]
    out = jnp.stack([x1 * c - x2 * s, x2 * c + x1 * s], -2)
    return out.reshape(x.shape)


def sink_attend(scores, values, sink_hg):
    bsz, hk, grp, q_len, _ = scores[0].shape
    sink_col = jnp.broadcast_to(sink_hg[None, :, :, None, None], (bsz, hk, grp, q_len, 1))
    p = jax.nn.softmax(jnp.concatenate([*scores, sink_col], -1), axis=-1)
    out, off = None, 0
    for s, v in zip(scores, values):
        n = s.shape[-1]
        term = jnp.einsum('bhgqk,bkhd->bqhgd', p[..., off:off + n], v)
        out = term if out is None else out + term
        off += n
    return out


def odd_mixer(h_ctx, h_lat, w_in, w_out, sink, need_ctx):
    bsz, t, _ = h_lat.shape
    hk, grp, hd = ATT_KV_HEADS, ATT_GROUP, ATT_HEAD
    nq, nkv = ATT_WIDTH, ATT_KV_HEADS * ATT_HEAD
    scale = hd ** -0.5
    sink_hg = sink.reshape(hk, grp)

    u = _mm(h_lat, w_in)
    cos, sin = rope_tables(t)
    q = apply_rope_2d(u[..., :nq].reshape(bsz, t, ATT_Q_HEADS, hd), cos, sin) * scale
    q = q.reshape(bsz, t, hk, grp, hd)
    k = apply_rope_2d(u[..., nq:nq + nkv].reshape(bsz, t, hk, hd), cos, sin)
    v = u[..., nq + nkv:nq + 2 * nkv].reshape(bsz, t, hk, hd)
    gate = u[..., nq + 2 * nkv:]

    kv_c = _mm(h_ctx, w_in[:, nq:nq + 2 * nkv]).reshape(bsz, -1, 2, hk, hd)
    k_c, v_c = kv_c[:, :, 0], kv_c[:, :, 1]

    nb = t // ATT_BLOCK
    span = ATT_BLOCK + 2 * WINDOW
    k_pad = jnp.pad(k, ((0, 0), (WINDOW, WINDOW), (0, 0), (0, 0)))
    v_pad = jnp.pad(v, ((0, 0), (WINDOW, WINDOW), (0, 0), (0, 0)))
    q_blocks = jnp.moveaxis(q.reshape(bsz, nb, ATT_BLOCK, hk, grp, hd), 1, 0)

    def attend_block(args):
        i, q_b = args
        start = i * ATT_BLOCK
        k_w = lax.dynamic_slice_in_dim(k_pad, start, span, axis=1)
        v_w = lax.dynamic_slice_in_dim(v_pad, start, span, axis=1)
        qpos = start + jnp.arange(ATT_BLOCK)
        kpos = start - WINDOW + jnp.arange(span)
        valid = (jnp.abs(qpos[:, None] - kpos[None, :]) <= WINDOW) & (kpos >= 0)[None, :] & (kpos < t)[None, :]
        s_loc = jnp.where(valid, jnp.einsum('bqhgd,bkhd->bhgqk', q_b, k_w), NEG_INF)
        s_ctx = jnp.einsum('bqhgd,bkhd->bhgqk', q_b, k_c)
        return sink_attend([s_loc, s_ctx], [v_w, v_c], sink_hg)

    o = lax.map(attend_block, (jnp.arange(nb), q_blocks))
    o = jnp.moveaxis(o, 0, 1).reshape(bsz, t, ATT_WIDTH)
    y_lat = _mm(o * jax.nn.silu(gate), w_out)

    y_ctx = None
    if need_ctx:
        q_c = _mm(h_ctx, w_in[:, :nq]).reshape(bsz, -1, hk, grp, hd) * scale
        gate_c = _mm(h_ctx, w_in[:, nq + 2 * nkv:])
        s_cc = jnp.einsum('bqhgd,bkhd->bhgqk', q_c, k_c)
        o_c = sink_attend([s_cc], [v_c], sink_hg).reshape(bsz, -1, ATT_WIDTH)
        y_ctx = _mm(o_c * jax.nn.silu(gate_c), w_out)
    return y_ctx, y_lat


def kernel(x, c, ctx, c_ctx, mod_w, mod_b, ln_g, ln_b, ev_w_in, ev_w_out, gdn_conv, gdn_a_log, gdn_dt_bias,
           gdn_norm_g, rwkv_mix, rwkv_w0, rwkv_w_up, rwkv_a0, rwkv_a_up, rwkv_k_k, rwkv_k_a, rwkv_r_k,
           rwkv_gn_g, rwkv_gn_b, od_w_in, od_w_out, od_sink):
    x_lat, x_ctx = x, ctx
    for layer in range(DEPTH):
        need_ctx = layer < DEPTH - 1
        i = layer // 2
        cc = jnp.concatenate([c, c_ctx[None]], 0)
        mod_all = _mm(jax.nn.silu(cc), mod_w[layer]) + mod_b[layer]
        mod, mod_c = mod_all[:-1], mod_all[-1]
        sh, sc, gt = jnp.split(mod[:, None, :], 3, axis=-1)
        sh_c, sc_c, gt_c = jnp.split(mod_c, 3, axis=-1)
        h_lat = x_lat * (1.0 + sc) + sh
        h_ctx = x_ctx * (1.0 + sc_c) + sh_c
        if layer % 2 == 0:
            gdn_p = (gdn_conv[i], gdn_a_log[i], gdn_dt_bias[i], gdn_norm_g[i])
            rwkv_p = (rwkv_mix[i], rwkv_w0[i], rwkv_w_up[i], rwkv_a0[i], rwkv_a_up[i], rwkv_k_k[i], rwkv_k_a[i],
                      rwkv_r_k[i], rwkv_gn_g[i], rwkv_gn_b[i])
            y_ctx, y_lat = even_mixer(h_ctx, h_lat, ev_w_in[i], ev_w_out[i], gdn_p, rwkv_p, need_ctx)
        else:
            y_ctx, y_lat = odd_mixer(h_ctx, h_lat, od_w_in[i], od_w_out[i], od_sink[i], need_ctx)
        x_lat = layer_norm(DEEPNORM_ALPHA * x_lat + gt * y_lat, ln_g[layer], ln_b[layer])
        if need_ctx:
            x_ctx = layer_norm(DEEPNORM_ALPHA * x_ctx + gt_c * y_ctx, ln_g[layer], ln_b[layer])
    return x_lat
```

```python
import functools
import math

import jax
import jax.numpy as jnp
import numpy as np
from jax import lax
from jax.experimental import pallas as pl
from jax.experimental.pallas import tpu as pltpu

F32 = jnp.float32
BF16 = jnp.bfloat16

D_MODEL = 2048
DEPTH = 4
GRID_W = 64
GDN_DK = 128
GDN_DV = 128
GDN_WIDTH = D_MODEL // 2
GDN_HEADS = GDN_WIDTH // GDN_DV
GDN_CHUNK = 64
RWKV_HEAD = 64
RWKV_WIDTH = D_MODEL // 2
RWKV_HEADS = RWKV_WIDTH // RWKV_HEAD
RWKV_LORA = 64
RWKV_GN_EPS = 64e-5
ATT_HEAD = 64
ATT_Q_HEADS = D_MODEL // ATT_HEAD
ATT_KV_HEADS = ATT_Q_HEADS // 8
ATT_GROUP = ATT_Q_HEADS // ATT_KV_HEADS
ATT_WIDTH = ATT_Q_HEADS * ATT_HEAD
WINDOW = 128
ATT_BLOCK = 128
ROPE_BASE = 10000.0
ROPE_FREQS = ATT_HEAD // 4
NEG_INF = -1e30
DEEPNORM_ALPHA = (2 * DEPTH) ** 0.25
LN_EPS = 1e-5
GDN_IN = 4 * GDN_WIDTH + 4 * GDN_HEADS
RWKV_IN = 4 * RWKV_WIDTH + 2 * RWKV_LORA

V7X_VMEM_LIMIT_BYTES = 48 * 1024 * 1024


def _mm_kernel(a_ref, w_ref, o_ref):
    o_ref[...] = jnp.dot(a_ref[...].astype(BF16), w_ref[...].astype(BF16), preferred_element_type=F32)


def _matmul(a, w, tm=512, tn=512):
    m, k = a.shape
    n = w.shape[1]
    mp = -(-m // 8) * 8
    tm = min(tm, mp)
    mp = -(-mp // tm) * tm
    np_ = -(-n // tn) * tn
    if mp != m:
        a = jnp.pad(a, ((0, mp - m), (0, 0)))
    if np_ != n:
        w = jnp.pad(w, ((0, 0), (0, np_ - n)))
    out = pl.pallas_call(
        _mm_kernel,
        grid=(np_ // tn, mp // tm),
        in_specs=[pl.BlockSpec((tm, k), lambda j, i: (i, 0)), pl.BlockSpec((k, tn), lambda j, i: (0, j))],
        out_specs=pl.BlockSpec((tm, tn), lambda j, i: (i, j)),
        out_shape=jax.ShapeDtypeStruct((mp, np_), F32),
        compiler_params=pltpu.CompilerParams(
            dimension_semantics=("arbitrary", "arbitrary"), vmem_limit_bytes=V7X_VMEM_LIMIT_BYTES),
    )(a, w)
    return out[:m, :n]


def _mm(a, w):
    lead = a.shape[:-1]
    return _matmul(a.reshape(-1, a.shape[-1]), w).reshape(*lead, w.shape[1])


HI = lax.Precision.HIGHEST
RWKV_CHUNK = 64
LANES = 128
HALF = 64
(PV_MR0, PV_MR1, PV_MK0, PV_MK1, PV_MV0, PV_MV1, PV_MG0, PV_MG1, PV_W00, PV_W01, PV_A00, PV_A01,
 PV_KK, PV_KA, PV_RK, PV_GNG, PV_GNB) = range(17)
PV_ROWS = 24
MK_ABD, MK_P, MK_Y, MK_L1 = 0, 1, 2, 3
MK_LV = 4
N_LV = 5
MK_PER_DIR = MK_LV + N_LV
MK_EYE = 2 * MK_PER_DIR
MK_BD = MK_EYE + 1
N_MASKS = MK_BD + 1


def rwkv_masks():
    ri = np.arange(128)[:, None]
    ci = np.arange(128)[None, :]
    i, j = ri & 63, ci & 63
    same = (ri < 64) == (ci < 64)
    out = np.zeros((N_MASKS, 128, 128), np.float32)
    for d in range(2):
        strict = (j < i) if d == 0 else (j > i)
        incl = (j <= i) if d == 0 else (j >= i)
        base = d * MK_PER_DIR
        out[base + MK_ABD] = strict & same
        out[base + MK_P] = strict & (ci >= 64)
        out[base + MK_Y] = incl
        for li, s in enumerate((1, 2, 4, 8, 16, 32)):
            blk = (i // (2 * s)) == (j // (2 * s))
            if d == 0:
                m = blk & ((i & s) != 0) & ((j & s) == 0)
            else:
                m = blk & ((i & s) == 0) & ((j & s) != 0)
            out[base + (MK_L1 if li == 0 else MK_LV + li - 1)] = m & same
    out[MK_EYE] = ri == ci
    out[MK_BD] = same
    return out


def cum_mats():
    i = np.arange(64)
    lo = (i[:, None] >= i[None, :]).astype(np.float32)
    return np.stack([lo, lo.T])


def _bdot(a, b):
    return jnp.dot(a.astype(BF16), b.astype(BF16), preferred_element_type=F32)


def _bdot_nt(a, b):
    return lax.dot_general(a.astype(BF16), b.astype(BF16), (((1,), (1,)), ((), ())), preferred_element_type=F32)


def _bdot_tn(a, b):
    return lax.dot_general(a.astype(BF16), b.astype(BF16), (((0,), (0,)), ((), ())), preferred_element_type=F32)


def _hdot(a, b):
    return jnp.dot(a, b, precision=HI, preferred_element_type=F32)


def _rwkv_kernel(lr, lk, lv, lg, ll, cr, ck, cv, cg, cl, pv_ref, mixl_ref, wup_ref, aup_ref, mk_ref, cum_ref,
                 o_lat, o_ctx,
                 s_r, s_v, s_kk, s_g, s_bonus, s_lw0, s_lw1, s_kd0, s_kd1, s_bd0, s_bd1, s_yf, s_yb, s_state,
                 *, t_lat, t_ctx, unroll):
    C = RWKV_CHUNK
    lane = lax.broadcasted_iota(jnp.int32, (C, LANES), 1)
    lo64 = lane < HALF
    bd_ones = mk_ref[MK_BD]

    def pvrow(i):
        return pv_ref[pl.ds(i, 1), :]

    def prep(xr, xk, xv, xg, xl, tseq):
        R = min(256, tseq)

        def tile(it, carry):
            s = pl.multiple_of(it * R, R)
            rows = lax.broadcasted_iota(jnp.int32, (R, LANES), 0)
            has_prev = jnp.where(s > 0, 1.0, 0.0).astype(F32)
            has_next = jnp.where(s + R < tseq, 1.0, 0.0).astype(F32)
            pidx = jnp.maximum(s - 1, 0)
            nidx = jnp.minimum(s + R, tseq - 1)

            def shifted(ref, m0, m1):
                cur = ref[0, pl.ds(s, R), :]
                prow = ref[0, pl.ds(pidx, 1), :] * has_prev
                nrow = ref[0, pl.ds(nidx, 1), :] * has_next
                prev = jnp.where(rows == 0, prow, pltpu.roll(cur, 1, axis=0))
                nxt = jnp.where(rows == R - 1, nrow, pltpu.roll(cur, R - 1, axis=0))
                return cur + m0 * (prev - cur) + m1 * (nxt - cur)

            r = shifted(xr, pvrow(PV_MR0), pvrow(PV_MR1))
            k = shifted(xk, pvrow(PV_MK0), pvrow(PV_MK1))
            v = shifted(xv, pvrow(PV_MV0), pvrow(PV_MV1))
            g = shifted(xg, pvrow(PV_MG0), pvrow(PV_MG1))
            lor = shifted(xl, mixl_ref[0:1, :], mixl_ref[1:2, :])
            th = jnp.tanh(lor)
            kkf = k * pvrow(PV_KK)
            ss = _hdot(kkf * kkf, bd_ones)
            kk = kkf * lax.rsqrt(ss + 1e-6)
            ka = pvrow(PV_KA)
            kd_sum = None
            for d, (s_lw, s_kd, s_bd) in enumerate(((s_lw0, s_kd0, s_bd0), (s_lw1, s_kd1, s_bd1))):
                w_pre = pvrow(PV_W00 + d) + _bdot(th, wup_ref[d])
                lw = -jnp.exp(-jax.nn.softplus(-w_pre) - 0.5)
                ag = jax.nn.sigmoid(pvrow(PV_A00 + d) + _bdot(lor, aup_ref[d]))
                kd = k * (1.0 + (ag - 1.0) * ka)
                s_lw[pl.ds(s, R), :] = lw
                s_kd[pl.ds(s, R), :] = kd
                s_bd[pl.ds(s, R), :] = kk * ag
                kd_sum = kd if kd_sum is None else kd_sum + kd
            bonus = _hdot(r * kd_sum * pvrow(PV_RK), bd_ones) * v
            s_r[pl.ds(s, R), :] = r
            s_v[pl.ds(s, R), :] = v
            s_kk[pl.ds(s, R), :] = kk
            s_g[pl.ds(s, R), :] = g
            s_bonus[pl.ds(s, R), :] = bonus
            return carry

        lax.fori_loop(0, tseq // R, tile, 0)

    def chunk_dir(n, d):
        base = d * MK_PER_DIR
        s_lw, s_kd, s_bd, s_y = ((s_lw0, s_kd0, s_bd0, s_yf), (s_lw1, s_kd1, s_bd1, s_yb))[d]
        st = pl.multiple_of(n * C, C)
        rows = pl.ds(st, C)
        r, v, kk = s_r[rows, :], s_v[rows, :], s_kk[rows, :]
        lw, kd, bdv = s_lw[rows, :], s_kd[rows, :], s_bd[rows, :]
        cw = _hdot(cum_ref[d], lw)
        tot = cw[C - 1:C, :] if d == 0 else cw[0:1, :]
        e_in = jnp.exp(-cw)
        at = -kk * jnp.exp(cw - lw)
        rt = r * jnp.exp(cw)
        bt = bdv * e_in
        kt = kd * e_in
        e_out = jnp.exp(tot - cw)
        bh = bdv * e_out
        kh = kd * e_out
        zero = jnp.zeros_like(at)
        l1 = jnp.concatenate([jnp.where(lo64, at, zero), jnp.where(lo64, zero, at),
                              jnp.where(lo64, rt, zero), jnp.where(lo64, zero, rt)], axis=0)
        gmat = _bdot_nt(l1, jnp.concatenate([bt, kt], axis=0))
        ga = gmat[0:2 * C]
        gr = gmat[2 * C:4 * C]
        ga_sw = jnp.concatenate([ga[0:C], pltpu.roll(ga[C:2 * C], HALF, axis=1)], axis=0)
        a_bd = ga_sw * mk_ref[base + MK_ABD]
        x = mk_ref[MK_EYE] + a_bd * mk_ref[base + MK_L1]
        for li in range(N_LV):
            am = a_bd * mk_ref[base + MK_LV + li]
            x = x + _bdot(_bdot(x, am), x)
        vv = jnp.concatenate([v, v], axis=0)
        p = _bdot(ga * mk_ref[base + MK_P], vv)
        s_old = s_state[d]
        xs = _bdot_nt(jnp.concatenate([at, rt], axis=0), s_old)
        xa = xs[0:C]
        ub = _bdot(x, jnp.concatenate([xa, xa], axis=0) + p)
        u = jnp.where(lo64, ub[0:C], ub[C:2 * C])
        uv = jnp.concatenate([u, v], axis=0)
        yb = _bdot(gr * mk_ref[base + MK_Y], uv)
        s_y[rows, :] = xs[C:2 * C] + jnp.where(lo64, yb[0:C], yb[C:2 * C])
        s_state[d] = s_old * jnp.exp(tot) + bd_ones * _bdot_tn(uv, jnp.concatenate([bh, kh], axis=0))

    def scan(tseq):
        nch = tseq // C
        un = min(unroll, nch)

        def body(it, carry):
            for j in range(un):
                n = it * un + j
                chunk_dir(n, 0)
                chunk_dir(nch - 1 - n, 1)
            return carry

        lax.fori_loop(0, nch // un, body, 0)

    def epilogue(o_ref, tseq):
        R = min(256, tseq)

        def tile(it, carry):
            s = pl.multiple_of(it * R, R)
            rows = pl.ds(s, R)
            y = s_yf[rows, :] + s_yb[rows, :]
            mu = _hdot(y, bd_ones) * (1.0 / HALF)
            yc = y - mu
            var = _hdot(yc * yc, bd_ones) * (1.0 / HALF)
            yn = yc * lax.rsqrt(var + RWKV_GN_EPS) * pvrow(PV_GNG) + pvrow(PV_GNB)
            g = s_g[rows, :]
            o_ref[0, rows, :] = (yn + s_bonus[rows, :]) * (g * jax.nn.sigmoid(g))
            return carry

        lax.fori_loop(0, tseq // R, tile, 0)

    s_state[...] = jnp.zeros_like(s_state)
    prep(cr, ck, cv, cg, cl, t_ctx)
    scan(t_ctx)
    epilogue(o_ctx, t_ctx)
    prep(lr, lk, lv, lg, ll, t_lat)
    scan(t_lat)
    epilogue(o_lat, t_lat)


def _col_block_map(b, p, *, off):
    return (b, 0, off + p)


def rwkv_mixer(u_main_lat, u_lora_lat, u_main_ctx, u_lora_ctx, pv, mixl, wup, aup, unroll=2):
    bsz, t_lat, _ = u_main_lat.shape
    t_ctx = u_main_ctx.shape[1]
    npair = RWKV_HEADS // 2
    masks = jnp.asarray(rwkv_masks())
    cums = jnp.asarray(cum_mats())

    def seq_specs(t):
        return [pl.BlockSpec((1, t, LANES), functools.partial(_col_block_map, off=o * npair)) for o in range(4)] + [
            pl.BlockSpec((1, t, LANES), lambda b, p: (b, 0, 0))]

    in_specs = seq_specs(t_lat) + seq_specs(t_ctx) + [
        pl.BlockSpec((PV_ROWS, LANES), lambda b, p: (0, p)),
        pl.BlockSpec((8, LANES), lambda b, p: (0, 0)),
        pl.BlockSpec((2, LANES, LANES), lambda b, p: (0, 0, p)),
        pl.BlockSpec((2, LANES, LANES), lambda b, p: (0, 0, p)),
        pl.BlockSpec((N_MASKS, LANES, LANES), lambda b, p: (0, 0, 0)),
        pl.BlockSpec((2, HALF, HALF), lambda b, p: (0, 0, 0)),
    ]
    out_specs = [pl.BlockSpec((1, t_lat, LANES), lambda b, p: (b, 0, p)),
                 pl.BlockSpec((1, t_ctx, LANES), lambda b, p: (b, 0, p))]
    scratch = [pltpu.VMEM((t_lat, LANES), F32) for _ in range(13)] + [pltpu.VMEM((2, LANES, LANES), F32)]
    kern = functools.partial(_rwkv_kernel, t_lat=t_lat, t_ctx=t_ctx, unroll=unroll)
    return pl.pallas_call(
        kern,
        grid=(bsz, npair),
        in_specs=in_specs,
        out_specs=out_specs,
        out_shape=[jax.ShapeDtypeStruct((bsz, t_lat, RWKV_WIDTH), F32),
                   jax.ShapeDtypeStruct((bsz, t_ctx, RWKV_WIDTH), F32)],
        scratch_shapes=scratch,
        compiler_params=pltpu.CompilerParams(dimension_semantics=("arbitrary", "arbitrary"),
                                             vmem_limit_bytes=56 * 1024 * 1024),
        name="rwkv7_chunked",
    )(*([u_main_lat] * 4), u_lora_lat, *([u_main_ctx] * 4), u_lora_ctx, pv, mixl, wup, aup, masks, cums)


def rwkv_params(mix, w0, w_up, a0, a_up, k_k, k_a, r_k, gn_g, gn_b):
    wd = RWKV_WIDTH
    rows = [mix[0, 0:wd], mix[1, 0:wd], mix[0, wd:2 * wd], mix[1, wd:2 * wd], mix[0, 2 * wd:3 * wd],
            mix[1, 2 * wd:3 * wd], mix[0, 3 * wd:4 * wd], mix[1, 3 * wd:4 * wd], w0[0], w0[1], a0[0], a0[1],
            k_k, k_a, r_k.reshape(-1), gn_g, gn_b]
    pv = jnp.concatenate([jnp.stack(rows), jnp.zeros((PV_ROWS - len(rows), wd), F32)], 0)
    mixl = jnp.concatenate([mix[:, 4 * wd:], jnp.zeros((6, LANES), F32)], 0)
    z = jnp.zeros((2, HALF, wd), F32)
    wup = jnp.concatenate([w_up, z], 1)
    aup = jnp.concatenate([z, a_up], 1)
    return pv, mixl, wup, aup


def layer_norm(x, g, b):
    mu = x.mean(-1, keepdims=True)
    var = jnp.square(x - mu).mean(-1, keepdims=True)
    return (x - mu) * lax.rsqrt(var + LN_EPS) * g + b


def l2_normalize(x):
    return x * lax.rsqrt(jnp.sum(x * x, -1, keepdims=True) + 1e-6)


def depthwise_conv(u, w):
    pad = w.shape[0] // 2
    return lax.conv_general_dilated(u, w[:, None, :], window_strides=(1,), padding=[(pad, pad)],
                                    dimension_numbers=('NWC', 'WIO', 'NWC'), feature_group_count=u.shape[-1])


def token_shift(u, mix):
    prev = jnp.pad(u, ((0, 0), (1, 0), (0, 0)))[:, :-1]
    nxt = jnp.pad(u, ((0, 0), (0, 1), (0, 0)))[:, 1:]
    return u + mix[0] * (prev - u) + mix[1] * (nxt - u)


def gated_delta_chunked(q, k, v, g, beta, s0):
    bsz, t, h, dk = q.shape
    dv = v.shape[-1]
    n = t // GDN_CHUNK

    def chunks(a):
        a = a.reshape(bsz, n, GDN_CHUNK, h, *a.shape[3:])
        return jnp.moveaxis(a, (1, 3), (0, 2))

    qc, kc, vc, gc, bc = (chunks(a) for a in (q, k, v, g, beta))
    gcum = jnp.cumsum(gc, -1)
    idx = jnp.arange(GDN_CHUNK)
    incl = idx[:, None] >= idx[None, :]
    strict = idx[:, None] > idx[None, :]
    diff = gcum[..., :, None] - gcum[..., None, :]
    decay = jnp.where(incl, jnp.exp(jnp.where(incl, diff, 0.0)), 0.0)
    kb = kc * bc[..., None]
    lmat = jnp.where(strict, jnp.einsum('nbhid,nbhjd->nbhij', kb, kc) * decay, 0.0)
    rhs = jnp.concatenate([vc * bc[..., None], kb * jnp.exp(gcum)[..., None]], -1)
    sol = lax.linalg.triangular_solve(lmat + jnp.eye(GDN_CHUNK, dtype=lmat.dtype), rhs,
                                      left_side=True, lower=True, unit_diagonal=True)
    u_c, w_c = sol[..., :dv], sol[..., dv:]
    attn = jnp.where(incl, jnp.einsum('nbhid,nbhjd->nbhij', qc, kc) * decay, 0.0)
    qg = qc * jnp.exp(gcum)[..., None]
    kg = kc * jnp.exp(gcum[..., -1:] - gcum)[..., None]
    glast = jnp.exp(gcum[..., -1])

    def step(s, inp):
        u_n, w_n, a_n, qg_n, kg_n, gl_n = inp
        v_new = u_n - jnp.einsum('bhcd,bhde->bhce', w_n, s)
        o = jnp.einsum('bhcd,bhde->bhce', qg_n, s) + jnp.einsum('bhij,bhje->bhie', a_n, v_new)
        s = s * gl_n[..., None, None] + jnp.einsum('bhcd,bhce->bhde', kg_n, v_new)
        return s, o

    s, o = lax.scan(step, s0, (u_c, w_c, attn, qg, kg, glast))
    o = jnp.moveaxis(o, (0, 2), (1, 3)).reshape(bsz, t, h, dv)
    return o, s


def gdn_branch(u, s0, conv_w, a_log, dt_bias, norm_g):
    bsz, t, _ = u.shape
    wd = GDN_WIDTH
    qkv = jax.nn.silu(depthwise_conv(u[..., :3 * wd], conv_w))
    z = u[..., 3 * wd:4 * wd]
    beta = jax.nn.sigmoid(u[..., 4 * wd:4 * wd + 2 * GDN_HEADS]).reshape(bsz, t, 2, GDN_HEADS)
    g = -jnp.exp(a_log) * jax.nn.softplus(
        u[..., 4 * wd + 2 * GDN_HEADS:].reshape(bsz, t, 2, GDN_HEADS) + dt_bias)
    q = l2_normalize(qkv[..., :wd].reshape(bsz, t, GDN_HEADS, GDN_DK)) * (GDN_DK ** -0.5)
    k = l2_normalize(qkv[..., wd:2 * wd].reshape(bsz, t, GDN_HEADS, GDN_DK))
    v = qkv[..., 2 * wd:].reshape(bsz, t, GDN_HEADS, GDN_DV)
    o_f, s_f = gated_delta_chunked(q, k, v, g[:, :, 0], beta[:, :, 0], s0[0])
    rev = lambda a: a[:, ::-1]
    o_b, s_b = gated_delta_chunked(rev(q), rev(k), rev(v), rev(g[:, :, 1]), rev(beta[:, :, 1]), s0[1])
    o = o_f + rev(o_b)
    o = o * lax.rsqrt(jnp.mean(o * o, -1, keepdims=True) + 1e-6) * norm_g
    o = o * jax.nn.silu(z).reshape(bsz, t, GDN_HEADS, GDN_DV)
    return o.reshape(bsz, t, wd), jnp.stack([s_f, s_b])


def even_mixer(h_ctx, h_lat, w_in, w_out, gdn_p, rwkv_p, need_ctx):
    bsz = h_lat.shape[0]
    u_ctx = _mm(h_ctx, w_in)
    u_lat = _mm(h_lat, w_in)
    s_gdn0 = jnp.zeros((2, bsz, GDN_HEADS, GDN_DK, GDN_DV), F32)
    og_ctx, s_gdn_c = gdn_branch(u_ctx[..., :GDN_IN], s_gdn0, *gdn_p)
    og_lat, _ = gdn_branch(u_lat[..., :GDN_IN], s_gdn_c, *gdn_p)
    m0, m1 = GDN_IN, GDN_IN + 4 * RWKV_WIDTH
    or_lat, or_ctx = rwkv_mixer(u_lat[..., m0:m1], u_lat[..., m1:], u_ctx[..., m0:m1], u_ctx[..., m1:],
                                *rwkv_params(*rwkv_p))
    y_lat = _mm(jnp.concatenate([og_lat, or_lat], -1), w_out)
    y_ctx = _mm(jnp.concatenate([og_ctx, or_ctx], -1), w_out) if need_ctx else None
    return y_ctx, y_lat


def rope_tables(t):
    rows = t // GRID_W
    row = jnp.broadcast_to(jnp.arange(rows)[:, None], (rows, GRID_W)).reshape(-1)
    col = jnp.broadcast_to(jnp.arange(GRID_W)[None, :], (rows, GRID_W)).reshape(-1)
    inv = ROPE_BASE ** (-jnp.arange(ROPE_FREQS, dtype=F32) / ROPE_FREQS)
    ang = jnp.stack([row, col], -1).astype(F32)[..., None] * inv
    return jnp.cos(ang), jnp.sin(ang)


def apply_rope_2d(x, cos, sin):
    xf = x.reshape(*x.shape[:-1], 2, 2, ROPE_FREQS)
    x1, x2 = xf[..., 0, :], xf[..., 1, :]
    c, s = cos[:, None], sin[:, None]
    out = jnp.stack([x1 * c - x2 * s, x2 * c + x1 * s], -2)
    return out.reshape(x.shape)


def sink_attend(scores, values, sink_hg):
    bsz, hk, grp, q_len, _ = scores[0].shape
    sink_col = jnp.broadcast_to(sink_hg[None, :, :, None, None], (bsz, hk, grp, q_len, 1))
    p = jax.nn.softmax(jnp.concatenate([*scores, sink_col], -1), axis=-1)
    out, off = None, 0
    for s, v in zip(scores, values):
        n = s.shape[-1]
        term = jnp.einsum('bhgqk,bkhd->bqhgd', p[..., off:off + n], v)
        out = term if out is None else out + term
        off += n
    return out


def odd_mixer(h_ctx, h_lat, w_in, w_out, sink, need_ctx):
    bsz, t, _ = h_lat.shape
    hk, grp, hd = ATT_KV_HEADS, ATT_GROUP, ATT_HEAD
    nq, nkv = ATT_WIDTH, ATT_KV_HEADS * ATT_HEAD
    scale = hd ** -0.5
    sink_hg = sink.reshape(hk, grp)

    u = _mm(h_lat, w_in)
    cos, sin = rope_tables(t)
    q = apply_rope_2d(u[..., :nq].reshape(bsz, t, ATT_Q_HEADS, hd), cos, sin) * scale
    q = q.reshape(bsz, t, hk, grp, hd)
    k = apply_rope_2d(u[..., nq:nq + nkv].reshape(bsz, t, hk, hd), cos, sin)
    v = u[..., nq + nkv:nq + 2 * nkv].reshape(bsz, t, hk, hd)
    gate = u[..., nq + 2 * nkv:]

    kv_c = _mm(h_ctx, w_in[:, nq:nq + 2 * nkv]).reshape(bsz, -1, 2, hk, hd)
    k_c, v_c = kv_c[:, :, 0], kv_c[:, :, 1]

    nb = t // ATT_BLOCK
    span = ATT_BLOCK + 2 * WINDOW
    k_pad = jnp.pad(k, ((0, 0), (WINDOW, WINDOW), (0, 0), (0, 0)))
    v_pad = jnp.pad(v, ((0, 0), (WINDOW, WINDOW), (0, 0), (0, 0)))
    q_blocks = jnp.moveaxis(q.reshape(bsz, nb, ATT_BLOCK, hk, grp, hd), 1, 0)

    def attend_block(args):
        i, q_b = args
        start = i * ATT_BLOCK
        k_w = lax.dynamic_slice_in_dim(k_pad, start, span, axis=1)
        v_w = lax.dynamic_slice_in_dim(v_pad, start, span, axis=1)
        qpos = start + jnp.arange(ATT_BLOCK)
        kpos = start - WINDOW + jnp.arange(span)
        valid = (jnp.abs(qpos[:, None] - kpos[None, :]) <= WINDOW) & (kpos >= 0)[None, :] & (kpos < t)[None, :]
        s_loc = jnp.where(valid, jnp.einsum('bqhgd,bkhd->bhgqk', q_b, k_w), NEG_INF)
        s_ctx = jnp.einsum('bqhgd,bkhd->bhgqk', q_b, k_c)
        return sink_attend([s_loc, s_ctx], [v_w, v_c], sink_hg)

    o = lax.map(attend_block, (jnp.arange(nb), q_blocks))
    o = jnp.moveaxis(o, 0, 1).reshape(bsz, t, ATT_WIDTH)
    y_lat = _mm(o * jax.nn.silu(gate), w_out)

    y_ctx = None
    if need_ctx:
        q_c = _mm(h_ctx, w_in[:, :nq]).reshape(bsz, -1, hk, grp, hd) * scale
        gate_c = _mm(h_ctx, w_in[:, nq + 2 * nkv:])
        s_cc = jnp.einsum('bqhgd,bkhd->bhgqk', q_c, k_c)
        o_c = sink_attend([s_cc], [v_c], sink_hg).reshape(bsz, -1, ATT_WIDTH)
        y_ctx = _mm(o_c * jax.nn.silu(gate_c), w_out)
    return y_ctx, y_lat


def kernel(x, c, ctx, c_ctx, mod_w, mod_b, ln_g, ln_b, ev_w_in, ev_w_out, gdn_conv, gdn_a_log, gdn_dt_bias,
           gdn_norm_g, rwkv_mix, rwkv_w0, rwkv_w_up, rwkv_a0, rwkv_a_up, rwkv_k_k, rwkv_k_a, rwkv_r_k,
           rwkv_gn_g, rwkv_gn_b, od_w_in, od_w_out, od_sink):
    x_lat, x_ctx = x, ctx
    for layer in range(DEPTH):
        need_ctx = layer < DEPTH - 1
        i = layer // 2
        cc = jnp.concatenate([c, c_ctx[None]], 0)
        mod_all = _mm(jax.nn.silu(cc), mod_w[layer]) + mod_b[layer]
        mod, mod_c = mod_all[:-1], mod_all[-1]
        sh, sc, gt = jnp.split(mod[:, None, :], 3, axis=-1)
        sh_c, sc_c, gt_c = jnp.split(mod_c, 3, axis=-1)
        h_lat = x_lat * (1.0 + sc) + sh
        h_ctx = x_ctx * (1.0 + sc_c) + sh_c
        if layer % 2 == 0:
            gdn_p = (gdn_conv[i], gdn_a_log[i], gdn_dt_bias[i], gdn_norm_g[i])
            rwkv_p = (rwkv_mix[i], rwkv_w0[i], rwkv_w_up[i], rwkv_a0[i], rwkv_a_up[i], rwkv_k_k[i], rwkv_k_a[i],
                      rwkv_r_k[i], rwkv_gn_g[i], rwkv_gn_b[i])
            y_ctx, y_lat = even_mixer(h_ctx, h_lat, ev_w_in[i], ev_w_out[i], gdn_p, rwkv_p, need_ctx)
        else:
            y_ctx, y_lat = odd_mixer(h_ctx, h_lat, od_w_in[i], od_w_out[i], od_sink[i], need_ctx)
        x_lat = layer_norm(DEEPNORM_ALPHA * x_lat + gt * y_lat, ln_g[layer], ln_b[layer])
        if need_ctx:
            x_ctx = layer_norm(DEEPNORM_ALPHA * x_ctx + gt_c * y_ctx, ln_g[layer], ln_b[layer])
    return x_lat
```

```python
import functools
import math

import jax
import jax.numpy as jnp
import numpy as np
from jax import lax
from jax.experimental import pallas as pl
from jax.experimental.pallas import tpu as pltpu

F32 = jnp.float32
BF16 = jnp.bfloat16

D_MODEL = 2048
DEPTH = 4
GRID_W = 64
GDN_DK = 128
GDN_DV = 128
GDN_WIDTH = D_MODEL // 2
GDN_HEADS = GDN_WIDTH // GDN_DV
GDN_CHUNK = 64
RWKV_HEAD = 64
RWKV_WIDTH = D_MODEL // 2
RWKV_HEADS = RWKV_WIDTH // RWKV_HEAD
RWKV_LORA = 64
RWKV_GN_EPS = 64e-5
ATT_HEAD = 64
ATT_Q_HEADS = D_MODEL // ATT_HEAD
ATT_KV_HEADS = ATT_Q_HEADS // 8
ATT_GROUP = ATT_Q_HEADS // ATT_KV_HEADS
ATT_WIDTH = ATT_Q_HEADS * ATT_HEAD
WINDOW = 128
ATT_BLOCK = 128
ROPE_BASE = 10000.0
ROPE_FREQS = ATT_HEAD // 4
NEG_INF = -1e30
DEEPNORM_ALPHA = (2 * DEPTH) ** 0.25
LN_EPS = 1e-5
GDN_IN = 4 * GDN_WIDTH + 4 * GDN_HEADS
RWKV_IN = 4 * RWKV_WIDTH + 2 * RWKV_LORA

V7X_VMEM_LIMIT_BYTES = 48 * 1024 * 1024


def _mm_kernel(a_ref, w_ref, o_ref):
    o_ref[...] = jnp.dot(a_ref[...].astype(BF16), w_ref[...].astype(BF16), preferred_element_type=F32)


def _matmul(a, w, tm=512, tn=512):
    m, k = a.shape
    n = w.shape[1]
    mp = -(-m // 8) * 8
    tm = min(tm, mp)
    mp = -(-mp // tm) * tm
    np_ = -(-n // tn) * tn
    if mp != m:
        a = jnp.pad(a, ((0, mp - m), (0, 0)))
    if np_ != n:
        w = jnp.pad(w, ((0, 0), (0, np_ - n)))
    out = pl.pallas_call(
        _mm_kernel,
        grid=(np_ // tn, mp // tm),
        in_specs=[pl.BlockSpec((tm, k), lambda j, i: (i, 0)), pl.BlockSpec((k, tn), lambda j, i: (0, j))],
        out_specs=pl.BlockSpec((tm, tn), lambda j, i: (i, j)),
        out_shape=jax.ShapeDtypeStruct((mp, np_), F32),
        compiler_params=pltpu.CompilerParams(
            dimension_semantics=("arbitrary", "arbitrary"), vmem_limit_bytes=V7X_VMEM_LIMIT_BYTES),
    )(a, w)
    return out[:m, :n]


def _mm(a, w):
    lead = a.shape[:-1]
    return _matmul(a.reshape(-1, a.shape[-1]), w).reshape(*lead, w.shape[1])


HI = lax.Precision.HIGHEST
RWKV_CHUNK = 64
LANES = 128
HALF = 64
(PV_MR0, PV_MR1, PV_MK0, PV_MK1, PV_MV0, PV_MV1, PV_MG0, PV_MG1, PV_W00, PV_W01, PV_A00, PV_A01,
 PV_KK, PV_KA, PV_RK, PV_GNG, PV_GNB) = range(17)
PV_ROWS = 24
MK_ABD, MK_P, MK_Y, MK_L1 = 0, 1, 2, 3
MK_LV = 4
N_LV = 5
MK_PER_DIR = MK_LV + N_LV
MK_EYE = 2 * MK_PER_DIR
MK_BD = MK_EYE + 1
N_MASKS = MK_BD + 1


def rwkv_masks():
    ri = np.arange(128)[:, None]
    ci = np.arange(128)[None, :]
    i, j = ri & 63, ci & 63
    same = (ri < 64) == (ci < 64)
    out = np.zeros((N_MASKS, 128, 128), np.float32)
    for d in range(2):
        strict = (j < i) if d == 0 else (j > i)
        incl = (j <= i) if d == 0 else (j >= i)
        base = d * MK_PER_DIR
        out[base + MK_ABD] = strict & same
        out[base + MK_P] = strict & (ci >= 64)
        out[base + MK_Y] = incl
        for li, s in enumerate((1, 2, 4, 8, 16, 32)):
            blk = (i // (2 * s)) == (j // (2 * s))
            if d == 0:
                m = blk & ((i & s) != 0) & ((j & s) == 0)
            else:
                m = blk & ((i & s) == 0) & ((j & s) != 0)
            out[base + (MK_L1 if li == 0 else MK_LV + li - 1)] = m & same
    out[MK_EYE] = ri == ci
    out[MK_BD] = same
    return out


def cum_mats():
    i = np.arange(64)
    lo = (i[:, None] >= i[None, :]).astype(np.float32)
    return np.stack([lo, lo.T])


def _bdot(a, b):
    return jnp.dot(a.astype(BF16), b.astype(BF16), preferred_element_type=F32)


def _bdot_nt(a, b):
    return lax.dot_general(a.astype(BF16), b.astype(BF16), (((1,), (1,)), ((), ())), preferred_element_type=F32)


def _bdot_tn(a, b):
    return lax.dot_general(a.astype(BF16), b.astype(BF16), (((0,), (0,)), ((), ())), preferred_element_type=F32)


def _hdot(a, b):
    return jnp.dot(a, b, precision=HI, preferred_element_type=F32)


def _rwkv_kernel(lr, lk, lv, lg, ll, cr, ck, cv, cg, cl, pv_ref, mixl_ref, wup_ref, aup_ref, mk_ref, cum_ref,
                 o_lat, o_ctx,
                 s_r, s_v, s_kk, s_g, s_bonus, s_lw0, s_lw1, s_kd0, s_kd1, s_bd0, s_bd1, s_yf, s_yb, s_state,
                 *, t_lat, t_ctx, unroll):
    C = RWKV_CHUNK
    lane = lax.broadcasted_iota(jnp.int32, (C, LANES), 1)
    lo64 = lane < HALF
    bd_ones = mk_ref[MK_BD]

    def pvrow(i):
        return pv_ref[pl.ds(i, 1), :]

    def prep(xr, xk, xv, xg, xl, tseq):
        R = min(256, tseq)

        def tile(it, carry):
            s = pl.multiple_of(it * R, R)
            rows = lax.broadcasted_iota(jnp.int32, (R, LANES), 0)
            has_prev = jnp.where(s > 0, 1.0, 0.0).astype(F32)
            has_next = jnp.where(s + R < tseq, 1.0, 0.0).astype(F32)
            pidx = jnp.maximum(s - 1, 0)
            nidx = jnp.minimum(s + R, tseq - 1)

            def shifted(ref, m0, m1):
                cur = ref[0, pl.ds(s, R), :]
                prow = ref[0, pl.ds(pidx, 1), :] * has_prev
                nrow = ref[0, pl.ds(nidx, 1), :] * has_next
                prev = jnp.where(rows == 0, prow, pltpu.roll(cur, 1, axis=0))
                nxt = jnp.where(rows == R - 1, nrow, pltpu.roll(cur, R - 1, axis=0))
                return cur + m0 * (prev - cur) + m1 * (nxt - cur)

            r = shifted(xr, pvrow(PV_MR0), pvrow(PV_MR1))
            k = shifted(xk, pvrow(PV_MK0), pvrow(PV_MK1))
            v = shifted(xv, pvrow(PV_MV0), pvrow(PV_MV1))
            g = shifted(xg, pvrow(PV_MG0), pvrow(PV_MG1))
            lor = shifted(xl, mixl_ref[0:1, :], mixl_ref[1:2, :])
            th = jnp.tanh(lor)
            kkf = k * pvrow(PV_KK)
            ss = _hdot(kkf * kkf, bd_ones)
            kk = kkf * lax.rsqrt(ss + 1e-6)
            ka = pvrow(PV_KA)
            kd_sum = None
            for d, (s_lw, s_kd, s_bd) in enumerate(((s_lw0, s_kd0, s_bd0), (s_lw1, s_kd1, s_bd1))):
                w_pre = pvrow(PV_W00 + d) + _bdot(th, wup_ref[d])
                lw = -jnp.exp(-jax.nn.softplus(-w_pre) - 0.5)
                ag = jax.nn.sigmoid(pvrow(PV_A00 + d) + _bdot(lor, aup_ref[d]))
                kd = k * (1.0 + (ag - 1.0) * ka)
                s_lw[pl.ds(s, R), :] = lw
                s_kd[pl.ds(s, R), :] = kd
                s_bd[pl.ds(s, R), :] = kk * ag
                kd_sum = kd if kd_sum is None else kd_sum + kd
            bonus = _hdot(r * kd_sum * pvrow(PV_RK), bd_ones) * v
            s_r[pl.ds(s, R), :] = r
            s_v[pl.ds(s, R), :] = v
            s_kk[pl.ds(s, R), :] = kk
            s_g[pl.ds(s, R), :] = g
            s_bonus[pl.ds(s, R), :] = bonus
            return carry

        lax.fori_loop(0, tseq // R, tile, 0)

    def chunk_dir(n, d):
        base = d * MK_PER_DIR
        s_lw, s_kd, s_bd, s_y = ((s_lw0, s_kd0, s_bd0, s_yf), (s_lw1, s_kd1, s_bd1, s_yb))[d]
        st = pl.multiple_of(n * C, C)
        rows = pl.ds(st, C)
        r, v, kk = s_r[rows, :], s_v[rows, :], s_kk[rows, :]
        lw, kd, bdv = s_lw[rows, :], s_kd[rows, :], s_bd[rows, :]
        cw = _hdot(cum_ref[d], lw)
        tot = cw[C - 1:C, :] if d == 0 else cw[0:1, :]
        e_in = jnp.exp(-cw)
        at = -kk * jnp.exp(cw - lw)
        rt = r * jnp.exp(cw)
        bt = bdv * e_in
        kt = kd * e_in
        e_out = jnp.exp(tot - cw)
        bh = bdv * e_out
        kh = kd * e_out
        zero = jnp.zeros_like(at)
        l1 = jnp.concatenate([jnp.where(lo64, at, zero), jnp.where(lo64, zero, at),
                              jnp.where(lo64, rt, zero), jnp.where(lo64, zero, rt)], axis=0)
        gmat = _bdot_nt(l1, jnp.concatenate([bt, kt], axis=0))
        ga = gmat[0:2 * C]
        gr = gmat[2 * C:4 * C]
        ga_sw = jnp.concatenate([ga[0:C], pltpu.roll(ga[C:2 * C], HALF, axis=1)], axis=0)
        a_bd = ga_sw * mk_ref[base + MK_ABD]
        x = mk_ref[MK_EYE] + a_bd * mk_ref[base + MK_L1]
        for li in range(N_LV):
            am = a_bd * mk_ref[base + MK_LV + li]
            x = x + _bdot(_bdot(x, am), x)
        vv = jnp.concatenate([v, v], axis=0)
        p = _bdot(ga * mk_ref[base + MK_P], vv)
        s_old = s_state[d]
        xs = _bdot_nt(jnp.concatenate([at, rt], axis=0), s_old)
        xa = xs[0:C]
        ub = _bdot(x, jnp.concatenate([xa, xa], axis=0) + p)
        u = jnp.where(lo64, ub[0:C], ub[C:2 * C])
        uv = jnp.concatenate([u, v], axis=0)
        yb = _bdot(gr * mk_ref[base + MK_Y], uv)
        s_y[rows, :] = xs[C:2 * C] + jnp.where(lo64, yb[0:C], yb[C:2 * C])
        s_state[d] = s_old * jnp.exp(tot) + bd_ones * _bdot_tn(uv, jnp.concatenate([bh, kh], axis=0))

    def scan(tseq):
        nch = tseq // C
        un = min(unroll, nch)

        def body(it, carry):
            for j in range(un):
                n = it * un + j
                chunk_dir(n, 0)
                chunk_dir(nch - 1 - n, 1)
            return carry

        lax.fori_loop(0, nch // un, body, 0)

    def epilogue(o_ref, tseq):
        R = min(256, tseq)

        def tile(it, carry):
            s = pl.multiple_of(it * R, R)
            rows = pl.ds(s, R)
            y = s_yf[rows, :] + s_yb[rows, :]
            mu = _hdot(y, bd_ones) * (1.0 / HALF)
            yc = y - mu
            var = _hdot(yc * yc, bd_ones) * (1.0 / HALF)
            yn = yc * lax.rsqrt(var + RWKV_GN_EPS) * pvrow(PV_GNG) + pvrow(PV_GNB)
            g = s_g[rows, :]
            o_ref[0, rows, :] = (yn + s_bonus[rows, :]) * (g * jax.nn.sigmoid(g))
            return carry

        lax.fori_loop(0, tseq // R, tile, 0)

    s_state[...] = jnp.zeros_like(s_state)
    prep(cr, ck, cv, cg, cl, t_ctx)
    scan(t_ctx)
    epilogue(o_ctx, t_ctx)
    prep(lr, lk, lv, lg, ll, t_lat)
    scan(t_lat)
    epilogue(o_lat, t_lat)


def _col_block_map(b, p, *, off):
    return (b, 0, off + p)


def rwkv_mixer(u_main_lat, u_lora_lat, u_main_ctx, u_lora_ctx, pv, mixl, wup, aup, unroll=2):
    bsz, t_lat, _ = u_main_lat.shape
    t_ctx = u_main_ctx.shape[1]
    npair = RWKV_HEADS // 2
    masks = jnp.asarray(rwkv_masks())
    cums = jnp.asarray(cum_mats())

    def seq_specs(t):
        return [pl.BlockSpec((1, t, LANES), functools.partial(_col_block_map, off=o * npair)) for o in range(4)] + [
            pl.BlockSpec((1, t, LANES), lambda b, p: (b, 0, 0))]

    in_specs = seq_specs(t_lat) + seq_specs(t_ctx) + [
        pl.BlockSpec((PV_ROWS, LANES), lambda b, p: (0, p)),
        pl.BlockSpec((8, LANES), lambda b, p: (0, 0)),
        pl.BlockSpec((2, LANES, LANES), lambda b, p: (0, 0, p)),
        pl.BlockSpec((2, LANES, LANES), lambda b, p: (0, 0, p)),
        pl.BlockSpec((N_MASKS, LANES, LANES), lambda b, p: (0, 0, 0)),
        pl.BlockSpec((2, HALF, HALF), lambda b, p: (0, 0, 0)),
    ]
    out_specs = [pl.BlockSpec((1, t_lat, LANES), lambda b, p: (b, 0, p)),
                 pl.BlockSpec((1, t_ctx, LANES), lambda b, p: (b, 0, p))]
    scratch = [pltpu.VMEM((t_lat, LANES), F32) for _ in range(13)] + [pltpu.VMEM((2, LANES, LANES), F32)]
    kern = functools.partial(_rwkv_kernel, t_lat=t_lat, t_ctx=t_ctx, unroll=unroll)
    return pl.pallas_call(
        kern,
        grid=(bsz, npair),
        in_specs=in_specs,
        out_specs=out_specs,
        out_shape=[jax.ShapeDtypeStruct((bsz, t_lat, RWKV_WIDTH), F32),
                   jax.ShapeDtypeStruct((bsz, t_ctx, RWKV_WIDTH), F32)],
        scratch_shapes=scratch,
        compiler_params=pltpu.CompilerParams(dimension_semantics=("arbitrary", "arbitrary"),
                                             vmem_limit_bytes=56 * 1024 * 1024),
        name="rwkv7_chunked",
    )(*([u_main_lat] * 4), u_lora_lat, *([u_main_ctx] * 4), u_lora_ctx, pv, mixl, wup, aup, masks, cums)


def rwkv_params(mix, w0, w_up, a0, a_up, k_k, k_a, r_k, gn_g, gn_b):
    wd = RWKV_WIDTH
    rows = [mix[0, 0:wd], mix[1, 0:wd], mix[0, wd:2 * wd], mix[1, wd:2 * wd], mix[0, 2 * wd:3 * wd],
            mix[1, 2 * wd:3 * wd], mix[0, 3 * wd:4 * wd], mix[1, 3 * wd:4 * wd], w0[0], w0[1], a0[0], a0[1],
            k_k, k_a, r_k.reshape(-1), gn_g, gn_b]
    pv = jnp.concatenate([jnp.stack(rows), jnp.zeros((PV_ROWS - len(rows), wd), F32)], 0)
    mixl = jnp.concatenate([mix[:, 4 * wd:], jnp.zeros((6, LANES), F32)], 0)
    z = jnp.zeros((2, HALF, wd), F32)
    wup = jnp.concatenate([w_up, z], 1)
    aup = jnp.concatenate([z, a_up], 1)
    return pv, mixl, wup, aup


GP_CONV_Q, GP_CONV_K, GP_CONV_V = 0, 5, 10
GP_NORM, GP_ALOG0, GP_ALOG1, GP_DTB0, GP_DTB1 = 15, 16, 17, 18, 19
GP_ROWS = 24
GM_STRICT, GM_INCL, GM_L1, GM_LV = 0, 1, 2, 3
GN_LV = 5
GM_PER_DIR = GM_LV + GN_LV
GM_EYE = 2 * GM_PER_DIR
GN_MASKS = GM_EYE + 1


def gdn_masks():
    ri = np.arange(128)[:, None]
    ci = np.arange(128)[None, :]
    i, j = ri & 63, ci & 63
    same = (ri < 64) == (ci < 64)
    out = np.zeros((GN_MASKS, 128, 128), np.float32)
    for d in range(2):
        base = d * GM_PER_DIR
        out[base + GM_STRICT] = ((j < i) if d == 0 else (j > i)) & same
        out[base + GM_INCL] = ((j <= i) if d == 0 else (j >= i)) & same
        for li, s in enumerate((1, 2, 4, 8, 16, 32)):
            blk = (i // (2 * s)) == (j // (2 * s))
            m = blk & (((i & s) != 0) & ((j & s) == 0) if d == 0 else ((i & s) == 0) & ((j & s) != 0))
            out[base + (GM_L1 if li == 0 else GM_LV + li - 1)] = m & same
    out[GM_EYE] = ri == ci
    return out


def _chunk_cumsum(x, reverse, chunk):
    n = x.shape[0]
    rowm = lax.broadcasted_iota(jnp.int32, x.shape, 0) & (chunk - 1)
    s = 1
    while s < chunk:
        if not reverse:
            x = x + jnp.where(rowm >= s, pltpu.roll(x, s, axis=0), 0.0)
        else:
            x = x + jnp.where(rowm < chunk - s, pltpu.roll(x, n - s, axis=0), 0.0)
        s *= 2
    return x


def _gdn_kernel(lq, lk, lv, lz, lbg, cq, ck, cv, cz, cbg, gp_ref, mk_ref, o_lat, o_ctx,
                s_q, s_k, s_kb0, s_kb1, s_kbg0, s_kbg1, s_qg0, s_qg1, s_kg0, s_kg1, s_vb0, s_vb1,
                s_gc0, s_gc1, s_of, s_ob, s_t, s_a, s_gl, s_state, *, t_lat, t_ctx):
    C = GDN_CHUNK
    s_kb, s_kbg, s_qg, s_kg, s_vb = (s_kb0, s_kb1), (s_kbg0, s_kbg1), (s_qg0, s_qg1), (s_kg0, s_kg1), (s_vb0, s_vb1)
    s_gc, s_o = (s_gc0, s_gc1), (s_of, s_ob)

    def gprow(i):
        return gp_ref[pl.ds(i, 1), :]

    def prep(xq, xk, xv, xbg, tseq):
        R = min(256, tseq)
        N = R + 16

        def tile(it, carry):
            s = pl.multiple_of(it * R, R)
            has_prev = jnp.where(s > 0, 1.0, 0.0).astype(F32)
            has_next = jnp.where(s + R < tseq, 1.0, 0.0).astype(F32)
            pidx = pl.multiple_of(jnp.maximum(s - 8, 0), 8)
            nidx = pl.multiple_of(jnp.minimum(s + R, tseq - 8), 8)

            def conv_silu(ref, row0):
                xe = jnp.concatenate([ref[0, pl.ds(pidx, 8), :] * has_prev, ref[0, pl.ds(s, R), :],
                                      ref[0, pl.ds(nidx, 8), :] * has_next], axis=0)
                acc = None
                for j in range(5):
                    sh = (N - (j - 2)) % N
                    xs = xe if sh == 0 else pltpu.roll(xe, sh, axis=0)
                    term = gprow(row0 + j) * xs[8:8 + R]
                    acc = term if acc is None else acc + term
                return acc * jax.nn.sigmoid(acc)

            qc = conv_silu(xq, GP_CONV_Q)
            kc = conv_silu(xk, GP_CONV_K)
            v = conv_silu(xv, GP_CONV_V)
            q = qc * (lax.rsqrt(jnp.sum(qc * qc, axis=1, keepdims=True) + 1e-6) * (LANES ** -0.5))
            k = kc * lax.rsqrt(jnp.sum(kc * kc, axis=1, keepdims=True) + 1e-6)
            bg = xbg[0, 0, pl.ds(s, R), :]
            s_q[pl.ds(s, R), :] = q.astype(BF16)
            s_k[pl.ds(s, R), :] = k.astype(BF16)
            for d in range(2):
                beta = jax.nn.sigmoid(bg[:, d:d + 1])
                g = -jnp.exp(gprow(GP_ALOG0 + d)) * jax.nn.softplus(bg[:, 2 + d:3 + d] + gprow(GP_DTB0 + d))
                c_fwd = _chunk_cumsum(g, False, C)
                c_bwd = _chunk_cumsum(g, True, C)
                gc, rest = (c_fwd, c_bwd - g) if d == 0 else (c_bwd, c_fwd - g)
                eg = jnp.exp(gc)
                kb = k * beta
                s_kb[d][pl.ds(s, R), :] = kb.astype(BF16)
                s_kbg[d][pl.ds(s, R), :] = (kb * eg).astype(BF16)
                s_qg[d][pl.ds(s, R), :] = (q * eg).astype(BF16)
                s_kg[d][pl.ds(s, R), :] = (k * jnp.exp(rest)).astype(BF16)
                s_vb[d][pl.ds(s, R), :] = (v * beta).astype(BF16)
                s_gc[d][pl.ds(s, R), :] = gc
                for c in range(R // C):
                    last = c * C + (C - 1 if d == 0 else 0)
                    s_gl[d, pl.ds(it * (R // C) + c, 1), :] = eg[last:last + 1, :]
            return carry

        lax.fori_loop(0, tseq // R, tile, 0)

    def build_pair(m, d):
        base = d * GM_PER_DIR
        rows = pl.ds(pl.multiple_of(m * 2 * C, 2 * C), 2 * C)
        gmat = _bdot_nt(jnp.concatenate([s_kb[d][rows, :], s_q[rows, :]], axis=0), s_k[rows, :])
        gc = s_gc[d][rows, :]
        m_incl = mk_ref[base + GM_INCL]
        dec = jnp.exp((gc - gc.T) * m_incl)
        a = -(gmat[0:2 * C] * dec * mk_ref[base + GM_STRICT])
        s_a[d, m] = (gmat[2 * C:4 * C] * dec * m_incl).astype(BF16)
        return a

    def invert(a_list, d_list):
        xs = [mk_ref[GM_EYE] + a * mk_ref[d * GM_PER_DIR + GM_L1] for a, d in zip(a_list, d_list)]
        for li in range(GN_LV):
            ams = [a * mk_ref[d * GM_PER_DIR + GM_LV + li] for a, d in zip(a_list, d_list)]
            t1 = [_bdot(x, am) for x, am in zip(xs, ams)]
            xs = [x + _bdot(t, x) for x, t in zip(xs, t1)]
        return xs

    def phase1(tseq):
        npair = tseq // (2 * C)

        def body(it, carry):
            probs = [(it * 2 + j, d) for j in range(2) for d in range(2)] if npair >= 2 else [(it, d) for d in range(2)]
            a_list = [build_pair(m, d) for m, d in probs]
            xs = invert(a_list, [d for _, d in probs])
            for (m, d), x in zip(probs, xs):
                s_t[d, m] = x.astype(BF16)
            return carry

        lax.fori_loop(0, max(npair // 2, 1), body, 0)

    def phase2(tseq):
        npair = tseq // (2 * C)
        z64 = jnp.zeros((C, LANES), F32)

        def pad(x, h):
            return jnp.concatenate([x, z64] if h == 0 else [z64, x], axis=0)

        def body(it, carry):
            pm = (it, npair - 1 - it)
            for step in range(2):
                half = (step, 1 - step)
                r0 = [pl.multiple_of(pm[d] * 2 * C + half[d] * C, C) for d in range(2)]
                rows = [pl.ds(r0[d], C) for d in range(2)]
                st = [s_state[d] for d in range(2)]
                zq = [_bdot(jnp.concatenate([s_kbg[d][rows[d], :], s_qg[d][rows[d], :]], axis=0), st[d])
                      for d in range(2)]
                rhs = [s_vb[d][rows[d], :].astype(F32) - zq[d][0:C] for d in range(2)]
                vnew = [_bdot(s_t[d, pm[d], half[d] * C:(half[d] + 1) * C, :], pad(rhs[d], half[d])) for d in range(2)]
                oo = [zq[d][C:2 * C] + _bdot(s_a[d, pm[d], half[d] * C:(half[d] + 1) * C, :], pad(vnew[d], half[d]))
                      for d in range(2)]
                for d in range(2):
                    s_o[d][rows[d], :] = oo[d]
                    n = pm[d] * 2 + half[d]
                    s_state[d] = st[d] * s_gl[d, pl.ds(n, 1), :] + _bdot_tn(s_kg[d][rows[d], :], vnew[d])
            return carry

        lax.fori_loop(0, npair, body, 0)

    def epilogue(o_ref, z_ref, tseq):
        R = min(256, tseq)

        def tile(it, carry):
            s = pl.multiple_of(it * R, R)
            rows = pl.ds(s, R)
            o = s_of[rows, :] + s_ob[rows, :]
            o = o * lax.rsqrt(jnp.mean(o * o, axis=1, keepdims=True) + 1e-6) * gprow(GP_NORM)
            z = z_ref[0, rows, :]
            o_ref[0, rows, :] = o * (z * jax.nn.sigmoid(z))
            return carry

        lax.fori_loop(0, tseq // R, tile, 0)

    s_state[...] = jnp.zeros_like(s_state)
    for (xq, xk, xv, xz, xbg, o_ref, tseq) in ((cq, ck, cv, cz, cbg, o_ctx, t_ctx), (lq, lk, lv, lz, lbg, o_lat, t_lat)):
        prep(xq, xk, xv, xbg, tseq)
        phase1(tseq)
        phase2(tseq)
        epilogue(o_ref, xz, tseq)


def gdn_mixer(u_lat, bg_lat, u_ctx, bg_ctx, gp):
    bsz, t_lat, _ = u_lat.shape
    t_ctx = u_ctx.shape[1]
    nh = GDN_HEADS
    masks = jnp.asarray(gdn_masks())

    def seq_specs(t):
        return [pl.BlockSpec((1, t, LANES), functools.partial(_col_block_map, off=o * nh)) for o in range(4)] + [
            pl.BlockSpec((1, 1, t, 4), lambda b, h: (b, h, 0, 0))]

    in_specs = seq_specs(t_lat) + seq_specs(t_ctx) + [
        pl.BlockSpec((GP_ROWS, LANES), lambda b, h: (0, h)),
        pl.BlockSpec((GN_MASKS, LANES, LANES), lambda b, h: (0, 0, 0)),
    ]
    out_specs = [pl.BlockSpec((1, t_lat, LANES), lambda b, h: (b, 0, h)),
                 pl.BlockSpec((1, t_ctx, LANES), lambda b, h: (b, 0, h))]
    npair = t_lat // (2 * GDN_CHUNK)
    scratch = ([pltpu.VMEM((t_lat, LANES), BF16) for _ in range(12)] + [pltpu.VMEM((t_lat, LANES), F32) for _ in range(4)]
               + [pltpu.VMEM((2, npair, LANES, LANES), BF16), pltpu.VMEM((2, npair, LANES, LANES), BF16),
                  pltpu.VMEM((2, 2 * npair, LANES), F32), pltpu.VMEM((2, LANES, LANES), F32)])
    kern = functools.partial(_gdn_kernel, t_lat=t_lat, t_ctx=t_ctx)
    return pl.pallas_call(
        kern,
        grid=(bsz, nh),
        in_specs=in_specs,
        out_specs=out_specs,
        out_shape=[jax.ShapeDtypeStruct((bsz, t_lat, GDN_WIDTH), F32), jax.ShapeDtypeStruct((bsz, t_ctx, GDN_WIDTH), F32)],
        scratch_shapes=scratch,
        compiler_params=pltpu.CompilerParams(dimension_semantics=("arbitrary", "arbitrary"),
                                             vmem_limit_bytes=56 * 1024 * 1024),
        name="gdn_chunked",
    )(*([u_lat] * 4), bg_lat, *([u_ctx] * 4), bg_ctx, gp, masks)


def gdn_params(conv_w, a_log, dt_bias, norm_g):
    wd = GDN_WIDTH
    rep = lambda x: jnp.repeat(x, LANES)
    rows = [conv_w[j, o * wd:(o + 1) * wd] for o in range(3) for j in range(5)]
    rows += [jnp.tile(norm_g, GDN_HEADS), rep(a_log[0]), rep(a_log[1]), rep(dt_bias[0]), rep(dt_bias[1])]
    return jnp.concatenate([jnp.stack(rows), jnp.zeros((GP_ROWS - len(rows), wd), F32)], 0)


def gdn_bg(u_small):
    bsz, t, _ = u_small.shape
    x = u_small.reshape(bsz, t, 2, 2, GDN_HEADS)
    return jnp.transpose(x, (0, 4, 1, 2, 3)).reshape(bsz, GDN_HEADS, t, 4)


def layer_norm(x, g, b):
    mu = x.mean(-1, keepdims=True)
    var = jnp.square(x - mu).mean(-1, keepdims=True)
    return (x - mu) * lax.rsqrt(var + LN_EPS) * g + b


def l2_normalize(x):
    return x * lax.rsqrt(jnp.sum(x * x, -1, keepdims=True) + 1e-6)


def depthwise_conv(u, w):
    pad = w.shape[0] // 2
    return lax.conv_general_dilated(u, w[:, None, :], window_strides=(1,), padding=[(pad, pad)],
                                    dimension_numbers=('NWC', 'WIO', 'NWC'), feature_group_count=u.shape[-1])


def token_shift(u, mix):
    prev = jnp.pad(u, ((0, 0), (1, 0), (0, 0)))[:, :-1]
    nxt = jnp.pad(u, ((0, 0), (0, 1), (0, 0)))[:, 1:]
    return u + mix[0] * (prev - u) + mix[1] * (nxt - u)


def gated_delta_chunked(q, k, v, g, beta, s0):
    bsz, t, h, dk = q.shape
    dv = v.shape[-1]
    n = t // GDN_CHUNK

    def chunks(a):
        a = a.reshape(bsz, n, GDN_CHUNK, h, *a.shape[3:])
        return jnp.moveaxis(a, (1, 3), (0, 2))

    qc, kc, vc, gc, bc = (chunks(a) for a in (q, k, v, g, beta))
    gcum = jnp.cumsum(gc, -1)
    idx = jnp.arange(GDN_CHUNK)
    incl = idx[:, None] >= idx[None, :]
    strict = idx[:, None] > idx[None, :]
    diff = gcum[..., :, None] - gcum[..., None, :]
    decay = jnp.where(incl, jnp.exp(jnp.where(incl, diff, 0.0)), 0.0)
    kb = kc * bc[..., None]
    lmat = jnp.where(strict, jnp.einsum('nbhid,nbhjd->nbhij', kb, kc) * decay, 0.0)
    rhs = jnp.concatenate([vc * bc[..., None], kb * jnp.exp(gcum)[..., None]], -1)
    sol = lax.linalg.triangular_solve(lmat + jnp.eye(GDN_CHUNK, dtype=lmat.dtype), rhs,
                                      left_side=True, lower=True, unit_diagonal=True)
    u_c, w_c = sol[..., :dv], sol[..., dv:]
    attn = jnp.where(incl, jnp.einsum('nbhid,nbhjd->nbhij', qc, kc) * decay, 0.0)
    qg = qc * jnp.exp(gcum)[..., None]
    kg = kc * jnp.exp(gcum[..., -1:] - gcum)[..., None]
    glast = jnp.exp(gcum[..., -1])

    def step(s, inp):
        u_n, w_n, a_n, qg_n, kg_n, gl_n = inp
        v_new = u_n - jnp.einsum('bhcd,bhde->bhce', w_n, s)
        o = jnp.einsum('bhcd,bhde->bhce', qg_n, s) + jnp.einsum('bhij,bhje->bhie', a_n, v_new)
        s = s * gl_n[..., None, None] + jnp.einsum('bhcd,bhce->bhde', kg_n, v_new)
        return s, o

    s, o = lax.scan(step, s0, (u_c, w_c, attn, qg, kg, glast))
    o = jnp.moveaxis(o, (0, 2), (1, 3)).reshape(bsz, t, h, dv)
    return o, s


def gdn_branch(u, s0, conv_w, a_log, dt_bias, norm_g):
    bsz, t, _ = u.shape
    wd = GDN_WIDTH
    qkv = jax.nn.silu(depthwise_conv(u[..., :3 * wd], conv_w))
    z = u[..., 3 * wd:4 * wd]
    beta = jax.nn.sigmoid(u[..., 4 * wd:4 * wd + 2 * GDN_HEADS]).reshape(bsz, t, 2, GDN_HEADS)
    g = -jnp.exp(a_log) * jax.nn.softplus(
        u[..., 4 * wd + 2 * GDN_HEADS:].reshape(bsz, t, 2, GDN_HEADS) + dt_bias)
    q = l2_normalize(qkv[..., :wd].reshape(bsz, t, GDN_HEADS, GDN_DK)) * (GDN_DK ** -0.5)
    k = l2_normalize(qkv[..., wd:2 * wd].reshape(bsz, t, GDN_HEADS, GDN_DK))
    v = qkv[..., 2 * wd:].reshape(bsz, t, GDN_HEADS, GDN_DV)
    o_f, s_f = gated_delta_chunked(q, k, v, g[:, :, 0], beta[:, :, 0], s0[0])
    rev = lambda a: a[:, ::-1]
    o_b, s_b = gated_delta_chunked(rev(q), rev(k), rev(v), rev(g[:, :, 1]), rev(beta[:, :, 1]), s0[1])
    o = o_f + rev(o_b)
    o = o * lax.rsqrt(jnp.mean(o * o, -1, keepdims=True) + 1e-6) * norm_g
    o = o * jax.nn.silu(z).reshape(bsz, t, GDN_HEADS, GDN_DV)
    return o.reshape(bsz, t, wd), jnp.stack([s_f, s_b])


def even_mixer(h_ctx, h_lat, w_in, w_out, gdn_p, rwkv_p, need_ctx):
    bsz = h_lat.shape[0]
    u_ctx = _mm(h_ctx, w_in)
    u_lat = _mm(h_lat, w_in)
    g1 = 4 * GDN_WIDTH
    og_lat, og_ctx = gdn_mixer(u_lat[..., :g1], gdn_bg(u_lat[..., g1:GDN_IN]), u_ctx[..., :g1],
                               gdn_bg(u_ctx[..., g1:GDN_IN]), gdn_params(*gdn_p))
    m0, m1 = GDN_IN, GDN_IN + 4 * RWKV_WIDTH
    or_lat, or_ctx = rwkv_mixer(u_lat[..., m0:m1], u_lat[..., m1:], u_ctx[..., m0:m1], u_ctx[..., m1:],
                                *rwkv_params(*rwkv_p))
    y_lat = _mm(jnp.concatenate([og_lat, or_lat], -1), w_out)
    y_ctx = _mm(jnp.concatenate([og_ctx, or_ctx], -1), w_out) if need_ctx else None
    return y_ctx, y_lat


def rope_tables(t):
    rows = t // GRID_W
    row = jnp.broadcast_to(jnp.arange(rows)[:, None], (rows, GRID_W)).reshape(-1)
    col = jnp.broadcast_to(jnp.arange(GRID_W)[None, :], (rows, GRID_W)).reshape(-1)
    inv = ROPE_BASE ** (-jnp.arange(ROPE_FREQS, dtype=F32) / ROPE_FREQS)
    ang = jnp.stack([row, col], -1).astype(F32)[..., None] * inv
    return jnp.cos(ang), jnp.sin(ang)


def apply_rope_2d(x, cos, sin):
    xf = x.reshape(*x.shape[:-1], 2, 2, ROPE_FREQS)
    x1, x2 = xf[..., 0, :], xf[..., 1, :]
    c, s = cos[:, None], sin[:, None]
    out = jnp.stack([x1 * c - x2 * s, x2 * c + x1 * s], -2)
    return out.reshape(x.shape)


def sink_attend(scores, values, sink_hg):
    bsz, hk, grp, q_len, _ = scores[0].shape
    sink_col = jnp.broadcast_to(sink_hg[None, :, :, None, None], (bsz, hk, grp, q_len, 1))
    p = jax.nn.softmax(jnp.concatenate([*scores, sink_col], -1), axis=-1)
    out, off = None, 0
    for s, v in zip(scores, values):
        n = s.shape[-1]
        term = jnp.einsum('bhgqk,bkhd->bqhgd', p[..., off:off + n], v)
        out = term if out is None else out + term
        off += n
    return out


def odd_mixer(h_ctx, h_lat, w_in, w_out, sink, need_ctx):
    bsz, t, _ = h_lat.shape
    hk, grp, hd = ATT_KV_HEADS, ATT_GROUP, ATT_HEAD
    nq, nkv = ATT_WIDTH, ATT_KV_HEADS * ATT_HEAD
    scale = hd ** -0.5
    sink_hg = sink.reshape(hk, grp)

    u = _mm(h_lat, w_in)
    cos, sin = rope_tables(t)
    q = apply_rope_2d(u[..., :nq].reshape(bsz, t, ATT_Q_HEADS, hd), cos, sin) * scale
    q = q.reshape(bsz, t, hk, grp, hd)
    k = apply_rope_2d(u[..., nq:nq + nkv].reshape(bsz, t, hk, hd), cos, sin)
    v = u[..., nq + nkv:nq + 2 * nkv].reshape(bsz, t, hk, hd)
    gate = u[..., nq + 2 * nkv:]

    kv_c = _mm(h_ctx, w_in[:, nq:nq + 2 * nkv]).reshape(bsz, -1, 2, hk, hd)
    k_c, v_c = kv_c[:, :, 0], kv_c[:, :, 1]

    nb = t // ATT_BLOCK
    span = ATT_BLOCK + 2 * WINDOW
    k_pad = jnp.pad(k, ((0, 0), (WINDOW, WINDOW), (0, 0), (0, 0)))
    v_pad = jnp.pad(v, ((0, 0), (WINDOW, WINDOW), (0, 0), (0, 0)))
    q_blocks = jnp.moveaxis(q.reshape(bsz, nb, ATT_BLOCK, hk, grp, hd), 1, 0)

    def attend_block(args):
        i, q_b = args
        start = i * ATT_BLOCK
        k_w = lax.dynamic_slice_in_dim(k_pad, start, span, axis=1)
        v_w = lax.dynamic_slice_in_dim(v_pad, start, span, axis=1)
        qpos = start + jnp.arange(ATT_BLOCK)
        kpos = start - WINDOW + jnp.arange(span)
        valid = (jnp.abs(qpos[:, None] - kpos[None, :]) <= WINDOW) & (kpos >= 0)[None, :] & (kpos < t)[None, :]
        s_loc = jnp.where(valid, jnp.einsum('bqhgd,bkhd->bhgqk', q_b, k_w), NEG_INF)
        s_ctx = jnp.einsum('bqhgd,bkhd->bhgqk', q_b, k_c)
        return sink_attend([s_loc, s_ctx], [v_w, v_c], sink_hg)

    o = lax.map(attend_block, (jnp.arange(nb), q_blocks))
    o = jnp.moveaxis(o, 0, 1).reshape(bsz, t, ATT_WIDTH)
    y_lat = _mm(o * jax.nn.silu(gate), w_out)

    y_ctx = None
    if need_ctx:
        q_c = _mm(h_ctx, w_in[:, :nq]).reshape(bsz, -1, hk, grp, hd) * scale
        gate_c = _mm(h_ctx, w_in[:, nq + 2 * nkv:])
        s_cc = jnp.einsum('bqhgd,bkhd->bhgqk', q_c, k_c)
        o_c = sink_attend([s_cc], [v_c], sink_hg).reshape(bsz, -1, ATT_WIDTH)
        y_ctx = _mm(o_c * jax.nn.silu(gate_c), w_out)
    return y_ctx, y_lat


def kernel(x, c, ctx, c_ctx, mod_w, mod_b, ln_g, ln_b, ev_w_in, ev_w_out, gdn_conv, gdn_a_log, gdn_dt_bias,
           gdn_norm_g, rwkv_mix, rwkv_w0, rwkv_w_up, rwkv_a0, rwkv_a_up, rwkv_k_k, rwkv_k_a, rwkv_r_k,
           rwkv_gn_g, rwkv_gn_b, od_w_in, od_w_out, od_sink):
    x_lat, x_ctx = x, ctx
    for layer in range(DEPTH):
        need_ctx = layer < DEPTH - 1
        i = layer // 2
        cc = jnp.concatenate([c, c_ctx[None]], 0)
        mod_all = _mm(jax.nn.silu(cc), mod_w[layer]) + mod_b[layer]
        mod, mod_c = mod_all[:-1], mod_all[-1]
        sh, sc, gt = jnp.split(mod[:, None, :], 3, axis=-1)
        sh_c, sc_c, gt_c = jnp.split(mod_c, 3, axis=-1)
        h_lat = x_lat * (1.0 + sc) + sh
        h_ctx = x_ctx * (1.0 + sc_c) + sh_c
        if layer % 2 == 0:
            gdn_p = (gdn_conv[i], gdn_a_log[i], gdn_dt_bias[i], gdn_norm_g[i])
            rwkv_p = (rwkv_mix[i], rwkv_w0[i], rwkv_w_up[i], rwkv_a0[i], rwkv_a_up[i], rwkv_k_k[i], rwkv_k_a[i],
                      rwkv_r_k[i], rwkv_gn_g[i], rwkv_gn_b[i])
            y_ctx, y_lat = even_mixer(h_ctx, h_lat, ev_w_in[i], ev_w_out[i], gdn_p, rwkv_p, need_ctx)
        else:
            y_ctx, y_lat = odd_mixer(h_ctx, h_lat, od_w_in[i], od_w_out[i], od_sink[i], need_ctx)
        x_lat = layer_norm(DEEPNORM_ALPHA * x_lat + gt * y_lat, ln_g[layer], ln_b[layer])
        if need_ctx:
            x_ctx = layer_norm(DEEPNORM_ALPHA * x_ctx + gt_c * y_ctx, ln_g[layer], ln_b[layer])
    return x_lat
```

```python
import functools

import jax
import jax.numpy as jnp
import numpy as np
from jax import lax
from jax.experimental import pallas as pl
from jax.experimental.pallas import tpu as pltpu

F32 = jnp.float32
BF16 = jnp.bfloat16

D_MODEL = 2048
DEPTH = 4
GRID_W = 64
GDN_WIDTH = D_MODEL // 2
GDN_HEADS = 8
GDN_CHUNK = 64
RWKV_WIDTH = D_MODEL // 2
RWKV_HEADS = 16
RWKV_CHUNK = 64
RWKV_GN_EPS = 64e-5
ATT_HEAD = 64
ATT_Q_HEADS = D_MODEL // ATT_HEAD
ATT_KV_HEADS = ATT_Q_HEADS // 8
ATT_GROUP = ATT_Q_HEADS // ATT_KV_HEADS
ATT_WIDTH = ATT_Q_HEADS * ATT_HEAD
WINDOW = 128
ATT_BLOCK = 128
ROPE_BASE = 10000.0
ROPE_FREQS = ATT_HEAD // 4
NEG_INF = -1e30
DEEPNORM_ALPHA = (2 * DEPTH) ** 0.25
LN_EPS = 1e-5
GDN_IN = 4 * GDN_WIDTH + 4 * GDN_HEADS
RWKV_IN = 4 * RWKV_WIDTH + 128

LANES = 128
HALF = 64
V7X_VMEM_LIMIT_BYTES = 56 * 1024 * 1024
EV_GDN_COL = 0
EV_RWKV_COL = 4 * GDN_WIDTH
EV_LORA_COL = EV_RWKV_COL + 4 * RWKV_WIDTH
EV_SMALL_COL = EV_LORA_COL + LANES
EV_COLS = EV_SMALL_COL + LANES
EV_TN = 768
OD_TN = 1024


def _mm_kernel(a_ref, w_ref, o_ref):
    o_ref[...] = jnp.dot(a_ref[...].astype(BF16), w_ref[...].astype(BF16), preferred_element_type=F32)


def _small_matmul(a, w, tn=512):
    m, k = a.shape
    n = w.shape[1]
    mp = -(-m // 8) * 8
    a = jnp.pad(a, ((0, mp - m), (0, 0)))
    out = pl.pallas_call(
        _mm_kernel,
        grid=(n // tn,),
        in_specs=[pl.BlockSpec((mp, k), lambda j: (0, 0)), pl.BlockSpec((k, tn), lambda j: (0, j))],
        out_specs=pl.BlockSpec((mp, tn), lambda j: (0, j)),
        out_shape=jax.ShapeDtypeStruct((mp, n), F32),
        compiler_params=pltpu.CompilerParams(dimension_semantics=("arbitrary",), vmem_limit_bytes=V7X_VMEM_LIMIT_BYTES),
        name="mod_matmul",
    )(a, w)
    return out[:m]


def _inproj_kernel(x_ref, sc_ref, sh_ref, w_ref, o_ref, h_ref):
    @pl.when(pl.program_id(2) == 0)
    def _():
        h_ref[...] = (x_ref[0] * (1.0 + sc_ref[0]) + sh_ref[0]).astype(BF16)

    o_ref[0] = jnp.dot(h_ref[...], w_ref[...], preferred_element_type=F32)


def inproj(x, sc, sh, w, tm, tn):
    bsz, t, d = x.shape
    n = w.shape[1]
    tm = min(tm, t)
    assert t % tm == 0 and n % tn == 0
    return pl.pallas_call(
        _inproj_kernel,
        grid=(bsz, t // tm, n // tn),
        in_specs=[pl.BlockSpec((1, tm, d), lambda b, i, j: (b, i, 0)),
                  pl.BlockSpec((1, 1, d), lambda b, i, j: (b, 0, 0)),
                  pl.BlockSpec((1, 1, d), lambda b, i, j: (b, 0, 0)),
                  pl.BlockSpec((d, tn), lambda b, i, j: (0, j))],
        out_specs=pl.BlockSpec((1, tm, tn), lambda b, i, j: (b, i, j)),
        out_shape=jax.ShapeDtypeStruct((bsz, t, n), F32),
        scratch_shapes=[pltpu.VMEM((tm, d), BF16)],
        compiler_params=pltpu.CompilerParams(dimension_semantics=("arbitrary", "arbitrary", "arbitrary"),
                                             vmem_limit_bytes=V7X_VMEM_LIMIT_BYTES),
        name="inproj",
    )(x, sc, sh, w)


def _outproj_kernel(*refs, n_a):
    a_refs, (w_ref, x_ref, gt_ref, g_ref, b_ref, o_ref) = refs[:n_a], refs[n_a:]
    y = None
    k0 = 0
    for a_ref in a_refs:
        kw = a_ref.shape[2]
        term = jnp.dot(a_ref[0].astype(BF16), w_ref[k0:k0 + kw, :], preferred_element_type=F32)
        y = term if y is None else y + term
        k0 += kw
    z = DEEPNORM_ALPHA * x_ref[0] + gt_ref[0] * y
    mu = jnp.mean(z, axis=1, keepdims=True)
    zc = z - mu
    var = jnp.mean(zc * zc, axis=1, keepdims=True)
    o_ref[0] = zc * lax.rsqrt(var + LN_EPS) * g_ref[...] + b_ref[...]


def outproj_ln(a_list, w, x, gt, ln_g, ln_b, tm=256):
    bsz, t, d = x.shape
    tm = min(tm, t)
    in_specs = [pl.BlockSpec((1, tm, a.shape[2]), lambda b, i: (b, i, 0)) for a in a_list] + [
        pl.BlockSpec(w.shape, lambda b, i: (0, 0)),
        pl.BlockSpec((1, tm, d), lambda b, i: (b, i, 0)),
        pl.BlockSpec((1, 1, d), lambda b, i: (b, 0, 0)),
        pl.BlockSpec((1, d), lambda b, i: (0, 0)),
        pl.BlockSpec((1, d), lambda b, i: (0, 0))]
    return pl.pallas_call(
        functools.partial(_outproj_kernel, n_a=len(a_list)),
        grid=(bsz, t // tm),
        in_specs=in_specs,
        out_specs=pl.BlockSpec((1, tm, d), lambda b, i: (b, i, 0)),
        out_shape=jax.ShapeDtypeStruct((bsz, t, d), F32),
        compiler_params=pltpu.CompilerParams(dimension_semantics=("arbitrary", "arbitrary"),
                                             vmem_limit_bytes=V7X_VMEM_LIMIT_BYTES),
        name="outproj_ln",
    )(*a_list, w, x, gt, ln_g.reshape(1, d), ln_b.reshape(1, d))


def _bdot(a, b):
    return jnp.dot(a.astype(BF16), b.astype(BF16), preferred_element_type=F32)


def _bdot_nt(a, b):
    return lax.dot_general(a.astype(BF16), b.astype(BF16), (((1,), (1,)), ((), ())), preferred_element_type=F32)


def _bdot_tn(a, b):
    return lax.dot_general(a.astype(BF16), b.astype(BF16), (((0,), (0,)), ((), ())), preferred_element_type=F32)


(PV_MR0, PV_MR1, PV_MK0, PV_MK1, PV_MV0, PV_MV1, PV_MG0, PV_MG1, PV_W00, PV_W01, PV_A00, PV_A01,
 PV_KK, PV_KA, PV_RK, PV_GNG, PV_GNB) = range(17)
PV_ROWS = 24
MK_ABD, MK_P, MK_Y, MK_L1 = 0, 1, 2, 3
MK_LV = 4
N_LV = 5
MK_PER_DIR = MK_LV + N_LV
MK_EYE = 2 * MK_PER_DIR
MK_BD = MK_EYE + 1
N_MASKS = MK_BD + 1


def rwkv_masks():
    ri = np.arange(128)[:, None]
    ci = np.arange(128)[None, :]
    i, j = ri & 63, ci & 63
    same = (ri < 64) == (ci < 64)
    out = np.zeros((N_MASKS, 128, 128), np.float32)
    for d in range(2):
        strict = (j < i) if d == 0 else (j > i)
        incl = (j <= i) if d == 0 else (j >= i)
        base = d * MK_PER_DIR
        out[base + MK_ABD] = strict & same
        out[base + MK_P] = strict & (ci >= 64)
        out[base + MK_Y] = incl
        for li, s in enumerate((1, 2, 4, 8, 16, 32)):
            blk = (i // (2 * s)) == (j // (2 * s))
            if d == 0:
                m = blk & ((i & s) != 0) & ((j & s) == 0)
            else:
                m = blk & ((i & s) == 0) & ((j & s) != 0)
            out[base + (MK_L1 if li == 0 else MK_LV + li - 1)] = m & same
    out[MK_EYE] = ri == ci
    out[MK_BD] = same
    return out


def _seg_sum(x, lo):
    s0 = jnp.sum(jnp.where(lo, x, 0.0), axis=1, keepdims=True)
    s1 = jnp.sum(jnp.where(lo, 0.0, x), axis=1, keepdims=True)
    return jnp.where(lo, s0, s1)


def _chunk_cumsum(x, reverse, chunk):
    n = x.shape[0]
    rowm = lax.broadcasted_iota(jnp.int32, x.shape, 0) & (chunk - 1)
    s = 1
    while s < chunk:
        if not reverse:
            x = x + jnp.where(rowm >= s, pltpu.roll(x, s, axis=0), 0.0)
        else:
            x = x + jnp.where(rowm < chunk - s, pltpu.roll(x, n - s, axis=0), 0.0)
        s *= 2
    return x


def _rwkv_kernel(lr, lk, lv, lg, ll, cr, ck, cv, cg, cl, pv_ref, mixl_ref, wup_ref, aup_ref, mk_ref,
                 o_lat, o_ctx,
                 s_at0, s_at1, s_rt0, s_rt1, s_bt0, s_bt1, s_kt0, s_kt1, s_bh0, s_bh1, s_kh0, s_kh1, s_v,
                 s_g, s_bonus, s_yf, s_yb, s_t, s_p, s_gr, s_et, s_state, *, t_lat, t_ctx, p1_chunks):
    C = RWKV_CHUNK
    s_at, s_rt, s_bt, s_kt, s_bh, s_kh = ((s_at0, s_at1), (s_rt0, s_rt1), (s_bt0, s_bt1), (s_kt0, s_kt1),
                                          (s_bh0, s_bh1), (s_kh0, s_kh1))
    s_y = (s_yf, s_yb)
    lo64 = lax.broadcasted_iota(jnp.int32, (C, LANES), 1) < HALF
    bd_ones = mk_ref[MK_BD]

    def pvrow(i):
        return pv_ref[pl.ds(i, 1), :]

    def prep(xr, xk, xv, xg, xl, tseq):
        R = min(256, tseq)
        lo = lax.broadcasted_iota(jnp.int32, (R, LANES), 1) < HALF

        def tile(it, carry):
            s = pl.multiple_of(it * R, R)
            rows = lax.broadcasted_iota(jnp.int32, (R, LANES), 0)
            has_prev = jnp.where(s > 0, 1.0, 0.0).astype(F32)
            has_next = jnp.where(s + R < tseq, 1.0, 0.0).astype(F32)
            pidx = jnp.maximum(s - 1, 0)
            nidx = jnp.minimum(s + R, tseq - 1)

            def shifted(ref, m0, m1):
                cur = ref[0, pl.ds(s, R), :]
                prow = ref[0, pl.ds(pidx, 1), :] * has_prev
                nrow = ref[0, pl.ds(nidx, 1), :] * has_next
                prev = jnp.where(rows == 0, prow, pltpu.roll(cur, 1, axis=0))
                nxt = jnp.where(rows == R - 1, nrow, pltpu.roll(cur, R - 1, axis=0))
                return cur + m0 * (prev - cur) + m1 * (nxt - cur)

            r = shifted(xr, pvrow(PV_MR0), pvrow(PV_MR1))
            k = shifted(xk, pvrow(PV_MK0), pvrow(PV_MK1))
            v = shifted(xv, pvrow(PV_MV0), pvrow(PV_MV1))
            g = shifted(xg, pvrow(PV_MG0), pvrow(PV_MG1))
            lor = shifted(xl, mixl_ref[0:1, :], mixl_ref[1:2, :])
            th = jnp.tanh(lor)
            kkf = k * pvrow(PV_KK)
            kk = kkf * lax.rsqrt(_seg_sum(kkf * kkf, lo) + 1e-6)
            ka = pvrow(PV_KA)
            kd_sum = None
            for d in range(2):
                w_pre = pvrow(PV_W00 + d) + _bdot(th, wup_ref[d])
                lw = -jnp.exp(-jax.nn.softplus(-w_pre) - 0.5)
                ag = jax.nn.sigmoid(pvrow(PV_A00 + d) + _bdot(lor, aup_ref[d]))
                kd = k * (1.0 + (ag - 1.0) * ka)
                bdv = kk * ag
                kd_sum = kd if kd_sum is None else kd_sum + kd
                c_fwd = _chunk_cumsum(lw, False, C)
                c_bwd = _chunk_cumsum(lw, True, C)
                cw, rest = (c_fwd, c_bwd - lw) if d == 0 else (c_bwd, c_fwd - lw)
                e_in = jnp.exp(-cw)
                e_out = jnp.exp(rest)
                s_at[d][pl.ds(s, R), :] = (-kk * jnp.exp(cw - lw)).astype(BF16)
                s_rt[d][pl.ds(s, R), :] = (r * jnp.exp(cw)).astype(BF16)
                s_bt[d][pl.ds(s, R), :] = (bdv * e_in).astype(BF16)
                s_kt[d][pl.ds(s, R), :] = (kd * e_in).astype(BF16)
                s_bh[d][pl.ds(s, R), :] = (bdv * e_out).astype(BF16)
                s_kh[d][pl.ds(s, R), :] = (kd * e_out).astype(BF16)
                for c in range(R // C):
                    last = c * C + (C - 1 if d == 0 else 0)
                    s_et[d, pl.ds(it * (R // C) + c, 1), :] = jnp.exp(cw[last:last + 1, :])
            s_v[pl.ds(s, R), :] = v.astype(BF16)
            s_g[pl.ds(s, R), :] = g
            s_bonus[pl.ds(s, R), :] = _seg_sum(r * kd_sum * pvrow(PV_RK), lo) * v
            return carry

        lax.fori_loop(0, tseq // R, tile, 0)

    def build(n, d):
        base = d * MK_PER_DIR
        rows = pl.ds(pl.multiple_of(n * C, C), C)
        at, rt = s_at[d][rows, :], s_rt[d][rows, :]
        zero = jnp.zeros_like(at)
        l1 = jnp.concatenate([jnp.where(lo64, at, zero), jnp.where(lo64, zero, at),
                              jnp.where(lo64, rt, zero), jnp.where(lo64, zero, rt)], axis=0)
        gmat = _bdot_nt(l1, jnp.concatenate([s_bt[d][rows, :], s_kt[d][rows, :]], axis=0))
        ga = gmat[0:2 * C]
        v = s_v[rows, :]
        s_p[d, n] = _bdot(ga * mk_ref[base + MK_P], jnp.concatenate([v, v], axis=0)).astype(BF16)
        s_gr[d, n] = (gmat[2 * C:4 * C] * mk_ref[base + MK_Y]).astype(BF16)
        ga_sw = jnp.concatenate([ga[0:C], pltpu.roll(ga[C:2 * C], HALF, axis=1)], axis=0)
        return ga_sw * mk_ref[base + MK_ABD]

    def invert(a_list, d_list):
        xs = [mk_ref[MK_EYE] + a * mk_ref[d * MK_PER_DIR + MK_L1] for a, d in zip(a_list, d_list)]
        for li in range(N_LV):
            ams = [a * mk_ref[d * MK_PER_DIR + MK_LV + li] for a, d in zip(a_list, d_list)]
            t1 = [_bdot(x, am) for x, am in zip(xs, ams)]
            xs = [x + _bdot(t, x) for x, t in zip(xs, t1)]
        return xs

    def phase1(tseq):
        nch = tseq // C
        per = min(p1_chunks, nch)

        def body(it, carry):
            probs = [(it * per + j, d) for j in range(per) for d in range(2)]
            a_list = [build(n, d) for n, d in probs]
            xs = invert(a_list, [d for _, d in probs])
            for (n, d), x in zip(probs, xs):
                s_t[d, n] = x.astype(BF16)
            return carry

        lax.fori_loop(0, nch // per, body, 0)

    def phase2(tseq):
        nch = tseq // C

        def body(it, carry):
            ns = (it, nch - 1 - it)
            rows = [pl.ds(pl.multiple_of(ns[d] * C, C), C) for d in range(2)]
            st = [s_state[d] for d in range(2)]
            xs = [_bdot_nt(jnp.concatenate([s_at[d][rows[d], :], s_rt[d][rows[d], :]], axis=0), st[d]) for d in range(2)]
            rhs = [jnp.concatenate([xs[d][0:C], xs[d][0:C]], axis=0) + s_p[d, ns[d]].astype(F32) for d in range(2)]
            ub = [_bdot(s_t[d, ns[d]], rhs[d]) for d in range(2)]
            uv = [jnp.concatenate([jnp.where(lo64, ub[d][0:C], ub[d][C:2 * C]).astype(BF16), s_v[rows[d], :]], axis=0)
                  for d in range(2)]
            yb = [_bdot(s_gr[d, ns[d]], uv[d]) for d in range(2)]
            for d in range(2):
                s_y[d][rows[d], :] = xs[d][C:2 * C] + jnp.where(lo64, yb[d][0:C], yb[d][C:2 * C])
                bk = jnp.concatenate([s_bh[d][rows[d], :], s_kh[d][rows[d], :]], axis=0)
                s_state[d] = st[d] * s_et[d, pl.ds(ns[d], 1), :] + bd_ones * _bdot_tn(uv[d], bk)
            return carry

        lax.fori_loop(0, nch, body, 0)

    def epilogue(o_ref, tseq):
        R = min(256, tseq)
        lo = lax.broadcasted_iota(jnp.int32, (R, LANES), 1) < HALF

        def tile(it, carry):
            s = pl.multiple_of(it * R, R)
            rows = pl.ds(s, R)
            y = s_yf[rows, :] + s_yb[rows, :]
            mu = _seg_sum(y, lo) * (1.0 / HALF)
            yc = y - mu
            var = _seg_sum(yc * yc, lo) * (1.0 / HALF)
            yn = yc * lax.rsqrt(var + RWKV_GN_EPS) * pvrow(PV_GNG) + pvrow(PV_GNB)
            g = s_g[rows, :]
            o_ref[0, rows, :] = (yn + s_bonus[rows, :]) * (g * jax.nn.sigmoid(g))
            return carry

        lax.fori_loop(0, tseq // R, tile, 0)

    s_state[...] = jnp.zeros_like(s_state)
    for (xr, xk, xv, xg, xl, o_ref, tseq) in ((cr, ck, cv, cg, cl, o_ctx, t_ctx), (lr, lk, lv, lg, ll, o_lat, t_lat)):
        prep(xr, xk, xv, xg, xl, tseq)
        phase1(tseq)
        phase2(tseq)
        epilogue(o_ref, tseq)


def _col_block_map(b, p, *, off):
    return (b, 0, off + p)


def rwkv_mixer(u_lat, u_ctx, pv, mixl, wup, aup, p1_chunks=8):
    bsz, t_lat, _ = u_lat.shape
    t_ctx = u_ctx.shape[1]
    npair = RWKV_HEADS // 2
    c0 = EV_RWKV_COL // LANES
    nch = t_lat // RWKV_CHUNK
    masks = jnp.asarray(rwkv_masks())

    def seq_specs(t):
        return [pl.BlockSpec((1, t, LANES), functools.partial(_col_block_map, off=c0 + o * npair)) for o in range(4)] + [
            pl.BlockSpec((1, t, LANES), lambda b, p: (b, 0, EV_LORA_COL // LANES))]

    in_specs = seq_specs(t_lat) + seq_specs(t_ctx) + [
        pl.BlockSpec((PV_ROWS, LANES), lambda b, p: (0, p)),
        pl.BlockSpec((8, LANES), lambda b, p: (0, 0)),
        pl.BlockSpec((2, LANES, LANES), lambda b, p: (0, 0, p)),
        pl.BlockSpec((2, LANES, LANES), lambda b, p: (0, 0, p)),
        pl.BlockSpec((N_MASKS, LANES, LANES), lambda b, p: (0, 0, 0)),
    ]
    out_specs = [pl.BlockSpec((1, t_lat, LANES), lambda b, p: (b, 0, p)),
                 pl.BlockSpec((1, t_ctx, LANES), lambda b, p: (b, 0, p))]
    scratch = ([pltpu.VMEM((t_lat, LANES), BF16) for _ in range(13)] + [pltpu.VMEM((t_lat, LANES), F32) for _ in range(4)]
               + [pltpu.VMEM((2, nch, LANES, LANES), BF16) for _ in range(3)]
               + [pltpu.VMEM((2, nch, LANES), F32), pltpu.VMEM((2, LANES, LANES), F32)])
    kern = functools.partial(_rwkv_kernel, t_lat=t_lat, t_ctx=t_ctx, p1_chunks=p1_chunks)
    return pl.pallas_call(
        kern,
        grid=(bsz, npair),
        in_specs=in_specs,
        out_specs=out_specs,
        out_shape=[jax.ShapeDtypeStruct((bsz, t_lat, RWKV_WIDTH), F32), jax.ShapeDtypeStruct((bsz, t_ctx, RWKV_WIDTH), F32)],
        scratch_shapes=scratch,
        compiler_params=pltpu.CompilerParams(dimension_semantics=("arbitrary", "arbitrary"),
                                             vmem_limit_bytes=V7X_VMEM_LIMIT_BYTES),
        name="rwkv7_chunked",
    )(*([u_lat] * 5), *([u_ctx] * 5), pv, mixl, wup, aup, masks)


def rwkv_params(mix, w0, w_up, a0, a_up, k_k, k_a, r_k, gn_g, gn_b):
    wd = RWKV_WIDTH
    rows = [mix[0, 0:wd], mix[1, 0:wd], mix[0, wd:2 * wd], mix[1, wd:2 * wd], mix[0, 2 * wd:3 * wd], mix[1, 2 * wd:3 * wd],
            mix[0, 3 * wd:4 * wd], mix[1, 3 * wd:4 * wd], w0[0], w0[1], a0[0], a0[1], k_k, k_a, r_k.reshape(-1), gn_g, gn_b]
    pv = jnp.concatenate([jnp.stack(rows), jnp.zeros((PV_ROWS - len(rows), wd), F32)], 0)
    mixl = jnp.concatenate([mix[:, 4 * wd:], jnp.zeros((6, LANES), F32)], 0)
    z = jnp.zeros((2, HALF, wd), F32)
    wup = jnp.concatenate([w_up, z], 1)
    aup = jnp.concatenate([z, a_up], 1)
    return pv, mixl, wup, aup


GP_CONV_Q, GP_CONV_K, GP_CONV_V = 0, 5, 10
GP_NORM, GP_ALOG0, GP_ALOG1, GP_DTB0, GP_DTB1 = 15, 16, 17, 18, 19
GP_ROWS = 24
GM_STRICT, GM_INCL, GM_L1, GM_LV = 0, 1, 2, 3
GN_LV = 5
GM_PER_DIR = GM_LV + GN_LV
GM_EYE = 2 * GM_PER_DIR
GN_MASKS = GM_EYE + 1


def gdn_masks():
    ri = np.arange(128)[:, None]
    ci = np.arange(128)[None, :]
    i, j = ri & 63, ci & 63
    same = (ri < 64) == (ci < 64)
    out = np.zeros((GN_MASKS, 128, 128), np.float32)
    for d in range(2):
        base = d * GM_PER_DIR
        out[base + GM_STRICT] = ((j < i) if d == 0 else (j > i)) & same
        out[base + GM_INCL] = ((j <= i) if d == 0 else (j >= i)) & same
        for li, s in enumerate((1, 2, 4, 8, 16, 32)):
            blk = (i // (2 * s)) == (j // (2 * s))
            m = blk & (((i & s) != 0) & ((j & s) == 0) if d == 0 else ((i & s) == 0) & ((j & s) != 0))
            out[base + (GM_L1 if li == 0 else GM_LV + li - 1)] = m & same
    out[GM_EYE] = ri == ci
    return out


def _gdn_kernel(lq, lk, lv, lz, lbg, cq, ck, cv, cz, cbg, gp_ref, mk_ref, o_lat, o_ctx,
                s_q, s_k, s_kb0, s_kb1, s_kbg0, s_kbg1, s_qg0, s_qg1, s_kg0, s_kg1, s_vb0, s_vb1,
                s_gc0, s_gc1, s_of, s_ob, s_t, s_a, s_gl, s_state, *, t_lat, t_ctx):
    C = GDN_CHUNK
    s_kb, s_kbg, s_qg, s_kg, s_vb = (s_kb0, s_kb1), (s_kbg0, s_kbg1), (s_qg0, s_qg1), (s_kg0, s_kg1), (s_vb0, s_vb1)
    s_gc, s_o = (s_gc0, s_gc1), (s_of, s_ob)

    def gprow(i):
        return gp_ref[pl.ds(i, 1), :]

    def prep(xq, xk, xv, xbg, tseq):
        R = min(256, tseq)
        N = R + 16

        def tile(it, carry):
            s = pl.multiple_of(it * R, R)
            has_prev = jnp.where(s > 0, 1.0, 0.0).astype(F32)
            has_next = jnp.where(s + R < tseq, 1.0, 0.0).astype(F32)
            pidx = pl.multiple_of(jnp.maximum(s - 8, 0), 8)
            nidx = pl.multiple_of(jnp.minimum(s + R, tseq - 8), 8)

            def conv_silu(ref, row0):
                xe = jnp.concatenate([ref[0, pl.ds(pidx, 8), :] * has_prev, ref[0, pl.ds(s, R), :],
                                      ref[0, pl.ds(nidx, 8), :] * has_next], axis=0)
                acc = None
                for j in range(5):
                    sh = (N - (j - 2)) % N
                    xs = xe if sh == 0 else pltpu.roll(xe, sh, axis=0)
                    term = gprow(row0 + j) * xs[8:8 + R]
                    acc = term if acc is None else acc + term
                return acc * jax.nn.sigmoid(acc)

            qc = conv_silu(xq, GP_CONV_Q)
            kc = conv_silu(xk, GP_CONV_K)
            v = conv_silu(xv, GP_CONV_V)
            q = qc * (lax.rsqrt(jnp.sum(qc * qc, axis=1, keepdims=True) + 1e-6) * (LANES ** -0.5))
            k = kc * lax.rsqrt(jnp.sum(kc * kc, axis=1, keepdims=True) + 1e-6)
            bg = xbg[0, 0, pl.ds(s, R), :]
            s_q[pl.ds(s, R), :] = q.astype(BF16)
            s_k[pl.ds(s, R), :] = k.astype(BF16)
            for d in range(2):
                beta = jax.nn.sigmoid(bg[:, d:d + 1])
                g = -jnp.exp(gprow(GP_ALOG0 + d)) * jax.nn.softplus(bg[:, 2 + d:3 + d] + gprow(GP_DTB0 + d))
                c_fwd = _chunk_cumsum(g, False, C)
                c_bwd = _chunk_cumsum(g, True, C)
                gc, rest = (c_fwd, c_bwd - g) if d == 0 else (c_bwd, c_fwd - g)
                eg = jnp.exp(gc)
                kb = k * beta
                s_kb[d][pl.ds(s, R), :] = kb.astype(BF16)
                s_kbg[d][pl.ds(s, R), :] = (kb * eg).astype(BF16)
                s_qg[d][pl.ds(s, R), :] = (q * eg).astype(BF16)
                s_kg[d][pl.ds(s, R), :] = (k * jnp.exp(rest)).astype(BF16)
                s_vb[d][pl.ds(s, R), :] = (v * beta).astype(BF16)
                s_gc[d][pl.ds(s, R), :] = gc
                for c in range(R // C):
                    last = c * C + (C - 1 if d == 0 else 0)
                    s_gl[d, pl.ds(it * (R // C) + c, 1), :] = eg[last:last + 1, :]
            return carry

        lax.fori_loop(0, tseq // R, tile, 0)

    def build_pair(m, d):
        base = d * GM_PER_DIR
        rows = pl.ds(pl.multiple_of(m * 2 * C, 2 * C), 2 * C)
        gmat = _bdot_nt(jnp.concatenate([s_kb[d][rows, :], s_q[rows, :]], axis=0), s_k[rows, :])
        gc = s_gc[d][rows, :]
        m_incl = mk_ref[base + GM_INCL]
        dec = jnp.exp((gc - gc.T) * m_incl)
        a = -(gmat[0:2 * C] * dec * mk_ref[base + GM_STRICT])
        s_a[d, m] = (gmat[2 * C:4 * C] * dec * m_incl).astype(BF16)
        return a

    def invert(a_list, d_list):
        xs = [mk_ref[GM_EYE] + a * mk_ref[d * GM_PER_DIR + GM_L1] for a, d in zip(a_list, d_list)]
        for li in range(GN_LV):
            ams = [a * mk_ref[d * GM_PER_DIR + GM_LV + li] for a, d in zip(a_list, d_list)]
            t1 = [_bdot(x, am) for x, am in zip(xs, ams)]
            xs = [x + _bdot(t, x) for x, t in zip(xs, t1)]
        return xs

    def phase1(tseq):
        npair = tseq // (2 * C)

        def body(it, carry):
            probs = [(it * 2 + j, d) for j in range(2) for d in range(2)] if npair >= 2 else [(it, d) for d in range(2)]
            a_list = [build_pair(m, d) for m, d in probs]
            xs = invert(a_list, [d for _, d in probs])
            for (m, d), x in zip(probs, xs):
                s_t[d, m] = x.astype(BF16)
            return carry

        lax.fori_loop(0, max(npair // 2, 1), body, 0)

    def phase2(tseq):
        npair = tseq // (2 * C)
        z64 = jnp.zeros((C, LANES), F32)

        def body(it, carry):
            pm = (it, npair - 1 - it)
            for step in range(2):
                half = (step, 1 - step)
                r0 = [pl.multiple_of(pm[d] * 2 * C + half[d] * C, C) for d in range(2)]
                rows = [pl.ds(r0[d], C) for d in range(2)]
                st = [s_state[d] for d in range(2)]
                zq = [_bdot(jnp.concatenate([s_kbg[d][rows[d], :], s_qg[d][rows[d], :]], axis=0), st[d])
                      for d in range(2)]
                rhs = [s_vb[d][rows[d], :].astype(F32) - zq[d][0:C] for d in range(2)]
                pad = lambda x, h: jnp.concatenate([x, z64] if h == 0 else [z64, x], axis=0)
                vnew = [_bdot(s_t[d, pm[d], half[d] * C:(half[d] + 1) * C, :], pad(rhs[d], half[d])) for d in range(2)]
                oo = [zq[d][C:2 * C] + _bdot(s_a[d, pm[d], half[d] * C:(half[d] + 1) * C, :], pad(vnew[d], half[d]))
                      for d in range(2)]
                for d in range(2):
                    s_o[d][rows[d], :] = oo[d]
                    n = pm[d] * 2 + half[d]
                    s_state[d] = st[d] * s_gl[d, pl.ds(n, 1), :] + _bdot_tn(s_kg[d][rows[d], :], vnew[d])
            return carry

        lax.fori_loop(0, npair, body, 0)

    def epilogue(o_ref, z_ref, tseq):
        R = min(256, tseq)

        def tile(it, carry):
            s = pl.multiple_of(it * R, R)
            rows = pl.ds(s, R)
            o = s_of[rows, :] + s_ob[rows, :]
            o = o * lax.rsqrt(jnp.mean(o * o, axis=1, keepdims=True) + 1e-6) * gprow(GP_NORM)
            z = z_ref[0, rows, :]
            o_ref[0, rows, :] = o * (z * jax.nn.sigmoid(z))
            return carry

        lax.fori_loop(0, tseq // R, tile, 0)

    s_state[...] = jnp.zeros_like(s_state)
    for (xq, xk, xv, xz, xbg, o_ref, tseq) in ((cq, ck, cv, cz, cbg, o_ctx, t_ctx), (lq, lk, lv, lz, lbg, o_lat, t_lat)):
        prep(xq, xk, xv, xbg, tseq)
        phase1(tseq)
        phase2(tseq)
        epilogue(o_ref, xz, tseq)


def gdn_mixer(u_lat, bg_lat, u_ctx, bg_ctx, gp):
    bsz, t_lat, _ = u_lat.shape
    t_ctx = u_ctx.shape[1]
    nh = GDN_HEADS
    masks = jnp.asarray(gdn_masks())

    def seq_specs(t):
        return [pl.BlockSpec((1, t, LANES), functools.partial(_col_block_map, off=o * nh)) for o in range(4)] + [
            pl.BlockSpec((1, 1, t, 4), lambda b, h: (b, h, 0, 0))]

    in_specs = seq_specs(t_lat) + seq_specs(t_ctx) + [
        pl.BlockSpec((GP_ROWS, LANES), lambda b, h: (0, h)),
        pl.BlockSpec((GN_MASKS, LANES, LANES), lambda b, h: (0, 0, 0)),
    ]
    out_specs = [pl.BlockSpec((1, t_lat, LANES), lambda b, h: (b, 0, h)),
                 pl.BlockSpec((1, t_ctx, LANES), lambda b, h: (b, 0, h))]
    npair = t_lat // (2 * GDN_CHUNK)
    scratch = ([pltpu.VMEM((t_lat, LANES), BF16) for _ in range(12)] + [pltpu.VMEM((t_lat, LANES), F32) for _ in range(4)]
               + [pltpu.VMEM((2, npair, LANES, LANES), BF16), pltpu.VMEM((2, npair, LANES, LANES), BF16),
                  pltpu.VMEM((2, 2 * npair, LANES), F32), pltpu.VMEM((2, LANES, LANES), F32)])
    kern = functools.partial(_gdn_kernel, t_lat=t_lat, t_ctx=t_ctx)
    return pl.pallas_call(
        kern,
        grid=(bsz, nh),
        in_specs=in_specs,
        out_specs=out_specs,
        out_shape=[jax.ShapeDtypeStruct((bsz, t_lat, GDN_WIDTH), F32), jax.ShapeDtypeStruct((bsz, t_ctx, GDN_WIDTH), F32)],
        scratch_shapes=scratch,
        compiler_params=pltpu.CompilerParams(dimension_semantics=("arbitrary", "arbitrary"),
                                             vmem_limit_bytes=V7X_VMEM_LIMIT_BYTES),
        name="gdn_chunked",
    )(*([u_lat] * 4), bg_lat, *([u_ctx] * 4), bg_ctx, gp, masks)


def gdn_params(conv_w, a_log, dt_bias, norm_g):
    wd = GDN_WIDTH
    rep = lambda x: jnp.repeat(x, LANES)
    rows = [conv_w[j, o * wd:(o + 1) * wd] for o in range(3) for j in range(5)]
    rows += [jnp.tile(norm_g, GDN_HEADS), rep(a_log[0]), rep(a_log[1]), rep(dt_bias[0]), rep(dt_bias[1])]
    return jnp.concatenate([jnp.stack(rows), jnp.zeros((GP_ROWS - len(rows), wd), F32)], 0)


def gdn_bg(u_small):
    bsz, t, _ = u_small.shape
    x = u_small.reshape(bsz, t, 2, 2, GDN_HEADS)
    return jnp.transpose(x, (0, 4, 1, 2, 3)).reshape(bsz, GDN_HEADS, t, 4)


OD_Q_COL, OD_GATE_COL, OD_KV_COL = 0, ATT_WIDTH, 2 * ATT_WIDTH
OD_KV_W = 4 * ATT_HEAD
OD_COLS = OD_KV_COL + ATT_KV_HEADS * OD_KV_W


def rope_lane_tables(t):
    pos = np.arange(t)
    inv = ROPE_BASE ** (-np.arange(ROPE_FREQS, dtype=np.float64) / ROPE_FREQS)
    ang_row = (pos // GRID_W)[:, None] * inv
    ang_col = (pos % GRID_W)[:, None] * inv
    ang = np.concatenate([ang_row, ang_row, ang_col, ang_col] * 2, axis=1)
    first = (np.arange(LANES) % (2 * ROPE_FREQS)) < ROPE_FREQS
    cos, sin = np.cos(ang), np.sin(ang)
    return (jnp.asarray(cos, F32), jnp.asarray(np.where(first, -sin, 0.0), F32), jnp.asarray(np.where(first, 0.0, sin), F32))


def _rope(x, cos, sina, sinb):
    n = x.shape[1]
    return x * cos + pltpu.roll(x, n - ROPE_FREQS, axis=1) * sina + pltpu.roll(x, ROPE_FREQS, axis=1) * sinb


def _kv_rope_kernel(u_ref, cos_ref, sina_ref, sinb_ref, o_ref):
    x = u_ref[0]
    cos, sina, sinb = cos_ref[...], sina_ref[...], sinb_ref[...]
    parts = []
    for h in range(ATT_KV_HEADS):
        k2 = x[:, h * OD_KV_W:h * OD_KV_W + LANES]
        parts += [_rope(k2, cos, sina, sinb), x[:, h * OD_KV_W + LANES:(h + 1) * OD_KV_W]]
    o_ref[0] = jnp.concatenate(parts, axis=1).astype(BF16)


def kv_rope(u, tables, tr=512):
    bsz, t, _ = u.shape
    tr = min(tr, t)
    kvw = ATT_KV_HEADS * OD_KV_W
    tab = pl.BlockSpec((tr, LANES), lambda b, i: (i, 0))
    return pl.pallas_call(
        _kv_rope_kernel,
        grid=(bsz, t // tr),
        in_specs=[pl.BlockSpec((1, tr, kvw), lambda b, i: (b, i, OD_KV_COL // kvw)), tab, tab, tab],
        out_specs=pl.BlockSpec((1, tr, kvw), lambda b, i: (b, i, 0)),
        out_shape=jax.ShapeDtypeStruct((bsz, t, kvw), BF16),
        compiler_params=pltpu.CompilerParams(dimension_semantics=("arbitrary", "arbitrary")),
        name="kv_rope",
    )(u, *tables)


def _attn_kernel(sink_ref, *refs, has_local, nb):
    if has_local:
        q_ref, g_ref, kvp_ref, kvc_ref, kvn_ref, cos_ref, sina_ref, sinb_ref, kvx_ref, o_ref = refs
    else:
        q_ref, g_ref, kvx_ref, o_ref = refs
    tq = q_ref.shape[1]
    i = pl.program_id(1)
    q = q_ref[0]
    if has_local:
        rep = ATT_WIDTH // LANES
        cos, sina, sinb = (jnp.concatenate([r[...]] * rep, axis=1) for r in (cos_ref, sina_ref, sinb_ref))
        q = _rope(q, cos, sina, sinb)
    q = q * (ATT_HEAD ** -0.5)
    lo = lax.broadcasted_iota(jnp.int32, (tq, LANES), 1) < HALF
    if has_local:
        a = lax.broadcasted_iota(jnp.int32, (tq, WINDOW), 0)
        c = lax.broadcasted_iota(jnp.int32, (tq, WINDOW), 1)
        ok_prev = (c >= a) & (i > 0)
        ok_next = (c <= a) & (i < nb - 1)
        always = jnp.ones((tq, tq + kvx_ref.shape[1]), jnp.bool_)
        valid = jnp.concatenate([ok_prev, always[:, :tq], ok_next, always[:, tq:]], axis=1)
    outs = []
    for hk in range(ATT_KV_HEADS):
        kcol = slice(hk * OD_KV_W, hk * OD_KV_W + LANES)
        vcol = slice(hk * OD_KV_W + LANES, (hk + 1) * OD_KV_W)
        qs = []
        for pair in range(ATT_GROUP // 2):
            grp = q[:, (hk * (ATT_GROUP // 2) + pair) * LANES:(hk * (ATT_GROUP // 2) + pair + 1) * LANES]
            qs += [jnp.where(lo, grp, 0.0), jnp.where(lo, 0.0, grp)]
        lhs = jnp.concatenate(qs, axis=0).astype(BF16)
        kx, vx = kvx_ref[0, :, kcol].astype(BF16), kvx_ref[0, :, vcol].astype(BF16)
        if has_local:
            kall = jnp.concatenate([kvp_ref[0, :, kcol], kvc_ref[0, :, kcol], kvn_ref[0, :, kcol], kx], axis=0)
            vall = jnp.concatenate([kvp_ref[0, :, vcol], kvc_ref[0, :, vcol], kvn_ref[0, :, vcol], vx], axis=0)
        else:
            kall, vall = kx, vx
        s = _bdot_nt(lhs, kall)
        lov = lax.broadcasted_iota(jnp.int32, vall.shape, 1) < HALF
        zv = jnp.zeros_like(vall)
        v2 = jnp.concatenate([jnp.where(lov, vall, zv), jnp.where(lov, zv, vall)], axis=0)
        for pair in range(ATT_GROUP // 2):
            ps, inv_l = [], []
            for sub in range(2):
                g = 2 * pair + sub
                sg = s[g * tq:(g + 1) * tq]
                if has_local:
                    sg = jnp.where(valid, sg, NEG_INF)
                sink = sink_ref[hk, g]
                m = jnp.maximum(jnp.max(sg, axis=1, keepdims=True), sink)
                p = jnp.exp(sg - m)
                inv_l.append(1.0 / (jnp.sum(p, axis=1, keepdims=True) + jnp.exp(sink - m)))
                ps.append(p.astype(BF16))
            o_pair = jnp.dot(jnp.concatenate(ps, axis=1), v2, preferred_element_type=F32)
            outs.append(o_pair * jnp.where(lo, inv_l[0], inv_l[1]))
    g = g_ref[0]
    o_ref[0] = jnp.concatenate(outs, axis=1) * (g * jax.nn.sigmoid(g))


def sink_attention(u, kv_local, tables, u_ctx, sink):
    bsz, t, _ = u.shape
    tq = ATT_BLOCK
    nb = t // tq
    tx = u_ctx.shape[1]
    kvw = ATT_KV_HEADS * OD_KV_W
    has_local = kv_local is not None
    in_specs = [pl.BlockSpec(memory_space=pltpu.SMEM),
                pl.BlockSpec((1, tq, ATT_WIDTH), lambda b, i: (b, i, OD_Q_COL // ATT_WIDTH)),
                pl.BlockSpec((1, tq, ATT_WIDTH), lambda b, i: (b, i, OD_GATE_COL // ATT_WIDTH))]
    args = [sink.reshape(ATT_KV_HEADS, ATT_GROUP), u, u]
    if has_local:
        in_specs += [pl.BlockSpec((1, tq, kvw), lambda b, i: (b, jnp.maximum(i - 1, 0), 0)),
                     pl.BlockSpec((1, tq, kvw), lambda b, i: (b, i, 0)),
                     pl.BlockSpec((1, tq, kvw), lambda b, i: (b, jnp.minimum(i + 1, nb - 1), 0))]
        in_specs += [pl.BlockSpec((tq, LANES), lambda b, i: (i, 0))] * 3
        args += [kv_local] * 3 + list(tables)
    in_specs += [pl.BlockSpec((1, tx, kvw), lambda b, i: (b, 0, OD_KV_COL // kvw))]
    args += [u_ctx]
    return pl.pallas_call(
        functools.partial(_attn_kernel, has_local=has_local, nb=nb),
        grid=(bsz, nb),
        in_specs=in_specs,
        out_specs=pl.BlockSpec((1, tq, ATT_WIDTH), lambda b, i: (b, i, 0)),
        out_shape=jax.ShapeDtypeStruct((bsz, t, ATT_WIDTH), F32),
        compiler_params=pltpu.CompilerParams(dimension_semantics=("arbitrary", "arbitrary"),
                                             vmem_limit_bytes=V7X_VMEM_LIMIT_BYTES),
        name="sink_attention",
    )(*args)


def pack_odd_w_in(w_in):
    nq, nkv = ATT_WIDTH, ATT_KV_HEADS * ATT_HEAD
    parts = [w_in[:, :nq], w_in[:, nq + 2 * nkv:]]
    for h in range(ATT_KV_HEADS):
        k = w_in[:, nq + h * ATT_HEAD:nq + (h + 1) * ATT_HEAD]
        v = w_in[:, nq + nkv + h * ATT_HEAD:nq + nkv + (h + 1) * ATT_HEAD]
        parts += [k, k, v, v]
    return jnp.concatenate(parts, 1).astype(BF16)


def pack_even_w_in(w_in):
    g1 = 4 * GDN_WIDTH
    r0 = GDN_IN
    parts = [w_in[:, :g1], w_in[:, r0:r0 + 4 * RWKV_WIDTH], w_in[:, r0 + 4 * RWKV_WIDTH:], w_in[:, g1:GDN_IN],
             jnp.zeros((D_MODEL, EV_COLS - EV_SMALL_COL - 4 * GDN_HEADS), F32)]
    return jnp.concatenate(parts, 1).astype(BF16)


def kernel(x, c, ctx, c_ctx, mod_w, mod_b, ln_g, ln_b, ev_w_in, ev_w_out, gdn_conv, gdn_a_log, gdn_dt_bias,
           gdn_norm_g, rwkv_mix, rwkv_w0, rwkv_w_up, rwkv_a0, rwkv_a_up, rwkv_k_k, rwkv_k_a, rwkv_r_k,
           rwkv_gn_g, rwkv_gn_b, od_w_in, od_w_out, od_sink):
    bsz, t_lat, d = x.shape
    t_ctx = ctx.shape[1]
    x_lat, x_ctx = x, ctx
    cc = jax.nn.silu(jnp.concatenate([c, c_ctx[None]], 0))
    for layer in range(DEPTH):
        need_ctx = layer < DEPTH - 1
        i = layer // 2
        mod = _small_matmul(cc, mod_w[layer]) + mod_b[layer]
        sh, sc, gt = (mod[:bsz, j * d:(j + 1) * d][:, None, :] for j in range(3))
        sh_c, sc_c, gt_c = (jnp.broadcast_to(mod[bsz, j * d:(j + 1) * d], (bsz, 1, d)) for j in range(3))
        if layer % 2 == 0:
            w_in = pack_even_w_in(ev_w_in[i])
            w_out = ev_w_out[i].astype(BF16)
            u_lat = inproj(x_lat, sc, sh, w_in, 1024, EV_TN)
            u_ctx = inproj(x_ctx, sc_c, sh_c, w_in, t_ctx, EV_TN)
            small = slice(EV_SMALL_COL, EV_SMALL_COL + 4 * GDN_HEADS)
            og_lat, og_ctx = gdn_mixer(u_lat, gdn_bg(u_lat[..., small]), u_ctx, gdn_bg(u_ctx[..., small]),
                                       gdn_params(gdn_conv[i], gdn_a_log[i], gdn_dt_bias[i], gdn_norm_g[i]))
            or_lat, or_ctx = rwkv_mixer(u_lat, u_ctx, *rwkv_params(
                rwkv_mix[i], rwkv_w0[i], rwkv_w_up[i], rwkv_a0[i], rwkv_a_up[i], rwkv_k_k[i], rwkv_k_a[i],
                rwkv_r_k[i], rwkv_gn_g[i], rwkv_gn_b[i]))
            a_lat, a_ctx = [og_lat, or_lat], [og_ctx, or_ctx]
        else:
            w_in = pack_odd_w_in(od_w_in[i])
            w_out = od_w_out[i].astype(BF16)
            u_lat = inproj(x_lat, sc, sh, w_in, 1024, OD_TN)
            u_ctx = inproj(x_ctx, sc_c, sh_c, w_in, t_ctx, OD_TN)
            tables = rope_lane_tables(t_lat)
            o_lat = sink_attention(u_lat, kv_rope(u_lat, tables), tables, u_ctx, od_sink[i])
            o_ctx = sink_attention(u_ctx, None, None, u_ctx, od_sink[i]) if need_ctx else None
            a_lat, a_ctx = [o_lat], [o_ctx]
        x_lat = outproj_ln(a_lat, w_out, x_lat, gt, ln_g[layer], ln_b[layer])
        if need_ctx:
            x_ctx = outproj_ln(a_ctx, w_out, x_ctx, gt_c, ln_g[layer], ln_b[layer])
    return x_lat
```

```python
import functools

import jax
import jax.numpy as jnp
import numpy as np
from jax import lax
from jax.experimental import pallas as pl
from jax.experimental.pallas import tpu as pltpu

F32 = jnp.float32
BF16 = jnp.bfloat16

D_MODEL = 2048
DEPTH = 4
GRID_W = 64
GDN_WIDTH = D_MODEL // 2
GDN_HEADS = 8
GDN_CHUNK = 128
RWKV_WIDTH = D_MODEL // 2
RWKV_HEADS = 16
RWKV_CHUNK = 64
RWKV_GN_EPS = 64e-5
ATT_HEAD = 64
ATT_Q_HEADS = D_MODEL // ATT_HEAD
ATT_KV_HEADS = ATT_Q_HEADS // 8
ATT_GROUP = ATT_Q_HEADS // ATT_KV_HEADS
ATT_WIDTH = ATT_Q_HEADS * ATT_HEAD
WINDOW = 128
ATT_BLOCK = 128
ROPE_BASE = 10000.0
ROPE_FREQS = ATT_HEAD // 4
NEG_INF = -1e30
DEEPNORM_ALPHA = (2 * DEPTH) ** 0.25
LN_EPS = 1e-5
GDN_IN = 4 * GDN_WIDTH + 4 * GDN_HEADS
RWKV_IN = 4 * RWKV_WIDTH + 128

LANES = 128
HALF = 64
V7X_VMEM_LIMIT_BYTES = 56 * 1024 * 1024
EV_GDN_COL = 0
EV_RWKV_COL = 4 * GDN_WIDTH
EV_LORA_COL = EV_RWKV_COL + 4 * RWKV_WIDTH
EV_SMALL_COL = EV_LORA_COL + LANES
EV_COLS = EV_SMALL_COL + LANES
EV_TN = 768
OD_TN = 1024


def _mm_kernel(a_ref, w_ref, o_ref):
    o_ref[...] = jnp.dot(a_ref[...].astype(BF16), w_ref[...].astype(BF16), preferred_element_type=F32)


def _small_matmul(a, w, tn=512):
    m, k = a.shape
    n = w.shape[1]
    mp = -(-m // 8) * 8
    a = jnp.pad(a, ((0, mp - m), (0, 0)))
    out = pl.pallas_call(
        _mm_kernel,
        grid=(n // tn,),
        in_specs=[pl.BlockSpec((mp, k), lambda j: (0, 0)), pl.BlockSpec((k, tn), lambda j: (0, j))],
        out_specs=pl.BlockSpec((mp, tn), lambda j: (0, j)),
        out_shape=jax.ShapeDtypeStruct((mp, n), F32),
        compiler_params=pltpu.CompilerParams(dimension_semantics=("arbitrary",), vmem_limit_bytes=V7X_VMEM_LIMIT_BYTES),
        name="mod_matmul",
    )(a, w)
    return out[:m]


def _inproj_kernel(x_ref, sc_ref, sh_ref, w_ref, o_ref, h_ref):
    @pl.when(pl.program_id(2) == 0)
    def _():
        h_ref[...] = (x_ref[0] * (1.0 + sc_ref[0]) + sh_ref[0]).astype(BF16)

    o_ref[0] = jnp.dot(h_ref[...], w_ref[...], preferred_element_type=F32)


def inproj(x, sc, sh, w, tm, tn):
    bsz, t, d = x.shape
    n = w.shape[1]
    tm = min(tm, t)
    assert t % tm == 0 and n % tn == 0
    return pl.pallas_call(
        _inproj_kernel,
        grid=(bsz, t // tm, n // tn),
        in_specs=[pl.BlockSpec((1, tm, d), lambda b, i, j: (b, i, 0)),
                  pl.BlockSpec((1, 1, d), lambda b, i, j: (b, 0, 0)),
                  pl.BlockSpec((1, 1, d), lambda b, i, j: (b, 0, 0)),
                  pl.BlockSpec((d, tn), lambda b, i, j: (0, j))],
        out_specs=pl.BlockSpec((1, tm, tn), lambda b, i, j: (b, i, j)),
        out_shape=jax.ShapeDtypeStruct((bsz, t, n), F32),
        scratch_shapes=[pltpu.VMEM((tm, d), BF16)],
        compiler_params=pltpu.CompilerParams(dimension_semantics=("arbitrary", "arbitrary", "arbitrary"),
                                             vmem_limit_bytes=V7X_VMEM_LIMIT_BYTES),
        name="inproj",
    )(x, sc, sh, w)


def _outproj_kernel(*refs, n_a):
    a_refs, (w_ref, x_ref, gt_ref, g_ref, b_ref, o_ref) = refs[:n_a], refs[n_a:]
    y = None
    k0 = 0
    for a_ref in a_refs:
        kw = a_ref.shape[2]
        term = jnp.dot(a_ref[0].astype(BF16), w_ref[k0:k0 + kw, :], preferred_element_type=F32)
        y = term if y is None else y + term
        k0 += kw
    z = DEEPNORM_ALPHA * x_ref[0] + gt_ref[0] * y
    mu = jnp.mean(z, axis=1, keepdims=True)
    zc = z - mu
    var = jnp.mean(zc * zc, axis=1, keepdims=True)
    o_ref[0] = zc * lax.rsqrt(var + LN_EPS) * g_ref[...] + b_ref[...]


def outproj_ln(a_list, w, x, gt, ln_g, ln_b, tm=256):
    bsz, t, d = x.shape
    tm = min(tm, t)
    in_specs = [pl.BlockSpec((1, tm, a.shape[2]), lambda b, i: (b, i, 0)) for a in a_list] + [
        pl.BlockSpec(w.shape, lambda b, i: (0, 0)),
        pl.BlockSpec((1, tm, d), lambda b, i: (b, i, 0)),
        pl.BlockSpec((1, 1, d), lambda b, i: (b, 0, 0)),
        pl.BlockSpec((1, d), lambda b, i: (0, 0)),
        pl.BlockSpec((1, d), lambda b, i: (0, 0))]
    return pl.pallas_call(
        functools.partial(_outproj_kernel, n_a=len(a_list)),
        grid=(bsz, t // tm),
        in_specs=in_specs,
        out_specs=pl.BlockSpec((1, tm, d), lambda b, i: (b, i, 0)),
        out_shape=jax.ShapeDtypeStruct((bsz, t, d), F32),
        compiler_params=pltpu.CompilerParams(dimension_semantics=("arbitrary", "arbitrary"),
                                             vmem_limit_bytes=V7X_VMEM_LIMIT_BYTES),
        name="outproj_ln",
    )(*a_list, w, x, gt, ln_g.reshape(1, d), ln_b.reshape(1, d))


def _bdot(a, b):
    return jnp.dot(a.astype(BF16), b.astype(BF16), preferred_element_type=F32)


def _bdot_nt(a, b):
    return lax.dot_general(a.astype(BF16), b.astype(BF16), (((1,), (1,)), ((), ())), preferred_element_type=F32)


def _bdot_tn(a, b):
    return lax.dot_general(a.astype(BF16), b.astype(BF16), (((0,), (0,)), ((), ())), preferred_element_type=F32)


(PV_MR0, PV_MR1, PV_MK0, PV_MK1, PV_MV0, PV_MV1, PV_MG0, PV_MG1, PV_W00, PV_W01, PV_A00, PV_A01,
 PV_KK, PV_KA, PV_RK, PV_GNG, PV_GNB) = range(17)
PV_ROWS = 24
MK_ABD, MK_P, MK_Y, MK_L1 = 0, 1, 2, 3
MK_LV = 4
N_LV = 5
MK_PER_DIR = MK_LV + N_LV
MK_EYE = 2 * MK_PER_DIR
MK_BD = MK_EYE + 1
N_MASKS = MK_BD + 1


def rwkv_masks():
    ri = np.arange(128)[:, None]
    ci = np.arange(128)[None, :]
    i, j = ri & 63, ci & 63
    same = (ri < 64) == (ci < 64)
    out = np.zeros((N_MASKS, 128, 128), np.float32)
    for d in range(2):
        strict = (j < i) if d == 0 else (j > i)
        incl = (j <= i) if d == 0 else (j >= i)
        base = d * MK_PER_DIR
        out[base + MK_ABD] = strict & same
        out[base + MK_P] = strict & (ci >= 64)
        out[base + MK_Y] = incl
        for li, s in enumerate((1, 2, 4, 8, 16, 32)):
            blk = (i // (2 * s)) == (j // (2 * s))
            if d == 0:
                m = blk & ((i & s) != 0) & ((j & s) == 0)
            else:
                m = blk & ((i & s) == 0) & ((j & s) != 0)
            out[base + (MK_L1 if li == 0 else MK_LV + li - 1)] = m & same
    out[MK_EYE] = ri == ci
    out[MK_BD] = same
    return out


def _seg_sum(x, lo):
    s0 = jnp.sum(jnp.where(lo, x, 0.0), axis=1, keepdims=True)
    s1 = jnp.sum(jnp.where(lo, 0.0, x), axis=1, keepdims=True)
    return jnp.where(lo, s0, s1)


def _chunk_cumsum(x, reverse, chunk):
    n = x.shape[0]
    rowm = lax.broadcasted_iota(jnp.int32, x.shape, 0) & (chunk - 1)
    s = 1
    while s < chunk:
        if not reverse:
            x = x + jnp.where(rowm >= s, pltpu.roll(x, s, axis=0), 0.0)
        else:
            x = x + jnp.where(rowm < chunk - s, pltpu.roll(x, n - s, axis=0), 0.0)
        s *= 2
    return x


def _rwkv_kernel(lr, lk, lv, lg, ll, cr, ck, cv, cg, cl, pv_ref, mixl_ref, wup_ref, aup_ref, mk_ref,
                 o_lat, o_ctx,
                 s_at0, s_at1, s_rt0, s_rt1, s_bt0, s_bt1, s_kt0, s_kt1, s_bh0, s_bh1, s_kh0, s_kh1, s_v,
                 s_g, s_bonus, s_yf, s_yb, s_w, s_tp, s_gr, s_et, s_state, *, t_lat, t_ctx, p1_chunks):
    C = RWKV_CHUNK
    s_at, s_rt, s_bt, s_kt, s_bh, s_kh = ((s_at0, s_at1), (s_rt0, s_rt1), (s_bt0, s_bt1), (s_kt0, s_kt1),
                                          (s_bh0, s_bh1), (s_kh0, s_kh1))
    s_y = (s_yf, s_yb)
    lo64 = lax.broadcasted_iota(jnp.int32, (C, LANES), 1) < HALF
    bd_ones = mk_ref[MK_BD]

    def pvrow(i):
        return pv_ref[pl.ds(i, 1), :]

    def prep(xr, xk, xv, xg, xl, tseq):
        R = min(256, tseq)
        lo = lax.broadcasted_iota(jnp.int32, (R, LANES), 1) < HALF

        def tile(it, carry):
            s = pl.multiple_of(it * R, R)
            rows = lax.broadcasted_iota(jnp.int32, (R, LANES), 0)
            has_prev = jnp.where(s > 0, 1.0, 0.0).astype(F32)
            has_next = jnp.where(s + R < tseq, 1.0, 0.0).astype(F32)
            pidx = jnp.maximum(s - 1, 0)
            nidx = jnp.minimum(s + R, tseq - 1)

            def shifted(ref, m0, m1):
                cur = ref[0, pl.ds(s, R), :]
                prow = ref[0, pl.ds(pidx, 1), :] * has_prev
                nrow = ref[0, pl.ds(nidx, 1), :] * has_next
                prev = jnp.where(rows == 0, prow, pltpu.roll(cur, 1, axis=0))
                nxt = jnp.where(rows == R - 1, nrow, pltpu.roll(cur, R - 1, axis=0))
                return cur + m0 * (prev - cur) + m1 * (nxt - cur)

            r = shifted(xr, pvrow(PV_MR0), pvrow(PV_MR1))
            k = shifted(xk, pvrow(PV_MK0), pvrow(PV_MK1))
            v = shifted(xv, pvrow(PV_MV0), pvrow(PV_MV1))
            g = shifted(xg, pvrow(PV_MG0), pvrow(PV_MG1))
            lor = shifted(xl, mixl_ref[0:1, :], mixl_ref[1:2, :])
            th = jnp.tanh(lor)
            kkf = k * pvrow(PV_KK)
            kk = kkf * lax.rsqrt(_seg_sum(kkf * kkf, lo) + 1e-6)
            ka = pvrow(PV_KA)
            kd_sum = None
            for d in range(2):
                w_pre = pvrow(PV_W00 + d) + _bdot(th, wup_ref[d])
                lw = -jnp.exp(-jax.nn.softplus(-w_pre) - 0.5)
                ag = jax.nn.sigmoid(pvrow(PV_A00 + d) + _bdot(lor, aup_ref[d]))
                kd = k * (1.0 + (ag - 1.0) * ka)
                bdv = kk * ag
                kd_sum = kd if kd_sum is None else kd_sum + kd
                c_fwd = _chunk_cumsum(lw, False, C)
                tot = jnp.concatenate([jnp.broadcast_to(c_fwd[(c + 1) * C - 1:(c + 1) * C, :], (C, LANES))
                                       for c in range(R // C)], axis=0)
                c_bwd = tot - c_fwd + lw
                cw, rest = (c_fwd, c_bwd - lw) if d == 0 else (c_bwd, c_fwd - lw)
                e_in = jnp.exp(-cw)
                e_out = jnp.exp(rest)
                s_at[d][pl.ds(s, R), :] = (-kk * jnp.exp(cw - lw)).astype(BF16)
                s_rt[d][pl.ds(s, R), :] = (r * jnp.exp(cw)).astype(BF16)
                s_bt[d][pl.ds(s, R), :] = (bdv * e_in).astype(BF16)
                s_kt[d][pl.ds(s, R), :] = (kd * e_in).astype(BF16)
                s_bh[d][pl.ds(s, R), :] = (bdv * e_out).astype(BF16)
                s_kh[d][pl.ds(s, R), :] = (kd * e_out).astype(BF16)
                for c in range(R // C):
                    last = c * C + (C - 1 if d == 0 else 0)
                    s_et[d, pl.ds(it * (R // C) + c, 1), :] = jnp.exp(cw[last:last + 1, :])
            s_v[pl.ds(s, R), :] = v.astype(BF16)
            s_g[pl.ds(s, R), :] = g
            s_bonus[pl.ds(s, R), :] = _seg_sum(r * kd_sum * pvrow(PV_RK), lo) * v
            return carry

        lax.fori_loop(0, tseq // R, tile, 0)

    def build(n, d):
        base = d * MK_PER_DIR
        rows = pl.ds(pl.multiple_of(n * C, C), C)
        at, rt = s_at[d][rows, :], s_rt[d][rows, :]
        zero = jnp.zeros_like(at)
        a2 = jnp.concatenate([jnp.where(lo64, at, zero), jnp.where(lo64, zero, at)], axis=0)
        l1 = jnp.concatenate([a2, jnp.where(lo64, rt, zero), jnp.where(lo64, zero, rt)], axis=0)
        gmat = _bdot_nt(l1, jnp.concatenate([s_bt[d][rows, :], s_kt[d][rows, :]], axis=0))
        ga = gmat[0:2 * C]
        v = s_v[rows, :]
        p = _bdot(ga * mk_ref[base + MK_P], jnp.concatenate([v, v], axis=0))
        s_gr[d, n] = (gmat[2 * C:4 * C] * mk_ref[base + MK_Y]).astype(BF16)
        ga_sw = jnp.concatenate([ga[0:C], pltpu.roll(ga[C:2 * C], HALF, axis=1)], axis=0)
        return ga_sw * mk_ref[base + MK_ABD], jnp.concatenate([a2.astype(F32), p], axis=1)

    def invert(a_list, d_list):
        xs = [mk_ref[MK_EYE] + a * mk_ref[d * MK_PER_DIR + MK_L1] for a, d in zip(a_list, d_list)]
        for li in range(N_LV):
            ams = [a * mk_ref[d * MK_PER_DIR + MK_LV + li] for a, d in zip(a_list, d_list)]
            t1 = [_bdot(x, am) for x, am in zip(xs, ams)]
            xs = [x + _bdot(t, x) for x, t in zip(xs, t1)]
        return xs

    def phase1(tseq):
        nch = tseq // C
        per = min(p1_chunks, nch)

        def body(it, carry):
            probs = [(it * per + j, d) for j in range(per) for d in range(2)]
            built = [build(n, d) for n, d in probs]
            xs = invert([b[0] for b in built], [d for _, d in probs])
            for (n, d), x, b in zip(probs, xs, built):
                wtp = _bdot(x, b[1])
                s_w[d, n] = wtp[:, 0:LANES].astype(BF16)
                s_tp[d, n] = jnp.where(lo64, wtp[0:C, LANES:2 * LANES], wtp[C:2 * C, LANES:2 * LANES])
            return carry

        lax.fori_loop(0, nch // per, body, 0)

    def phase2(tseq):
        nch = tseq // C

        def body(it, carry):
            ns = (it, nch - 1 - it)
            rows = [pl.ds(pl.multiple_of(ns[d] * C, C), C) for d in range(2)]
            st = [s_state[d] for d in range(2)]
            xs = [_bdot_nt(jnp.concatenate([s_w[d, ns[d]], s_rt[d][rows[d], :]], axis=0), st[d]) for d in range(2)]
            uv = [jnp.concatenate([(xs[d][0:C] + xs[d][C:2 * C] + s_tp[d, ns[d]]).astype(BF16), s_v[rows[d], :]], axis=0)
                  for d in range(2)]
            yb = [_bdot(s_gr[d, ns[d]], uv[d]) for d in range(2)]
            for d in range(2):
                s_y[d][rows[d], :] = xs[d][2 * C:3 * C] + jnp.where(lo64, yb[d][0:C], yb[d][C:2 * C])
                bk = jnp.concatenate([s_bh[d][rows[d], :], s_kh[d][rows[d], :]], axis=0)
                s_state[d] = st[d] * s_et[d, pl.ds(ns[d], 1), :] + bd_ones * _bdot_tn(uv[d], bk)
            return carry

        lax.fori_loop(0, nch, body, 0)

    def epilogue(o_ref, tseq):
        R = min(256, tseq)
        lo = lax.broadcasted_iota(jnp.int32, (R, LANES), 1) < HALF

        def tile(it, carry):
            s = pl.multiple_of(it * R, R)
            rows = pl.ds(s, R)
            y = s_yf[rows, :] + s_yb[rows, :]
            mu = _seg_sum(y, lo) * (1.0 / HALF)
            yc = y - mu
            var = _seg_sum(yc * yc, lo) * (1.0 / HALF)
            yn = yc * lax.rsqrt(var + RWKV_GN_EPS) * pvrow(PV_GNG) + pvrow(PV_GNB)
            g = s_g[rows, :]
            o_ref[0, rows, :] = (yn + s_bonus[rows, :]) * (g * jax.nn.sigmoid(g))
            return carry

        lax.fori_loop(0, tseq // R, tile, 0)

    s_state[...] = jnp.zeros_like(s_state)
    for (xr, xk, xv, xg, xl, o_ref, tseq) in ((cr, ck, cv, cg, cl, o_ctx, t_ctx), (lr, lk, lv, lg, ll, o_lat, t_lat)):
        prep(xr, xk, xv, xg, xl, tseq)
        phase1(tseq)
        phase2(tseq)
        epilogue(o_ref, tseq)


def _col_block_map(b, p, *, off):
    return (b, 0, off + p)


def rwkv_mixer(u_lat, u_ctx, pv, mixl, wup, aup, p1_chunks=8):
    bsz, t_lat, _ = u_lat.shape
    t_ctx = u_ctx.shape[1]
    npair = RWKV_HEADS // 2
    c0 = EV_RWKV_COL // LANES
    nch = t_lat // RWKV_CHUNK
    masks = jnp.asarray(rwkv_masks())

    def seq_specs(t):
        return [pl.BlockSpec((1, t, LANES), functools.partial(_col_block_map, off=c0 + o * npair)) for o in range(4)] + [
            pl.BlockSpec((1, t, LANES), lambda b, p: (b, 0, EV_LORA_COL // LANES))]

    in_specs = seq_specs(t_lat) + seq_specs(t_ctx) + [
        pl.BlockSpec((PV_ROWS, LANES), lambda b, p: (0, p)),
        pl.BlockSpec((8, LANES), lambda b, p: (0, 0)),
        pl.BlockSpec((2, LANES, LANES), lambda b, p: (0, 0, p)),
        pl.BlockSpec((2, LANES, LANES), lambda b, p: (0, 0, p)),
        pl.BlockSpec((N_MASKS, LANES, LANES), lambda b, p: (0, 0, 0)),
    ]
    out_specs = [pl.BlockSpec((1, t_lat, LANES), lambda b, p: (b, 0, p)),
                 pl.BlockSpec((1, t_ctx, LANES), lambda b, p: (b, 0, p))]
    scratch = ([pltpu.VMEM((t_lat, LANES), BF16) for _ in range(13)] + [pltpu.VMEM((t_lat, LANES), F32) for _ in range(4)]
               + [pltpu.VMEM((2, nch, LANES, LANES), BF16), pltpu.VMEM((2, nch, RWKV_CHUNK, LANES), F32),
                  pltpu.VMEM((2, nch, LANES, LANES), BF16)]
               + [pltpu.VMEM((2, nch, LANES), F32), pltpu.VMEM((2, LANES, LANES), F32)])
    kern = functools.partial(_rwkv_kernel, t_lat=t_lat, t_ctx=t_ctx, p1_chunks=p1_chunks)
    return pl.pallas_call(
        kern,
        grid=(bsz, npair),
        in_specs=in_specs,
        out_specs=out_specs,
        out_shape=[jax.ShapeDtypeStruct((bsz, t_lat, RWKV_WIDTH), F32), jax.ShapeDtypeStruct((bsz, t_ctx, RWKV_WIDTH), F32)],
        scratch_shapes=scratch,
        compiler_params=pltpu.CompilerParams(dimension_semantics=("arbitrary", "arbitrary"),
                                             vmem_limit_bytes=V7X_VMEM_LIMIT_BYTES),
        name="rwkv7_chunked",
    )(*([u_lat] * 5), *([u_ctx] * 5), pv, mixl, wup, aup, masks)


def rwkv_params(mix, w0, w_up, a0, a_up, k_k, k_a, r_k, gn_g, gn_b):
    wd = RWKV_WIDTH
    rows = [mix[0, 0:wd], mix[1, 0:wd], mix[0, wd:2 * wd], mix[1, wd:2 * wd], mix[0, 2 * wd:3 * wd], mix[1, 2 * wd:3 * wd],
            mix[0, 3 * wd:4 * wd], mix[1, 3 * wd:4 * wd], w0[0], w0[1], a0[0], a0[1], k_k, k_a, r_k.reshape(-1), gn_g, gn_b]
    pv = jnp.concatenate([jnp.stack(rows), jnp.zeros((PV_ROWS - len(rows), wd), F32)], 0)
    mixl = jnp.concatenate([mix[:, 4 * wd:], jnp.zeros((6, LANES), F32)], 0)
    z = jnp.zeros((2, HALF, wd), F32)
    wup = jnp.concatenate([w_up, z], 1)
    aup = jnp.concatenate([z, a_up], 1)
    return pv, mixl, wup, aup


GP_CONV_Q, GP_CONV_K, GP_CONV_V = 0, 5, 10
GP_NORM, GP_ALOG0, GP_ALOG1, GP_DTB0, GP_DTB1 = 15, 16, 17, 18, 19
GP_ROWS = 24
GM_STRICT, GM_INCL, GM_L1, GM_LV = 0, 1, 2, 3
GN_LV = 6
GM_PER_DIR = GM_LV + GN_LV
GM_EYE = 2 * GM_PER_DIR
GN_MASKS = GM_EYE + 1
GDN_P1_CHUNKS = 4


def gdn_masks():
    i = np.arange(GDN_CHUNK)[:, None]
    j = np.arange(GDN_CHUNK)[None, :]
    out = np.zeros((GN_MASKS, GDN_CHUNK, GDN_CHUNK), np.float32)
    for d in range(2):
        base = d * GM_PER_DIR
        out[base + GM_STRICT] = (j < i) if d == 0 else (j > i)
        out[base + GM_INCL] = (j <= i) if d == 0 else (j >= i)
        for li in range(GN_LV + 1):
            s = 1 << li
            blk = (i // (2 * s)) == (j // (2 * s))
            m = blk & (((i & s) != 0) & ((j & s) == 0) if d == 0 else ((i & s) == 0) & ((j & s) != 0))
            out[base + (GM_L1 if li == 0 else GM_LV + li - 1)] = m
    out[GM_EYE] = i == j
    return out


def _gdn_kernel(lq, lk, lv, lz, lbg, cq, ck, cv, cz, cbg, gp_ref, mk_ref, o_lat, o_ctx,
                s_q, s_k, s_kb0, s_kb1, s_kbg0, s_kbg1, s_qg0, s_qg1, s_kg0, s_kg1, s_vb0, s_vb1,
                s_gc0, s_gc1, s_of, s_ob, s_tk, s_tvb, s_a, s_gl, s_state, *, t_lat, t_ctx):
    C = GDN_CHUNK
    s_kb, s_kbg, s_qg, s_kg, s_vb = (s_kb0, s_kb1), (s_kbg0, s_kbg1), (s_qg0, s_qg1), (s_kg0, s_kg1), (s_vb0, s_vb1)
    s_gc, s_o = (s_gc0, s_gc1), (s_of, s_ob)

    def gprow(i):
        return gp_ref[pl.ds(i, 1), :]

    def prep(xq, xk, xv, xbg, tseq):
        R = min(256, tseq)
        N = R + 16

        def tile(it, carry):
            s = pl.multiple_of(it * R, R)
            has_prev = jnp.where(s > 0, 1.0, 0.0).astype(F32)
            has_next = jnp.where(s + R < tseq, 1.0, 0.0).astype(F32)
            pidx = pl.multiple_of(jnp.maximum(s - 8, 0), 8)
            nidx = pl.multiple_of(jnp.minimum(s + R, tseq - 8), 8)

            def conv_silu(ref, row0):
                xe = jnp.concatenate([ref[0, pl.ds(pidx, 8), :] * has_prev, ref[0, pl.ds(s, R), :],
                                      ref[0, pl.ds(nidx, 8), :] * has_next], axis=0)
                acc = None
                for j in range(5):
                    sh = (N - (j - 2)) % N
                    xs = xe if sh == 0 else pltpu.roll(xe, sh, axis=0)
                    term = gprow(row0 + j) * xs[8:8 + R]
                    acc = term if acc is None else acc + term
                return acc * jax.nn.sigmoid(acc)

            qc = conv_silu(xq, GP_CONV_Q)
            kc = conv_silu(xk, GP_CONV_K)
            v = conv_silu(xv, GP_CONV_V)
            q = qc * (lax.rsqrt(jnp.sum(qc * qc, axis=1, keepdims=True) + 1e-6) * (LANES ** -0.5))
            k = kc * lax.rsqrt(jnp.sum(kc * kc, axis=1, keepdims=True) + 1e-6)
            bg = xbg[0, 0, pl.ds(s, R), :]
            s_q[pl.ds(s, R), :] = q.astype(BF16)
            s_k[pl.ds(s, R), :] = k.astype(BF16)
            for d in range(2):
                beta = jax.nn.sigmoid(bg[:, d:d + 1])
                g = -jnp.exp(gprow(GP_ALOG0 + d)) * jax.nn.softplus(bg[:, 2 + d:3 + d] + gprow(GP_DTB0 + d))
                c_fwd = _chunk_cumsum(g, False, C)
                tot = jnp.concatenate([jnp.broadcast_to(c_fwd[(c + 1) * C - 1:(c + 1) * C, :], (C, LANES))
                                       for c in range(R // C)], axis=0)
                c_bwd = tot - c_fwd + g
                gc, rest = (c_fwd, c_bwd - g) if d == 0 else (c_bwd, c_fwd - g)
                eg = jnp.exp(gc)
                kb = k * beta
                s_kb[d][pl.ds(s, R), :] = kb.astype(BF16)
                s_kbg[d][pl.ds(s, R), :] = (kb * eg).astype(BF16)
                s_qg[d][pl.ds(s, R), :] = (q * eg).astype(BF16)
                s_kg[d][pl.ds(s, R), :] = (k * jnp.exp(rest)).astype(BF16)
                s_vb[d][pl.ds(s, R), :] = (v * beta).astype(BF16)
                s_gc[d][pl.ds(s, R), :] = gc
                for c in range(R // C):
                    last = c * C + (C - 1 if d == 0 else 0)
                    s_gl[d, pl.ds(it * (R // C) + c, 1), :] = eg[last:last + 1, :]
            return carry

        lax.fori_loop(0, tseq // R, tile, 0)

    def build(n, d):
        base = d * GM_PER_DIR
        rows = pl.ds(pl.multiple_of(n * C, C), C)
        gmat = _bdot_nt(jnp.concatenate([s_kb[d][rows, :], s_q[rows, :]], axis=0), s_k[rows, :])
        gc = s_gc[d][rows, :]
        m_incl = mk_ref[base + GM_INCL]
        dec = jnp.exp((gc - gc.T) * m_incl)
        s_a[d, n] = (gmat[C:2 * C] * dec * m_incl).astype(BF16)
        return -(gmat[0:C] * dec * mk_ref[base + GM_STRICT])

    def invert(a_list, d_list):
        xs = [mk_ref[GM_EYE] + a * mk_ref[d * GM_PER_DIR + GM_L1] for a, d in zip(a_list, d_list)]
        for li in range(GN_LV):
            ams = [a * mk_ref[d * GM_PER_DIR + GM_LV + li] for a, d in zip(a_list, d_list)]
            t1 = [_bdot(x, am) for x, am in zip(xs, ams)]
            xs = [x + _bdot(t, x) for x, t in zip(xs, t1)]
        return xs

    def phase1(tseq):
        nch = tseq // C
        per = min(GDN_P1_CHUNKS, nch)

        def body(it, carry):
            probs = [(it * per + j, d) for j in range(per) for d in range(2)]
            xs = invert([build(n, d) for n, d in probs], [d for _, d in probs])
            for (n, d), x in zip(probs, xs):
                rows = pl.ds(pl.multiple_of(n * C, C), C)
                tkv = _bdot(x, jnp.concatenate([s_kbg[d][rows, :], s_vb[d][rows, :]], axis=1))
                s_tk[d, n] = tkv[:, 0:LANES].astype(BF16)
                s_tvb[d, n] = tkv[:, LANES:2 * LANES]
            return carry

        lax.fori_loop(0, nch // per, body, 0)

    def phase2(tseq):
        nch = tseq // C

        def body(it, carry):
            ns = (it, nch - 1 - it)
            rows = [pl.ds(pl.multiple_of(ns[d] * C, C), C) for d in range(2)]
            st = [s_state[d] for d in range(2)]
            aq = [_bdot(jnp.concatenate([s_tk[d, ns[d]], s_qg[d][rows[d], :]], axis=0), st[d]) for d in range(2)]
            vnew = [s_tvb[d, ns[d]] - aq[d][0:C] for d in range(2)]
            oo = [aq[d][C:2 * C] + _bdot(s_a[d, ns[d]], vnew[d]) for d in range(2)]
            for d in range(2):
                s_o[d][rows[d], :] = oo[d]
                s_state[d] = st[d] * s_gl[d, pl.ds(ns[d], 1), :] + _bdot_tn(s_kg[d][rows[d], :], vnew[d])
            return carry

        lax.fori_loop(0, nch, body, 0)

    def epilogue(o_ref, z_ref, tseq):
        R = min(256, tseq)

        def tile(it, carry):
            s = pl.multiple_of(it * R, R)
            rows = pl.ds(s, R)
            o = s_of[rows, :] + s_ob[rows, :]
            o = o * lax.rsqrt(jnp.mean(o * o, axis=1, keepdims=True) + 1e-6) * gprow(GP_NORM)
            z = z_ref[0, rows, :]
            o_ref[0, rows, :] = o * (z * jax.nn.sigmoid(z))
            return carry

        lax.fori_loop(0, tseq // R, tile, 0)

    s_state[...] = jnp.zeros_like(s_state)
    for (xq, xk, xv, xz, xbg, o_ref, tseq) in ((cq, ck, cv, cz, cbg, o_ctx, t_ctx), (lq, lk, lv, lz, lbg, o_lat, t_lat)):
        prep(xq, xk, xv, xbg, tseq)
        phase1(tseq)
        phase2(tseq)
        epilogue(o_ref, xz, tseq)


def gdn_mixer(u_lat, bg_lat, u_ctx, bg_ctx, gp):
    bsz, t_lat, _ = u_lat.shape
    t_ctx = u_ctx.shape[1]
    nh = GDN_HEADS
    masks = jnp.asarray(gdn_masks())

    def seq_specs(t):
        return [pl.BlockSpec((1, t, LANES), functools.partial(_col_block_map, off=o * nh)) for o in range(4)] + [
            pl.BlockSpec((1, 1, t, 4), lambda b, h: (b, h, 0, 0))]

    in_specs = seq_specs(t_lat) + seq_specs(t_ctx) + [
        pl.BlockSpec((GP_ROWS, LANES), lambda b, h: (0, h)),
        pl.BlockSpec((GN_MASKS, LANES, LANES), lambda b, h: (0, 0, 0)),
    ]
    out_specs = [pl.BlockSpec((1, t_lat, LANES), lambda b, h: (b, 0, h)),
                 pl.BlockSpec((1, t_ctx, LANES), lambda b, h: (b, 0, h))]
    nch = t_lat // GDN_CHUNK
    scratch = ([pltpu.VMEM((t_lat, LANES), BF16) for _ in range(12)] + [pltpu.VMEM((t_lat, LANES), F32) for _ in range(4)]
               + [pltpu.VMEM((2, nch, LANES, LANES), BF16), pltpu.VMEM((2, nch, LANES, LANES), F32),
                  pltpu.VMEM((2, nch, LANES, LANES), BF16),
                  pltpu.VMEM((2, max(nch, 8), LANES), F32), pltpu.VMEM((2, LANES, LANES), F32)])
    kern = functools.partial(_gdn_kernel, t_lat=t_lat, t_ctx=t_ctx)
    return pl.pallas_call(
        kern,
        grid=(bsz, nh),
        in_specs=in_specs,
        out_specs=out_specs,
        out_shape=[jax.ShapeDtypeStruct((bsz, t_lat, GDN_WIDTH), F32), jax.ShapeDtypeStruct((bsz, t_ctx, GDN_WIDTH), F32)],
        scratch_shapes=scratch,
        compiler_params=pltpu.CompilerParams(dimension_semantics=("arbitrary", "arbitrary"),
                                             vmem_limit_bytes=V7X_VMEM_LIMIT_BYTES),
        name="gdn_chunked",
    )(*([u_lat] * 4), bg_lat, *([u_ctx] * 4), bg_ctx, gp, masks)


def gdn_params(conv_w, a_log, dt_bias, norm_g):
    wd = GDN_WIDTH
    rep = lambda x: jnp.repeat(x, LANES)
    rows = [conv_w[j, o * wd:(o + 1) * wd] for o in range(3) for j in range(5)]
    rows += [jnp.tile(norm_g, GDN_HEADS), rep(a_log[0]), rep(a_log[1]), rep(dt_bias[0]), rep(dt_bias[1])]
    return jnp.concatenate([jnp.stack(rows), jnp.zeros((GP_ROWS - len(rows), wd), F32)], 0)


def gdn_bg(u_small):
    bsz, t, _ = u_small.shape
    x = u_small.reshape(bsz, t, 2, 2, GDN_HEADS)
    return jnp.transpose(x, (0, 4, 1, 2, 3)).reshape(bsz, GDN_HEADS, t, 4)


OD_Q_COL, OD_GATE_COL, OD_KV_COL = 0, ATT_WIDTH, 2 * ATT_WIDTH
OD_KV_W = 4 * ATT_HEAD
OD_COLS = OD_KV_COL + ATT_KV_HEADS * OD_KV_W


def rope_lane_tables(t):
    pos = np.arange(t)
    inv = ROPE_BASE ** (-np.arange(ROPE_FREQS, dtype=np.float64) / ROPE_FREQS)
    ang_row = (pos // GRID_W)[:, None] * inv
    ang_col = (pos % GRID_W)[:, None] * inv
    ang = np.concatenate([ang_row, ang_row, ang_col, ang_col] * 2, axis=1)
    first = (np.arange(LANES) % (2 * ROPE_FREQS)) < ROPE_FREQS
    cos, sin = np.cos(ang), np.sin(ang)
    return (jnp.asarray(cos, F32), jnp.asarray(np.where(first, -sin, 0.0), F32), jnp.asarray(np.where(first, 0.0, sin), F32))


def _rope(x, cos, sina, sinb):
    n = x.shape[1]
    return x * cos + pltpu.roll(x, n - ROPE_FREQS, axis=1) * sina + pltpu.roll(x, ROPE_FREQS, axis=1) * sinb


def _kv_rope_kernel(u_ref, cos_ref, sina_ref, sinb_ref, o_ref):
    x = u_ref[0]
    cos, sina, sinb = cos_ref[...], sina_ref[...], sinb_ref[...]
    parts = []
    for h in range(ATT_KV_HEADS):
        k2 = x[:, h * OD_KV_W:h * OD_KV_W + LANES]
        parts += [_rope(k2, cos, sina, sinb), x[:, h * OD_KV_W + LANES:(h + 1) * OD_KV_W]]
    o_ref[0] = jnp.concatenate(parts, axis=1).astype(BF16)


def kv_rope(u, tables, tr=512):
    bsz, t, _ = u.shape
    tr = min(tr, t)
    kvw = ATT_KV_HEADS * OD_KV_W
    tab = pl.BlockSpec((tr, LANES), lambda b, i: (i, 0))
    return pl.pallas_call(
        _kv_rope_kernel,
        grid=(bsz, t // tr),
        in_specs=[pl.BlockSpec((1, tr, kvw), lambda b, i: (b, i, OD_KV_COL // kvw)), tab, tab, tab],
        out_specs=pl.BlockSpec((1, tr, kvw), lambda b, i: (b, i, 0)),
        out_shape=jax.ShapeDtypeStruct((bsz, t, kvw), BF16),
        compiler_params=pltpu.CompilerParams(dimension_semantics=("arbitrary", "arbitrary")),
        name="kv_rope",
    )(u, *tables)


def _attn_kernel(sink_ref, *refs, has_local, nb):
    if has_local:
        q_ref, g_ref, kvp_ref, kvc_ref, kvn_ref, cos_ref, sina_ref, sinb_ref, kvx_ref, o_ref = refs
    else:
        q_ref, g_ref, kvx_ref, o_ref = refs
    tq = q_ref.shape[1]
    i = pl.program_id(1)
    q = q_ref[0]
    if has_local:
        rep = ATT_WIDTH // LANES
        cos, sina, sinb = (jnp.concatenate([r[...]] * rep, axis=1) for r in (cos_ref, sina_ref, sinb_ref))
        q = _rope(q, cos, sina, sinb)
    q = q * (ATT_HEAD ** -0.5)
    lo = lax.broadcasted_iota(jnp.int32, (tq, LANES), 1) < HALF
    if has_local:
        a = lax.broadcasted_iota(jnp.int32, (tq, WINDOW), 0)
        c = lax.broadcasted_iota(jnp.int32, (tq, WINDOW), 1)
        ok_prev = (c >= a) & (i > 0)
        ok_next = (c <= a) & (i < nb - 1)
        always = jnp.ones((tq, tq + kvx_ref.shape[1]), jnp.bool_)
        valid = jnp.concatenate([ok_prev, always[:, :tq], ok_next, always[:, tq:]], axis=1)
    outs = []
    for hk in range(ATT_KV_HEADS):
        kcol = slice(hk * OD_KV_W, hk * OD_KV_W + LANES)
        vcol = slice(hk * OD_KV_W + LANES, (hk + 1) * OD_KV_W)
        qs = []
        for pair in range(ATT_GROUP // 2):
            grp = q[:, (hk * (ATT_GROUP // 2) + pair) * LANES:(hk * (ATT_GROUP // 2) + pair + 1) * LANES]
            qs += [jnp.where(lo, grp, 0.0), jnp.where(lo, 0.0, grp)]
        lhs = jnp.concatenate(qs, axis=0).astype(BF16)
        kx, vx = kvx_ref[0, :, kcol].astype(BF16), kvx_ref[0, :, vcol].astype(BF16)
        if has_local:
            kall = jnp.concatenate([kvp_ref[0, :, kcol], kvc_ref[0, :, kcol], kvn_ref[0, :, kcol], kx], axis=0)
            vall = jnp.concatenate([kvp_ref[0, :, vcol], kvc_ref[0, :, vcol], kvn_ref[0, :, vcol], vx], axis=0)
        else:
            kall, vall = kx, vx
        s = _bdot_nt(lhs, kall)
        lov = lax.broadcasted_iota(jnp.int32, vall.shape, 1) < HALF
        zv = jnp.zeros_like(vall)
        v2 = jnp.concatenate([jnp.where(lov, vall, zv), jnp.where(lov, zv, vall)], axis=0)
        for pair in range(ATT_GROUP // 2):
            ps, inv_l = [], []
            for sub in range(2):
                g = 2 * pair + sub
                sg = s[g * tq:(g + 1) * tq]
                if has_local:
                    sg = jnp.where(valid, sg, NEG_INF)
                sink = sink_ref[hk, g]
                m = jnp.maximum(jnp.max(sg, axis=1, keepdims=True), sink)
                p = jnp.exp(sg - m)
                inv_l.append(1.0 / (jnp.sum(p, axis=1, keepdims=True) + jnp.exp(sink - m)))
                ps.append(p.astype(BF16))
            o_pair = jnp.dot(jnp.concatenate(ps, axis=1), v2, preferred_element_type=F32)
            outs.append(o_pair * jnp.where(lo, inv_l[0], inv_l[1]))
    g = g_ref[0]
    o_ref[0] = jnp.concatenate(outs, axis=1) * (g * jax.nn.sigmoid(g))


def sink_attention(u, kv_local, tables, u_ctx, sink):
    bsz, t, _ = u.shape
    tq = ATT_BLOCK
    nb = t // tq
    tx = u_ctx.shape[1]
    kvw = ATT_KV_HEADS * OD_KV_W
    has_local = kv_local is not None
    in_specs = [pl.BlockSpec(memory_space=pltpu.SMEM),
                pl.BlockSpec((1, tq, ATT_WIDTH), lambda b, i: (b, i, OD_Q_COL // ATT_WIDTH)),
                pl.BlockSpec((1, tq, ATT_WIDTH), lambda b, i: (b, i, OD_GATE_COL // ATT_WIDTH))]
    args = [sink.reshape(ATT_KV_HEADS, ATT_GROUP), u, u]
    if has_local:
        in_specs += [pl.BlockSpec((1, tq, kvw), lambda b, i: (b, jnp.maximum(i - 1, 0), 0)),
                     pl.BlockSpec((1, tq, kvw), lambda b, i: (b, i, 0)),
                     pl.BlockSpec((1, tq, kvw), lambda b, i: (b, jnp.minimum(i + 1, nb - 1), 0))]
        in_specs += [pl.BlockSpec((tq, LANES), lambda b, i: (i, 0))] * 3
        args += [kv_local] * 3 + list(tables)
    in_specs += [pl.BlockSpec((1, tx, kvw), lambda b, i: (b, 0, OD_KV_COL // kvw))]
    args += [u_ctx]
    return pl.pallas_call(
        functools.partial(_attn_kernel, has_local=has_local, nb=nb),
        grid=(bsz, nb),
        in_specs=in_specs,
        out_specs=pl.BlockSpec((1, tq, ATT_WIDTH), lambda b, i: (b, i, 0)),
        out_shape=jax.ShapeDtypeStruct((bsz, t, ATT_WIDTH), F32),
        compiler_params=pltpu.CompilerParams(dimension_semantics=("arbitrary", "arbitrary"),
                                             vmem_limit_bytes=V7X_VMEM_LIMIT_BYTES),
        name="sink_attention",
    )(*args)


def pack_odd_w_in(w_in):
    nq, nkv = ATT_WIDTH, ATT_KV_HEADS * ATT_HEAD
    parts = [w_in[:, :nq], w_in[:, nq + 2 * nkv:]]
    for h in range(ATT_KV_HEADS):
        k = w_in[:, nq + h * ATT_HEAD:nq + (h + 1) * ATT_HEAD]
        v = w_in[:, nq + nkv + h * ATT_HEAD:nq + nkv + (h + 1) * ATT_HEAD]
        parts += [k, k, v, v]
    return jnp.concatenate(parts, 1).astype(BF16)


def pack_even_w_in(w_in):
    g1 = 4 * GDN_WIDTH
    r0 = GDN_IN
    parts = [w_in[:, :g1], w_in[:, r0:r0 + 4 * RWKV_WIDTH], w_in[:, r0 + 4 * RWKV_WIDTH:], w_in[:, g1:GDN_IN],
             jnp.zeros((D_MODEL, EV_COLS - EV_SMALL_COL - 4 * GDN_HEADS), F32)]
    return jnp.concatenate(parts, 1).astype(BF16)


def kernel(x, c, ctx, c_ctx, mod_w, mod_b, ln_g, ln_b, ev_w_in, ev_w_out, gdn_conv, gdn_a_log, gdn_dt_bias,
           gdn_norm_g, rwkv_mix, rwkv_w0, rwkv_w_up, rwkv_a0, rwkv_a_up, rwkv_k_k, rwkv_k_a, rwkv_r_k,
           rwkv_gn_g, rwkv_gn_b, od_w_in, od_w_out, od_sink):
    bsz, t_lat, d = x.shape
    t_ctx = ctx.shape[1]
    x_lat, x_ctx = x, ctx
    cc = jax.nn.silu(jnp.concatenate([c, c_ctx[None]], 0))
    for layer in range(DEPTH):
        need_ctx = layer < DEPTH - 1
        i = layer // 2
        mod = _small_matmul(cc, mod_w[layer]) + mod_b[layer]
        sh, sc, gt = (mod[:bsz, j * d:(j + 1) * d][:, None, :] for j in range(3))
        sh_c, sc_c, gt_c = (jnp.broadcast_to(mod[bsz, j * d:(j + 1) * d], (bsz, 1, d)) for j in range(3))
        if layer % 2 == 0:
            w_in = pack_even_w_in(ev_w_in[i])
            w_out = ev_w_out[i].astype(BF16)
            u_lat = inproj(x_lat, sc, sh, w_in, 1024, EV_TN)
            u_ctx = inproj(x_ctx, sc_c, sh_c, w_in, t_ctx, EV_TN)
            small = slice(EV_SMALL_COL, EV_SMALL_COL + 4 * GDN_HEADS)
            og_lat, og_ctx = gdn_mixer(u_lat, gdn_bg(u_lat[..., small]), u_ctx, gdn_bg(u_ctx[..., small]),
                                       gdn_params(gdn_conv[i], gdn_a_log[i], gdn_dt_bias[i], gdn_norm_g[i]))
            or_lat, or_ctx = rwkv_mixer(u_lat, u_ctx, *rwkv_params(
                rwkv_mix[i], rwkv_w0[i], rwkv_w_up[i], rwkv_a0[i], rwkv_a_up[i], rwkv_k_k[i], rwkv_k_a[i],
                rwkv_r_k[i], rwkv_gn_g[i], rwkv_gn_b[i]))
            a_lat, a_ctx = [og_lat, or_lat], [og_ctx, or_ctx]
        else:
            w_in = pack_odd_w_in(od_w_in[i])
            w_out = od_w_out[i].astype(BF16)
            u_lat = inproj(x_lat, sc, sh, w_in, 1024, OD_TN)
            u_ctx = inproj(x_ctx, sc_c, sh_c, w_in, t_ctx, OD_TN)
            tables = rope_lane_tables(t_lat)
            o_lat = sink_attention(u_lat, kv_rope(u_lat, tables), tables, u_ctx, od_sink[i])
            o_ctx = sink_attention(u_ctx, None, None, u_ctx, od_sink[i]) if need_ctx else None
            a_lat, a_ctx = [o_lat], [o_ctx]
        x_lat = outproj_ln(a_lat, w_out, x_lat, gt, ln_g[layer], ln_b[layer])
        if need_ctx:
            x_ctx = outproj_ln(a_ctx, w_out, x_ctx, gt_c, ln_g[layer], ln_b[layer])
    return x_lat
```

```python
import functools

import jax
import jax.numpy as jnp
import numpy as np
from jax import lax
from jax.experimental import pallas as pl
from jax.experimental.pallas import tpu as pltpu

F32 = jnp.float32
BF16 = jnp.bfloat16

D_MODEL = 2048
DEPTH = 4
GRID_W = 64
GDN_WIDTH = D_MODEL // 2
GDN_HEADS = 8
GDN_CHUNK = 128
RWKV_WIDTH = D_MODEL // 2
RWKV_HEADS = 16
RWKV_CHUNK = 64
RWKV_GN_EPS = 64e-5
ATT_HEAD = 64
ATT_Q_HEADS = D_MODEL // ATT_HEAD
ATT_KV_HEADS = ATT_Q_HEADS // 8
ATT_GROUP = ATT_Q_HEADS // ATT_KV_HEADS
ATT_WIDTH = ATT_Q_HEADS * ATT_HEAD
WINDOW = 128
ATT_BLOCK = 128
ROPE_BASE = 10000.0
ROPE_FREQS = ATT_HEAD // 4
NEG_INF = -1e30
DEEPNORM_ALPHA = (2 * DEPTH) ** 0.25
LN_EPS = 1e-5
GDN_IN = 4 * GDN_WIDTH + 4 * GDN_HEADS
RWKV_IN = 4 * RWKV_WIDTH + 128

LANES = 128
HALF = 64
V7X_VMEM_LIMIT_BYTES = 56 * 1024 * 1024
EV_GDN_COL = 0
EV_RWKV_COL = 4 * GDN_WIDTH
EV_LORA_COL = EV_RWKV_COL + 4 * RWKV_WIDTH
EV_SMALL_COL = EV_LORA_COL + LANES
EV_COLS = EV_SMALL_COL + LANES
EV_TN = 768
OD_TN = 1024


def _mod_kernel(a_ref, w_ref, b_ref, o_ref):
    o_ref[0] = jnp.dot(a_ref[...].astype(BF16), w_ref[0].astype(BF16), preferred_element_type=F32) + b_ref[0]


def mod_all_layers(a, w, b, tn=512):
    m, k = a.shape
    nl, _, n = w.shape
    return pl.pallas_call(
        _mod_kernel,
        grid=(nl, n // tn),
        in_specs=[pl.BlockSpec((m, k), lambda l, j: (0, 0)), pl.BlockSpec((1, k, tn), lambda l, j: (l, 0, j)),
                  pl.BlockSpec((1, 1, tn), lambda l, j: (l, 0, j))],
        out_specs=pl.BlockSpec((1, m, tn), lambda l, j: (l, 0, j)),
        out_shape=jax.ShapeDtypeStruct((nl, m, n), F32),
        compiler_params=pltpu.CompilerParams(dimension_semantics=("arbitrary", "arbitrary"),
                                             vmem_limit_bytes=V7X_VMEM_LIMIT_BYTES),
        name="mod_matmul",
    )(a, w, b.reshape(nl, 1, n))


def _inproj_kernel(x_ref, sc_ref, sh_ref, w_ref, o_ref, h_ref):
    @pl.when(pl.program_id(2) == 0)
    def _():
        h_ref[...] = (x_ref[0] * (1.0 + sc_ref[0]) + sh_ref[0]).astype(BF16)

    o_ref[0] = jnp.dot(h_ref[...], w_ref[...], preferred_element_type=F32)


def inproj(x, sc, sh, w, tm, tn):
    bsz, t, d = x.shape
    n = w.shape[1]
    tm = min(tm, t)
    assert t % tm == 0 and n % tn == 0
    return pl.pallas_call(
        _inproj_kernel,
        grid=(bsz, t // tm, n // tn),
        in_specs=[pl.BlockSpec((1, tm, d), lambda b, i, j: (b, i, 0)),
                  pl.BlockSpec((1, 1, d), lambda b, i, j: (b, 0, 0)),
                  pl.BlockSpec((1, 1, d), lambda b, i, j: (b, 0, 0)),
                  pl.BlockSpec((d, tn), lambda b, i, j: (0, j))],
        out_specs=pl.BlockSpec((1, tm, tn), lambda b, i, j: (b, i, j)),
        out_shape=jax.ShapeDtypeStruct((bsz, t, n), F32),
        scratch_shapes=[pltpu.VMEM((tm, d), BF16)],
        compiler_params=pltpu.CompilerParams(dimension_semantics=("arbitrary", "arbitrary", "arbitrary"),
                                             vmem_limit_bytes=V7X_VMEM_LIMIT_BYTES),
        name="inproj",
    )(x, sc, sh, w)


def _outproj_kernel(*refs, n_a):
    a_refs, (w_ref, x_ref, gt_ref, g_ref, b_ref, o_ref) = refs[:n_a], refs[n_a:]
    y = None
    k0 = 0
    for a_ref in a_refs:
        kw = a_ref.shape[2]
        term = jnp.dot(a_ref[0].astype(BF16), w_ref[k0:k0 + kw, :], preferred_element_type=F32)
        y = term if y is None else y + term
        k0 += kw
    z = DEEPNORM_ALPHA * x_ref[0] + gt_ref[0] * y
    mu = jnp.mean(z, axis=1, keepdims=True)
    zc = z - mu
    var = jnp.mean(zc * zc, axis=1, keepdims=True)
    o_ref[0] = zc * lax.rsqrt(var + LN_EPS) * g_ref[...] + b_ref[...]


def outproj_ln(a_list, w, x, gt, ln_g, ln_b, tm=256):
    bsz, t, d = x.shape
    tm = min(tm, t)
    in_specs = [pl.BlockSpec((1, tm, a.shape[2]), lambda b, i: (b, i, 0)) for a in a_list] + [
        pl.BlockSpec(w.shape, lambda b, i: (0, 0)),
        pl.BlockSpec((1, tm, d), lambda b, i: (b, i, 0)),
        pl.BlockSpec((1, 1, d), lambda b, i: (b, 0, 0)),
        pl.BlockSpec((1, d), lambda b, i: (0, 0)),
        pl.BlockSpec((1, d), lambda b, i: (0, 0))]
    return pl.pallas_call(
        functools.partial(_outproj_kernel, n_a=len(a_list)),
        grid=(bsz, t // tm),
        in_specs=in_specs,
        out_specs=pl.BlockSpec((1, tm, d), lambda b, i: (b, i, 0)),
        out_shape=jax.ShapeDtypeStruct((bsz, t, d), F32),
        compiler_params=pltpu.CompilerParams(dimension_semantics=("arbitrary", "arbitrary"),
                                             vmem_limit_bytes=V7X_VMEM_LIMIT_BYTES),
        name="outproj_ln",
    )(*a_list, w, x, gt, ln_g.reshape(1, d), ln_b.reshape(1, d))


def _bdot(a, b):
    return jnp.dot(a.astype(BF16), b.astype(BF16), preferred_element_type=F32)


def _bdot_nt(a, b):
    return lax.dot_general(a.astype(BF16), b.astype(BF16), (((1,), (1,)), ((), ())), preferred_element_type=F32)


def _bdot_tn(a, b):
    return lax.dot_general(a.astype(BF16), b.astype(BF16), (((0,), (0,)), ((), ())), preferred_element_type=F32)


(PV_MR0, PV_MR1, PV_MK0, PV_MK1, PV_MV0, PV_MV1, PV_MG0, PV_MG1, PV_W00, PV_W01, PV_A00, PV_A01,
 PV_KK, PV_KA, PV_RK, PV_GNG, PV_GNB) = range(17)
PV_ROWS = 24
MK_ABD, MK_P, MK_Y, MK_L1 = 0, 1, 2, 3
MK_LV = 4
N_LV = 5
MK_PER_DIR = MK_LV + N_LV
MK_EYE = 2 * MK_PER_DIR
MK_BD = MK_EYE + 1
N_MASKS = MK_BD + 1


def rwkv_masks():
    ri = np.arange(128)[:, None]
    ci = np.arange(128)[None, :]
    i, j = ri & 63, ci & 63
    same = (ri < 64) == (ci < 64)
    out = np.zeros((N_MASKS, 128, 128), np.float32)
    for d in range(2):
        strict = (j < i) if d == 0 else (j > i)
        incl = (j <= i) if d == 0 else (j >= i)
        base = d * MK_PER_DIR
        out[base + MK_ABD] = strict & same
        out[base + MK_P] = strict & (ci >= 64)
        out[base + MK_Y] = incl
        for li, s in enumerate((1, 2, 4, 8, 16, 32)):
            blk = (i // (2 * s)) == (j // (2 * s))
            if d == 0:
                m = blk & ((i & s) != 0) & ((j & s) == 0)
            else:
                m = blk & ((i & s) == 0) & ((j & s) != 0)
            out[base + (MK_L1 if li == 0 else MK_LV + li - 1)] = m & same
    out[MK_EYE] = ri == ci
    out[MK_BD] = same
    return out


def _seg_sum(x, lo):
    s0 = jnp.sum(jnp.where(lo, x, 0.0), axis=1, keepdims=True)
    s1 = jnp.sum(jnp.where(lo, 0.0, x), axis=1, keepdims=True)
    return jnp.where(lo, s0, s1)


def _chunk_cumsum(x, reverse, chunk):
    n = x.shape[0]
    rowm = lax.broadcasted_iota(jnp.int32, x.shape, 0) & (chunk - 1)
    s = 1
    while s < chunk:
        if not reverse:
            x = x + jnp.where(rowm >= s, pltpu.roll(x, s, axis=0), 0.0)
        else:
            x = x + jnp.where(rowm < chunk - s, pltpu.roll(x, n - s, axis=0), 0.0)
        s *= 2
    return x


def _interleave(*gens):
    live = list(gens)
    while live:
        for g in list(live):
            try:
                next(g)
            except StopIteration:
                live.remove(g)


def _rwkv_kernel(lr, lk, lv, lg, ll, cr, ck, cv, cg, cl, pv_ref, mixl_ref, wup_ref, aup_ref, mk_ref,
                 o_lat, o_ctx,
                 s_at0, s_at1, s_rt0, s_rt1, s_bt0, s_bt1, s_kt0, s_kt1, s_bh0, s_bh1, s_kh0, s_kh1, s_v,
                 s_g, s_bonus, s_yf, s_yb, s_w, s_tp, s_gr, s_et, s_state, *, t_lat, t_ctx, p1_chunks):
    C = RWKV_CHUNK
    s_at, s_rt, s_bt, s_kt, s_bh, s_kh = ((s_at0, s_at1), (s_rt0, s_rt1), (s_bt0, s_bt1), (s_kt0, s_kt1),
                                          (s_bh0, s_bh1), (s_kh0, s_kh1))
    s_y = (s_yf, s_yb)
    lo64 = lax.broadcasted_iota(jnp.int32, (C, LANES), 1) < HALF
    bd_ones = mk_ref[MK_BD]

    def pvrow(i):
        return pv_ref[pl.ds(i, 1), :]

    def prep(xr, xk, xv, xg, xl, tseq):
        R = min(256, tseq)
        lo = lax.broadcasted_iota(jnp.int32, (R, LANES), 1) < HALF

        def tile(it, carry):
            s = pl.multiple_of(it * R, R)
            rows = lax.broadcasted_iota(jnp.int32, (R, LANES), 0)
            has_prev = jnp.where(s > 0, 1.0, 0.0).astype(F32)
            has_next = jnp.where(s + R < tseq, 1.0, 0.0).astype(F32)
            pidx = jnp.maximum(s - 1, 0)
            nidx = jnp.minimum(s + R, tseq - 1)

            def shifted(ref, m0, m1):
                cur = ref[0, pl.ds(s, R), :]
                prow = ref[0, pl.ds(pidx, 1), :] * has_prev
                nrow = ref[0, pl.ds(nidx, 1), :] * has_next
                prev = jnp.where(rows == 0, prow, pltpu.roll(cur, 1, axis=0))
                nxt = jnp.where(rows == R - 1, nrow, pltpu.roll(cur, R - 1, axis=0))
                return cur + m0 * (prev - cur) + m1 * (nxt - cur)

            r = shifted(xr, pvrow(PV_MR0), pvrow(PV_MR1))
            k = shifted(xk, pvrow(PV_MK0), pvrow(PV_MK1))
            v = shifted(xv, pvrow(PV_MV0), pvrow(PV_MV1))
            g = shifted(xg, pvrow(PV_MG0), pvrow(PV_MG1))
            lor = shifted(xl, mixl_ref[0:1, :], mixl_ref[1:2, :])
            th = jnp.tanh(lor)
            kkf = k * pvrow(PV_KK)
            kk = kkf * lax.rsqrt(_seg_sum(kkf * kkf, lo) + 1e-6)
            ka = pvrow(PV_KA)
            kd_sum = None
            for d in range(2):
                w_pre = pvrow(PV_W00 + d) + _bdot(th, wup_ref[d])
                lw = -jnp.exp(-jax.nn.softplus(-w_pre) - 0.5)
                ag = jax.nn.sigmoid(pvrow(PV_A00 + d) + _bdot(lor, aup_ref[d]))
                kd = k * (1.0 + (ag - 1.0) * ka)
                bdv = kk * ag
                kd_sum = kd if kd_sum is None else kd_sum + kd
                c_fwd = _chunk_cumsum(lw, False, C)
                tot = jnp.concatenate([jnp.broadcast_to(c_fwd[(c + 1) * C - 1:(c + 1) * C, :], (C, LANES))
                                       for c in range(R // C)], axis=0)
                c_bwd = tot - c_fwd + lw
                cw, rest = (c_fwd, c_bwd - lw) if d == 0 else (c_bwd, c_fwd - lw)
                e_in = jnp.exp(-cw)
                e_out = jnp.exp(rest)
                s_at[d][pl.ds(s, R), :] = (-kk * jnp.exp(cw - lw)).astype(BF16)
                s_rt[d][pl.ds(s, R), :] = (r * jnp.exp(cw)).astype(BF16)
                s_bt[d][pl.ds(s, R), :] = (bdv * e_in).astype(BF16)
                s_kt[d][pl.ds(s, R), :] = (kd * e_in).astype(BF16)
                s_bh[d][pl.ds(s, R), :] = (bdv * e_out).astype(BF16)
                s_kh[d][pl.ds(s, R), :] = (kd * e_out).astype(BF16)
                for c in range(R // C):
                    last = c * C + (C - 1 if d == 0 else 0)
                    s_et[d, pl.ds(it * (R // C) + c, 1), :] = jnp.exp(cw[last:last + 1, :])
            s_v[pl.ds(s, R), :] = v.astype(BF16)
            s_g[pl.ds(s, R), :] = g
            s_bonus[pl.ds(s, R), :] = _seg_sum(r * kd_sum * pvrow(PV_RK), lo) * v
            return carry

        lax.fori_loop(0, tseq // R, tile, 0)

    def p1_stages(probs):
        ds = [d for _, d in probs]
        rows = [pl.ds(pl.multiple_of(n * C, C), C) for n, _ in probs]
        a2s, gms = [], []
        for (n, d), rw in zip(probs, rows):
            at, rt = s_at[d][rw, :], s_rt[d][rw, :]
            zero = jnp.zeros_like(at)
            a2 = jnp.concatenate([jnp.where(lo64, at, zero), jnp.where(lo64, zero, at)], axis=0)
            l1 = jnp.concatenate([a2, jnp.where(lo64, rt, zero), jnp.where(lo64, zero, rt)], axis=0)
            a2s.append(a2)
            gms.append(_bdot_nt(l1, jnp.concatenate([s_bt[d][rw, :], s_kt[d][rw, :]], axis=0)))
        yield
        ps, a_list = [], []
        for (n, d), rw, gm in zip(probs, rows, gms):
            base = d * MK_PER_DIR
            ga = gm[0:2 * C]
            v = s_v[rw, :]
            ps.append(_bdot(ga * mk_ref[base + MK_P], jnp.concatenate([v, v], axis=0)))
            s_gr[d, n] = (gm[2 * C:4 * C] * mk_ref[base + MK_Y]).astype(BF16)
            ga_sw = jnp.concatenate([ga[0:C], pltpu.roll(ga[C:2 * C], HALF, axis=1)], axis=0)
            a_list.append(ga_sw * mk_ref[base + MK_ABD])
        yield
        xs = [mk_ref[MK_EYE] + a * mk_ref[d * MK_PER_DIR + MK_L1] for a, d in zip(a_list, ds)]
        for li in range(N_LV):
            t1 = [_bdot(x, a * mk_ref[d * MK_PER_DIR + MK_LV + li]) for x, a, d in zip(xs, a_list, ds)]
            yield
            xs = [x + _bdot(t, x) for x, t in zip(xs, t1)]
            yield
        for (n, d), x, a2, p in zip(probs, xs, a2s, ps):
            wtp = _bdot(x, jnp.concatenate([a2.astype(F32), p], axis=1))
            s_w[d, n] = wtp[:, 0:LANES].astype(BF16)
            s_tp[d, n] = jnp.where(lo64, wtp[0:C, LANES:2 * LANES], wtp[C:2 * C, LANES:2 * LANES])
        yield

    def p2_stages(steps):
        for ns in steps:
            rows = [pl.ds(pl.multiple_of(ns[d] * C, C), C) for d in range(2)]
            st = [s_state[d] for d in range(2)]
            xs = [_bdot_nt(jnp.concatenate([s_w[d, ns[d]], s_rt[d][rows[d], :]], axis=0), st[d]) for d in range(2)]
            yield
            uv = [jnp.concatenate([(xs[d][0:C] + xs[d][C:2 * C] + s_tp[d, ns[d]]).astype(BF16), s_v[rows[d], :]], axis=0)
                  for d in range(2)]
            yb = [_bdot(s_gr[d, ns[d]], uv[d]) for d in range(2)]
            for d in range(2):
                s_y[d][rows[d], :] = xs[d][2 * C:3 * C] + jnp.where(lo64, yb[d][0:C], yb[d][C:2 * C])
                bk = jnp.concatenate([s_bh[d][rows[d], :], s_kh[d][rows[d], :]], axis=0)
                s_state[d] = st[d] * s_et[d, pl.ds(ns[d], 1), :] + bd_ones * _bdot_tn(uv[d], bk)
            yield

    def scan(tseq):
        nch = tseq // C
        per = min(p1_chunks, nch)
        ng = nch // per

        def probs(g):
            return [(g * per + j, 0) for j in range(per)] + [(nch - 1 - (g * per + j), 1) for j in range(per)]

        def steps(g):
            return [(g * per + j, nch - 1 - (g * per + j)) for j in range(per)]

        _interleave(p1_stages(probs(0)))
        if ng > 1:
            def body(g, carry):
                _interleave(p1_stages(probs(g + 1)), p2_stages(steps(g)))
                return carry

            lax.fori_loop(0, ng - 1, body, 0)
        _interleave(p2_stages(steps(ng - 1)))

    def epilogue(o_ref, tseq):
        R = min(256, tseq)
        lo = lax.broadcasted_iota(jnp.int32, (R, LANES), 1) < HALF

        def tile(it, carry):
            s = pl.multiple_of(it * R, R)
            rows = pl.ds(s, R)
            y = s_yf[rows, :] + s_yb[rows, :]
            mu = _seg_sum(y, lo) * (1.0 / HALF)
            yc = y - mu
            var = _seg_sum(yc * yc, lo) * (1.0 / HALF)
            yn = yc * lax.rsqrt(var + RWKV_GN_EPS) * pvrow(PV_GNG) + pvrow(PV_GNB)
            g = s_g[rows, :]
            o_ref[0, rows, :] = (yn + s_bonus[rows, :]) * (g * jax.nn.sigmoid(g))
            return carry

        lax.fori_loop(0, tseq // R, tile, 0)

    s_state[...] = jnp.zeros_like(s_state)
    for (xr, xk, xv, xg, xl, o_ref, tseq) in ((cr, ck, cv, cg, cl, o_ctx, t_ctx), (lr, lk, lv, lg, ll, o_lat, t_lat)):
        prep(xr, xk, xv, xg, xl, tseq)
        scan(tseq)
        epilogue(o_ref, tseq)


def _col_block_map(b, p, *, off):
    return (b, 0, off + p)


def rwkv_mixer(u_lat, u_ctx, pv, mixl, wup, aup, p1_chunks=8):
    bsz, t_lat, _ = u_lat.shape
    t_ctx = u_ctx.shape[1]
    npair = RWKV_HEADS // 2
    c0 = EV_RWKV_COL // LANES
    nch = t_lat // RWKV_CHUNK
    masks = jnp.asarray(rwkv_masks())

    def seq_specs(t):
        return [pl.BlockSpec((1, t, LANES), functools.partial(_col_block_map, off=c0 + o * npair)) for o in range(4)] + [
            pl.BlockSpec((1, t, LANES), lambda b, p: (b, 0, EV_LORA_COL // LANES))]

    in_specs = seq_specs(t_lat) + seq_specs(t_ctx) + [
        pl.BlockSpec((PV_ROWS, LANES), lambda b, p: (0, p)),
        pl.BlockSpec((8, LANES), lambda b, p: (0, 0)),
        pl.BlockSpec((2, LANES, LANES), lambda b, p: (0, 0, p)),
        pl.BlockSpec((2, LANES, LANES), lambda b, p: (0, 0, p)),
        pl.BlockSpec((N_MASKS, LANES, LANES), lambda b, p: (0, 0, 0)),
    ]
    out_specs = [pl.BlockSpec((1, t_lat, LANES), lambda b, p: (b, 0, p)),
                 pl.BlockSpec((1, t_ctx, LANES), lambda b, p: (b, 0, p))]
    scratch = ([pltpu.VMEM((t_lat, LANES), BF16) for _ in range(13)] + [pltpu.VMEM((t_lat, LANES), F32) for _ in range(4)]
               + [pltpu.VMEM((2, nch, LANES, LANES), BF16), pltpu.VMEM((2, nch, RWKV_CHUNK, LANES), F32),
                  pltpu.VMEM((2, nch, LANES, LANES), BF16)]
               + [pltpu.VMEM((2, nch, LANES), F32), pltpu.VMEM((2, LANES, LANES), F32)])
    kern = functools.partial(_rwkv_kernel, t_lat=t_lat, t_ctx=t_ctx, p1_chunks=p1_chunks)
    return pl.pallas_call(
        kern,
        grid=(bsz, npair),
        in_specs=in_specs,
        out_specs=out_specs,
        out_shape=[jax.ShapeDtypeStruct((bsz, t_lat, RWKV_WIDTH), F32), jax.ShapeDtypeStruct((bsz, t_ctx, RWKV_WIDTH), F32)],
        scratch_shapes=scratch,
        compiler_params=pltpu.CompilerParams(dimension_semantics=("arbitrary", "arbitrary"),
                                             vmem_limit_bytes=V7X_VMEM_LIMIT_BYTES),
        name="rwkv7_chunked",
    )(*([u_lat] * 5), *([u_ctx] * 5), pv, mixl, wup, aup, masks)


def rwkv_params(mix, w0, w_up, a0, a_up, k_k, k_a, r_k, gn_g, gn_b):
    wd = RWKV_WIDTH
    rows = [mix[0, 0:wd], mix[1, 0:wd], mix[0, wd:2 * wd], mix[1, wd:2 * wd], mix[0, 2 * wd:3 * wd], mix[1, 2 * wd:3 * wd],
            mix[0, 3 * wd:4 * wd], mix[1, 3 * wd:4 * wd], w0[0], w0[1], a0[0], a0[1], k_k, k_a, r_k.reshape(-1), gn_g, gn_b]
    pv = jnp.concatenate([jnp.stack(rows), jnp.zeros((PV_ROWS - len(rows), wd), F32)], 0)
    mixl = jnp.concatenate([mix[:, 4 * wd:], jnp.zeros((6, LANES), F32)], 0)
    z = jnp.zeros((2, HALF, wd), F32)
    wup = jnp.concatenate([w_up, z], 1)
    aup = jnp.concatenate([z, a_up], 1)
    return pv, mixl, wup, aup


GP_CONV_Q, GP_CONV_K, GP_CONV_V = 0, 5, 10
GP_NORM, GP_ALOG0, GP_ALOG1, GP_DTB0, GP_DTB1 = 15, 16, 17, 18, 19
GP_ROWS = 24
GM_STRICT, GM_INCL, GM_L1, GM_LV = 0, 1, 2, 3
GN_LV = 6
GM_PER_DIR = GM_LV + GN_LV
GM_EYE = 2 * GM_PER_DIR
GN_MASKS = GM_EYE + 1
GDN_P1_CHUNKS = 4


def gdn_masks():
    i = np.arange(GDN_CHUNK)[:, None]
    j = np.arange(GDN_CHUNK)[None, :]
    out = np.zeros((GN_MASKS, GDN_CHUNK, GDN_CHUNK), np.float32)
    for d in range(2):
        base = d * GM_PER_DIR
        out[base + GM_STRICT] = (j < i) if d == 0 else (j > i)
        out[base + GM_INCL] = (j <= i) if d == 0 else (j >= i)
        for li in range(GN_LV + 1):
            s = 1 << li
            blk = (i // (2 * s)) == (j // (2 * s))
            m = blk & (((i & s) != 0) & ((j & s) == 0) if d == 0 else ((i & s) == 0) & ((j & s) != 0))
            out[base + (GM_L1 if li == 0 else GM_LV + li - 1)] = m
    out[GM_EYE] = i == j
    return out


def _gdn_kernel(lq, lk, lv, lz, lbg, cq, ck, cv, cz, cbg, gp_ref, mk_ref, o_lat, o_ctx,
                s_q, s_k, s_kb0, s_kb1, s_kbg0, s_kbg1, s_qg0, s_qg1, s_kg0, s_kg1, s_vb0, s_vb1,
                s_gc0, s_gc1, s_of, s_ob, s_tk, s_tvb, s_a, s_gl, s_state, *, t_lat, t_ctx):
    C = GDN_CHUNK
    s_kb, s_kbg, s_qg, s_kg, s_vb = (s_kb0, s_kb1), (s_kbg0, s_kbg1), (s_qg0, s_qg1), (s_kg0, s_kg1), (s_vb0, s_vb1)
    s_gc, s_o = (s_gc0, s_gc1), (s_of, s_ob)

    def gprow(i):
        return gp_ref[pl.ds(i, 1), :]

    def prep(xq, xk, xv, xbg, tseq):
        R = min(256, tseq)
        N = R + 16

        def tile(it, carry):
            s = pl.multiple_of(it * R, R)
            has_prev = jnp.where(s > 0, 1.0, 0.0).astype(F32)
            has_next = jnp.where(s + R < tseq, 1.0, 0.0).astype(F32)
            pidx = pl.multiple_of(jnp.maximum(s - 8, 0), 8)
            nidx = pl.multiple_of(jnp.minimum(s + R, tseq - 8), 8)

            def conv_silu(ref, row0):
                xe = jnp.concatenate([ref[0, pl.ds(pidx, 8), :] * has_prev, ref[0, pl.ds(s, R), :],
                                      ref[0, pl.ds(nidx, 8), :] * has_next], axis=0)
                acc = None
                for j in range(5):
                    sh = (N - (j - 2)) % N
                    xs = xe if sh == 0 else pltpu.roll(xe, sh, axis=0)
                    term = gprow(row0 + j) * xs[8:8 + R]
                    acc = term if acc is None else acc + term
                return acc * jax.nn.sigmoid(acc)

            qc = conv_silu(xq, GP_CONV_Q)
            kc = conv_silu(xk, GP_CONV_K)
            v = conv_silu(xv, GP_CONV_V)
            q = qc * (lax.rsqrt(jnp.sum(qc * qc, axis=1, keepdims=True) + 1e-6) * (LANES ** -0.5))
            k = kc * lax.rsqrt(jnp.sum(kc * kc, axis=1, keepdims=True) + 1e-6)
            bg = xbg[0, 0, pl.ds(s, R), :]
            s_q[pl.ds(s, R), :] = q.astype(BF16)
            s_k[pl.ds(s, R), :] = k.astype(BF16)
            for d in range(2):
                beta = jax.nn.sigmoid(bg[:, d:d + 1])
                g = -jnp.exp(gprow(GP_ALOG0 + d)) * jax.nn.softplus(bg[:, 2 + d:3 + d] + gprow(GP_DTB0 + d))
                c_fwd = _chunk_cumsum(g, False, C)
                tot = jnp.concatenate([jnp.broadcast_to(c_fwd[(c + 1) * C - 1:(c + 1) * C, :], (C, LANES))
                                       for c in range(R // C)], axis=0)
                c_bwd = tot - c_fwd + g
                gc, rest = (c_fwd, c_bwd - g) if d == 0 else (c_bwd, c_fwd - g)
                eg = jnp.exp(gc)
                kb = k * beta
                s_kb[d][pl.ds(s, R), :] = kb.astype(BF16)
                s_kbg[d][pl.ds(s, R), :] = (kb * eg).astype(BF16)
                s_qg[d][pl.ds(s, R), :] = (q * eg).astype(BF16)
                s_kg[d][pl.ds(s, R), :] = (k * jnp.exp(rest)).astype(BF16)
                s_vb[d][pl.ds(s, R), :] = (v * beta).astype(BF16)
                s_gc[d][pl.ds(s, R), :] = gc
                for c in range(R // C):
                    last = c * C + (C - 1 if d == 0 else 0)
                    s_gl[d, pl.ds(it * (R // C) + c, 1), :] = eg[last:last + 1, :]
            return carry

        lax.fori_loop(0, tseq // R, tile, 0)

    def p1_stages(probs):
        ds = [d for _, d in probs]
        rows = [pl.ds(pl.multiple_of(n * C, C), C) for n, _ in probs]
        gms = [_bdot_nt(jnp.concatenate([s_kb[d][rw, :], s_q[rw, :]], axis=0), s_k[rw, :])
               for (n, d), rw in zip(probs, rows)]
        yield
        a_list = []
        for (n, d), rw, gm in zip(probs, rows, gms):
            base = d * GM_PER_DIR
            gc = s_gc[d][rw, :]
            m_incl = mk_ref[base + GM_INCL]
            dec = jnp.exp((gc - gc.T) * m_incl)
            s_a[d, n] = (gm[C:2 * C] * dec * m_incl).astype(BF16)
            a_list.append(-(gm[0:C] * dec * mk_ref[base + GM_STRICT]))
        xs = [mk_ref[GM_EYE] + a * mk_ref[d * GM_PER_DIR + GM_L1] for a, d in zip(a_list, ds)]
        for li in range(GN_LV):
            t1 = [_bdot(x, a * mk_ref[d * GM_PER_DIR + GM_LV + li]) for x, a, d in zip(xs, a_list, ds)]
            yield
            xs = [x + _bdot(t, x) for x, t in zip(xs, t1)]
            yield
        for (n, d), rw, x in zip(probs, rows, xs):
            tkv = _bdot(x, jnp.concatenate([s_kbg[d][rw, :], s_vb[d][rw, :]], axis=1))
            s_tk[d, n] = tkv[:, 0:LANES].astype(BF16)
            s_tvb[d, n] = tkv[:, LANES:2 * LANES]
        yield

    def p2_stages(steps):
        for ns in steps:
            rows = [pl.ds(pl.multiple_of(ns[d] * C, C), C) for d in range(2)]
            st = [s_state[d] for d in range(2)]
            aq = [_bdot(jnp.concatenate([s_tk[d, ns[d]], s_qg[d][rows[d], :]], axis=0), st[d]) for d in range(2)]
            yield
            vnew = [s_tvb[d, ns[d]] - aq[d][0:C] for d in range(2)]
            oo = [aq[d][C:2 * C] + _bdot(s_a[d, ns[d]], vnew[d]) for d in range(2)]
            for d in range(2):
                s_o[d][rows[d], :] = oo[d]
                s_state[d] = st[d] * s_gl[d, pl.ds(ns[d], 1), :] + _bdot_tn(s_kg[d][rows[d], :], vnew[d])
            yield

    def scan(tseq):
        nch = tseq // C
        per = min(GDN_P1_CHUNKS, nch)
        ng = nch // per

        def probs(g):
            return [(g * per + j, 0) for j in range(per)] + [(nch - 1 - (g * per + j), 1) for j in range(per)]

        def steps(g):
            return [(g * per + j, nch - 1 - (g * per + j)) for j in range(per)]

        _interleave(p1_stages(probs(0)))
        if ng > 1:
            def body(g, carry):
                _interleave(p1_stages(probs(g + 1)), p2_stages(steps(g)))
                return carry

            lax.fori_loop(0, ng - 1, body, 0)
        _interleave(p2_stages(steps(ng - 1)))

    def epilogue(o_ref, z_ref, tseq):
        R = min(256, tseq)

        def tile(it, carry):
            s = pl.multiple_of(it * R, R)
            rows = pl.ds(s, R)
            o = s_of[rows, :] + s_ob[rows, :]
            o = o * lax.rsqrt(jnp.mean(o * o, axis=1, keepdims=True) + 1e-6) * gprow(GP_NORM)
            z = z_ref[0, rows, :]
            o_ref[0, rows, :] = o * (z * jax.nn.sigmoid(z))
            return carry

        lax.fori_loop(0, tseq // R, tile, 0)

    s_state[...] = jnp.zeros_like(s_state)
    for (xq, xk, xv, xz, xbg, o_ref, tseq) in ((cq, ck, cv, cz, cbg, o_ctx, t_ctx), (lq, lk, lv, lz, lbg, o_lat, t_lat)):
        prep(xq, xk, xv, xbg, tseq)
        scan(tseq)
        epilogue(o_ref, xz, tseq)


def gdn_mixer(u_lat, bg_lat, u_ctx, bg_ctx, gp):
    bsz, t_lat, _ = u_lat.shape
    t_ctx = u_ctx.shape[1]
    nh = GDN_HEADS
    masks = jnp.asarray(gdn_masks())

    def seq_specs(t):
        return [pl.BlockSpec((1, t, LANES), functools.partial(_col_block_map, off=o * nh)) for o in range(4)] + [
            pl.BlockSpec((1, 1, t, 4), lambda b, h: (b, h, 0, 0))]

    in_specs = seq_specs(t_lat) + seq_specs(t_ctx) + [
        pl.BlockSpec((GP_ROWS, LANES), lambda b, h: (0, h)),
        pl.BlockSpec((GN_MASKS, LANES, LANES), lambda b, h: (0, 0, 0)),
    ]
    out_specs = [pl.BlockSpec((1, t_lat, LANES), lambda b, h: (b, 0, h)),
                 pl.BlockSpec((1, t_ctx, LANES), lambda b, h: (b, 0, h))]
    nch = t_lat // GDN_CHUNK
    scratch = ([pltpu.VMEM((t_lat, LANES), BF16) for _ in range(12)] + [pltpu.VMEM((t_lat, LANES), F32) for _ in range(4)]
               + [pltpu.VMEM((2, nch, LANES, LANES), BF16), pltpu.VMEM((2, nch, LANES, LANES), F32),
                  pltpu.VMEM((2, nch, LANES, LANES), BF16),
                  pltpu.VMEM((2, max(nch, 8), LANES), F32), pltpu.VMEM((2, LANES, LANES), F32)])
    kern = functools.partial(_gdn_kernel, t_lat=t_lat, t_ctx=t_ctx)
    return pl.pallas_call(
        kern,
        grid=(bsz, nh),
        in_specs=in_specs,
        out_specs=out_specs,
        out_shape=[jax.ShapeDtypeStruct((bsz, t_lat, GDN_WIDTH), F32), jax.ShapeDtypeStruct((bsz, t_ctx, GDN_WIDTH), F32)],
        scratch_shapes=scratch,
        compiler_params=pltpu.CompilerParams(dimension_semantics=("arbitrary", "arbitrary"),
                                             vmem_limit_bytes=V7X_VMEM_LIMIT_BYTES),
        name="gdn_chunked",
    )(*([u_lat] * 4), bg_lat, *([u_ctx] * 4), bg_ctx, gp, masks)


def gdn_params(conv_w, a_log, dt_bias, norm_g):
    wd = GDN_WIDTH
    rep = lambda x: jnp.repeat(x, LANES)
    rows = [conv_w[j, o * wd:(o + 1) * wd] for o in range(3) for j in range(5)]
    rows += [jnp.tile(norm_g, GDN_HEADS), rep(a_log[0]), rep(a_log[1]), rep(dt_bias[0]), rep(dt_bias[1])]
    return jnp.concatenate([jnp.stack(rows), jnp.zeros((GP_ROWS - len(rows), wd), F32)], 0)


def gdn_bg(u_small):
    bsz, t, _ = u_small.shape
    x = u_small.reshape(bsz, t, 2, 2, GDN_HEADS)
    return jnp.transpose(x, (0, 4, 1, 2, 3)).reshape(bsz, GDN_HEADS, t, 4)


OD_Q_COL, OD_GATE_COL, OD_KV_COL = 0, ATT_WIDTH, 2 * ATT_WIDTH
OD_KV_W = 4 * ATT_HEAD
OD_COLS = OD_KV_COL + ATT_KV_HEADS * OD_KV_W


def rope_lane_tables(t):
    pos = np.arange(t)
    inv = ROPE_BASE ** (-np.arange(ROPE_FREQS, dtype=np.float64) / ROPE_FREQS)
    ang_row = (pos // GRID_W)[:, None] * inv
    ang_col = (pos % GRID_W)[:, None] * inv
    ang = np.concatenate([ang_row, ang_row, ang_col, ang_col] * 2, axis=1)
    first = (np.arange(LANES) % (2 * ROPE_FREQS)) < ROPE_FREQS
    cos, sin = np.cos(ang), np.sin(ang)
    return (jnp.asarray(cos, F32), jnp.asarray(np.where(first, -sin, 0.0), F32), jnp.asarray(np.where(first, 0.0, sin), F32))


def _rope(x, cos, sina, sinb):
    n = x.shape[1]
    return x * cos + pltpu.roll(x, n - ROPE_FREQS, axis=1) * sina + pltpu.roll(x, ROPE_FREQS, axis=1) * sinb


def _kv_rope_kernel(u_ref, cos_ref, sina_ref, sinb_ref, o_ref):
    x = u_ref[0]
    cos, sina, sinb = cos_ref[...], sina_ref[...], sinb_ref[...]
    parts = []
    for h in range(ATT_KV_HEADS):
        k2 = x[:, h * OD_KV_W:h * OD_KV_W + LANES]
        parts += [_rope(k2, cos, sina, sinb), x[:, h * OD_KV_W + LANES:(h + 1) * OD_KV_W]]
    o_ref[0] = jnp.concatenate(parts, axis=1).astype(BF16)


def kv_rope(u, tables, tr=512):
    bsz, t, _ = u.shape
    tr = min(tr, t)
    kvw = ATT_KV_HEADS * OD_KV_W
    tab = pl.BlockSpec((tr, LANES), lambda b, i: (i, 0))
    return pl.pallas_call(
        _kv_rope_kernel,
        grid=(bsz, t // tr),
        in_specs=[pl.BlockSpec((1, tr, kvw), lambda b, i: (b, i, OD_KV_COL // kvw)), tab, tab, tab],
        out_specs=pl.BlockSpec((1, tr, kvw), lambda b, i: (b, i, 0)),
        out_shape=jax.ShapeDtypeStruct((bsz, t, kvw), BF16),
        compiler_params=pltpu.CompilerParams(dimension_semantics=("arbitrary", "arbitrary")),
        name="kv_rope",
    )(u, *tables)


def _attn_kernel(sink_ref, *refs, has_local, nb):
    if has_local:
        q_ref, g_ref, kvp_ref, kvc_ref, kvn_ref, cos_ref, sina_ref, sinb_ref, kvx_ref, o_ref = refs
    else:
        q_ref, g_ref, kvx_ref, o_ref = refs
    tq = q_ref.shape[1]
    i = pl.program_id(1)
    q = q_ref[0]
    if has_local:
        rep = ATT_WIDTH // LANES
        cos, sina, sinb = (jnp.concatenate([r[...]] * rep, axis=1) for r in (cos_ref, sina_ref, sinb_ref))
        q = _rope(q, cos, sina, sinb)
    q = q * (ATT_HEAD ** -0.5)
    lo = lax.broadcasted_iota(jnp.int32, (tq, LANES), 1) < HALF
    if has_local:
        a = lax.broadcasted_iota(jnp.int32, (tq, WINDOW), 0)
        c = lax.broadcasted_iota(jnp.int32, (tq, WINDOW), 1)
        ok_prev = (c >= a) & (i > 0)
        ok_next = (c <= a) & (i < nb - 1)
        always = jnp.ones((tq, tq + kvx_ref.shape[1]), jnp.bool_)
        valid = jnp.concatenate([ok_prev, always[:, :tq], ok_next, always[:, tq:]], axis=1)
    outs = []
    for hk in range(ATT_KV_HEADS):
        kcol = slice(hk * OD_KV_W, hk * OD_KV_W + LANES)
        vcol = slice(hk * OD_KV_W + LANES, (hk + 1) * OD_KV_W)
        qs = []
        for pair in range(ATT_GROUP // 2):
            grp = q[:, (hk * (ATT_GROUP // 2) + pair) * LANES:(hk * (ATT_GROUP // 2) + pair + 1) * LANES]
            qs += [jnp.where(lo, grp, 0.0), jnp.where(lo, 0.0, grp)]
        lhs = jnp.concatenate(qs, axis=0).astype(BF16)
        kx, vx = kvx_ref[0, :, kcol].astype(BF16), kvx_ref[0, :, vcol].astype(BF16)
        if has_local:
            kall = jnp.concatenate([kvp_ref[0, :, kcol], kvc_ref[0, :, kcol], kvn_ref[0, :, kcol], kx], axis=0)
            vall = jnp.concatenate([kvp_ref[0, :, vcol], kvc_ref[0, :, vcol], kvn_ref[0, :, vcol], vx], axis=0)
        else:
            kall, vall = kx, vx
        s = _bdot_nt(lhs, kall)
        lov = lax.broadcasted_iota(jnp.int32, vall.shape, 1) < HALF
        zv = jnp.zeros_like(vall)
        v2 = jnp.concatenate([jnp.where(lov, vall, zv), jnp.where(lov, zv, vall)], axis=0)
        for pair in range(ATT_GROUP // 2):
            ps, inv_l = [], []
            for sub in range(2):
                g = 2 * pair + sub
                sg = s[g * tq:(g + 1) * tq]
                if has_local:
                    sg = jnp.where(valid, sg, NEG_INF)
                sink = sink_ref[hk, g]
                m = jnp.maximum(jnp.max(sg, axis=1, keepdims=True), sink)
                p = jnp.exp(sg - m)
                inv_l.append(1.0 / (jnp.sum(p, axis=1, keepdims=True) + jnp.exp(sink - m)))
                ps.append(p.astype(BF16))
            o_pair = jnp.dot(jnp.concatenate(ps, axis=1), v2, preferred_element_type=F32)
            outs.append(o_pair * jnp.where(lo, inv_l[0], inv_l[1]))
    g = g_ref[0]
    o_ref[0] = jnp.concatenate(outs, axis=1) * (g * jax.nn.sigmoid(g))


def sink_attention(u, kv_local, tables, u_ctx, sink):
    bsz, t, _ = u.shape
    tq = ATT_BLOCK
    nb = t // tq
    tx = u_ctx.shape[1]
    kvw = ATT_KV_HEADS * OD_KV_W
    has_local = kv_local is not None
    in_specs = [pl.BlockSpec(memory_space=pltpu.SMEM),
                pl.BlockSpec((1, tq, ATT_WIDTH), lambda b, i: (b, i, OD_Q_COL // ATT_WIDTH)),
                pl.BlockSpec((1, tq, ATT_WIDTH), lambda b, i: (b, i, OD_GATE_COL // ATT_WIDTH))]
    args = [sink.reshape(ATT_KV_HEADS, ATT_GROUP), u, u]
    if has_local:
        in_specs += [pl.BlockSpec((1, tq, kvw), lambda b, i: (b, jnp.maximum(i - 1, 0), 0)),
                     pl.BlockSpec((1, tq, kvw), lambda b, i: (b, i, 0)),
                     pl.BlockSpec((1, tq, kvw), lambda b, i: (b, jnp.minimum(i + 1, nb - 1), 0))]
        in_specs += [pl.BlockSpec((tq, LANES), lambda b, i: (i, 0))] * 3
        args += [kv_local] * 3 + list(tables)
    in_specs += [pl.BlockSpec((1, tx, kvw), lambda b, i: (b, 0, OD_KV_COL // kvw))]
    args += [u_ctx]
    return pl.pallas_call(
        functools.partial(_attn_kernel, has_local=has_local, nb=nb),
        grid=(bsz, nb),
        in_specs=in_specs,
        out_specs=pl.BlockSpec((1, tq, ATT_WIDTH), lambda b, i: (b, i, 0)),
        out_shape=jax.ShapeDtypeStruct((bsz, t, ATT_WIDTH), F32),
        compiler_params=pltpu.CompilerParams(dimension_semantics=("arbitrary", "arbitrary"),
                                             vmem_limit_bytes=V7X_VMEM_LIMIT_BYTES),
        name="sink_attention",
    )(*args)


def pack_odd_w_in(w_in):
    nq, nkv = ATT_WIDTH, ATT_KV_HEADS * ATT_HEAD
    parts = [w_in[:, :nq], w_in[:, nq + 2 * nkv:]]
    for h in range(ATT_KV_HEADS):
        k = w_in[:, nq + h * ATT_HEAD:nq + (h + 1) * ATT_HEAD]
        v = w_in[:, nq + nkv + h * ATT_HEAD:nq + nkv + (h + 1) * ATT_HEAD]
        parts += [k, k, v, v]
    return jnp.concatenate(parts, 1).astype(BF16)


def pack_even_w_in(w_in):
    g1 = 4 * GDN_WIDTH
    r0 = GDN_IN
    parts = [w_in[:, :g1], w_in[:, r0:r0 + 4 * RWKV_WIDTH], w_in[:, r0 + 4 * RWKV_WIDTH:], w_in[:, g1:GDN_IN],
             jnp.zeros((D_MODEL, EV_COLS - EV_SMALL_COL - 4 * GDN_HEADS), F32)]
    return jnp.concatenate(parts, 1).astype(BF16)


def kernel(x, c, ctx, c_ctx, mod_w, mod_b, ln_g, ln_b, ev_w_in, ev_w_out, gdn_conv, gdn_a_log, gdn_dt_bias,
           gdn_norm_g, rwkv_mix, rwkv_w0, rwkv_w_up, rwkv_a0, rwkv_a_up, rwkv_k_k, rwkv_k_a, rwkv_r_k,
           rwkv_gn_g, rwkv_gn_b, od_w_in, od_w_out, od_sink):
    bsz, t_lat, d = x.shape
    t_ctx = ctx.shape[1]
    x_lat, x_ctx = x, ctx
    n_mod = -(-(bsz + 1) // 8) * 8
    cc = jax.nn.silu(jnp.concatenate([c, c_ctx[None], jnp.zeros((n_mod - bsz - 1, d), F32)], 0))
    mods = mod_all_layers(cc, mod_w, mod_b)
    for layer in range(DEPTH):
        need_ctx = layer < DEPTH - 1
        i = layer // 2
        mod = mods[layer]
        sh, sc, gt = (mod[:bsz, j * d:(j + 1) * d][:, None, :] for j in range(3))
        sh_c, sc_c, gt_c = (jnp.broadcast_to(mod[bsz, j * d:(j + 1) * d], (bsz, 1, d)) for j in range(3))
        if layer % 2 == 0:
            w_in = pack_even_w_in(ev_w_in[i])
            w_out = ev_w_out[i].astype(BF16)
            u_lat = inproj(x_lat, sc, sh, w_in, 1024, EV_TN)
            u_ctx = inproj(x_ctx.reshape(1, bsz * t_ctx, d), sc_c[:1], sh_c[:1], w_in, 1024, EV_TN).reshape(bsz, t_ctx, -1)
            small = slice(EV_SMALL_COL, EV_SMALL_COL + 4 * GDN_HEADS)
            og_lat, og_ctx = gdn_mixer(u_lat, gdn_bg(u_lat[..., small]), u_ctx, gdn_bg(u_ctx[..., small]),
                                       gdn_params(gdn_conv[i], gdn_a_log[i], gdn_dt_bias[i], gdn_norm_g[i]))
            or_lat, or_ctx = rwkv_mixer(u_lat, u_ctx, *rwkv_params(
                rwkv_mix[i], rwkv_w0[i], rwkv_w_up[i], rwkv_a0[i], rwkv_a_up[i], rwkv_k_k[i], rwkv_k_a[i],
                rwkv_r_k[i], rwkv_gn_g[i], rwkv_gn_b[i]))
            a_lat, a_ctx = [og_lat, or_lat], [og_ctx, or_ctx]
        else:
            w_in = pack_odd_w_in(od_w_in[i])
            w_out = od_w_out[i].astype(BF16)
            u_lat = inproj(x_lat, sc, sh, w_in, 1024, OD_TN)
            u_ctx = inproj(x_ctx.reshape(1, bsz * t_ctx, d), sc_c[:1], sh_c[:1], w_in, 1024, OD_TN).reshape(bsz, t_ctx, -1)
            tables = rope_lane_tables(t_lat)
            o_lat = sink_attention(u_lat, kv_rope(u_lat, tables), tables, u_ctx, od_sink[i])
            o_ctx = sink_attention(u_ctx, None, None, u_ctx, od_sink[i]) if need_ctx else None
            a_lat, a_ctx = [o_lat], [o_ctx]
        x_lat = outproj_ln(a_lat, w_out, x_lat, gt, ln_g[layer], ln_b[layer])
        if need_ctx:
            x_ctx = outproj_ln(a_ctx, w_out, x_ctx, gt_c, ln_g[layer], ln_b[layer])
    return x_lat
```

```python
import functools

import jax
import jax.numpy as jnp
import numpy as np
from jax import lax
from jax.experimental import pallas as pl
from jax.experimental.pallas import tpu as pltpu

F32 = jnp.float32
BF16 = jnp.bfloat16

D_MODEL = 2048
DEPTH = 4
GRID_W = 64
GDN_WIDTH = D_MODEL // 2
GDN_HEADS = 8
GDN_CHUNK = 128
RWKV_WIDTH = D_MODEL // 2
RWKV_HEADS = 16
RWKV_CHUNK = 64
RWKV_GN_EPS = 64e-5
ATT_HEAD = 64
ATT_Q_HEADS = D_MODEL // ATT_HEAD
ATT_KV_HEADS = ATT_Q_HEADS // 8
ATT_GROUP = ATT_Q_HEADS // ATT_KV_HEADS
ATT_WIDTH = ATT_Q_HEADS * ATT_HEAD
WINDOW = 128
ATT_BLOCK = 128
ROPE_BASE = 10000.0
ROPE_FREQS = ATT_HEAD // 4
NEG_INF = -1e30
DEEPNORM_ALPHA = (2 * DEPTH) ** 0.25
LN_EPS = 1e-5
GDN_IN = 4 * GDN_WIDTH + 4 * GDN_HEADS
RWKV_IN = 4 * RWKV_WIDTH + 128

LANES = 128
HALF = 64
V7X_VMEM_LIMIT_BYTES = 56 * 1024 * 1024
EV_GDN_COL = 0
EV_RWKV_COL = 4 * GDN_WIDTH
EV_LORA_COL = EV_RWKV_COL + 4 * RWKV_WIDTH
EV_SMALL_COL = EV_LORA_COL + LANES
EV_COLS = EV_SMALL_COL + LANES
EV_TN = 1408
OD_TN = 1280


def _mod_kernel(a_ref, w_ref, b_ref, o_ref):
    o_ref[0] = jnp.dot(a_ref[...].astype(BF16), w_ref[0].astype(BF16), preferred_element_type=F32) + b_ref[0]


def mod_all_layers(a, w, b, tn=512):
    m, k = a.shape
    nl, _, n = w.shape
    return pl.pallas_call(
        _mod_kernel,
        grid=(nl, n // tn),
        in_specs=[pl.BlockSpec((m, k), lambda l, j: (0, 0)), pl.BlockSpec((1, k, tn), lambda l, j: (l, 0, j)),
                  pl.BlockSpec((1, 1, tn), lambda l, j: (l, 0, j))],
        out_specs=pl.BlockSpec((1, m, tn), lambda l, j: (l, 0, j)),
        out_shape=jax.ShapeDtypeStruct((nl, m, n), F32),
        compiler_params=pltpu.CompilerParams(dimension_semantics=("arbitrary", "arbitrary"),
                                             vmem_limit_bytes=V7X_VMEM_LIMIT_BYTES),
        name="mod_matmul",
    )(a, w, b.reshape(nl, 1, n))


def _inproj_kernel(x_ref, sc_ref, sh_ref, w_ref, o_ref, h_ref):
    @pl.when(pl.program_id(2) == 0)
    def _():
        h_ref[...] = (x_ref[0] * (1.0 + sc_ref[0]) + sh_ref[0]).astype(BF16)

    o_ref[0] = jnp.dot(h_ref[...], w_ref[...], preferred_element_type=F32)


def inproj(x, sc, sh, w, tm, tn):
    bsz, t, d = x.shape
    n = w.shape[1]
    tm = min(tm, t)
    assert t % tm == 0 and n % tn == 0
    return pl.pallas_call(
        _inproj_kernel,
        grid=(bsz, t // tm, n // tn),
        in_specs=[pl.BlockSpec((1, tm, d), lambda b, i, j: (b, i, 0)),
                  pl.BlockSpec((1, 1, d), lambda b, i, j: (b, 0, 0)),
                  pl.BlockSpec((1, 1, d), lambda b, i, j: (b, 0, 0)),
                  pl.BlockSpec((d, tn), lambda b, i, j: (0, j))],
        out_specs=pl.BlockSpec((1, tm, tn), lambda b, i, j: (b, i, j)),
        out_shape=jax.ShapeDtypeStruct((bsz, t, n), F32),
        scratch_shapes=[pltpu.VMEM((tm, d), BF16)],
        compiler_params=pltpu.CompilerParams(dimension_semantics=("arbitrary", "arbitrary", "arbitrary"),
                                             vmem_limit_bytes=V7X_VMEM_LIMIT_BYTES),
        name="inproj",
    )(x, sc, sh, w)


def _outproj_kernel(*refs, n_a):
    a_refs, (w_ref, x_ref, gt_ref, g_ref, b_ref, o_ref) = refs[:n_a], refs[n_a:]
    y = None
    k0 = 0
    for a_ref in a_refs:
        kw = a_ref.shape[2]
        term = jnp.dot(a_ref[0].astype(BF16), w_ref[k0:k0 + kw, :], preferred_element_type=F32)
        y = term if y is None else y + term
        k0 += kw
    z = DEEPNORM_ALPHA * x_ref[0] + gt_ref[0] * y
    mu = jnp.mean(z, axis=1, keepdims=True)
    zc = z - mu
    var = jnp.mean(zc * zc, axis=1, keepdims=True)
    o_ref[0] = zc * lax.rsqrt(var + LN_EPS) * g_ref[...] + b_ref[...]


def outproj_ln(a_list, w, x, gt, ln_g, ln_b, tm=512):
    bsz, t, d = x.shape
    tm = min(tm, t)
    in_specs = [pl.BlockSpec((1, tm, a.shape[2]), lambda b, i: (b, i, 0)) for a in a_list] + [
        pl.BlockSpec(w.shape, lambda b, i: (0, 0)),
        pl.BlockSpec((1, tm, d), lambda b, i: (b, i, 0)),
        pl.BlockSpec((1, 1, d), lambda b, i: (b, 0, 0)),
        pl.BlockSpec((1, d), lambda b, i: (0, 0)),
        pl.BlockSpec((1, d), lambda b, i: (0, 0))]
    return pl.pallas_call(
        functools.partial(_outproj_kernel, n_a=len(a_list)),
        grid=(bsz, t // tm),
        in_specs=in_specs,
        out_specs=pl.BlockSpec((1, tm, d), lambda b, i: (b, i, 0)),
        out_shape=jax.ShapeDtypeStruct((bsz, t, d), F32),
        compiler_params=pltpu.CompilerParams(dimension_semantics=("arbitrary", "arbitrary"),
                                             vmem_limit_bytes=V7X_VMEM_LIMIT_BYTES),
        name="outproj_ln",
    )(*a_list, w, x, gt, ln_g.reshape(1, d), ln_b.reshape(1, d))


def _bdot(a, b):
    return jnp.dot(a.astype(BF16), b.astype(BF16), preferred_element_type=F32)


def _bdot_nt(a, b):
    return lax.dot_general(a.astype(BF16), b.astype(BF16), (((1,), (1,)), ((), ())), preferred_element_type=F32)


def _bdot_tn(a, b):
    return lax.dot_general(a.astype(BF16), b.astype(BF16), (((0,), (0,)), ((), ())), preferred_element_type=F32)


(PV_MR0, PV_MR1, PV_MK0, PV_MK1, PV_MV0, PV_MV1, PV_MG0, PV_MG1, PV_W00, PV_W01, PV_A00, PV_A01,
 PV_KK, PV_KA, PV_RK, PV_GNG, PV_GNB) = range(17)
PV_ROWS = 24
MK_ABD, MK_P, MK_Y, MK_L1 = 0, 1, 2, 3
MK_LV = 4
N_LV = 5
MK_PER_DIR = MK_LV + N_LV
MK_EYE = 2 * MK_PER_DIR
MK_BD = MK_EYE + 1
N_MASKS = MK_BD + 1


def rwkv_masks():
    ri = np.arange(128)[:, None]
    ci = np.arange(128)[None, :]
    i, j = ri & 63, ci & 63
    same = (ri < 64) == (ci < 64)
    out = np.zeros((N_MASKS, 128, 128), np.float32)
    for d in range(2):
        strict = (j < i) if d == 0 else (j > i)
        incl = (j <= i) if d == 0 else (j >= i)
        base = d * MK_PER_DIR
        out[base + MK_ABD] = strict & same
        out[base + MK_P] = strict & (ci >= 64)
        out[base + MK_Y] = incl
        for li, s in enumerate((1, 2, 4, 8, 16, 32)):
            blk = (i // (2 * s)) == (j // (2 * s))
            if d == 0:
                m = blk & ((i & s) != 0) & ((j & s) == 0)
            else:
                m = blk & ((i & s) == 0) & ((j & s) != 0)
            out[base + (MK_L1 if li == 0 else MK_LV + li - 1)] = m & same
    out[MK_EYE] = ri == ci
    out[MK_BD] = same
    return out


def _seg_sum(x, lo):
    s0 = jnp.sum(jnp.where(lo, x, 0.0), axis=1, keepdims=True)
    s1 = jnp.sum(jnp.where(lo, 0.0, x), axis=1, keepdims=True)
    return jnp.where(lo, s0, s1)


def _chunk_cumsum(x, reverse, chunk):
    n = x.shape[0]
    rowm = lax.broadcasted_iota(jnp.int32, x.shape, 0) & (chunk - 1)
    s = 1
    while s < chunk:
        if not reverse:
            x = x + jnp.where(rowm >= s, pltpu.roll(x, s, axis=0), 0.0)
        else:
            x = x + jnp.where(rowm < chunk - s, pltpu.roll(x, n - s, axis=0), 0.0)
        s *= 2
    return x


def _interleave(*gens):
    live = list(gens)
    while live:
        for g in list(live):
            try:
                next(g)
            except StopIteration:
                live.remove(g)


def _rwkv_kernel(lr, lk, lv, lg, ll, cr, ck, cv, cg, cl, pv_ref, mixl_ref, wup_ref, aup_ref, mk_ref,
                 o_lat, o_ctx,
                 s_at0, s_at1, s_rt0, s_rt1, s_bt0, s_bt1, s_kt0, s_kt1, s_bh0, s_bh1, s_kh0, s_kh1, s_v,
                 s_g, s_bonus, s_yf, s_yb, s_w, s_tp, s_gr, s_et, s_state, *, t_lat, t_ctx, p1_chunks):
    C = RWKV_CHUNK
    s_at, s_rt, s_bt, s_kt, s_bh, s_kh = ((s_at0, s_at1), (s_rt0, s_rt1), (s_bt0, s_bt1), (s_kt0, s_kt1),
                                          (s_bh0, s_bh1), (s_kh0, s_kh1))
    s_y = (s_yf, s_yb)
    lo64 = lax.broadcasted_iota(jnp.int32, (C, LANES), 1) < HALF
    bd_ones = mk_ref[MK_BD]

    def pvrow(i):
        return pv_ref[pl.ds(i, 1), :]

    def prep(xr, xk, xv, xg, xl, tseq):
        R = min(256, tseq)
        lo = lax.broadcasted_iota(jnp.int32, (R, LANES), 1) < HALF

        def tile(it, carry):
            s = pl.multiple_of(it * R, R)
            rows = lax.broadcasted_iota(jnp.int32, (R, LANES), 0)
            has_prev = jnp.where(s > 0, 1.0, 0.0).astype(F32)
            has_next = jnp.where(s + R < tseq, 1.0, 0.0).astype(F32)
            pidx = jnp.maximum(s - 1, 0)
            nidx = jnp.minimum(s + R, tseq - 1)

            def shifted(ref, m0, m1):
                cur = ref[0, pl.ds(s, R), :]
                prow = ref[0, pl.ds(pidx, 1), :] * has_prev
                nrow = ref[0, pl.ds(nidx, 1), :] * has_next
                prev = jnp.where(rows == 0, prow, pltpu.roll(cur, 1, axis=0))
                nxt = jnp.where(rows == R - 1, nrow, pltpu.roll(cur, R - 1, axis=0))
                return cur + m0 * (prev - cur) + m1 * (nxt - cur)

            r = shifted(xr, pvrow(PV_MR0), pvrow(PV_MR1))
            k = shifted(xk, pvrow(PV_MK0), pvrow(PV_MK1))
            v = shifted(xv, pvrow(PV_MV0), pvrow(PV_MV1))
            g = shifted(xg, pvrow(PV_MG0), pvrow(PV_MG1))
            lor = shifted(xl, mixl_ref[0:1, :], mixl_ref[1:2, :])
            th = jnp.tanh(lor)
            kkf = k * pvrow(PV_KK)
            kk = kkf * lax.rsqrt(_seg_sum(kkf * kkf, lo) + 1e-6)
            ka = pvrow(PV_KA)
            kd_sum = None
            for d in range(2):
                w_pre = pvrow(PV_W00 + d) + _bdot(th, wup_ref[d])
                lw = -jnp.exp(-jax.nn.softplus(-w_pre) - 0.5)
                ag = jax.nn.sigmoid(pvrow(PV_A00 + d) + _bdot(lor, aup_ref[d]))
                kd = k * (1.0 + (ag - 1.0) * ka)
                bdv = kk * ag
                kd_sum = kd if kd_sum is None else kd_sum + kd
                c_fwd = _chunk_cumsum(lw, False, C)
                tot = jnp.concatenate([jnp.broadcast_to(c_fwd[(c + 1) * C - 1:(c + 1) * C, :], (C, LANES))
                                       for c in range(R // C)], axis=0)
                c_bwd = tot - c_fwd + lw
                cw, rest = (c_fwd, c_bwd - lw) if d == 0 else (c_bwd, c_fwd - lw)
                e_in = jnp.exp(-cw)
                e_out = jnp.exp(rest)
                s_at[d][pl.ds(s, R), :] = (-kk * jnp.exp(cw - lw)).astype(BF16)
                s_rt[d][pl.ds(s, R), :] = (r * jnp.exp(cw)).astype(BF16)
                s_bt[d][pl.ds(s, R), :] = (bdv * e_in).astype(BF16)
                s_kt[d][pl.ds(s, R), :] = (kd * e_in).astype(BF16)
                s_bh[d][pl.ds(s, R), :] = (bdv * e_out).astype(BF16)
                s_kh[d][pl.ds(s, R), :] = (kd * e_out).astype(BF16)
                for c in range(R // C):
                    last = c * C + (C - 1 if d == 0 else 0)
                    s_et[d, pl.ds(it * (R // C) + c, 1), :] = jnp.exp(cw[last:last + 1, :])
            s_v[pl.ds(s, R), :] = v.astype(BF16)
            s_g[pl.ds(s, R), :] = g
            s_bonus[pl.ds(s, R), :] = _seg_sum(r * kd_sum * pvrow(PV_RK), lo) * v
            return carry

        lax.fori_loop(0, tseq // R, tile, 0)

    def p1_stages(probs):
        ds = [d for _, d in probs]
        rows = [pl.ds(pl.multiple_of(n * C, C), C) for n, _ in probs]
        a2s, gms = [], []
        for (n, d), rw in zip(probs, rows):
            at, rt = s_at[d][rw, :], s_rt[d][rw, :]
            zero = jnp.zeros_like(at)
            a2 = jnp.concatenate([jnp.where(lo64, at, zero), jnp.where(lo64, zero, at)], axis=0)
            l1 = jnp.concatenate([a2, jnp.where(lo64, rt, zero), jnp.where(lo64, zero, rt)], axis=0)
            a2s.append(a2)
            gms.append(_bdot_nt(l1, jnp.concatenate([s_bt[d][rw, :], s_kt[d][rw, :]], axis=0)))
        yield
        ps, a_list = [], []
        for (n, d), rw, gm in zip(probs, rows, gms):
            base = d * MK_PER_DIR
            ga = gm[0:2 * C]
            v = s_v[rw, :]
            ps.append(_bdot(ga * mk_ref[base + MK_P], jnp.concatenate([v, v], axis=0)))
            s_gr[d, n] = (gm[2 * C:4 * C] * mk_ref[base + MK_Y]).astype(BF16)
            ga_sw = jnp.concatenate([ga[0:C], pltpu.roll(ga[C:2 * C], HALF, axis=1)], axis=0)
            a_list.append(ga_sw * mk_ref[base + MK_ABD])
        yield
        xs = [mk_ref[MK_EYE] + a * mk_ref[d * MK_PER_DIR + MK_L1] for a, d in zip(a_list, ds)]
        for li in range(N_LV):
            t1 = [_bdot(x, a * mk_ref[d * MK_PER_DIR + MK_LV + li]) for x, a, d in zip(xs, a_list, ds)]
            yield
            xs = [x + _bdot(t, x) for x, t in zip(xs, t1)]
            yield
        for (n, d), x, a2, p in zip(probs, xs, a2s, ps):
            wtp = _bdot(x, jnp.concatenate([a2.astype(F32), p], axis=1))
            s_w[d, n] = wtp[:, 0:LANES].astype(BF16)
            s_tp[d, n] = jnp.where(lo64, wtp[0:C, LANES:2 * LANES], wtp[C:2 * C, LANES:2 * LANES])
        yield

    def p2_stages(steps):
        for ns in steps:
            rows = [pl.ds(pl.multiple_of(ns[d] * C, C), C) for d in range(2)]
            st = [s_state[d] for d in range(2)]
            xs = [_bdot_nt(jnp.concatenate([s_w[d, ns[d]], s_rt[d][rows[d], :]], axis=0), st[d]) for d in range(2)]
            yield
            uv = [jnp.concatenate([(xs[d][0:C] + xs[d][C:2 * C] + s_tp[d, ns[d]]).astype(BF16), s_v[rows[d], :]], axis=0)
                  for d in range(2)]
            yb = [_bdot(s_gr[d, ns[d]], uv[d]) for d in range(2)]
            for d in range(2):
                s_y[d][rows[d], :] = xs[d][2 * C:3 * C] + jnp.where(lo64, yb[d][0:C], yb[d][C:2 * C])
                bk = jnp.concatenate([s_bh[d][rows[d], :], s_kh[d][rows[d], :]], axis=0)
                s_state[d] = st[d] * s_et[d, pl.ds(ns[d], 1), :] + bd_ones * _bdot_tn(uv[d], bk)
            yield

    def scan(tseq):
        nch = tseq // C
        per = min(p1_chunks, nch)
        ng = nch // per

        def probs(g):
            return [(g * per + j, 0) for j in range(per)] + [(nch - 1 - (g * per + j), 1) for j in range(per)]

        def steps(g):
            return [(g * per + j, nch - 1 - (g * per + j)) for j in range(per)]

        _interleave(p1_stages(probs(0)))
        if ng > 1:
            def body(g, carry):
                _interleave(p1_stages(probs(g + 1)), p2_stages(steps(g)))
                return carry

            lax.fori_loop(0, ng - 1, body, 0)
        _interleave(p2_stages(steps(ng - 1)))

    def epilogue(o_ref, tseq):
        R = min(256, tseq)
        lo = lax.broadcasted_iota(jnp.int32, (R, LANES), 1) < HALF

        def tile(it, carry):
            s = pl.multiple_of(it * R, R)
            rows = pl.ds(s, R)
            y = s_yf[rows, :] + s_yb[rows, :]
            mu = _seg_sum(y, lo) * (1.0 / HALF)
            yc = y - mu
            var = _seg_sum(yc * yc, lo) * (1.0 / HALF)
            yn = yc * lax.rsqrt(var + RWKV_GN_EPS) * pvrow(PV_GNG) + pvrow(PV_GNB)
            g = s_g[rows, :]
            o_ref[0, rows, :] = (yn + s_bonus[rows, :]) * (g * jax.nn.sigmoid(g))
            return carry

        lax.fori_loop(0, tseq // R, tile, 0)

    s_state[...] = jnp.zeros_like(s_state)
    for (xr, xk, xv, xg, xl, o_ref, tseq) in ((cr, ck, cv, cg, cl, o_ctx, t_ctx), (lr, lk, lv, lg, ll, o_lat, t_lat)):
        prep(xr, xk, xv, xg, xl, tseq)
        scan(tseq)
        epilogue(o_ref, tseq)


def _col_block_map(b, p, *, off):
    return (b, 0, off + p)


def rwkv_mixer(u_lat, u_ctx, pv, mixl, wup, aup, p1_chunks=8):
    bsz, t_lat, _ = u_lat.shape
    t_ctx = u_ctx.shape[1]
    npair = RWKV_HEADS // 2
    c0 = EV_RWKV_COL // LANES
    nch = t_lat // RWKV_CHUNK
    masks = jnp.asarray(rwkv_masks())

    def seq_specs(t):
        return [pl.BlockSpec((1, t, LANES), functools.partial(_col_block_map, off=c0 + o * npair)) for o in range(4)] + [
            pl.BlockSpec((1, t, LANES), lambda b, p: (b, 0, EV_LORA_COL // LANES))]

    in_specs = seq_specs(t_lat) + seq_specs(t_ctx) + [
        pl.BlockSpec((PV_ROWS, LANES), lambda b, p: (0, p)),
        pl.BlockSpec((8, LANES), lambda b, p: (0, 0)),
        pl.BlockSpec((2, LANES, LANES), lambda b, p: (0, 0, p)),
        pl.BlockSpec((2, LANES, LANES), lambda b, p: (0, 0, p)),
        pl.BlockSpec((N_MASKS, LANES, LANES), lambda b, p: (0, 0, 0)),
    ]
    out_specs = [pl.BlockSpec((1, t_lat, LANES), lambda b, p: (b, 0, p)),
                 pl.BlockSpec((1, t_ctx, LANES), lambda b, p: (b, 0, p))]
    scratch = ([pltpu.VMEM((t_lat, LANES), BF16) for _ in range(13)] + [pltpu.VMEM((t_lat, LANES), F32) for _ in range(4)]
               + [pltpu.VMEM((2, nch, LANES, LANES), BF16), pltpu.VMEM((2, nch, RWKV_CHUNK, LANES), F32),
                  pltpu.VMEM((2, nch, LANES, LANES), BF16)]
               + [pltpu.VMEM((2, nch, LANES), F32), pltpu.VMEM((2, LANES, LANES), F32)])
    kern = functools.partial(_rwkv_kernel, t_lat=t_lat, t_ctx=t_ctx, p1_chunks=p1_chunks)
    return pl.pallas_call(
        kern,
        grid=(bsz, npair),
        in_specs=in_specs,
        out_specs=out_specs,
        out_shape=[jax.ShapeDtypeStruct((bsz, t_lat, RWKV_WIDTH), F32), jax.ShapeDtypeStruct((bsz, t_ctx, RWKV_WIDTH), F32)],
        scratch_shapes=scratch,
        compiler_params=pltpu.CompilerParams(dimension_semantics=("arbitrary", "arbitrary"),
                                             vmem_limit_bytes=V7X_VMEM_LIMIT_BYTES),
        name="rwkv7_chunked",
    )(*([u_lat] * 5), *([u_ctx] * 5), pv, mixl, wup, aup, masks)


def rwkv_params(mix, w0, w_up, a0, a_up, k_k, k_a, r_k, gn_g, gn_b):
    wd = RWKV_WIDTH
    rows = [mix[0, 0:wd], mix[1, 0:wd], mix[0, wd:2 * wd], mix[1, wd:2 * wd], mix[0, 2 * wd:3 * wd], mix[1, 2 * wd:3 * wd],
            mix[0, 3 * wd:4 * wd], mix[1, 3 * wd:4 * wd], w0[0], w0[1], a0[0], a0[1], k_k, k_a, r_k.reshape(-1), gn_g, gn_b]
    pv = jnp.concatenate([jnp.stack(rows), jnp.zeros((PV_ROWS - len(rows), wd), F32)], 0)
    mixl = jnp.concatenate([mix[:, 4 * wd:], jnp.zeros((6, LANES), F32)], 0)
    z = jnp.zeros((2, HALF, wd), F32)
    wup = jnp.concatenate([w_up, z], 1)
    aup = jnp.concatenate([z, a_up], 1)
    return pv, mixl, wup, aup


GP_CONV_Q, GP_CONV_K, GP_CONV_V = 0, 5, 10
GP_NORM, GP_ALOG0, GP_ALOG1, GP_DTB0, GP_DTB1 = 15, 16, 17, 18, 19
GP_ROWS = 24
GM_STRICT, GM_INCL, GM_L1, GM_LV = 0, 1, 2, 3
GN_LV = 6
GM_PER_DIR = GM_LV + GN_LV
GM_EYE = 2 * GM_PER_DIR
GN_MASKS = GM_EYE + 1
GDN_P1_CHUNKS = 4


def gdn_masks():
    i = np.arange(GDN_CHUNK)[:, None]
    j = np.arange(GDN_CHUNK)[None, :]
    out = np.zeros((GN_MASKS, GDN_CHUNK, GDN_CHUNK), np.float32)
    for d in range(2):
        base = d * GM_PER_DIR
        out[base + GM_STRICT] = (j < i) if d == 0 else (j > i)
        out[base + GM_INCL] = (j <= i) if d == 0 else (j >= i)
        for li in range(GN_LV + 1):
            s = 1 << li
            blk = (i // (2 * s)) == (j // (2 * s))
            m = blk & (((i & s) != 0) & ((j & s) == 0) if d == 0 else ((i & s) == 0) & ((j & s) != 0))
            out[base + (GM_L1 if li == 0 else GM_LV + li - 1)] = m
    out[GM_EYE] = i == j
    return out


def _gdn_kernel(lq, lk, lv, lz, lbg, cq, ck, cv, cz, cbg, gp_ref, mk_ref, o_lat, o_ctx,
                s_q, s_k, s_kb0, s_kb1, s_kbg0, s_kbg1, s_qg0, s_qg1, s_kg0, s_kg1, s_vb0, s_vb1,
                s_gc0, s_gc1, s_of, s_ob, s_tk, s_tvb, s_a, s_gl, s_state, *, t_lat, t_ctx):
    C = GDN_CHUNK
    s_kb, s_kbg, s_qg, s_kg, s_vb = (s_kb0, s_kb1), (s_kbg0, s_kbg1), (s_qg0, s_qg1), (s_kg0, s_kg1), (s_vb0, s_vb1)
    s_gc, s_o = (s_gc0, s_gc1), (s_of, s_ob)

    def gprow(i):
        return gp_ref[pl.ds(i, 1), :]

    def prep(xq, xk, xv, xbg, tseq):
        R = min(256, tseq)
        N = R + 16

        def tile(it, carry):
            s = pl.multiple_of(it * R, R)
            has_prev = jnp.where(s > 0, 1.0, 0.0).astype(F32)
            has_next = jnp.where(s + R < tseq, 1.0, 0.0).astype(F32)
            pidx = pl.multiple_of(jnp.maximum(s - 8, 0), 8)
            nidx = pl.multiple_of(jnp.minimum(s + R, tseq - 8), 8)

            def conv_silu(ref, row0):
                xe = jnp.concatenate([ref[0, pl.ds(pidx, 8), :] * has_prev, ref[0, pl.ds(s, R), :],
                                      ref[0, pl.ds(nidx, 8), :] * has_next], axis=0)
                acc = None
                for j in range(5):
                    sh = (N - (j - 2)) % N
                    xs = xe if sh == 0 else pltpu.roll(xe, sh, axis=0)
                    term = gprow(row0 + j) * xs[8:8 + R]
                    acc = term if acc is None else acc + term
                return acc * jax.nn.sigmoid(acc)

            qc = conv_silu(xq, GP_CONV_Q)
            kc = conv_silu(xk, GP_CONV_K)
            v = conv_silu(xv, GP_CONV_V)
            q = qc * (lax.rsqrt(jnp.sum(qc * qc, axis=1, keepdims=True) + 1e-6) * (LANES ** -0.5))
            k = kc * lax.rsqrt(jnp.sum(kc * kc, axis=1, keepdims=True) + 1e-6)
            bg = xbg[0, 0, pl.ds(s, R), :]
            s_q[pl.ds(s, R), :] = q.astype(BF16)
            s_k[pl.ds(s, R), :] = k.astype(BF16)
            for d in range(2):
                beta = jax.nn.sigmoid(bg[:, d:d + 1])
                g = -jnp.exp(gprow(GP_ALOG0 + d)) * jax.nn.softplus(bg[:, 2 + d:3 + d] + gprow(GP_DTB0 + d))
                c_fwd = _chunk_cumsum(g, False, C)
                tot = jnp.concatenate([jnp.broadcast_to(c_fwd[(c + 1) * C - 1:(c + 1) * C, :], (C, LANES))
                                       for c in range(R // C)], axis=0)
                c_bwd = tot - c_fwd + g
                gc, rest = (c_fwd, c_bwd - g) if d == 0 else (c_bwd, c_fwd - g)
                eg = jnp.exp(gc)
                kb = k * beta
                s_kb[d][pl.ds(s, R), :] = kb.astype(BF16)
                s_kbg[d][pl.ds(s, R), :] = (kb * eg).astype(BF16)
                s_qg[d][pl.ds(s, R), :] = (q * eg).astype(BF16)
                s_kg[d][pl.ds(s, R), :] = (k * jnp.exp(rest)).astype(BF16)
                s_vb[d][pl.ds(s, R), :] = (v * beta).astype(BF16)
                s_gc[d][pl.ds(s, R), :] = gc
                for c in range(R // C):
                    last = c * C + (C - 1 if d == 0 else 0)
                    s_gl[d, pl.ds(it * (R // C) + c, 1), :] = eg[last:last + 1, :]
            return carry

        lax.fori_loop(0, tseq // R, tile, 0)

    def p1_stages(probs):
        ds = [d for _, d in probs]
        rows = [pl.ds(pl.multiple_of(n * C, C), C) for n, _ in probs]
        gms = [_bdot_nt(jnp.concatenate([s_kb[d][rw, :], s_q[rw, :]], axis=0), s_k[rw, :])
               for (n, d), rw in zip(probs, rows)]
        yield
        a_list = []
        for (n, d), rw, gm in zip(probs, rows, gms):
            base = d * GM_PER_DIR
            gc = s_gc[d][rw, :]
            m_incl = mk_ref[base + GM_INCL]
            dec = jnp.exp((gc - gc.T) * m_incl)
            s_a[d, n] = (gm[C:2 * C] * dec * m_incl).astype(BF16)
            a_list.append(-(gm[0:C] * dec * mk_ref[base + GM_STRICT]))
        xs = [mk_ref[GM_EYE] + a * mk_ref[d * GM_PER_DIR + GM_L1] for a, d in zip(a_list, ds)]
        for li in range(GN_LV):
            t1 = [_bdot(x, a * mk_ref[d * GM_PER_DIR + GM_LV + li]) for x, a, d in zip(xs, a_list, ds)]
            yield
            xs = [x + _bdot(t, x) for x, t in zip(xs, t1)]
            yield
        for (n, d), rw, x in zip(probs, rows, xs):
            tkv = _bdot(x, jnp.concatenate([s_kbg[d][rw, :], s_vb[d][rw, :]], axis=1))
            s_tk[d, n] = tkv[:, 0:LANES].astype(BF16)
            s_tvb[d, n] = tkv[:, LANES:2 * LANES]
        yield

    def p2_stages(steps):
        for ns in steps:
            rows = [pl.ds(pl.multiple_of(ns[d] * C, C), C) for d in range(2)]
            st = [s_state[d] for d in range(2)]
            aq = [_bdot(jnp.concatenate([s_tk[d, ns[d]], s_qg[d][rows[d], :]], axis=0), st[d]) for d in range(2)]
            yield
            vnew = [s_tvb[d, ns[d]] - aq[d][0:C] for d in range(2)]
            oo = [aq[d][C:2 * C] + _bdot(s_a[d, ns[d]], vnew[d]) for d in range(2)]
            for d in range(2):
                s_o[d][rows[d], :] = oo[d]
                s_state[d] = st[d] * s_gl[d, pl.ds(ns[d], 1), :] + _bdot_tn(s_kg[d][rows[d], :], vnew[d])
            yield

    def scan(tseq):
        nch = tseq // C
        per = min(GDN_P1_CHUNKS, nch)
        ng = nch // per

        def probs(g):
            return [(g * per + j, 0) for j in range(per)] + [(nch - 1 - (g * per + j), 1) for j in range(per)]

        def steps(g):
            return [(g * per + j, nch - 1 - (g * per + j)) for j in range(per)]

        _interleave(p1_stages(probs(0)))
        if ng > 1:
            def body(g, carry):
                _interleave(p1_stages(probs(g + 1)), p2_stages(steps(g)))
                return carry

            lax.fori_loop(0, ng - 1, body, 0)
        _interleave(p2_stages(steps(ng - 1)))

    def epilogue(o_ref, z_ref, tseq):
        R = min(256, tseq)

        def tile(it, carry):
            s = pl.multiple_of(it * R, R)
            rows = pl.ds(s, R)
            o = s_of[rows, :] + s_ob[rows, :]
            o = o * lax.rsqrt(jnp.mean(o * o, axis=1, keepdims=True) + 1e-6) * gprow(GP_NORM)
            z = z_ref[0, rows, :]
            o_ref[0, rows, :] = o * (z * jax.nn.sigmoid(z))
            return carry

        lax.fori_loop(0, tseq // R, tile, 0)

    s_state[...] = jnp.zeros_like(s_state)
    for (xq, xk, xv, xz, xbg, o_ref, tseq) in ((cq, ck, cv, cz, cbg, o_ctx, t_ctx), (lq, lk, lv, lz, lbg, o_lat, t_lat)):
        prep(xq, xk, xv, xbg, tseq)
        scan(tseq)
        epilogue(o_ref, xz, tseq)


def gdn_mixer(u_lat, bg_lat, u_ctx, bg_ctx, gp):
    bsz, t_lat, _ = u_lat.shape
    t_ctx = u_ctx.shape[1]
    nh = GDN_HEADS
    masks = jnp.asarray(gdn_masks())

    def seq_specs(t):
        return [pl.BlockSpec((1, t, LANES), functools.partial(_col_block_map, off=o * nh)) for o in range(4)] + [
            pl.BlockSpec((1, 1, t, 4), lambda b, h: (b, h, 0, 0))]

    in_specs = seq_specs(t_lat) + seq_specs(t_ctx) + [
        pl.BlockSpec((GP_ROWS, LANES), lambda b, h: (0, h)),
        pl.BlockSpec((GN_MASKS, LANES, LANES), lambda b, h: (0, 0, 0)),
    ]
    out_specs = [pl.BlockSpec((1, t_lat, LANES), lambda b, h: (b, 0, h)),
                 pl.BlockSpec((1, t_ctx, LANES), lambda b, h: (b, 0, h))]
    nch = t_lat // GDN_CHUNK
    scratch = ([pltpu.VMEM((t_lat, LANES), BF16) for _ in range(12)] + [pltpu.VMEM((t_lat, LANES), F32) for _ in range(4)]
               + [pltpu.VMEM((2, nch, LANES, LANES), BF16), pltpu.VMEM((2, nch, LANES, LANES), F32),
                  pltpu.VMEM((2, nch, LANES, LANES), BF16),
                  pltpu.VMEM((2, max(nch, 8), LANES), F32), pltpu.VMEM((2, LANES, LANES), F32)])
    kern = functools.partial(_gdn_kernel, t_lat=t_lat, t_ctx=t_ctx)
    return pl.pallas_call(
        kern,
        grid=(bsz, nh),
        in_specs=in_specs,
        out_specs=out_specs,
        out_shape=[jax.ShapeDtypeStruct((bsz, t_lat, GDN_WIDTH), F32), jax.ShapeDtypeStruct((bsz, t_ctx, GDN_WIDTH), F32)],
        scratch_shapes=scratch,
        compiler_params=pltpu.CompilerParams(dimension_semantics=("arbitrary", "arbitrary"),
                                             vmem_limit_bytes=V7X_VMEM_LIMIT_BYTES),
        name="gdn_chunked",
    )(*([u_lat] * 4), bg_lat, *([u_ctx] * 4), bg_ctx, gp, masks)


def gdn_params(conv_w, a_log, dt_bias, norm_g):
    wd = GDN_WIDTH
    rep = lambda x: jnp.repeat(x, LANES)
    rows = [conv_w[j, o * wd:(o + 1) * wd] for o in range(3) for j in range(5)]
    rows += [jnp.tile(norm_g, GDN_HEADS), rep(a_log[0]), rep(a_log[1]), rep(dt_bias[0]), rep(dt_bias[1])]
    return jnp.concatenate([jnp.stack(rows), jnp.zeros((GP_ROWS - len(rows), wd), F32)], 0)


def gdn_bg(u_small):
    bsz, t, _ = u_small.shape
    x = u_small.reshape(bsz, t, 2, 2, GDN_HEADS)
    return jnp.transpose(x, (0, 4, 1, 2, 3)).reshape(bsz, GDN_HEADS, t, 4)


OD_Q_COL, OD_GATE_COL, OD_KV_COL = 0, ATT_WIDTH, 2 * ATT_WIDTH
OD_KV_W = 4 * ATT_HEAD
OD_COLS = OD_KV_COL + ATT_KV_HEADS * OD_KV_W


def rope_lane_tables(t):
    pos = np.arange(t)
    inv = ROPE_BASE ** (-np.arange(ROPE_FREQS, dtype=np.float64) / ROPE_FREQS)
    ang_row = (pos // GRID_W)[:, None] * inv
    ang_col = (pos % GRID_W)[:, None] * inv
    ang = np.concatenate([ang_row, ang_row, ang_col, ang_col] * 2, axis=1)
    first = (np.arange(LANES) % (2 * ROPE_FREQS)) < ROPE_FREQS
    cos, sin = np.cos(ang), np.sin(ang)
    return (jnp.asarray(cos, F32), jnp.asarray(np.where(first, -sin, 0.0), F32), jnp.asarray(np.where(first, 0.0, sin), F32))


def _rope(x, cos, sina, sinb):
    n = x.shape[1]
    return x * cos + pltpu.roll(x, n - ROPE_FREQS, axis=1) * sina + pltpu.roll(x, ROPE_FREQS, axis=1) * sinb


def _kv_rope_kernel(u_ref, cos_ref, sina_ref, sinb_ref, o_ref):
    x = u_ref[0]
    cos, sina, sinb = cos_ref[...], sina_ref[...], sinb_ref[...]
    parts = []
    for h in range(ATT_KV_HEADS):
        k2 = x[:, h * OD_KV_W:h * OD_KV_W + LANES]
        parts += [_rope(k2, cos, sina, sinb), x[:, h * OD_KV_W + LANES:(h + 1) * OD_KV_W]]
    o_ref[0] = jnp.concatenate(parts, axis=1).astype(BF16)


def kv_rope(u, tables, tr=512):
    bsz, t, _ = u.shape
    tr = min(tr, t)
    kvw = ATT_KV_HEADS * OD_KV_W
    tab = pl.BlockSpec((tr, LANES), lambda b, i: (i, 0))
    return pl.pallas_call(
        _kv_rope_kernel,
        grid=(bsz, t // tr),
        in_specs=[pl.BlockSpec((1, tr, kvw), lambda b, i: (b, i, OD_KV_COL // kvw)), tab, tab, tab],
        out_specs=pl.BlockSpec((1, tr, kvw), lambda b, i: (b, i, 0)),
        out_shape=jax.ShapeDtypeStruct((bsz, t, kvw), BF16),
        compiler_params=pltpu.CompilerParams(dimension_semantics=("arbitrary", "arbitrary")),
        name="kv_rope",
    )(u, *tables)


def _attn_kernel(sink_ref, *refs, has_local, nb):
    if has_local:
        q_ref, g_ref, kvp_ref, kvc_ref, kvn_ref, cos_ref, sina_ref, sinb_ref, kvx_ref, o_ref = refs
    else:
        q_ref, g_ref, kvx_ref, o_ref = refs
    tq = q_ref.shape[1]
    i = pl.program_id(1)
    q = q_ref[0]
    if has_local:
        rep = ATT_WIDTH // LANES
        cos, sina, sinb = (jnp.concatenate([r[...]] * rep, axis=1) for r in (cos_ref, sina_ref, sinb_ref))
        q = _rope(q, cos, sina, sinb)
    q = q * (ATT_HEAD ** -0.5)
    lo = lax.broadcasted_iota(jnp.int32, (tq, LANES), 1) < HALF
    if has_local:
        a = lax.broadcasted_iota(jnp.int32, (tq, WINDOW), 0)
        c = lax.broadcasted_iota(jnp.int32, (tq, WINDOW), 1)
        ok_prev = (c >= a) & (i > 0)
        ok_next = (c <= a) & (i < nb - 1)
    outs = []
    for hk in range(ATT_KV_HEADS):
        kcol = slice(hk * OD_KV_W, hk * OD_KV_W + LANES)
        vcol = slice(hk * OD_KV_W + LANES, (hk + 1) * OD_KV_W)
        qs = []
        for pair in range(ATT_GROUP // 2):
            grp = q[:, (hk * (ATT_GROUP // 2) + pair) * LANES:(hk * (ATT_GROUP // 2) + pair + 1) * LANES]
            qs += [jnp.where(lo, grp, 0.0), jnp.where(lo, 0.0, grp)]
        lhs = jnp.concatenate(qs, axis=0).astype(BF16)
        kx, vx = kvx_ref[0, :, kcol].astype(BF16), kvx_ref[0, :, vcol].astype(BF16)
        if has_local:
            kall = jnp.concatenate([kvp_ref[0, :, kcol], kvc_ref[0, :, kcol], kvn_ref[0, :, kcol], kx], axis=0)
            vall = jnp.concatenate([kvp_ref[0, :, vcol], kvc_ref[0, :, vcol], kvn_ref[0, :, vcol], vx], axis=0)
        else:
            kall, vall = kx, vx
        s = _bdot_nt(lhs, kall)
        lov = lax.broadcasted_iota(jnp.int32, vall.shape, 1) < HALF
        zv = jnp.zeros_like(vall)
        v2 = jnp.concatenate([jnp.where(lov, vall, zv), jnp.where(lov, zv, vall)], axis=0)
        for pair in range(ATT_GROUP // 2):
            ps, inv_l = [], []
            for sub in range(2):
                g = 2 * pair + sub
                sg = s[g * tq:(g + 1) * tq]
                if has_local:
                    sg = jnp.concatenate([jnp.where(ok_prev, sg[:, 0:tq], NEG_INF), sg[:, tq:2 * tq],
                                          jnp.where(ok_next, sg[:, 2 * tq:3 * tq], NEG_INF), sg[:, 3 * tq:]], axis=1)
                sink = sink_ref[hk, g]
                m = jnp.maximum(jnp.max(sg, axis=1, keepdims=True), sink)
                p = jnp.exp(sg - m)
                inv_l.append(1.0 / (jnp.sum(p, axis=1, keepdims=True) + jnp.exp(sink - m)))
                ps.append(p.astype(BF16))
            o_pair = jnp.dot(jnp.concatenate(ps, axis=1), v2, preferred_element_type=F32)
            outs.append(o_pair * jnp.where(lo, inv_l[0], inv_l[1]))
    g = g_ref[0]
    o_ref[0] = jnp.concatenate(outs, axis=1) * (g * jax.nn.sigmoid(g))


def sink_attention(u, kv_local, tables, u_ctx, sink):
    bsz, t, _ = u.shape
    tq = ATT_BLOCK
    nb = t // tq
    tx = u_ctx.shape[1]
    kvw = ATT_KV_HEADS * OD_KV_W
    has_local = kv_local is not None
    in_specs = [pl.BlockSpec(memory_space=pltpu.SMEM),
                pl.BlockSpec((1, tq, ATT_WIDTH), lambda b, i: (b, i, OD_Q_COL // ATT_WIDTH)),
                pl.BlockSpec((1, tq, ATT_WIDTH), lambda b, i: (b, i, OD_GATE_COL // ATT_WIDTH))]
    args = [sink.reshape(ATT_KV_HEADS, ATT_GROUP), u, u]
    if has_local:
        in_specs += [pl.BlockSpec((1, tq, kvw), lambda b, i: (b, jnp.maximum(i - 1, 0), 0)),
                     pl.BlockSpec((1, tq, kvw), lambda b, i: (b, i, 0)),
                     pl.BlockSpec((1, tq, kvw), lambda b, i: (b, jnp.minimum(i + 1, nb - 1), 0))]
        in_specs += [pl.BlockSpec((tq, LANES), lambda b, i: (i, 0))] * 3
        args += [kv_local] * 3 + list(tables)
    in_specs += [pl.BlockSpec((1, tx, kvw), lambda b, i: (b, 0, OD_KV_COL // kvw))]
    args += [u_ctx]
    return pl.pallas_call(
        functools.partial(_attn_kernel, has_local=has_local, nb=nb),
        grid=(bsz, nb),
        in_specs=in_specs,
        out_specs=pl.BlockSpec((1, tq, ATT_WIDTH), lambda b, i: (b, i, 0)),
        out_shape=jax.ShapeDtypeStruct((bsz, t, ATT_WIDTH), F32),
        compiler_params=pltpu.CompilerParams(dimension_semantics=("arbitrary", "arbitrary"),
                                             vmem_limit_bytes=V7X_VMEM_LIMIT_BYTES),
        name="sink_attention",
    )(*args)


def pack_odd_w_in(w_in):
    nq, nkv = ATT_WIDTH, ATT_KV_HEADS * ATT_HEAD
    parts = [w_in[:, :nq], w_in[:, nq + 2 * nkv:]]
    for h in range(ATT_KV_HEADS):
        k = w_in[:, nq + h * ATT_HEAD:nq + (h + 1) * ATT_HEAD]
        v = w_in[:, nq + nkv + h * ATT_HEAD:nq + nkv + (h + 1) * ATT_HEAD]
        parts += [k, k, v, v]
    return jnp.concatenate(parts, 1).astype(BF16)


def pack_even_w_in(w_in):
    g1 = 4 * GDN_WIDTH
    r0 = GDN_IN
    parts = [w_in[:, :g1], w_in[:, r0:r0 + 4 * RWKV_WIDTH], w_in[:, r0 + 4 * RWKV_WIDTH:], w_in[:, g1:GDN_IN],
             jnp.zeros((D_MODEL, EV_COLS - EV_SMALL_COL - 4 * GDN_HEADS), F32)]
    return jnp.concatenate(parts, 1).astype(BF16)


def kernel(x, c, ctx, c_ctx, mod_w, mod_b, ln_g, ln_b, ev_w_in, ev_w_out, gdn_conv, gdn_a_log, gdn_dt_bias,
           gdn_norm_g, rwkv_mix, rwkv_w0, rwkv_w_up, rwkv_a0, rwkv_a_up, rwkv_k_k, rwkv_k_a, rwkv_r_k,
           rwkv_gn_g, rwkv_gn_b, od_w_in, od_w_out, od_sink):
    bsz, t_lat, d = x.shape
    t_ctx = ctx.shape[1]
    x_lat, x_ctx = x, ctx
    n_mod = -(-(bsz + 1) // 8) * 8
    cc = jax.nn.silu(jnp.concatenate([c, c_ctx[None], jnp.zeros((n_mod - bsz - 1, d), F32)], 0))
    mods = mod_all_layers(cc, mod_w, mod_b)
    for layer in range(DEPTH):
        need_ctx = layer < DEPTH - 1
        i = layer // 2
        mod = mods[layer]
        sh, sc, gt = (mod[:bsz, j * d:(j + 1) * d][:, None, :] for j in range(3))
        sh_c, sc_c, gt_c = (jnp.broadcast_to(mod[bsz, j * d:(j + 1) * d], (bsz, 1, d)) for j in range(3))
        if layer % 2 == 0:
            w_in = pack_even_w_in(ev_w_in[i])
            w_out = ev_w_out[i].astype(BF16)
            u_lat = inproj(x_lat, sc, sh, w_in, 1024, EV_TN)
            u_ctx = inproj(x_ctx.reshape(1, bsz * t_ctx, d), sc_c[:1], sh_c[:1], w_in, 1024, EV_TN).reshape(bsz, t_ctx, -1)
            small = slice(EV_SMALL_COL, EV_SMALL_COL + 4 * GDN_HEADS)
            og_lat, og_ctx = gdn_mixer(u_lat, gdn_bg(u_lat[..., small]), u_ctx, gdn_bg(u_ctx[..., small]),
                                       gdn_params(gdn_conv[i], gdn_a_log[i], gdn_dt_bias[i], gdn_norm_g[i]))
            or_lat, or_ctx = rwkv_mixer(u_lat, u_ctx, *rwkv_params(
                rwkv_mix[i], rwkv_w0[i], rwkv_w_up[i], rwkv_a0[i], rwkv_a_up[i], rwkv_k_k[i], rwkv_k_a[i],
                rwkv_r_k[i], rwkv_gn_g[i], rwkv_gn_b[i]))
            a_lat, a_ctx = [og_lat, or_lat], [og_ctx, or_ctx]
        else:
            w_in = pack_odd_w_in(od_w_in[i])
            w_out = od_w_out[i].astype(BF16)
            u_lat = inproj(x_lat, sc, sh, w_in, 1024, OD_TN)
            u_ctx = inproj(x_ctx.reshape(1, bsz * t_ctx, d), sc_c[:1], sh_c[:1], w_in, 1024, OD_TN).reshape(bsz, t_ctx, -1)
            tables = rope_lane_tables(t_lat)
            o_lat = sink_attention(u_lat, kv_rope(u_lat, tables), tables, u_ctx, od_sink[i])
            o_ctx = sink_attention(u_ctx, None, None, u_ctx, od_sink[i]) if need_ctx else None
            a_lat, a_ctx = [o_lat], [o_ctx]
        x_lat = outproj_ln(a_lat, w_out, x_lat, gt, ln_g[layer], ln_b[layer])
        if need_ctx:
            x_ctx = outproj_ln(a_ctx, w_out, x_ctx, gt_c, ln_g[layer], ln_b[layer])
    return x_lat
```

```python
import functools

import jax
import jax.numpy as jnp
import numpy as np
from jax import lax
from jax.experimental import pallas as pl
from jax.experimental.pallas import tpu as pltpu

F32 = jnp.float32
BF16 = jnp.bfloat16

D_MODEL = 2048
DEPTH = 4
GRID_W = 64
GDN_WIDTH = D_MODEL // 2
GDN_HEADS = 8
GDN_CHUNK = 128
RWKV_WIDTH = D_MODEL // 2
RWKV_HEADS = 16
RWKV_CHUNK = 64
RWKV_GN_EPS = 64e-5
ATT_HEAD = 64
ATT_Q_HEADS = D_MODEL // ATT_HEAD
ATT_KV_HEADS = ATT_Q_HEADS // 8
ATT_GROUP = ATT_Q_HEADS // ATT_KV_HEADS
ATT_WIDTH = ATT_Q_HEADS * ATT_HEAD
WINDOW = 128
ATT_BLOCK = 128
ROPE_BASE = 10000.0
ROPE_FREQS = ATT_HEAD // 4
NEG_INF = -1e30
DEEPNORM_ALPHA = (2 * DEPTH) ** 0.25
LN_EPS = 1e-5
GDN_IN = 4 * GDN_WIDTH + 4 * GDN_HEADS
RWKV_IN = 4 * RWKV_WIDTH + 128

LANES = 128
PREP_ROWS = 256
CONV_HALO = 8
HALF = 64
V7X_VMEM_LIMIT_BYTES = 56 * 1024 * 1024
EV_GDN_COL = 0
EV_RWKV_COL = 4 * GDN_WIDTH
EV_LORA_COL = EV_RWKV_COL + 4 * RWKV_WIDTH
EV_SMALL_COL = EV_LORA_COL + LANES
EV_COLS = EV_SMALL_COL + LANES
EV_TN = 1408
OD_TN = 1280


def _mod_kernel(a_ref, w_ref, b_ref, o_ref):
    o_ref[0] = jnp.dot(a_ref[...].astype(BF16), w_ref[0].astype(BF16), preferred_element_type=F32) + b_ref[0]


def mod_all_layers(a, w, b, tn=512):
    m, k = a.shape
    nl, _, n = w.shape
    return pl.pallas_call(
        _mod_kernel,
        grid=(nl, n // tn),
        in_specs=[pl.BlockSpec((m, k), lambda l, j: (0, 0)), pl.BlockSpec((1, k, tn), lambda l, j: (l, 0, j)),
                  pl.BlockSpec((1, 1, tn), lambda l, j: (l, 0, j))],
        out_specs=pl.BlockSpec((1, m, tn), lambda l, j: (l, 0, j)),
        out_shape=jax.ShapeDtypeStruct((nl, m, n), F32),
        compiler_params=pltpu.CompilerParams(dimension_semantics=("arbitrary", "arbitrary"),
                                             vmem_limit_bytes=V7X_VMEM_LIMIT_BYTES),
        name="mod_matmul",
    )(a, w, b.reshape(nl, 1, n))


def _inproj_kernel(x_ref, sc_ref, sh_ref, w_ref, o_ref, h_ref):
    @pl.when(pl.program_id(2) == 0)
    def _():
        h_ref[...] = (x_ref[0] * (1.0 + sc_ref[0]) + sh_ref[0]).astype(BF16)

    o_ref[0] = jnp.dot(h_ref[...], w_ref[...], preferred_element_type=F32)


def inproj(x, sc, sh, w, tm, tn):
    bsz, t, d = x.shape
    n = w.shape[1]
    tm = min(tm, t)
    assert t % tm == 0 and n % tn == 0
    return pl.pallas_call(
        _inproj_kernel,
        grid=(bsz, t // tm, n // tn),
        in_specs=[pl.BlockSpec((1, tm, d), lambda b, i, j: (b, i, 0)),
                  pl.BlockSpec((1, 1, d), lambda b, i, j: (b, 0, 0)),
                  pl.BlockSpec((1, 1, d), lambda b, i, j: (b, 0, 0)),
                  pl.BlockSpec((d, tn), lambda b, i, j: (0, j))],
        out_specs=pl.BlockSpec((1, tm, tn), lambda b, i, j: (b, i, j)),
        out_shape=jax.ShapeDtypeStruct((bsz, t, n), F32),
        scratch_shapes=[pltpu.VMEM((tm, d), BF16)],
        compiler_params=pltpu.CompilerParams(dimension_semantics=("arbitrary", "arbitrary", "arbitrary"),
                                             vmem_limit_bytes=V7X_VMEM_LIMIT_BYTES),
        name="inproj",
    )(x, sc, sh, w)


def _outproj_kernel(*refs, n_a):
    a_refs, (w_ref, x_ref, gt_ref, g_ref, b_ref, o_ref) = refs[:n_a], refs[n_a:]
    y = None
    k0 = 0
    for a_ref in a_refs:
        kw = a_ref.shape[2]
        term = jnp.dot(a_ref[0].astype(BF16), w_ref[k0:k0 + kw, :], preferred_element_type=F32)
        y = term if y is None else y + term
        k0 += kw
    z = DEEPNORM_ALPHA * x_ref[0] + gt_ref[0] * y
    mu = jnp.mean(z, axis=1, keepdims=True)
    zc = z - mu
    var = jnp.mean(zc * zc, axis=1, keepdims=True)
    o_ref[0] = zc * lax.rsqrt(var + LN_EPS) * g_ref[...] + b_ref[...]


def outproj_ln(a_list, w, x, gt, ln_g, ln_b, tm=512):
    bsz, t, d = x.shape
    tm = min(tm, t)
    in_specs = [pl.BlockSpec((1, tm, a.shape[2]), lambda b, i: (b, i, 0)) for a in a_list] + [
        pl.BlockSpec(w.shape, lambda b, i: (0, 0)),
        pl.BlockSpec((1, tm, d), lambda b, i: (b, i, 0)),
        pl.BlockSpec((1, 1, d), lambda b, i: (b, 0, 0)),
        pl.BlockSpec((1, d), lambda b, i: (0, 0)),
        pl.BlockSpec((1, d), lambda b, i: (0, 0))]
    return pl.pallas_call(
        functools.partial(_outproj_kernel, n_a=len(a_list)),
        grid=(bsz, t // tm),
        in_specs=in_specs,
        out_specs=pl.BlockSpec((1, tm, d), lambda b, i: (b, i, 0)),
        out_shape=jax.ShapeDtypeStruct((bsz, t, d), F32),
        compiler_params=pltpu.CompilerParams(dimension_semantics=("arbitrary", "arbitrary"),
                                             vmem_limit_bytes=V7X_VMEM_LIMIT_BYTES),
        name="outproj_ln",
    )(*a_list, w, x, gt, ln_g.reshape(1, d), ln_b.reshape(1, d))


def _bdot(a, b):
    return jnp.dot(a.astype(BF16), b.astype(BF16), preferred_element_type=F32)


def _bdot_nt(a, b):
    return lax.dot_general(a.astype(BF16), b.astype(BF16), (((1,), (1,)), ((), ())), preferred_element_type=F32)


def _bdot_tn(a, b):
    return lax.dot_general(a.astype(BF16), b.astype(BF16), (((0,), (0,)), ((), ())), preferred_element_type=F32)


(PV_MR0, PV_MR1, PV_MK0, PV_MK1, PV_MV0, PV_MV1, PV_MG0, PV_MG1, PV_W00, PV_W01, PV_A00, PV_A01,
 PV_KK, PV_KA, PV_RK, PV_GNG, PV_GNB) = range(17)
PV_ROWS = 24
MK_ABD, MK_P, MK_Y, MK_L1 = 0, 1, 2, 3
MK_LV = 4
N_LV = 5
MK_PER_DIR = MK_LV + N_LV
MK_EYE = 2 * MK_PER_DIR
MK_BD = MK_EYE + 1
N_MASKS = MK_BD + 1


def rwkv_masks():
    ri = np.arange(128)[:, None]
    ci = np.arange(128)[None, :]
    i, j = ri & 63, ci & 63
    same = (ri < 64) == (ci < 64)
    out = np.zeros((N_MASKS, 128, 128), np.float32)
    for d in range(2):
        strict = (j < i) if d == 0 else (j > i)
        incl = (j <= i) if d == 0 else (j >= i)
        base = d * MK_PER_DIR
        out[base + MK_ABD] = strict & same
        out[base + MK_P] = strict & (ci >= 64)
        out[base + MK_Y] = incl
        for li, s in enumerate((1, 2, 4, 8, 16, 32)):
            blk = (i // (2 * s)) == (j // (2 * s))
            if d == 0:
                m = blk & ((i & s) != 0) & ((j & s) == 0)
            else:
                m = blk & ((i & s) == 0) & ((j & s) != 0)
            out[base + (MK_L1 if li == 0 else MK_LV + li - 1)] = m & same
    out[MK_EYE] = ri == ci
    out[MK_BD] = same
    return out


def _seg_sum(x, lo):
    s0 = jnp.sum(jnp.where(lo, x, 0.0), axis=1, keepdims=True)
    s1 = jnp.sum(jnp.where(lo, 0.0, x), axis=1, keepdims=True)
    return jnp.where(lo, s0, s1)


def _chunk_cumsum(x, reverse, chunk):
    n = x.shape[0]
    rowm = lax.broadcasted_iota(jnp.int32, x.shape, 0) & (chunk - 1)
    s = 1
    while s < chunk:
        if not reverse:
            x = x + jnp.where(rowm >= s, pltpu.roll(x, s, axis=0), 0.0)
        else:
            x = x + jnp.where(rowm < chunk - s, pltpu.roll(x, n - s, axis=0), 0.0)
        s *= 2
    return x


def _interleave(*gens):
    live = list(gens)
    while live:
        for g in list(live):
            try:
                next(g)
            except StopIteration:
                live.remove(g)


def _rwkv_kernel(lr, lk, lv, lg, ll, cr, ck, cv, cg, cl, pv_ref, mixl_ref, wup_ref, aup_ref, mk_ref,
                 o_lat, o_ctx,
                 s_at0, s_at1, s_rt0, s_rt1, s_bt0, s_bt1, s_kt0, s_kt1, s_bh0, s_bh1, s_kh0, s_kh1, s_v,
                 s_g, s_bonus, s_yf, s_yb, s_w, s_tp, s_gr, s_et, s_state, *, t_lat, t_ctx, p1_chunks):
    C = RWKV_CHUNK
    s_at, s_rt, s_bt, s_kt, s_bh, s_kh = ((s_at0, s_at1), (s_rt0, s_rt1), (s_bt0, s_bt1), (s_kt0, s_kt1),
                                          (s_bh0, s_bh1), (s_kh0, s_kh1))
    s_y = (s_yf, s_yb)
    lo64 = lax.broadcasted_iota(jnp.int32, (C, LANES), 1) < HALF
    bd_ones = mk_ref[MK_BD]

    def pvrow(i):
        return pv_ref[pl.ds(i, 1), :]

    def prep_tile(xs, tseq, it, row0, ch0):
        xr, xk, xv, xg, xl = xs
        R = min(PREP_ROWS, tseq)
        s = it * R
        lo = lax.broadcasted_iota(jnp.int32, (R, LANES), 1) < HALF
        rows = lax.broadcasted_iota(jnp.int32, (R, LANES), 0)

        def shifted(ref, m0, m1):
            cur = ref[0, s:s + R, :]
            prow = ref[0, s - 1:s, :] if s > 0 else jnp.zeros((1, LANES), F32)
            nrow = ref[0, s + R:s + R + 1, :] if s + R < tseq else jnp.zeros((1, LANES), F32)
            prev = jnp.where(rows == 0, prow, pltpu.roll(cur, 1, axis=0))
            nxt = jnp.where(rows == R - 1, nrow, pltpu.roll(cur, R - 1, axis=0))
            return cur + m0 * (prev - cur) + m1 * (nxt - cur)

        r = shifted(xr, pvrow(PV_MR0), pvrow(PV_MR1))
        k = shifted(xk, pvrow(PV_MK0), pvrow(PV_MK1))
        v = shifted(xv, pvrow(PV_MV0), pvrow(PV_MV1))
        g = shifted(xg, pvrow(PV_MG0), pvrow(PV_MG1))
        lor = shifted(xl, mixl_ref[0:1, :], mixl_ref[1:2, :])
        th = jnp.tanh(lor)
        kkf = k * pvrow(PV_KK)
        kk = kkf * lax.rsqrt(_seg_sum(kkf * kkf, lo) + 1e-6)
        ka = pvrow(PV_KA)
        out = slice(row0 + s, row0 + s + R)
        kd_sum = None
        for d in range(2):
            w_pre = pvrow(PV_W00 + d) + _bdot(th, wup_ref[d])
            lw = -jnp.exp(-jax.nn.softplus(-w_pre) - 0.5)
            ag = jax.nn.sigmoid(pvrow(PV_A00 + d) + _bdot(lor, aup_ref[d]))
            kd = k * (1.0 + (ag - 1.0) * ka)
            bdv = kk * ag
            kd_sum = kd if kd_sum is None else kd_sum + kd
            c_fwd = _chunk_cumsum(lw, False, C)
            tot = jnp.concatenate([jnp.broadcast_to(c_fwd[(c + 1) * C - 1:(c + 1) * C, :], (C, LANES))
                                   for c in range(R // C)], axis=0)
            c_bwd = tot - c_fwd + lw
            cw, rest = (c_fwd, c_bwd - lw) if d == 0 else (c_bwd, c_fwd - lw)
            e_in = jnp.exp(-cw)
            e_out = jnp.exp(rest)
            s_at[d][out, :] = (-kk * jnp.exp(cw - lw)).astype(BF16)
            s_rt[d][out, :] = (r * jnp.exp(cw)).astype(BF16)
            s_bt[d][out, :] = (bdv * e_in).astype(BF16)
            s_kt[d][out, :] = (kd * e_in).astype(BF16)
            s_bh[d][out, :] = (bdv * e_out).astype(BF16)
            s_kh[d][out, :] = (kd * e_out).astype(BF16)
            for c in range(R // C):
                last = c * C + (C - 1 if d == 0 else 0)
                n = ch0 + it * (R // C) + c
                s_et[d, n:n + 1, :] = jnp.exp(cw[last:last + 1, :])
        s_v[out, :] = v.astype(BF16)
        s_g[out, :] = g
        s_bonus[out, :] = _seg_sum(r * kd_sum * pvrow(PV_RK), lo) * v

    def prep_stages(xs, tseq, tiles, row0, ch0):
        for it in tiles:
            prep_tile(xs, tseq, it, row0, ch0)
            yield

    def p1_stages(probs, row0, ch0):
        ds = [d for _, d in probs]
        rows = [pl.ds(pl.multiple_of(row0 + n * C, C), C) for n, _ in probs]
        a2s, gms = [], []
        for (n, d), rw in zip(probs, rows):
            at, rt = s_at[d][rw, :], s_rt[d][rw, :]
            zero = jnp.zeros_like(at)
            a2 = jnp.concatenate([jnp.where(lo64, at, zero), jnp.where(lo64, zero, at)], axis=0)
            l1 = jnp.concatenate([a2, jnp.where(lo64, rt, zero), jnp.where(lo64, zero, rt)], axis=0)
            a2s.append(a2)
            gms.append(_bdot_nt(l1, jnp.concatenate([s_bt[d][rw, :], s_kt[d][rw, :]], axis=0)))
        yield
        ps, a_list = [], []
        for (n, d), rw, gm in zip(probs, rows, gms):
            base = d * MK_PER_DIR
            ga = gm[0:2 * C]
            v = s_v[rw, :]
            ps.append(_bdot(ga * mk_ref[base + MK_P], jnp.concatenate([v, v], axis=0)))
            s_gr[d, ch0 + n] = (gm[2 * C:4 * C] * mk_ref[base + MK_Y]).astype(BF16)
            ga_sw = jnp.concatenate([ga[0:C], pltpu.roll(ga[C:2 * C], HALF, axis=1)], axis=0)
            a_list.append(ga_sw * mk_ref[base + MK_ABD])
        yield
        xs = [mk_ref[MK_EYE] + a * mk_ref[d * MK_PER_DIR + MK_L1] for a, d in zip(a_list, ds)]
        for li in range(N_LV):
            t1 = [_bdot(x, a * mk_ref[d * MK_PER_DIR + MK_LV + li]) for x, a, d in zip(xs, a_list, ds)]
            yield
            xs = [x + _bdot(t, x) for x, t in zip(xs, t1)]
            yield
        for (n, d), x, a2, p in zip(probs, xs, a2s, ps):
            wtp = _bdot(x, jnp.concatenate([a2.astype(F32), p], axis=1))
            s_w[d, ch0 + n] = wtp[:, 0:LANES].astype(BF16)
            s_tp[d, ch0 + n] = jnp.where(lo64, wtp[0:C, LANES:2 * LANES], wtp[C:2 * C, LANES:2 * LANES])
        yield

    def p2_stages(steps, row0, ch0):
        for ns0 in steps:
            ns = [ch0 + n for n in ns0]
            rows = [pl.ds(pl.multiple_of(row0 + ns0[d] * C, C), C) for d in range(2)]
            st = [s_state[d] for d in range(2)]
            xs = [_bdot_nt(jnp.concatenate([s_w[d, ns[d]], s_rt[d][rows[d], :]], axis=0), st[d]) for d in range(2)]
            yield
            uv = [jnp.concatenate([(xs[d][0:C] + xs[d][C:2 * C] + s_tp[d, ns[d]]).astype(BF16), s_v[rows[d], :]], axis=0)
                  for d in range(2)]
            yb = [_bdot(s_gr[d, ns[d]], uv[d]) for d in range(2)]
            for d in range(2):
                s_y[d][rows[d], :] = xs[d][2 * C:3 * C] + jnp.where(lo64, yb[d][0:C], yb[d][C:2 * C])
                bk = jnp.concatenate([s_bh[d][rows[d], :], s_kh[d][rows[d], :]], axis=0)
                s_state[d] = st[d] * s_et[d, pl.ds(ns[d], 1), :] + bd_ones * _bdot_tn(uv[d], bk)
            yield

    def groups(tseq):
        nch = tseq // C
        per = min(p1_chunks, nch)
        probs = lambda g: [(g * per + j, 0) for j in range(per)] + [(nch - 1 - (g * per + j), 1) for j in range(per)]
        steps = lambda g: [(g * per + j, nch - 1 - (g * per + j)) for j in range(per)]
        return nch // per, per, probs, steps

    def epilogue(o_ref, tseq, row0):
        R = min(PREP_ROWS, tseq)
        lo = lax.broadcasted_iota(jnp.int32, (R, LANES), 1) < HALF

        def tile(it, carry):
            s = pl.multiple_of(it * R, R)
            rows = pl.ds(pl.multiple_of(row0 + it * R, R), R)
            y = s_yf[rows, :] + s_yb[rows, :]
            mu = _seg_sum(y, lo) * (1.0 / HALF)
            yc = y - mu
            var = _seg_sum(yc * yc, lo) * (1.0 / HALF)
            yn = yc * lax.rsqrt(var + RWKV_GN_EPS) * pvrow(PV_GNG) + pvrow(PV_GNB)
            g = s_g[rows, :]
            o_ref[0, pl.ds(s, R), :] = (yn + s_bonus[rows, :]) * (g * jax.nn.sigmoid(g))
            return carry

        lax.fori_loop(0, tseq // R, tile, 0)

    lat, ctx = (lr, lk, lv, lg, ll), (cr, ck, cv, cg, cl)
    row_c, ch_c = t_lat, t_lat // C
    s_state[...] = jnp.zeros_like(s_state)
    _interleave(prep_stages(ctx, t_ctx, range(t_ctx // min(PREP_ROWS, t_ctx)), row_c, ch_c))
    ng_c, _, probs_c, steps_c = groups(t_ctx)
    ng, per, probs, steps = groups(t_lat)
    ntile = t_lat // min(PREP_ROWS, t_lat)
    tpg = max(per * C // min(PREP_ROWS, t_lat), 1)
    first = sorted(set(list(range(min(tpg, ntile))) + list(range(max(ntile - tpg, 0), ntile))))
    rest = [t for t in range(ntile) if t not in first]

    def ctx_scan():
        for g in range(ng_c):
            yield from p1_stages(probs_c(g), row_c, ch_c)
            yield from p2_stages(steps_c(g), row_c, ch_c)

    _interleave(prep_stages(lat, t_lat, first, 0, 0), ctx_scan())
    _interleave(prep_stages(lat, t_lat, rest, 0, 0), p1_stages(probs(0), 0, 0))
    epilogue(o_ctx, t_ctx, row_c)
    if ng > 1:
        def body(g, carry):
            _interleave(p1_stages(probs(g + 1), 0, 0), p2_stages(steps(g), 0, 0))
            return carry

        lax.fori_loop(0, ng - 1, body, 0)
    _interleave(p2_stages(steps(ng - 1), 0, 0))
    epilogue(o_lat, t_lat, 0)


def _col_block_map(b, p, *, off):
    return (b, 0, off + p)


def rwkv_mixer(u_lat, u_ctx, pv, mixl, wup, aup, p1_chunks=8):
    bsz, t_lat, _ = u_lat.shape
    t_ctx = u_ctx.shape[1]
    npair = RWKV_HEADS // 2
    c0 = EV_RWKV_COL // LANES
    nch = (t_lat + t_ctx) // RWKV_CHUNK
    t_all = t_lat + t_ctx
    masks = jnp.asarray(rwkv_masks())

    def seq_specs(t):
        return [pl.BlockSpec((1, t, LANES), functools.partial(_col_block_map, off=c0 + o * npair)) for o in range(4)] + [
            pl.BlockSpec((1, t, LANES), lambda b, p: (b, 0, EV_LORA_COL // LANES))]

    in_specs = seq_specs(t_lat) + seq_specs(t_ctx) + [
        pl.BlockSpec((PV_ROWS, LANES), lambda b, p: (0, p)),
        pl.BlockSpec((8, LANES), lambda b, p: (0, 0)),
        pl.BlockSpec((2, LANES, LANES), lambda b, p: (0, 0, p)),
        pl.BlockSpec((2, LANES, LANES), lambda b, p: (0, 0, p)),
        pl.BlockSpec((N_MASKS, LANES, LANES), lambda b, p: (0, 0, 0)),
    ]
    out_specs = [pl.BlockSpec((1, t_lat, LANES), lambda b, p: (b, 0, p)),
                 pl.BlockSpec((1, t_ctx, LANES), lambda b, p: (b, 0, p))]
    scratch = ([pltpu.VMEM((t_all, LANES), BF16) for _ in range(13)] + [pltpu.VMEM((t_all, LANES), F32) for _ in range(4)]
               + [pltpu.VMEM((2, nch, LANES, LANES), BF16), pltpu.VMEM((2, nch, RWKV_CHUNK, LANES), F32),
                  pltpu.VMEM((2, nch, LANES, LANES), BF16)]
               + [pltpu.VMEM((2, nch, LANES), F32), pltpu.VMEM((2, LANES, LANES), F32)])
    kern = functools.partial(_rwkv_kernel, t_lat=t_lat, t_ctx=t_ctx, p1_chunks=p1_chunks)
    return pl.pallas_call(
        kern,
        grid=(bsz, npair),
        in_specs=in_specs,
        out_specs=out_specs,
        out_shape=[jax.ShapeDtypeStruct((bsz, t_lat, RWKV_WIDTH), F32), jax.ShapeDtypeStruct((bsz, t_ctx, RWKV_WIDTH), F32)],
        scratch_shapes=scratch,
        compiler_params=pltpu.CompilerParams(dimension_semantics=("arbitrary", "arbitrary"),
                                             vmem_limit_bytes=V7X_VMEM_LIMIT_BYTES),
        name="rwkv7_chunked",
    )(*([u_lat] * 5), *([u_ctx] * 5), pv, mixl, wup, aup, masks)


def rwkv_params(mix, w0, w_up, a0, a_up, k_k, k_a, r_k, gn_g, gn_b):
    wd = RWKV_WIDTH
    rows = [mix[0, 0:wd], mix[1, 0:wd], mix[0, wd:2 * wd], mix[1, wd:2 * wd], mix[0, 2 * wd:3 * wd], mix[1, 2 * wd:3 * wd],
            mix[0, 3 * wd:4 * wd], mix[1, 3 * wd:4 * wd], w0[0], w0[1], a0[0], a0[1], k_k, k_a, r_k.reshape(-1), gn_g, gn_b]
    pv = jnp.concatenate([jnp.stack(rows), jnp.zeros((PV_ROWS - len(rows), wd), F32)], 0)
    mixl = jnp.concatenate([mix[:, 4 * wd:], jnp.zeros((6, LANES), F32)], 0)
    z = jnp.zeros((2, HALF, wd), F32)
    wup = jnp.concatenate([w_up, z], 1)
    aup = jnp.concatenate([z, a_up], 1)
    return pv, mixl, wup, aup


GP_CONV_Q, GP_CONV_K, GP_CONV_V = 0, 5, 10
GP_NORM, GP_ALOG0, GP_ALOG1, GP_DTB0, GP_DTB1 = 15, 16, 17, 18, 19
GP_ROWS = 24
GM_STRICT, GM_INCL, GM_L1, GM_LV = 0, 1, 2, 3
GN_LV = 6
GM_PER_DIR = GM_LV + GN_LV
GM_EYE = 2 * GM_PER_DIR
GN_MASKS = GM_EYE + 1
GDN_P1_CHUNKS = 4


def gdn_masks():
    i = np.arange(GDN_CHUNK)[:, None]
    j = np.arange(GDN_CHUNK)[None, :]
    out = np.zeros((GN_MASKS, GDN_CHUNK, GDN_CHUNK), np.float32)
    for d in range(2):
        base = d * GM_PER_DIR
        out[base + GM_STRICT] = (j < i) if d == 0 else (j > i)
        out[base + GM_INCL] = (j <= i) if d == 0 else (j >= i)
        for li in range(GN_LV + 1):
            s = 1 << li
            blk = (i // (2 * s)) == (j // (2 * s))
            m = blk & (((i & s) != 0) & ((j & s) == 0) if d == 0 else ((i & s) == 0) & ((j & s) != 0))
            out[base + (GM_L1 if li == 0 else GM_LV + li - 1)] = m
    out[GM_EYE] = i == j
    return out


def _gdn_kernel(lq, lk, lv, lz, lbg, cq, ck, cv, cz, cbg, gp_ref, mk_ref, o_lat, o_ctx,
                s_q, s_k, s_kb0, s_kb1, s_kbg0, s_kbg1, s_qg0, s_qg1, s_kg0, s_kg1, s_vb0, s_vb1,
                s_gc0, s_gc1, s_of, s_ob, s_tk, s_tvb, s_a, s_gl, s_state, *, t_lat, t_ctx):
    C = GDN_CHUNK
    s_kb, s_kbg, s_qg, s_kg, s_vb = (s_kb0, s_kb1), (s_kbg0, s_kbg1), (s_qg0, s_qg1), (s_kg0, s_kg1), (s_vb0, s_vb1)
    s_gc, s_o = (s_gc0, s_gc1), (s_of, s_ob)

    def gprow(i):
        return gp_ref[pl.ds(i, 1), :]

    def prep_tile(xs, tseq, it, row0, ch0):
        xq, xk, xv, xbg = xs
        R = min(PREP_ROWS, tseq)
        N = R + 2 * CONV_HALO
        s = it * R
        zero_halo = jnp.zeros((CONV_HALO, LANES), F32)

        def conv_silu(ref, prow0):
            head = ref[0, s - CONV_HALO:s, :] if s > 0 else zero_halo
            tail = ref[0, s + R:s + R + CONV_HALO, :] if s + R < tseq else zero_halo
            xe = jnp.concatenate([head, ref[0, s:s + R, :], tail], axis=0)
            acc = None
            for j in range(5):
                sh = (N - (j - 2)) % N
                xsh = xe if sh == 0 else pltpu.roll(xe, sh, axis=0)
                term = gprow(prow0 + j) * xsh[CONV_HALO:CONV_HALO + R]
                acc = term if acc is None else acc + term
            return acc * jax.nn.sigmoid(acc)

        qc = conv_silu(xq, GP_CONV_Q)
        kc = conv_silu(xk, GP_CONV_K)
        v = conv_silu(xv, GP_CONV_V)
        q = qc * (lax.rsqrt(jnp.sum(qc * qc, axis=1, keepdims=True) + 1e-6) * (LANES ** -0.5))
        k = kc * lax.rsqrt(jnp.sum(kc * kc, axis=1, keepdims=True) + 1e-6)
        bg = xbg[0, 0, s:s + R, :]
        out = slice(row0 + s, row0 + s + R)
        s_q[out, :] = q.astype(BF16)
        s_k[out, :] = k.astype(BF16)
        for d in range(2):
            beta = jax.nn.sigmoid(bg[:, d:d + 1])
            g = -jnp.exp(gprow(GP_ALOG0 + d)) * jax.nn.softplus(bg[:, 2 + d:3 + d] + gprow(GP_DTB0 + d))
            c_fwd = _chunk_cumsum(g, False, C)
            tot = jnp.concatenate([jnp.broadcast_to(c_fwd[(c + 1) * C - 1:(c + 1) * C, :], (C, LANES))
                                   for c in range(R // C)], axis=0)
            c_bwd = tot - c_fwd + g
            gc, rest = (c_fwd, c_bwd - g) if d == 0 else (c_bwd, c_fwd - g)
            eg = jnp.exp(gc)
            kb = k * beta
            s_kb[d][out, :] = kb.astype(BF16)
            s_kbg[d][out, :] = (kb * eg).astype(BF16)
            s_qg[d][out, :] = (q * eg).astype(BF16)
            s_kg[d][out, :] = (k * jnp.exp(rest)).astype(BF16)
            s_vb[d][out, :] = (v * beta).astype(BF16)
            s_gc[d][out, :] = gc
            for c in range(R // C):
                last = c * C + (C - 1 if d == 0 else 0)
                n = ch0 + it * (R // C) + c
                s_gl[d, n:n + 1, :] = eg[last:last + 1, :]

    def prep_stages(xs, tseq, tiles, row0, ch0):
        for it in tiles:
            prep_tile(xs, tseq, it, row0, ch0)
            yield

    def p1_stages(probs, row0, ch0):
        ds = [d for _, d in probs]
        rows = [pl.ds(pl.multiple_of(row0 + n * C, C), C) for n, _ in probs]
        gms = [_bdot_nt(jnp.concatenate([s_kb[d][rw, :], s_q[rw, :]], axis=0), s_k[rw, :])
               for (n, d), rw in zip(probs, rows)]
        yield
        a_list = []
        for (n, d), rw, gm in zip(probs, rows, gms):
            base = d * GM_PER_DIR
            gc = s_gc[d][rw, :]
            m_incl = mk_ref[base + GM_INCL]
            dec = jnp.exp((gc - gc.T) * m_incl)
            s_a[d, ch0 + n] = (gm[C:2 * C] * dec * m_incl).astype(BF16)
            a_list.append(-(gm[0:C] * dec * mk_ref[base + GM_STRICT]))
        xs = [mk_ref[GM_EYE] + a * mk_ref[d * GM_PER_DIR + GM_L1] for a, d in zip(a_list, ds)]
        for li in range(GN_LV):
            t1 = [_bdot(x, a * mk_ref[d * GM_PER_DIR + GM_LV + li]) for x, a, d in zip(xs, a_list, ds)]
            yield
            xs = [x + _bdot(t, x) for x, t in zip(xs, t1)]
            yield
        for (n, d), rw, x in zip(probs, rows, xs):
            tkv = _bdot(x, jnp.concatenate([s_kbg[d][rw, :], s_vb[d][rw, :]], axis=1))
            s_tk[d, ch0 + n] = tkv[:, 0:LANES].astype(BF16)
            s_tvb[d, ch0 + n] = tkv[:, LANES:2 * LANES]
        yield

    def p2_stages(steps, row0, ch0):
        for ns0 in steps:
            ns = [ch0 + n for n in ns0]
            rows = [pl.ds(pl.multiple_of(row0 + ns0[d] * C, C), C) for d in range(2)]
            st = [s_state[d] for d in range(2)]
            aq = [_bdot(jnp.concatenate([s_tk[d, ns[d]], s_qg[d][rows[d], :]], axis=0), st[d]) for d in range(2)]
            yield
            vnew = [s_tvb[d, ns[d]] - aq[d][0:C] for d in range(2)]
            oo = [aq[d][C:2 * C] + _bdot(s_a[d, ns[d]], vnew[d]) for d in range(2)]
            for d in range(2):
                s_o[d][rows[d], :] = oo[d]
                s_state[d] = st[d] * s_gl[d, pl.ds(ns[d], 1), :] + _bdot_tn(s_kg[d][rows[d], :], vnew[d])
            yield

    def groups(tseq):
        nch = tseq // C
        per = min(GDN_P1_CHUNKS, nch)
        probs = lambda g: [(g * per + j, 0) for j in range(per)] + [(nch - 1 - (g * per + j), 1) for j in range(per)]
        steps = lambda g: [(g * per + j, nch - 1 - (g * per + j)) for j in range(per)]
        return nch // per, per, probs, steps

    def epilogue(o_ref, z_ref, tseq, row0):
        R = min(PREP_ROWS, tseq)

        def tile(it, carry):
            s = pl.multiple_of(it * R, R)
            rows = pl.ds(pl.multiple_of(row0 + it * R, R), R)
            o = s_of[rows, :] + s_ob[rows, :]
            o = o * lax.rsqrt(jnp.mean(o * o, axis=1, keepdims=True) + 1e-6) * gprow(GP_NORM)
            z = z_ref[0, pl.ds(s, R), :]
            o_ref[0, pl.ds(s, R), :] = o * (z * jax.nn.sigmoid(z))
            return carry

        lax.fori_loop(0, tseq // R, tile, 0)

    lat, ctx = (lq, lk, lv, lbg), (cq, ck, cv, cbg)
    row_c, ch_c = t_lat, t_lat // C
    s_state[...] = jnp.zeros_like(s_state)
    _interleave(prep_stages(ctx, t_ctx, range(t_ctx // min(PREP_ROWS, t_ctx)), row_c, ch_c))
    ng_c, _, probs_c, steps_c = groups(t_ctx)
    ng, per, probs, steps = groups(t_lat)
    ntile = t_lat // min(PREP_ROWS, t_lat)
    tpg = max(per * C // min(PREP_ROWS, t_lat), 1)
    first = sorted(set(list(range(min(tpg, ntile))) + list(range(max(ntile - tpg, 0), ntile))))
    rest = [t for t in range(ntile) if t not in first]

    def ctx_scan():
        for g in range(ng_c):
            yield from p1_stages(probs_c(g), row_c, ch_c)
            yield from p2_stages(steps_c(g), row_c, ch_c)

    _interleave(prep_stages(lat, t_lat, first, 0, 0), ctx_scan())
    _interleave(prep_stages(lat, t_lat, rest, 0, 0), p1_stages(probs(0), 0, 0))
    epilogue(o_ctx, cz, t_ctx, row_c)
    if ng > 1:
        def body(g, carry):
            _interleave(p1_stages(probs(g + 1), 0, 0), p2_stages(steps(g), 0, 0))
            return carry

        lax.fori_loop(0, ng - 1, body, 0)
    _interleave(p2_stages(steps(ng - 1), 0, 0))
    epilogue(o_lat, lz, t_lat, 0)


def gdn_mixer(u_lat, bg_lat, u_ctx, bg_ctx, gp):
    bsz, t_lat, _ = u_lat.shape
    t_ctx = u_ctx.shape[1]
    nh = GDN_HEADS
    masks = jnp.asarray(gdn_masks())

    def seq_specs(t):
        return [pl.BlockSpec((1, t, LANES), functools.partial(_col_block_map, off=o * nh)) for o in range(4)] + [
            pl.BlockSpec((1, 1, t, 4), lambda b, h: (b, h, 0, 0))]

    in_specs = seq_specs(t_lat) + seq_specs(t_ctx) + [
        pl.BlockSpec((GP_ROWS, LANES), lambda b, h: (0, h)),
        pl.BlockSpec((GN_MASKS, LANES, LANES), lambda b, h: (0, 0, 0)),
    ]
    out_specs = [pl.BlockSpec((1, t_lat, LANES), lambda b, h: (b, 0, h)),
                 pl.BlockSpec((1, t_ctx, LANES), lambda b, h: (b, 0, h))]
    nch = (t_lat + t_ctx) // GDN_CHUNK
    t_all = t_lat + t_ctx
    scratch = ([pltpu.VMEM((t_all, LANES), BF16) for _ in range(12)] + [pltpu.VMEM((t_all, LANES), F32) for _ in range(4)]
               + [pltpu.VMEM((2, nch, LANES, LANES), BF16), pltpu.VMEM((2, nch, LANES, LANES), F32),
                  pltpu.VMEM((2, nch, LANES, LANES), BF16),
                  pltpu.VMEM((2, max(nch, 8), LANES), F32), pltpu.VMEM((2, LANES, LANES), F32)])
    kern = functools.partial(_gdn_kernel, t_lat=t_lat, t_ctx=t_ctx)
    return pl.pallas_call(
        kern,
        grid=(bsz, nh),
        in_specs=in_specs,
        out_specs=out_specs,
        out_shape=[jax.ShapeDtypeStruct((bsz, t_lat, GDN_WIDTH), F32), jax.ShapeDtypeStruct((bsz, t_ctx, GDN_WIDTH), F32)],
        scratch_shapes=scratch,
        compiler_params=pltpu.CompilerParams(dimension_semantics=("arbitrary", "arbitrary"),
                                             vmem_limit_bytes=V7X_VMEM_LIMIT_BYTES),
        name="gdn_chunked",
    )(*([u_lat] * 4), bg_lat, *([u_ctx] * 4), bg_ctx, gp, masks)


def gdn_params(conv_w, a_log, dt_bias, norm_g):
    wd = GDN_WIDTH
    rep = lambda x: jnp.repeat(x, LANES)
    rows = [conv_w[j, o * wd:(o + 1) * wd] for o in range(3) for j in range(5)]
    rows += [jnp.tile(norm_g, GDN_HEADS), rep(a_log[0]), rep(a_log[1]), rep(dt_bias[0]), rep(dt_bias[1])]
    return jnp.concatenate([jnp.stack(rows), jnp.zeros((GP_ROWS - len(rows), wd), F32)], 0)


def gdn_bg(u_small):
    bsz, t, _ = u_small.shape
    x = u_small.reshape(bsz, t, 2, 2, GDN_HEADS)
    return jnp.transpose(x, (0, 4, 1, 2, 3)).reshape(bsz, GDN_HEADS, t, 4)


OD_Q_COL, OD_GATE_COL, OD_KV_COL = 0, ATT_WIDTH, 2 * ATT_WIDTH
OD_KV_W = 4 * ATT_HEAD
OD_COLS = OD_KV_COL + ATT_KV_HEADS * OD_KV_W


def rope_lane_tables(t):
    pos = np.arange(t)
    inv = ROPE_BASE ** (-np.arange(ROPE_FREQS, dtype=np.float64) / ROPE_FREQS)
    ang_row = (pos // GRID_W)[:, None] * inv
    ang_col = (pos % GRID_W)[:, None] * inv
    ang = np.concatenate([ang_row, ang_row, ang_col, ang_col] * 2, axis=1)
    first = (np.arange(LANES) % (2 * ROPE_FREQS)) < ROPE_FREQS
    cos, sin = np.cos(ang), np.sin(ang)
    return (jnp.asarray(cos, F32), jnp.asarray(np.where(first, -sin, 0.0), F32), jnp.asarray(np.where(first, 0.0, sin), F32))


def _rope(x, cos, sina, sinb):
    n = x.shape[1]
    return x * cos + pltpu.roll(x, n - ROPE_FREQS, axis=1) * sina + pltpu.roll(x, ROPE_FREQS, axis=1) * sinb


def _kv_rope_kernel(u_ref, cos_ref, sina_ref, sinb_ref, o_ref):
    x = u_ref[0]
    cos, sina, sinb = cos_ref[...], sina_ref[...], sinb_ref[...]
    parts = []
    for h in range(ATT_KV_HEADS):
        k2 = x[:, h * OD_KV_W:h * OD_KV_W + LANES]
        parts += [_rope(k2, cos, sina, sinb), x[:, h * OD_KV_W + LANES:(h + 1) * OD_KV_W]]
    o_ref[0] = jnp.concatenate(parts, axis=1).astype(BF16)


def kv_rope(u, tables, tr=512):
    bsz, t, _ = u.shape
    tr = min(tr, t)
    kvw = ATT_KV_HEADS * OD_KV_W
    tab = pl.BlockSpec((tr, LANES), lambda b, i: (i, 0))
    return pl.pallas_call(
        _kv_rope_kernel,
        grid=(bsz, t // tr),
        in_specs=[pl.BlockSpec((1, tr, kvw), lambda b, i: (b, i, OD_KV_COL // kvw)), tab, tab, tab],
        out_specs=pl.BlockSpec((1, tr, kvw), lambda b, i: (b, i, 0)),
        out_shape=jax.ShapeDtypeStruct((bsz, t, kvw), BF16),
        compiler_params=pltpu.CompilerParams(dimension_semantics=("arbitrary", "arbitrary")),
        name="kv_rope",
    )(u, *tables)


def _attn_kernel(sink_ref, *refs, has_local, nb):
    if has_local:
        q_ref, g_ref, kvp_ref, kvc_ref, kvn_ref, cos_ref, sina_ref, sinb_ref, kvx_ref, o_ref = refs
    else:
        q_ref, g_ref, kvx_ref, o_ref = refs
    tq = q_ref.shape[1]
    i = pl.program_id(1)
    q = q_ref[0]
    if has_local:
        rep = ATT_WIDTH // LANES
        cos, sina, sinb = (jnp.concatenate([r[...]] * rep, axis=1) for r in (cos_ref, sina_ref, sinb_ref))
        q = _rope(q, cos, sina, sinb)
    q = q * (ATT_HEAD ** -0.5)
    lo = lax.broadcasted_iota(jnp.int32, (tq, LANES), 1) < HALF
    if has_local:
        a = lax.broadcasted_iota(jnp.int32, (tq, WINDOW), 0)
        c = lax.broadcasted_iota(jnp.int32, (tq, WINDOW), 1)
        ok_prev = (c >= a) & (i > 0)
        ok_next = (c <= a) & (i < nb - 1)
    outs = []
    for hk in range(ATT_KV_HEADS):
        kcol = slice(hk * OD_KV_W, hk * OD_KV_W + LANES)
        vcol = slice(hk * OD_KV_W + LANES, (hk + 1) * OD_KV_W)
        qs = []
        for pair in range(ATT_GROUP // 2):
            grp = q[:, (hk * (ATT_GROUP // 2) + pair) * LANES:(hk * (ATT_GROUP // 2) + pair + 1) * LANES]
            qs += [jnp.where(lo, grp, 0.0), jnp.where(lo, 0.0, grp)]
        lhs = jnp.concatenate(qs, axis=0).astype(BF16)
        kx, vx = kvx_ref[0, :, kcol].astype(BF16), kvx_ref[0, :, vcol].astype(BF16)
        if has_local:
            kall = jnp.concatenate([kvp_ref[0, :, kcol], kvc_ref[0, :, kcol], kvn_ref[0, :, kcol], kx], axis=0)
            vall = jnp.concatenate([kvp_ref[0, :, vcol], kvc_ref[0, :, vcol], kvn_ref[0, :, vcol], vx], axis=0)
        else:
            kall, vall = kx, vx
        s = _bdot_nt(lhs, kall)
        lov = lax.broadcasted_iota(jnp.int32, vall.shape, 1) < HALF
        zv = jnp.zeros_like(vall)
        v2 = jnp.concatenate([jnp.where(lov, vall, zv), jnp.where(lov, zv, vall)], axis=0)
        for pair in range(ATT_GROUP // 2):
            ps, inv_l = [], []
            for sub in range(2):
                g = 2 * pair + sub
                sg = s[g * tq:(g + 1) * tq]
                if has_local:
                    sg = jnp.concatenate([jnp.where(ok_prev, sg[:, 0:tq], NEG_INF), sg[:, tq:2 * tq],
                                          jnp.where(ok_next, sg[:, 2 * tq:3 * tq], NEG_INF), sg[:, 3 * tq:]], axis=1)
                sink = sink_ref[hk, g]
                m = jnp.maximum(jnp.max(sg, axis=1, keepdims=True), sink)
                p = jnp.exp(sg - m)
                inv_l.append(1.0 / (jnp.sum(p, axis=1, keepdims=True) + jnp.exp(sink - m)))
                ps.append(p.astype(BF16))
            o_pair = jnp.dot(jnp.concatenate(ps, axis=1), v2, preferred_element_type=F32)
            outs.append(o_pair * jnp.where(lo, inv_l[0], inv_l[1]))
    g = g_ref[0]
    o_ref[0] = jnp.concatenate(outs, axis=1) * (g * jax.nn.sigmoid(g))


def sink_attention(u, kv_local, tables, u_ctx, sink):
    bsz, t, _ = u.shape
    tq = ATT_BLOCK
    nb = t // tq
    tx = u_ctx.shape[1]
    kvw = ATT_KV_HEADS * OD_KV_W
    has_local = kv_local is not None
    in_specs = [pl.BlockSpec(memory_space=pltpu.SMEM),
                pl.BlockSpec((1, tq, ATT_WIDTH), lambda b, i: (b, i, OD_Q_COL // ATT_WIDTH)),
                pl.BlockSpec((1, tq, ATT_WIDTH), lambda b, i: (b, i, OD_GATE_COL // ATT_WIDTH))]
    args = [sink.reshape(ATT_KV_HEADS, ATT_GROUP), u, u]
    if has_local:
        in_specs += [pl.BlockSpec((1, tq, kvw), lambda b, i: (b, jnp.maximum(i - 1, 0), 0)),
                     pl.BlockSpec((1, tq, kvw), lambda b, i: (b, i, 0)),
                     pl.BlockSpec((1, tq, kvw), lambda b, i: (b, jnp.minimum(i + 1, nb - 1), 0))]
        in_specs += [pl.BlockSpec((tq, LANES), lambda b, i: (i, 0))] * 3
        args += [kv_local] * 3 + list(tables)
    in_specs += [pl.BlockSpec((1, tx, kvw), lambda b, i: (b, 0, OD_KV_COL // kvw))]
    args += [u_ctx]
    return pl.pallas_call(
        functools.partial(_attn_kernel, has_local=has_local, nb=nb),
        grid=(bsz, nb),
        in_specs=in_specs,
        out_specs=pl.BlockSpec((1, tq, ATT_WIDTH), lambda b, i: (b, i, 0)),
        out_shape=jax.ShapeDtypeStruct((bsz, t, ATT_WIDTH), F32),
        compiler_params=pltpu.CompilerParams(dimension_semantics=("arbitrary", "arbitrary"),
                                             vmem_limit_bytes=V7X_VMEM_LIMIT_BYTES),
        name="sink_attention",
    )(*args)


def pack_odd_w_in(w_in):
    nq, nkv = ATT_WIDTH, ATT_KV_HEADS * ATT_HEAD
    parts = [w_in[:, :nq], w_in[:, nq + 2 * nkv:]]
    for h in range(ATT_KV_HEADS):
        k = w_in[:, nq + h * ATT_HEAD:nq + (h + 1) * ATT_HEAD]
        v = w_in[:, nq + nkv + h * ATT_HEAD:nq + nkv + (h + 1) * ATT_HEAD]
        parts += [k, k, v, v]
    return jnp.concatenate(parts, 1).astype(BF16)


def pack_even_w_in(w_in):
    g1 = 4 * GDN_WIDTH
    r0 = GDN_IN
    parts = [w_in[:, :g1], w_in[:, r0:r0 + 4 * RWKV_WIDTH], w_in[:, r0 + 4 * RWKV_WIDTH:], w_in[:, g1:GDN_IN],
             jnp.zeros((D_MODEL, EV_COLS - EV_SMALL_COL - 4 * GDN_HEADS), F32)]
    return jnp.concatenate(parts, 1).astype(BF16)


def kernel(x, c, ctx, c_ctx, mod_w, mod_b, ln_g, ln_b, ev_w_in, ev_w_out, gdn_conv, gdn_a_log, gdn_dt_bias,
           gdn_norm_g, rwkv_mix, rwkv_w0, rwkv_w_up, rwkv_a0, rwkv_a_up, rwkv_k_k, rwkv_k_a, rwkv_r_k,
           rwkv_gn_g, rwkv_gn_b, od_w_in, od_w_out, od_sink):
    bsz, t_lat, d = x.shape
    t_ctx = ctx.shape[1]
    x_lat, x_ctx = x, ctx
    n_mod = -(-(bsz + 1) // 8) * 8
    cc = jax.nn.silu(jnp.concatenate([c, c_ctx[None], jnp.zeros((n_mod - bsz - 1, d), F32)], 0))
    mods = mod_all_layers(cc, mod_w, mod_b)
    for layer in range(DEPTH):
        need_ctx = layer < DEPTH - 1
        i = layer // 2
        mod = mods[layer]
        sh, sc, gt = (mod[:bsz, j * d:(j + 1) * d][:, None, :] for j in range(3))
        sh_c, sc_c, gt_c = (jnp.broadcast_to(mod[bsz, j * d:(j + 1) * d], (bsz, 1, d)) for j in range(3))
        if layer % 2 == 0:
            w_in = pack_even_w_in(ev_w_in[i])
            w_out = ev_w_out[i].astype(BF16)
            u_lat = inproj(x_lat, sc, sh, w_in, 1024, EV_TN)
            u_ctx = inproj(x_ctx.reshape(1, bsz * t_ctx, d), sc_c[:1], sh_c[:1], w_in, 1024, EV_TN).reshape(bsz, t_ctx, -1)
            small = slice(EV_SMALL_COL, EV_SMALL_COL + 4 * GDN_HEADS)
            og_lat, og_ctx = gdn_mixer(u_lat, gdn_bg(u_lat[..., small]), u_ctx, gdn_bg(u_ctx[..., small]),
                                       gdn_params(gdn_conv[i], gdn_a_log[i], gdn_dt_bias[i], gdn_norm_g[i]))
            or_lat, or_ctx = rwkv_mixer(u_lat, u_ctx, *rwkv_params(
                rwkv_mix[i], rwkv_w0[i], rwkv_w_up[i], rwkv_a0[i], rwkv_a_up[i], rwkv_k_k[i], rwkv_k_a[i],
                rwkv_r_k[i], rwkv_gn_g[i], rwkv_gn_b[i]))
            a_lat, a_ctx = [og_lat, or_lat], [og_ctx, or_ctx]
        else:
            w_in = pack_odd_w_in(od_w_in[i])
            w_out = od_w_out[i].astype(BF16)
            u_lat = inproj(x_lat, sc, sh, w_in, 1024, OD_TN)
            u_ctx = inproj(x_ctx.reshape(1, bsz * t_ctx, d), sc_c[:1], sh_c[:1], w_in, 1024, OD_TN).reshape(bsz, t_ctx, -1)
            tables = rope_lane_tables(t_lat)
            o_lat = sink_attention(u_lat, kv_rope(u_lat, tables), tables, u_ctx, od_sink[i])
            o_ctx = sink_attention(u_ctx, None, None, u_ctx, od_sink[i]) if need_ctx else None
            a_lat, a_ctx = [o_lat], [o_ctx]
        x_lat = outproj_ln(a_lat, w_out, x_lat, gt, ln_g[layer], ln_b[layer])
        if need_ctx:
            x_ctx = outproj_ln(a_ctx, w_out, x_ctx, gt_c, ln_g[layer], ln_b[layer])
    return x_lat
```

```python
import functools

import jax
import jax.numpy as jnp
import numpy as np
from jax import lax
from jax.experimental import pallas as pl
from jax.experimental.pallas import tpu as pltpu

F32 = jnp.float32
BF16 = jnp.bfloat16

D_MODEL = 2048
DEPTH = 4
GRID_W = 64
GDN_WIDTH = D_MODEL // 2
GDN_HEADS = 8
GDN_CHUNK = 128
RWKV_WIDTH = D_MODEL // 2
RWKV_HEADS = 16
RWKV_CHUNK = 64
RWKV_GN_EPS = 64e-5
ATT_HEAD = 64
ATT_Q_HEADS = D_MODEL // ATT_HEAD
ATT_KV_HEADS = ATT_Q_HEADS // 8
ATT_GROUP = ATT_Q_HEADS // ATT_KV_HEADS
ATT_WIDTH = ATT_Q_HEADS * ATT_HEAD
WINDOW = 128
ATT_BLOCK = 128
ROPE_BASE = 10000.0
ROPE_FREQS = ATT_HEAD // 4
NEG_INF = -1e30
DEEPNORM_ALPHA = (2 * DEPTH) ** 0.25
LN_EPS = 1e-5
GDN_IN = 4 * GDN_WIDTH + 4 * GDN_HEADS
RWKV_IN = 4 * RWKV_WIDTH + 128

LANES = 128
PREP_ROWS = 256
HALF = 64
V7X_VMEM_LIMIT_BYTES = 56 * 1024 * 1024
EV_GDN_COL = 0
EV_RWKV_COL = 4 * GDN_WIDTH
EV_LORA_COL = EV_RWKV_COL + 4 * RWKV_WIDTH
EV_SMALL_COL = EV_LORA_COL + LANES
EV_COLS = EV_SMALL_COL + LANES
EV_TN = 1408
OD_TN = 1280


def _mod_kernel(a_ref, w_ref, b_ref, o_ref):
    o_ref[0] = jnp.dot(a_ref[...].astype(BF16), w_ref[0].astype(BF16), preferred_element_type=F32) + b_ref[0]


def mod_all_layers(a, w, b, tn=512):
    m, k = a.shape
    nl, _, n = w.shape
    return pl.pallas_call(
        _mod_kernel,
        grid=(nl, n // tn),
        in_specs=[pl.BlockSpec((m, k), lambda l, j: (0, 0)), pl.BlockSpec((1, k, tn), lambda l, j: (l, 0, j)),
                  pl.BlockSpec((1, 1, tn), lambda l, j: (l, 0, j))],
        out_specs=pl.BlockSpec((1, m, tn), lambda l, j: (l, 0, j)),
        out_shape=jax.ShapeDtypeStruct((nl, m, n), F32),
        compiler_params=pltpu.CompilerParams(dimension_semantics=("arbitrary", "arbitrary"),
                                             vmem_limit_bytes=V7X_VMEM_LIMIT_BYTES),
        name="mod_matmul",
    )(a, w, b.reshape(nl, 1, n))


def _inproj_kernel(x_ref, sc_ref, sh_ref, w_ref, o_ref, h_ref):
    @pl.when(pl.program_id(2) == 0)
    def _():
        h_ref[...] = (x_ref[0] * (1.0 + sc_ref[0]) + sh_ref[0]).astype(BF16)

    o_ref[0] = jnp.dot(h_ref[...], w_ref[...], preferred_element_type=F32)


def inproj(x, sc, sh, w, tm, tn):
    bsz, t, d = x.shape
    n = w.shape[1]
    tm = min(tm, t)
    assert t % tm == 0 and n % tn == 0
    return pl.pallas_call(
        _inproj_kernel,
        grid=(bsz, t // tm, n // tn),
        in_specs=[pl.BlockSpec((1, tm, d), lambda b, i, j: (b, i, 0)),
                  pl.BlockSpec((1, 1, d), lambda b, i, j: (b, 0, 0)),
                  pl.BlockSpec((1, 1, d), lambda b, i, j: (b, 0, 0)),
                  pl.BlockSpec((d, tn), lambda b, i, j: (0, j))],
        out_specs=pl.BlockSpec((1, tm, tn), lambda b, i, j: (b, i, j)),
        out_shape=jax.ShapeDtypeStruct((bsz, t, n), F32),
        scratch_shapes=[pltpu.VMEM((tm, d), BF16)],
        compiler_params=pltpu.CompilerParams(dimension_semantics=("arbitrary", "arbitrary", "arbitrary"),
                                             vmem_limit_bytes=V7X_VMEM_LIMIT_BYTES),
        name="inproj",
    )(x, sc, sh, w)


def _outproj_kernel(*refs, n_a):
    a_refs, (w_ref, x_ref, gt_ref, g_ref, b_ref, o_ref) = refs[:n_a], refs[n_a:]
    y = None
    k0 = 0
    for a_ref in a_refs:
        kw = a_ref.shape[2]
        term = jnp.dot(a_ref[0].astype(BF16), w_ref[k0:k0 + kw, :], preferred_element_type=F32)
        y = term if y is None else y + term
        k0 += kw
    z = DEEPNORM_ALPHA * x_ref[0] + gt_ref[0] * y
    mu = jnp.mean(z, axis=1, keepdims=True)
    zc = z - mu
    var = jnp.mean(zc * zc, axis=1, keepdims=True)
    o_ref[0] = zc * lax.rsqrt(var + LN_EPS) * g_ref[...] + b_ref[...]


def outproj_ln(a_list, w, x, gt, ln_g, ln_b, tm=512):
    bsz, t, d = x.shape
    tm = min(tm, t)
    in_specs = [pl.BlockSpec((1, tm, a.shape[2]), lambda b, i: (b, i, 0)) for a in a_list] + [
        pl.BlockSpec(w.shape, lambda b, i: (0, 0)),
        pl.BlockSpec((1, tm, d), lambda b, i: (b, i, 0)),
        pl.BlockSpec((1, 1, d), lambda b, i: (b, 0, 0)),
        pl.BlockSpec((1, d), lambda b, i: (0, 0)),
        pl.BlockSpec((1, d), lambda b, i: (0, 0))]
    return pl.pallas_call(
        functools.partial(_outproj_kernel, n_a=len(a_list)),
        grid=(bsz, t // tm),
        in_specs=in_specs,
        out_specs=pl.BlockSpec((1, tm, d), lambda b, i: (b, i, 0)),
        out_shape=jax.ShapeDtypeStruct((bsz, t, d), F32),
        compiler_params=pltpu.CompilerParams(dimension_semantics=("arbitrary", "arbitrary"),
                                             vmem_limit_bytes=V7X_VMEM_LIMIT_BYTES),
        name="outproj_ln",
    )(*a_list, w, x, gt, ln_g.reshape(1, d), ln_b.reshape(1, d))


def _bdot(a, b):
    return jnp.dot(a.astype(BF16), b.astype(BF16), preferred_element_type=F32)


def _bdot_nt(a, b):
    return lax.dot_general(a.astype(BF16), b.astype(BF16), (((1,), (1,)), ((), ())), preferred_element_type=F32)


def _bdot_tn(a, b):
    return lax.dot_general(a.astype(BF16), b.astype(BF16), (((0,), (0,)), ((), ())), preferred_element_type=F32)


(PV_MR0, PV_MR1, PV_MK0, PV_MK1, PV_MV0, PV_MV1, PV_MG0, PV_MG1, PV_W00, PV_W01, PV_A00, PV_A01,
 PV_KK, PV_KA, PV_RK, PV_GNG, PV_GNB) = range(17)
PV_ROWS = 24
MK_ABD, MK_P, MK_Y, MK_L1 = 0, 1, 2, 3
MK_LV = 4
N_LV = 5
MK_PER_DIR = MK_LV + N_LV
MK_EYE = 2 * MK_PER_DIR
MK_BD = MK_EYE + 1
N_MASKS = MK_BD + 1


def rwkv_masks():
    ri = np.arange(128)[:, None]
    ci = np.arange(128)[None, :]
    i, j = ri & 63, ci & 63
    same = (ri < 64) == (ci < 64)
    out = np.zeros((N_MASKS, 128, 128), np.float32)
    for d in range(2):
        strict = (j < i) if d == 0 else (j > i)
        incl = (j <= i) if d == 0 else (j >= i)
        base = d * MK_PER_DIR
        out[base + MK_ABD] = strict & same
        out[base + MK_P] = strict & (ci >= 64)
        out[base + MK_Y] = incl
        for li, s in enumerate((1, 2, 4, 8, 16, 32)):
            blk = (i // (2 * s)) == (j // (2 * s))
            if d == 0:
                m = blk & ((i & s) != 0) & ((j & s) == 0)
            else:
                m = blk & ((i & s) == 0) & ((j & s) != 0)
            out[base + (MK_L1 if li == 0 else MK_LV + li - 1)] = m & same
    out[MK_EYE] = ri == ci
    out[MK_BD] = same
    return out


def _seg_sum(x, lo):
    s0 = jnp.sum(jnp.where(lo, x, 0.0), axis=1, keepdims=True)
    s1 = jnp.sum(jnp.where(lo, 0.0, x), axis=1, keepdims=True)
    return jnp.where(lo, s0, s1)


def _chunk_cumsum(x, reverse, chunk):
    n = x.shape[0]
    rowm = lax.broadcasted_iota(jnp.int32, x.shape, 0) & (chunk - 1)
    s = 1
    while s < chunk:
        if not reverse:
            x = x + jnp.where(rowm >= s, pltpu.roll(x, s, axis=0), 0.0)
        else:
            x = x + jnp.where(rowm < chunk - s, pltpu.roll(x, n - s, axis=0), 0.0)
        s *= 2
    return x


def _interleave(*gens):
    live = list(gens)
    while live:
        for g in list(live):
            try:
                next(g)
            except StopIteration:
                live.remove(g)


def _rwkv_kernel(lr, lk, lv, lg, ll, cr, ck, cv, cg, cl, pv_ref, mixl_ref, wup_ref, aup_ref, mk_ref,
                 o_lat, o_ctx,
                 s_at0, s_at1, s_rt0, s_rt1, s_bt0, s_bt1, s_kt0, s_kt1, s_bh0, s_bh1, s_kh0, s_kh1, s_v,
                 s_g, s_bonus, s_yf, s_yb, s_w, s_tp, s_gr, s_et, s_state, *, t_lat, t_ctx, p1_chunks):
    C = RWKV_CHUNK
    s_at, s_rt, s_bt, s_kt, s_bh, s_kh = ((s_at0, s_at1), (s_rt0, s_rt1), (s_bt0, s_bt1), (s_kt0, s_kt1),
                                          (s_bh0, s_bh1), (s_kh0, s_kh1))
    s_y = (s_yf, s_yb)
    lo64 = lax.broadcasted_iota(jnp.int32, (C, LANES), 1) < HALF
    bd_ones = mk_ref[MK_BD]

    def pvrow(i):
        return pv_ref[pl.ds(i, 1), :]

    def prep_tile(xs, tseq, it, row0, ch0):
        xr, xk, xv, xg, xl = xs
        R = min(PREP_ROWS, tseq)
        s = it * R
        lo = lax.broadcasted_iota(jnp.int32, (R, LANES), 1) < HALF
        rows = lax.broadcasted_iota(jnp.int32, (R, LANES), 0)

        def shifted(ref, m0, m1):
            cur = ref[0, s:s + R, :]
            prev = ref[0, s - 1:s - 1 + R, :] if s > 0 else jnp.where(rows == 0, 0.0, pltpu.roll(cur, 1, axis=0))
            nxt = (ref[0, s + 1:s + 1 + R, :] if s + R < tseq
                   else jnp.where(rows == R - 1, 0.0, pltpu.roll(cur, R - 1, axis=0)))
            return cur + m0 * (prev - cur) + m1 * (nxt - cur)

        r = shifted(xr, pvrow(PV_MR0), pvrow(PV_MR1))
        k = shifted(xk, pvrow(PV_MK0), pvrow(PV_MK1))
        v = shifted(xv, pvrow(PV_MV0), pvrow(PV_MV1))
        g = shifted(xg, pvrow(PV_MG0), pvrow(PV_MG1))
        lor = shifted(xl, mixl_ref[0:1, :], mixl_ref[1:2, :])
        th = jnp.tanh(lor)
        kkf = k * pvrow(PV_KK)
        kk = kkf * lax.rsqrt(_seg_sum(kkf * kkf, lo) + 1e-6)
        ka = pvrow(PV_KA)
        out = slice(row0 + s, row0 + s + R)
        kd_sum = None
        for d in range(2):
            w_pre = pvrow(PV_W00 + d) + _bdot(th, wup_ref[d])
            lw = (-np.exp(-0.5)) * jax.nn.sigmoid(w_pre)
            ag = jax.nn.sigmoid(pvrow(PV_A00 + d) + _bdot(lor, aup_ref[d]))
            kd = k * (1.0 + (ag - 1.0) * ka)
            bdv = kk * ag
            kd_sum = kd if kd_sum is None else kd_sum + kd
            c_fwd = _chunk_cumsum(lw, False, C)
            tot = jnp.concatenate([jnp.broadcast_to(c_fwd[(c + 1) * C - 1:(c + 1) * C, :], (C, LANES))
                                   for c in range(R // C)], axis=0)
            c_bwd = tot - c_fwd + lw
            cw, rest = (c_fwd, c_bwd - lw) if d == 0 else (c_bwd, c_fwd - lw)
            e_in = jnp.exp(-cw)
            e_out = jnp.exp(rest)
            s_at[d][out, :] = (-kk * jnp.exp(cw - lw)).astype(BF16)
            s_rt[d][out, :] = (r * jnp.exp(cw)).astype(BF16)
            s_bt[d][out, :] = (bdv * e_in).astype(BF16)
            s_kt[d][out, :] = (kd * e_in).astype(BF16)
            s_bh[d][out, :] = (bdv * e_out).astype(BF16)
            s_kh[d][out, :] = (kd * e_out).astype(BF16)
            for c in range(R // C):
                last = c * C + (C - 1 if d == 0 else 0)
                n = ch0 + it * (R // C) + c
                s_et[d, n:n + 1, :] = jnp.exp(cw[last:last + 1, :])
        s_v[out, :] = v.astype(BF16)
        s_g[out, :] = g
        s_bonus[out, :] = _seg_sum(r * kd_sum * pvrow(PV_RK), lo) * v

    def prep_stages(xs, tseq, tiles, row0, ch0):
        for it in tiles:
            prep_tile(xs, tseq, it, row0, ch0)
            yield

    def p1_stages(probs, row0, ch0):
        ds = [d for _, d in probs]
        rows = [pl.ds(pl.multiple_of(row0 + n * C, C), C) for n, _ in probs]
        a2s, gms = [], []
        for (n, d), rw in zip(probs, rows):
            at, rt = s_at[d][rw, :], s_rt[d][rw, :]
            zero = jnp.zeros_like(at)
            a2 = jnp.concatenate([jnp.where(lo64, at, zero), jnp.where(lo64, zero, at)], axis=0)
            l1 = jnp.concatenate([a2, jnp.where(lo64, rt, zero), jnp.where(lo64, zero, rt)], axis=0)
            a2s.append(a2)
            gms.append(_bdot_nt(l1, jnp.concatenate([s_bt[d][rw, :], s_kt[d][rw, :]], axis=0)))
        yield
        ps, a_list = [], []
        for (n, d), rw, gm in zip(probs, rows, gms):
            base = d * MK_PER_DIR
            ga = gm[0:2 * C]
            v = s_v[rw, :]
            ps.append(_bdot(ga * mk_ref[base + MK_P], jnp.concatenate([v, v], axis=0)))
            s_gr[d, ch0 + n] = (gm[2 * C:4 * C] * mk_ref[base + MK_Y]).astype(BF16)
            ga_sw = jnp.concatenate([ga[0:C], pltpu.roll(ga[C:2 * C], HALF, axis=1)], axis=0)
            a_list.append(ga_sw * mk_ref[base + MK_ABD])
        yield
        xs = [mk_ref[MK_EYE] + a * mk_ref[d * MK_PER_DIR + MK_L1] for a, d in zip(a_list, ds)]
        for li in range(N_LV):
            t1 = [_bdot(x, a * mk_ref[d * MK_PER_DIR + MK_LV + li]) for x, a, d in zip(xs, a_list, ds)]
            yield
            xs = [x + _bdot(t, x) for x, t in zip(xs, t1)]
            yield
        for (n, d), x, a2, p in zip(probs, xs, a2s, ps):
            wtp = _bdot(x, jnp.concatenate([a2.astype(F32), p], axis=1))
            s_w[d, ch0 + n] = wtp[:, 0:LANES].astype(BF16)
            s_tp[d, ch0 + n] = jnp.where(lo64, wtp[0:C, LANES:2 * LANES], wtp[C:2 * C, LANES:2 * LANES])
        yield

    def p2_stages(steps, row0, ch0):
        for ns0 in steps:
            ns = [ch0 + n for n in ns0]
            rows = [pl.ds(pl.multiple_of(row0 + ns0[d] * C, C), C) for d in range(2)]
            st = [s_state[d] for d in range(2)]
            xs = [_bdot_nt(jnp.concatenate([s_w[d, ns[d]], s_rt[d][rows[d], :]], axis=0), st[d]) for d in range(2)]
            yield
            uv = [jnp.concatenate([(xs[d][0:C] + xs[d][C:2 * C] + s_tp[d, ns[d]]).astype(BF16), s_v[rows[d], :]], axis=0)
                  for d in range(2)]
            yb = [_bdot(s_gr[d, ns[d]], uv[d]) for d in range(2)]
            for d in range(2):
                s_y[d][rows[d], :] = xs[d][2 * C:3 * C] + jnp.where(lo64, yb[d][0:C], yb[d][C:2 * C])
                bk = jnp.concatenate([s_bh[d][rows[d], :], s_kh[d][rows[d], :]], axis=0)
                s_state[d] = st[d] * s_et[d, pl.ds(ns[d], 1), :] + bd_ones * _bdot_tn(uv[d], bk)
            yield

    def groups(tseq):
        nch = tseq // C
        per = min(p1_chunks, nch)
        probs = lambda g: [(g * per + j, 0) for j in range(per)] + [(nch - 1 - (g * per + j), 1) for j in range(per)]
        steps = lambda g: [(g * per + j, nch - 1 - (g * per + j)) for j in range(per)]
        return nch // per, per, probs, steps

    def epilogue_tile(o_ref, tseq, it, row0):
        R = min(PREP_ROWS, tseq)
        lo = lax.broadcasted_iota(jnp.int32, (R, LANES), 1) < HALF
        rows = slice(row0 + it * R, row0 + (it + 1) * R)
        y = s_yf[rows, :] + s_yb[rows, :]
        mu = _seg_sum(y, lo) * (1.0 / HALF)
        yc = y - mu
        var = _seg_sum(yc * yc, lo) * (1.0 / HALF)
        yn = yc * lax.rsqrt(var + RWKV_GN_EPS) * pvrow(PV_GNG) + pvrow(PV_GNB)
        g = s_g[rows, :]
        o_ref[0, it * R:(it + 1) * R, :] = (yn + s_bonus[rows, :]) * (g * jax.nn.sigmoid(g))

    def epilogue_stages(o_ref, tseq, tiles, row0):
        for it in tiles:
            epilogue_tile(o_ref, tseq, it, row0)
            yield

    lat, ctx = (lr, lk, lv, lg, ll), (cr, ck, cv, cg, cl)
    row_c, ch_c = t_lat, t_lat // C
    s_state[...] = jnp.zeros_like(s_state)
    _interleave(prep_stages(ctx, t_ctx, range(t_ctx // min(PREP_ROWS, t_ctx)), row_c, ch_c))
    ng_c, _, probs_c, steps_c = groups(t_ctx)
    ng, per, probs, steps = groups(t_lat)
    ntile = t_lat // min(PREP_ROWS, t_lat)
    tpg = max(per * C // min(PREP_ROWS, t_lat), 1)
    first = sorted(set(list(range(min(tpg, ntile))) + list(range(max(ntile - tpg, 0), ntile))))
    rest = [t for t in range(ntile) if t not in first]

    def ctx_scan():
        for g in range(ng_c):
            yield from p1_stages(probs_c(g), row_c, ch_c)
            yield from p2_stages(steps_c(g), row_c, ch_c)

    _interleave(prep_stages(lat, t_lat, first, 0, 0), ctx_scan())
    _interleave(prep_stages(lat, t_lat, rest, 0, 0), p1_stages(probs(0), 0, 0),
                epilogue_stages(o_ctx, t_ctx, range(t_ctx // min(PREP_ROWS, t_ctx)), row_c))
    if ng > 1:
        def body(g, carry):
            _interleave(p1_stages(probs(g + 1), 0, 0), p2_stages(steps(g), 0, 0))
            return carry

        lax.fori_loop(0, ng - 1, body, 0)
    done = [t for t in range(ntile) if per * C <= t * min(PREP_ROWS, t_lat) and (t + 1) * min(PREP_ROWS, t_lat) <= (ng - 1) * per * C]
    _interleave(p2_stages(steps(ng - 1), 0, 0), epilogue_stages(o_lat, t_lat, done, 0))
    _interleave(epilogue_stages(o_lat, t_lat, [t for t in range(ntile) if t not in done], 0))


def _col_block_map(b, p, *, off):
    return (b, 0, off + p)


def rwkv_mixer(u_lat, u_ctx, pv, mixl, wup, aup, p1_chunks=8):
    bsz, t_lat, _ = u_lat.shape
    t_ctx = u_ctx.shape[1]
    npair = RWKV_HEADS // 2
    c0 = EV_RWKV_COL // LANES
    nch = (t_lat + t_ctx) // RWKV_CHUNK
    t_all = t_lat + t_ctx
    masks = jnp.asarray(rwkv_masks())

    def seq_specs(t):
        return [pl.BlockSpec((1, t, LANES), functools.partial(_col_block_map, off=c0 + o * npair)) for o in range(4)] + [
            pl.BlockSpec((1, t, LANES), lambda b, p: (b, 0, EV_LORA_COL // LANES))]

    in_specs = seq_specs(t_lat) + seq_specs(t_ctx) + [
        pl.BlockSpec((PV_ROWS, LANES), lambda b, p: (0, p)),
        pl.BlockSpec((8, LANES), lambda b, p: (0, 0)),
        pl.BlockSpec((2, LANES, LANES), lambda b, p: (0, 0, p)),
        pl.BlockSpec((2, LANES, LANES), lambda b, p: (0, 0, p)),
        pl.BlockSpec((N_MASKS, LANES, LANES), lambda b, p: (0, 0, 0)),
    ]
    out_specs = [pl.BlockSpec((1, t_lat, LANES), lambda b, p: (b, 0, p)),
                 pl.BlockSpec((1, t_ctx, LANES), lambda b, p: (b, 0, p))]
    scratch = ([pltpu.VMEM((t_all, LANES), BF16) for _ in range(13)] + [pltpu.VMEM((t_all, LANES), F32) for _ in range(4)]
               + [pltpu.VMEM((2, nch, LANES, LANES), BF16), pltpu.VMEM((2, nch, RWKV_CHUNK, LANES), F32),
                  pltpu.VMEM((2, nch, LANES, LANES), BF16)]
               + [pltpu.VMEM((2, nch, LANES), F32), pltpu.VMEM((2, LANES, LANES), F32)])
    kern = functools.partial(_rwkv_kernel, t_lat=t_lat, t_ctx=t_ctx, p1_chunks=p1_chunks)
    return pl.pallas_call(
        kern,
        grid=(bsz, npair),
        in_specs=in_specs,
        out_specs=out_specs,
        out_shape=[jax.ShapeDtypeStruct((bsz, t_lat, RWKV_WIDTH), F32), jax.ShapeDtypeStruct((bsz, t_ctx, RWKV_WIDTH), F32)],
        scratch_shapes=scratch,
        compiler_params=pltpu.CompilerParams(dimension_semantics=("arbitrary", "arbitrary"),
                                             vmem_limit_bytes=V7X_VMEM_LIMIT_BYTES),
        name="rwkv7_chunked",
    )(*([u_lat] * 5), *([u_ctx] * 5), pv, mixl, wup, aup, masks)


def rwkv_params(mix, w0, w_up, a0, a_up, k_k, k_a, r_k, gn_g, gn_b):
    wd = RWKV_WIDTH
    rows = [mix[0, 0:wd], mix[1, 0:wd], mix[0, wd:2 * wd], mix[1, wd:2 * wd], mix[0, 2 * wd:3 * wd], mix[1, 2 * wd:3 * wd],
            mix[0, 3 * wd:4 * wd], mix[1, 3 * wd:4 * wd], w0[0], w0[1], a0[0], a0[1], k_k, k_a, r_k.reshape(-1), gn_g, gn_b]
    pv = jnp.concatenate([jnp.stack(rows), jnp.zeros((PV_ROWS - len(rows), wd), F32)], 0)
    mixl = jnp.concatenate([mix[:, 4 * wd:], jnp.zeros((6, LANES), F32)], 0)
    z = jnp.zeros((2, HALF, wd), F32)
    wup = jnp.concatenate([w_up, z], 1)
    aup = jnp.concatenate([z, a_up], 1)
    return pv, mixl, wup, aup


GP_CONV_Q, GP_CONV_K, GP_CONV_V = 0, 5, 10
GP_NORM, GP_ALOG0, GP_ALOG1, GP_DTB0, GP_DTB1 = 15, 16, 17, 18, 19
GP_ROWS = 24
GM_STRICT, GM_INCL, GM_L1, GM_LV = 0, 1, 2, 3
GN_LV = 6
GM_PER_DIR = GM_LV + GN_LV
GM_EYE = 2 * GM_PER_DIR
GN_MASKS = GM_EYE + 1
GDN_P1_CHUNKS = 4


def gdn_masks():
    i = np.arange(GDN_CHUNK)[:, None]
    j = np.arange(GDN_CHUNK)[None, :]
    out = np.zeros((GN_MASKS, GDN_CHUNK, GDN_CHUNK), np.float32)
    for d in range(2):
        base = d * GM_PER_DIR
        out[base + GM_STRICT] = (j < i) if d == 0 else (j > i)
        out[base + GM_INCL] = (j <= i) if d == 0 else (j >= i)
        for li in range(GN_LV + 1):
            s = 1 << li
            blk = (i // (2 * s)) == (j // (2 * s))
            m = blk & (((i & s) != 0) & ((j & s) == 0) if d == 0 else ((i & s) == 0) & ((j & s) != 0))
            out[base + (GM_L1 if li == 0 else GM_LV + li - 1)] = m
    out[GM_EYE] = i == j
    return out


def _gdn_kernel(lq, lk, lv, lz, lbg, cq, ck, cv, cz, cbg, gp_ref, mk_ref, o_lat, o_ctx,
                s_q, s_k, s_kb0, s_kb1, s_kbg0, s_kbg1, s_qg0, s_qg1, s_kg0, s_kg1, s_vb0, s_vb1,
                s_gc0, s_gc1, s_of, s_ob, s_tk, s_tvb, s_a, s_gl, s_state, *, t_lat, t_ctx):
    C = GDN_CHUNK
    s_kb, s_kbg, s_qg, s_kg, s_vb = (s_kb0, s_kb1), (s_kbg0, s_kbg1), (s_qg0, s_qg1), (s_kg0, s_kg1), (s_vb0, s_vb1)
    s_gc, s_o = (s_gc0, s_gc1), (s_of, s_ob)

    def gprow(i):
        return gp_ref[pl.ds(i, 1), :]

    def prep_tile(xs, tseq, it, row0, ch0):
        xq, xk, xv, xbg = xs
        R = min(PREP_ROWS, tseq)
        s = it * R

        def conv_silu(ref, prow0):
            rows = lax.broadcasted_iota(jnp.int32, (R, LANES), 0)
            acc = None
            for j in range(5):
                off = j - 2
                lo_r, hi_r = s + off, s + off + R
                if lo_r < 0:
                    x = jnp.where(rows >= -off, pltpu.roll(ref[0, 0:R, :], -off, axis=0), 0.0)
                elif hi_r > tseq:
                    x = jnp.where(rows < R - off, pltpu.roll(ref[0, tseq - R:tseq, :], R - off, axis=0), 0.0)
                else:
                    x = ref[0, lo_r:hi_r, :]
                term = gprow(prow0 + j) * x
                acc = term if acc is None else acc + term
            return acc * jax.nn.sigmoid(acc)

        qc = conv_silu(xq, GP_CONV_Q)
        kc = conv_silu(xk, GP_CONV_K)
        v = conv_silu(xv, GP_CONV_V)
        q = qc * (lax.rsqrt(jnp.sum(qc * qc, axis=1, keepdims=True) + 1e-6) * (LANES ** -0.5))
        k = kc * lax.rsqrt(jnp.sum(kc * kc, axis=1, keepdims=True) + 1e-6)
        bg = xbg[0, 0, s:s + R, :]
        out = slice(row0 + s, row0 + s + R)
        s_q[out, :] = q.astype(BF16)
        s_k[out, :] = k.astype(BF16)
        for d in range(2):
            beta = jax.nn.sigmoid(bg[:, d:d + 1])
            g = -jnp.exp(gprow(GP_ALOG0 + d)) * jax.nn.softplus(bg[:, 2 + d:3 + d] + gprow(GP_DTB0 + d))
            c_fwd = _chunk_cumsum(g, False, C)
            tot = jnp.concatenate([jnp.broadcast_to(c_fwd[(c + 1) * C - 1:(c + 1) * C, :], (C, LANES))
                                   for c in range(R // C)], axis=0)
            c_bwd = tot - c_fwd + g
            gc, rest = (c_fwd, c_bwd - g) if d == 0 else (c_bwd, c_fwd - g)
            eg = jnp.exp(gc)
            kb = k * beta
            s_kb[d][out, :] = kb.astype(BF16)
            s_kbg[d][out, :] = (kb * eg).astype(BF16)
            s_qg[d][out, :] = (q * eg).astype(BF16)
            s_kg[d][out, :] = (k * jnp.exp(rest)).astype(BF16)
            s_vb[d][out, :] = (v * beta).astype(BF16)
            s_gc[d][out, :] = gc
            for c in range(R // C):
                last = c * C + (C - 1 if d == 0 else 0)
                n = ch0 + it * (R // C) + c
                s_gl[d, n:n + 1, :] = eg[last:last + 1, :]

    def prep_stages(xs, tseq, tiles, row0, ch0):
        for it in tiles:
            prep_tile(xs, tseq, it, row0, ch0)
            yield

    def p1_stages(probs, row0, ch0):
        ds = [d for _, d in probs]
        rows = [pl.ds(pl.multiple_of(row0 + n * C, C), C) for n, _ in probs]
        gms = [_bdot_nt(jnp.concatenate([s_kb[d][rw, :], s_q[rw, :]], axis=0), s_k[rw, :])
               for (n, d), rw in zip(probs, rows)]
        yield
        a_list = []
        for (n, d), rw, gm in zip(probs, rows, gms):
            base = d * GM_PER_DIR
            gc = s_gc[d][rw, :]
            m_incl = mk_ref[base + GM_INCL]
            dec = jnp.exp((gc - gc.T) * m_incl)
            s_a[d, ch0 + n] = (gm[C:2 * C] * dec * m_incl).astype(BF16)
            a_list.append(-(gm[0:C] * dec * mk_ref[base + GM_STRICT]))
        xs = [mk_ref[GM_EYE] + a * mk_ref[d * GM_PER_DIR + GM_L1] for a, d in zip(a_list, ds)]
        for li in range(GN_LV):
            t1 = [_bdot(x, a * mk_ref[d * GM_PER_DIR + GM_LV + li]) for x, a, d in zip(xs, a_list, ds)]
            yield
            xs = [x + _bdot(t, x) for x, t in zip(xs, t1)]
            yield
        for (n, d), rw, x in zip(probs, rows, xs):
            tkv = _bdot(x, jnp.concatenate([s_kbg[d][rw, :], s_vb[d][rw, :]], axis=1))
            s_tk[d, ch0 + n] = tkv[:, 0:LANES].astype(BF16)
            s_tvb[d, ch0 + n] = tkv[:, LANES:2 * LANES]
        yield

    def p2_stages(steps, row0, ch0):
        for ns0 in steps:
            ns = [ch0 + n for n in ns0]
            rows = [pl.ds(pl.multiple_of(row0 + ns0[d] * C, C), C) for d in range(2)]
            st = [s_state[d] for d in range(2)]
            aq = [_bdot(jnp.concatenate([s_tk[d, ns[d]], s_qg[d][rows[d], :]], axis=0), st[d]) for d in range(2)]
            yield
            vnew = [s_tvb[d, ns[d]] - aq[d][0:C] for d in range(2)]
            oo = [aq[d][C:2 * C] + _bdot(s_a[d, ns[d]], vnew[d]) for d in range(2)]
            for d in range(2):
                s_o[d][rows[d], :] = oo[d]
                s_state[d] = st[d] * s_gl[d, pl.ds(ns[d], 1), :] + _bdot_tn(s_kg[d][rows[d], :], vnew[d])
            yield

    def groups(tseq):
        nch = tseq // C
        per = min(GDN_P1_CHUNKS, nch)
        probs = lambda g: [(g * per + j, 0) for j in range(per)] + [(nch - 1 - (g * per + j), 1) for j in range(per)]
        steps = lambda g: [(g * per + j, nch - 1 - (g * per + j)) for j in range(per)]
        return nch // per, per, probs, steps

    def epilogue_tile(o_ref, z_ref, tseq, it, row0):
        R = min(PREP_ROWS, tseq)
        rows = slice(row0 + it * R, row0 + (it + 1) * R)
        o = s_of[rows, :] + s_ob[rows, :]
        o = o * lax.rsqrt(jnp.mean(o * o, axis=1, keepdims=True) + 1e-6) * gprow(GP_NORM)
        z = z_ref[0, it * R:(it + 1) * R, :]
        o_ref[0, it * R:(it + 1) * R, :] = o * (z * jax.nn.sigmoid(z))

    def epilogue_stages(o_ref, z_ref, tseq, tiles, row0):
        for it in tiles:
            epilogue_tile(o_ref, z_ref, tseq, it, row0)
            yield

    lat, ctx = (lq, lk, lv, lbg), (cq, ck, cv, cbg)
    row_c, ch_c = t_lat, t_lat // C
    s_state[...] = jnp.zeros_like(s_state)
    _interleave(prep_stages(ctx, t_ctx, range(t_ctx // min(PREP_ROWS, t_ctx)), row_c, ch_c))
    ng_c, _, probs_c, steps_c = groups(t_ctx)
    ng, per, probs, steps = groups(t_lat)
    ntile = t_lat // min(PREP_ROWS, t_lat)
    tpg = max(per * C // min(PREP_ROWS, t_lat), 1)
    first = sorted(set(list(range(min(tpg, ntile))) + list(range(max(ntile - tpg, 0), ntile))))
    rest = [t for t in range(ntile) if t not in first]

    def ctx_scan():
        for g in range(ng_c):
            yield from p1_stages(probs_c(g), row_c, ch_c)
            yield from p2_stages(steps_c(g), row_c, ch_c)

    _interleave(prep_stages(lat, t_lat, first, 0, 0), ctx_scan())
    _interleave(prep_stages(lat, t_lat, rest, 0, 0), p1_stages(probs(0), 0, 0),
                epilogue_stages(o_ctx, cz, t_ctx, range(t_ctx // min(PREP_ROWS, t_ctx)), row_c))
    if ng > 1:
        def body(g, carry):
            _interleave(p1_stages(probs(g + 1), 0, 0), p2_stages(steps(g), 0, 0))
            return carry

        lax.fori_loop(0, ng - 1, body, 0)
    done = [t for t in range(ntile) if per * C <= t * min(PREP_ROWS, t_lat) and (t + 1) * min(PREP_ROWS, t_lat) <= (ng - 1) * per * C]
    _interleave(p2_stages(steps(ng - 1), 0, 0), epilogue_stages(o_lat, lz, t_lat, done, 0))
    _interleave(epilogue_stages(o_lat, lz, t_lat, [t for t in range(ntile) if t not in done], 0))


def gdn_mixer(u_lat, bg_lat, u_ctx, bg_ctx, gp):
    bsz, t_lat, _ = u_lat.shape
    t_ctx = u_ctx.shape[1]
    nh = GDN_HEADS
    masks = jnp.asarray(gdn_masks())

    def seq_specs(t):
        return [pl.BlockSpec((1, t, LANES), functools.partial(_col_block_map, off=o * nh)) for o in range(4)] + [
            pl.BlockSpec((1, 1, t, 4), lambda b, h: (b, h, 0, 0))]

    in_specs = seq_specs(t_lat) + seq_specs(t_ctx) + [
        pl.BlockSpec((GP_ROWS, LANES), lambda b, h: (0, h)),
        pl.BlockSpec((GN_MASKS, LANES, LANES), lambda b, h: (0, 0, 0)),
    ]
    out_specs = [pl.BlockSpec((1, t_lat, LANES), lambda b, h: (b, 0, h)),
                 pl.BlockSpec((1, t_ctx, LANES), lambda b, h: (b, 0, h))]
    nch = (t_lat + t_ctx) // GDN_CHUNK
    t_all = t_lat + t_ctx
    scratch = ([pltpu.VMEM((t_all, LANES), BF16) for _ in range(12)] + [pltpu.VMEM((t_all, LANES), F32) for _ in range(4)]
               + [pltpu.VMEM((2, nch, LANES, LANES), BF16), pltpu.VMEM((2, nch, LANES, LANES), F32),
                  pltpu.VMEM((2, nch, LANES, LANES), BF16),
                  pltpu.VMEM((2, max(nch, 8), LANES), F32), pltpu.VMEM((2, LANES, LANES), F32)])
    kern = functools.partial(_gdn_kernel, t_lat=t_lat, t_ctx=t_ctx)
    return pl.pallas_call(
        kern,
        grid=(bsz, nh),
        in_specs=in_specs,
        out_specs=out_specs,
        out_shape=[jax.ShapeDtypeStruct((bsz, t_lat, GDN_WIDTH), F32), jax.ShapeDtypeStruct((bsz, t_ctx, GDN_WIDTH), F32)],
        scratch_shapes=scratch,
        compiler_params=pltpu.CompilerParams(dimension_semantics=("arbitrary", "arbitrary"),
                                             vmem_limit_bytes=V7X_VMEM_LIMIT_BYTES),
        name="gdn_chunked",
    )(*([u_lat] * 4), bg_lat, *([u_ctx] * 4), bg_ctx, gp, masks)


def gdn_params(conv_w, a_log, dt_bias, norm_g):
    wd = GDN_WIDTH
    rep = lambda x: jnp.repeat(x, LANES)
    rows = [conv_w[j, o * wd:(o + 1) * wd] for o in range(3) for j in range(5)]
    rows += [jnp.tile(norm_g, GDN_HEADS), rep(a_log[0]), rep(a_log[1]), rep(dt_bias[0]), rep(dt_bias[1])]
    return jnp.concatenate([jnp.stack(rows), jnp.zeros((GP_ROWS - len(rows), wd), F32)], 0)


def gdn_bg(u_small):
    bsz, t, _ = u_small.shape
    x = u_small.reshape(bsz, t, 2, 2, GDN_HEADS)
    return jnp.transpose(x, (0, 4, 1, 2, 3)).reshape(bsz, GDN_HEADS, t, 4)


OD_Q_COL, OD_GATE_COL, OD_KV_COL = 0, ATT_WIDTH, 2 * ATT_WIDTH
OD_KV_W = 4 * ATT_HEAD
OD_COLS = OD_KV_COL + ATT_KV_HEADS * OD_KV_W


def rope_lane_tables(t):
    pos = np.arange(t)
    inv = ROPE_BASE ** (-np.arange(ROPE_FREQS, dtype=np.float64) / ROPE_FREQS)
    ang_row = (pos // GRID_W)[:, None] * inv
    ang_col = (pos % GRID_W)[:, None] * inv
    ang = np.concatenate([ang_row, ang_row, ang_col, ang_col] * 2, axis=1)
    first = (np.arange(LANES) % (2 * ROPE_FREQS)) < ROPE_FREQS
    cos, sin = np.cos(ang), np.sin(ang)
    return (jnp.asarray(cos, F32), jnp.asarray(np.where(first, -sin, 0.0), F32), jnp.asarray(np.where(first, 0.0, sin), F32))


def _rope(x, cos, sina, sinb):
    n = x.shape[1]
    return x * cos + pltpu.roll(x, n - ROPE_FREQS, axis=1) * sina + pltpu.roll(x, ROPE_FREQS, axis=1) * sinb


def _kv_rope_kernel(u_ref, cos_ref, sina_ref, sinb_ref, o_ref):
    x = u_ref[0]
    cos, sina, sinb = cos_ref[...], sina_ref[...], sinb_ref[...]
    parts = []
    for h in range(ATT_KV_HEADS):
        k2 = x[:, h * OD_KV_W:h * OD_KV_W + LANES]
        parts += [_rope(k2, cos, sina, sinb), x[:, h * OD_KV_W + LANES:(h + 1) * OD_KV_W]]
    o_ref[0] = jnp.concatenate(parts, axis=1).astype(BF16)


def kv_rope(u, tables, tr=512):
    bsz, t, _ = u.shape
    tr = min(tr, t)
    kvw = ATT_KV_HEADS * OD_KV_W
    tab = pl.BlockSpec((tr, LANES), lambda b, i: (i, 0))
    return pl.pallas_call(
        _kv_rope_kernel,
        grid=(bsz, t // tr),
        in_specs=[pl.BlockSpec((1, tr, kvw), lambda b, i: (b, i, OD_KV_COL // kvw)), tab, tab, tab],
        out_specs=pl.BlockSpec((1, tr, kvw), lambda b, i: (b, i, 0)),
        out_shape=jax.ShapeDtypeStruct((bsz, t, kvw), BF16),
        compiler_params=pltpu.CompilerParams(dimension_semantics=("arbitrary", "arbitrary")),
        name="kv_rope",
    )(u, *tables)


def _attn_kernel(sink_ref, *refs, has_local, nb):
    if has_local:
        q_ref, g_ref, kvp_ref, kvc_ref, kvn_ref, cos_ref, sina_ref, sinb_ref, kvx_ref, o_ref = refs
    else:
        q_ref, g_ref, kvx_ref, o_ref = refs
    tq = q_ref.shape[1]
    i = pl.program_id(1)
    q = q_ref[0]
    if has_local:
        rep = ATT_WIDTH // LANES
        cos, sina, sinb = (jnp.concatenate([r[...]] * rep, axis=1) for r in (cos_ref, sina_ref, sinb_ref))
        q = _rope(q, cos, sina, sinb)
    q = q * (ATT_HEAD ** -0.5)
    lo = lax.broadcasted_iota(jnp.int32, (tq, LANES), 1) < HALF
    if has_local:
        a = lax.broadcasted_iota(jnp.int32, (tq, WINDOW), 0)
        c = lax.broadcasted_iota(jnp.int32, (tq, WINDOW), 1)
        ok_prev = (c >= a) & (i > 0)
        ok_next = (c <= a) & (i < nb - 1)
    outs = []
    for hk in range(ATT_KV_HEADS):
        kcol = slice(hk * OD_KV_W, hk * OD_KV_W + LANES)
        vcol = slice(hk * OD_KV_W + LANES, (hk + 1) * OD_KV_W)
        qs = []
        for pair in range(ATT_GROUP // 2):
            grp = q[:, (hk * (ATT_GROUP // 2) + pair) * LANES:(hk * (ATT_GROUP // 2) + pair + 1) * LANES]
            qs += [jnp.where(lo, grp, 0.0), jnp.where(lo, 0.0, grp)]
        lhs = jnp.concatenate(qs, axis=0).astype(BF16)
        kx, vx = kvx_ref[0, :, kcol].astype(BF16), kvx_ref[0, :, vcol].astype(BF16)
        if has_local:
            kall = jnp.concatenate([kvp_ref[0, :, kcol], kvc_ref[0, :, kcol], kvn_ref[0, :, kcol], kx], axis=0)
            vall = jnp.concatenate([kvp_ref[0, :, vcol], kvc_ref[0, :, vcol], kvn_ref[0, :, vcol], vx], axis=0)
        else:
            kall, vall = kx, vx
        s = _bdot_nt(lhs, kall)
        lov = lax.broadcasted_iota(jnp.int32, vall.shape, 1) < HALF
        zv = jnp.zeros_like(vall)
        v2 = jnp.concatenate([jnp.where(lov, vall, zv), jnp.where(lov, zv, vall)], axis=0)
        for pair in range(ATT_GROUP // 2):
            ps, inv_l = [], []
            for sub in range(2):
                g = 2 * pair + sub
                sg = s[g * tq:(g + 1) * tq]
                if has_local:
                    sg = jnp.concatenate([jnp.where(ok_prev, sg[:, 0:tq], NEG_INF), sg[:, tq:2 * tq],
                                          jnp.where(ok_next, sg[:, 2 * tq:3 * tq], NEG_INF), sg[:, 3 * tq:]], axis=1)
                sink = sink_ref[hk, g]
                m = jnp.maximum(jnp.max(sg, axis=1, keepdims=True), sink)
                p = jnp.exp(sg - m)
                inv_l.append(1.0 / (jnp.sum(p, axis=1, keepdims=True) + jnp.exp(sink - m)))
                ps.append(p.astype(BF16))
            o_pair = jnp.dot(jnp.concatenate(ps, axis=1), v2, preferred_element_type=F32)
            outs.append(o_pair * jnp.where(lo, inv_l[0], inv_l[1]))
    g = g_ref[0]
    o_ref[0] = jnp.concatenate(outs, axis=1) * (g * jax.nn.sigmoid(g))


def sink_attention(u, kv_local, tables, u_ctx, sink):
    bsz, t, _ = u.shape
    tq = ATT_BLOCK
    nb = t // tq
    tx = u_ctx.shape[1]
    kvw = ATT_KV_HEADS * OD_KV_W
    has_local = kv_local is not None
    in_specs = [pl.BlockSpec(memory_space=pltpu.SMEM),
                pl.BlockSpec((1, tq, ATT_WIDTH), lambda b, i: (b, i, OD_Q_COL // ATT_WIDTH)),
                pl.BlockSpec((1, tq, ATT_WIDTH), lambda b, i: (b, i, OD_GATE_COL // ATT_WIDTH))]
    args = [sink.reshape(ATT_KV_HEADS, ATT_GROUP), u, u]
    if has_local:
        in_specs += [pl.BlockSpec((1, tq, kvw), lambda b, i: (b, jnp.maximum(i - 1, 0), 0)),
                     pl.BlockSpec((1, tq, kvw), lambda b, i: (b, i, 0)),
                     pl.BlockSpec((1, tq, kvw), lambda b, i: (b, jnp.minimum(i + 1, nb - 1), 0))]
        in_specs += [pl.BlockSpec((tq, LANES), lambda b, i: (i, 0))] * 3
        args += [kv_local] * 3 + list(tables)
    in_specs += [pl.BlockSpec((1, tx, kvw), lambda b, i: (b, 0, OD_KV_COL // kvw))]
    args += [u_ctx]
    return pl.pallas_call(
        functools.partial(_attn_kernel, has_local=has_local, nb=nb),
        grid=(bsz, nb),
        in_specs=in_specs,
        out_specs=pl.BlockSpec((1, tq, ATT_WIDTH), lambda b, i: (b, i, 0)),
        out_shape=jax.ShapeDtypeStruct((bsz, t, ATT_WIDTH), F32),
        compiler_params=pltpu.CompilerParams(dimension_semantics=("arbitrary", "arbitrary"),
                                             vmem_limit_bytes=V7X_VMEM_LIMIT_BYTES),
        name="sink_attention",
    )(*args)


def pack_odd_w_in(w_in):
    nq, nkv = ATT_WIDTH, ATT_KV_HEADS * ATT_HEAD
    parts = [w_in[:, :nq], w_in[:, nq + 2 * nkv:]]
    for h in range(ATT_KV_HEADS):
        k = w_in[:, nq + h * ATT_HEAD:nq + (h + 1) * ATT_HEAD]
        v = w_in[:, nq + nkv + h * ATT_HEAD:nq + nkv + (h + 1) * ATT_HEAD]
        parts += [k, k, v, v]
    return jnp.concatenate(parts, 1).astype(BF16)


def pack_even_w_in(w_in):
    g1 = 4 * GDN_WIDTH
    r0 = GDN_IN
    parts = [w_in[:, :g1], w_in[:, r0:r0 + 4 * RWKV_WIDTH], w_in[:, r0 + 4 * RWKV_WIDTH:], w_in[:, g1:GDN_IN],
             jnp.zeros((D_MODEL, EV_COLS - EV_SMALL_COL - 4 * GDN_HEADS), F32)]
    return jnp.concatenate(parts, 1).astype(BF16)


def kernel(x, c, ctx, c_ctx, mod_w, mod_b, ln_g, ln_b, ev_w_in, ev_w_out, gdn_conv, gdn_a_log, gdn_dt_bias,
           gdn_norm_g, rwkv_mix, rwkv_w0, rwkv_w_up, rwkv_a0, rwkv_a_up, rwkv_k_k, rwkv_k_a, rwkv_r_k,
           rwkv_gn_g, rwkv_gn_b, od_w_in, od_w_out, od_sink):
    bsz, t_lat, d = x.shape
    t_ctx = ctx.shape[1]
    x_lat, x_ctx = x, ctx
    n_mod = -(-(bsz + 1) // 8) * 8
    cc = jax.nn.silu(jnp.concatenate([c, c_ctx[None], jnp.zeros((n_mod - bsz - 1, d), F32)], 0))
    mods = mod_all_layers(cc, mod_w, mod_b)
    for layer in range(DEPTH):
        need_ctx = layer < DEPTH - 1
        i = layer // 2
        mod = mods[layer]
        sh, sc, gt = (mod[:bsz, j * d:(j + 1) * d][:, None, :] for j in range(3))
        sh_c, sc_c, gt_c = (jnp.broadcast_to(mod[bsz, j * d:(j + 1) * d], (bsz, 1, d)) for j in range(3))
        if layer % 2 == 0:
            w_in = pack_even_w_in(ev_w_in[i])
            w_out = ev_w_out[i].astype(BF16)
            u_lat = inproj(x_lat, sc, sh, w_in, 1024, EV_TN)
            u_ctx = inproj(x_ctx.reshape(1, bsz * t_ctx, d), sc_c[:1], sh_c[:1], w_in, 1024, EV_TN).reshape(bsz, t_ctx, -1)
            small = slice(EV_SMALL_COL, EV_SMALL_COL + 4 * GDN_HEADS)
            og_lat, og_ctx = gdn_mixer(u_lat, gdn_bg(u_lat[..., small]), u_ctx, gdn_bg(u_ctx[..., small]),
                                       gdn_params(gdn_conv[i], gdn_a_log[i], gdn_dt_bias[i], gdn_norm_g[i]))
            or_lat, or_ctx = rwkv_mixer(u_lat, u_ctx, *rwkv_params(
                rwkv_mix[i], rwkv_w0[i], rwkv_w_up[i], rwkv_a0[i], rwkv_a_up[i], rwkv_k_k[i], rwkv_k_a[i],
                rwkv_r_k[i], rwkv_gn_g[i], rwkv_gn_b[i]))
            a_lat, a_ctx = [og_lat, or_lat], [og_ctx, or_ctx]
        else:
            w_in = pack_odd_w_in(od_w_in[i])
            w_out = od_w_out[i].astype(BF16)
            u_lat = inproj(x_lat, sc, sh, w_in, 1024, OD_TN)
            u_ctx = inproj(x_ctx.reshape(1, bsz * t_ctx, d), sc_c[:1], sh_c[:1], w_in, 1024, OD_TN).reshape(bsz, t_ctx, -1)
            tables = rope_lane_tables(t_lat)
            o_lat = sink_attention(u_lat, kv_rope(u_lat, tables), tables, u_ctx, od_sink[i])
            o_ctx = sink_attention(u_ctx, None, None, u_ctx, od_sink[i]) if need_ctx else None
            a_lat, a_ctx = [o_lat], [o_ctx]
        x_lat = outproj_ln(a_lat, w_out, x_lat, gt, ln_g[layer], ln_b[layer])
        if need_ctx:
            x_ctx = outproj_ln(a_ctx, w_out, x_ctx, gt_c, ln_g[layer], ln_b[layer])
    return x_lat
```

```python
import functools

import jax
import jax.numpy as jnp
import numpy as np
from jax import lax
from jax.experimental import pallas as pl
from jax.experimental.pallas import tpu as pltpu

F32 = jnp.float32
BF16 = jnp.bfloat16

D_MODEL = 2048
DEPTH = 4
GRID_W = 64
GDN_WIDTH = D_MODEL // 2
GDN_HEADS = 8
GDN_CHUNK = 128
RWKV_WIDTH = D_MODEL // 2
RWKV_HEADS = 16
RWKV_CHUNK = 64
RWKV_GN_EPS = 64e-5
ATT_HEAD = 64
ATT_Q_HEADS = D_MODEL // ATT_HEAD
ATT_KV_HEADS = ATT_Q_HEADS // 8
ATT_GROUP = ATT_Q_HEADS // ATT_KV_HEADS
ATT_WIDTH = ATT_Q_HEADS * ATT_HEAD
WINDOW = 128
ATT_BLOCK = 128
ROPE_BASE = 10000.0
ROPE_FREQS = ATT_HEAD // 4
NEG_INF = -1e30
LOG2E = 1.4426950408889634
DEEPNORM_ALPHA = (2 * DEPTH) ** 0.25
LN_EPS = 1e-5
GDN_IN = 4 * GDN_WIDTH + 4 * GDN_HEADS
RWKV_IN = 4 * RWKV_WIDTH + 128

LANES = 128
PREP_ROWS = 256
HALF = 64
V7X_VMEM_LIMIT_BYTES = 56 * 1024 * 1024
EV_GDN_COL = 0
EV_RWKV_COL = 4 * GDN_WIDTH
EV_LORA_COL = EV_RWKV_COL + 4 * RWKV_WIDTH
EV_SMALL_COL = EV_LORA_COL + LANES
EV_COLS = EV_SMALL_COL + LANES
EV_TN = 1408
OD_TN = 1280


def _mod_kernel(a_ref, w_ref, b_ref, o_ref):
    o_ref[0] = jnp.dot(a_ref[...].astype(BF16), w_ref[0].astype(BF16), preferred_element_type=F32) + b_ref[0]


def mod_all_layers(a, w, b, tn=512):
    m, k = a.shape
    nl, _, n = w.shape
    return pl.pallas_call(
        _mod_kernel,
        grid=(nl, n // tn),
        in_specs=[pl.BlockSpec((m, k), lambda l, j: (0, 0)), pl.BlockSpec((1, k, tn), lambda l, j: (l, 0, j)),
                  pl.BlockSpec((1, 1, tn), lambda l, j: (l, 0, j))],
        out_specs=pl.BlockSpec((1, m, tn), lambda l, j: (l, 0, j)),
        out_shape=jax.ShapeDtypeStruct((nl, m, n), F32),
        compiler_params=pltpu.CompilerParams(dimension_semantics=("arbitrary", "arbitrary"),
                                             vmem_limit_bytes=V7X_VMEM_LIMIT_BYTES),
        name="mod_matmul",
    )(a, w, b.reshape(nl, 1, n))


def _inproj_kernel(x_ref, sc_ref, sh_ref, w_ref, o_ref, h_ref):
    @pl.when(pl.program_id(2) == 0)
    def _():
        h_ref[...] = (x_ref[0] * (1.0 + sc_ref[0]) + sh_ref[0]).astype(BF16)

    o_ref[0] = jnp.dot(h_ref[...], w_ref[...], preferred_element_type=F32)


def inproj(x, sc, sh, w, tm, tn):
    bsz, t, d = x.shape
    n = w.shape[1]
    tm = min(tm, t)
    assert t % tm == 0 and n % tn == 0
    return pl.pallas_call(
        _inproj_kernel,
        grid=(bsz, t // tm, n // tn),
        in_specs=[pl.BlockSpec((1, tm, d), lambda b, i, j: (b, i, 0)),
                  pl.BlockSpec((1, 1, d), lambda b, i, j: (b, 0, 0)),
                  pl.BlockSpec((1, 1, d), lambda b, i, j: (b, 0, 0)),
                  pl.BlockSpec((d, tn), lambda b, i, j: (0, j))],
        out_specs=pl.BlockSpec((1, tm, tn), lambda b, i, j: (b, i, j)),
        out_shape=jax.ShapeDtypeStruct((bsz, t, n), F32),
        scratch_shapes=[pltpu.VMEM((tm, d), BF16)],
        compiler_params=pltpu.CompilerParams(dimension_semantics=("arbitrary", "arbitrary", "arbitrary"),
                                             vmem_limit_bytes=V7X_VMEM_LIMIT_BYTES),
        name="inproj",
    )(x, sc, sh, w)


def _outproj_kernel(*refs, n_a):
    a_refs, (w_ref, x_ref, gt_ref, g_ref, b_ref, o_ref) = refs[:n_a], refs[n_a:]
    y = None
    k0 = 0
    for a_ref in a_refs:
        kw = a_ref.shape[2]
        term = jnp.dot(a_ref[0].astype(BF16), w_ref[k0:k0 + kw, :], preferred_element_type=F32)
        y = term if y is None else y + term
        k0 += kw
    z = DEEPNORM_ALPHA * x_ref[0] + gt_ref[0] * y
    mu = jnp.mean(z, axis=1, keepdims=True)
    zc = z - mu
    var = jnp.mean(zc * zc, axis=1, keepdims=True)
    o_ref[0] = zc * lax.rsqrt(var + LN_EPS) * g_ref[...] + b_ref[...]


def outproj_ln(a_list, w, x, gt, ln_g, ln_b, tm=512):
    bsz, t, d = x.shape
    tm = min(tm, t)
    in_specs = [pl.BlockSpec((1, tm, a.shape[2]), lambda b, i: (b, i, 0)) for a in a_list] + [
        pl.BlockSpec(w.shape, lambda b, i: (0, 0)),
        pl.BlockSpec((1, tm, d), lambda b, i: (b, i, 0)),
        pl.BlockSpec((1, 1, d), lambda b, i: (b, 0, 0)),
        pl.BlockSpec((1, d), lambda b, i: (0, 0)),
        pl.BlockSpec((1, d), lambda b, i: (0, 0))]
    return pl.pallas_call(
        functools.partial(_outproj_kernel, n_a=len(a_list)),
        grid=(bsz, t // tm),
        in_specs=in_specs,
        out_specs=pl.BlockSpec((1, tm, d), lambda b, i: (b, i, 0)),
        out_shape=jax.ShapeDtypeStruct((bsz, t, d), F32),
        compiler_params=pltpu.CompilerParams(dimension_semantics=("arbitrary", "arbitrary"),
                                             vmem_limit_bytes=V7X_VMEM_LIMIT_BYTES),
        name="outproj_ln",
    )(*a_list, w, x, gt, ln_g.reshape(1, d), ln_b.reshape(1, d))


def _bdot(a, b):
    return jnp.dot(a.astype(BF16), b.astype(BF16), preferred_element_type=F32)


def _bdot_nt(a, b):
    return lax.dot_general(a.astype(BF16), b.astype(BF16), (((1,), (1,)), ((), ())), preferred_element_type=F32)


def _bdot_tn(a, b):
    return lax.dot_general(a.astype(BF16), b.astype(BF16), (((0,), (0,)), ((), ())), preferred_element_type=F32)


(PV_MR0, PV_MR1, PV_MK0, PV_MK1, PV_MV0, PV_MV1, PV_MG0, PV_MG1, PV_W00, PV_W01, PV_A00, PV_A01,
 PV_KK, PV_KA, PV_RK, PV_GNG, PV_GNB) = range(17)
PV_ROWS = 24
MK_ABD, MK_P, MK_Y, MK_L1 = 0, 1, 2, 3
MK_LV = 4
N_LV = 5
MK_PER_DIR = MK_LV + N_LV
MK_EYE = 2 * MK_PER_DIR
MK_BD = MK_EYE + 1
N_MASKS = MK_BD + 1


def rwkv_masks():
    ri = np.arange(128)[:, None]
    ci = np.arange(128)[None, :]
    i, j = ri & 63, ci & 63
    same = (ri < 64) == (ci < 64)
    out = np.zeros((N_MASKS, 128, 128), np.float32)
    for d in range(2):
        strict = (j < i) if d == 0 else (j > i)
        incl = (j <= i) if d == 0 else (j >= i)
        base = d * MK_PER_DIR
        out[base + MK_ABD] = strict & same
        out[base + MK_P] = strict & (ci >= 64)
        out[base + MK_Y] = incl
        for li, s in enumerate((1, 2, 4, 8, 16, 32)):
            blk = (i // (2 * s)) == (j // (2 * s))
            if d == 0:
                m = blk & ((i & s) != 0) & ((j & s) == 0)
            else:
                m = blk & ((i & s) == 0) & ((j & s) != 0)
            out[base + (MK_L1 if li == 0 else MK_LV + li - 1)] = m & same
    out[MK_EYE] = ri == ci
    out[MK_BD] = same
    return out


def _seg_sum(x, lo):
    s0 = jnp.sum(jnp.where(lo, x, 0.0), axis=1, keepdims=True)
    s1 = jnp.sum(jnp.where(lo, 0.0, x), axis=1, keepdims=True)
    return jnp.where(lo, s0, s1)


def _chunk_cumsum(x, reverse, chunk):
    n = x.shape[0]
    rowm = lax.broadcasted_iota(jnp.int32, x.shape, 0) & (chunk - 1)
    s = 1
    while s < chunk:
        if not reverse:
            x = x + jnp.where(rowm >= s, pltpu.roll(x, s, axis=0), 0.0)
        else:
            x = x + jnp.where(rowm < chunk - s, pltpu.roll(x, n - s, axis=0), 0.0)
        s *= 2
    return x


def _interleave(*gens):
    live = list(gens)
    while live:
        for g in list(live):
            try:
                next(g)
            except StopIteration:
                live.remove(g)


def _rwkv_kernel(lr, lk, lv, lg, ll, cr, ck, cv, cg, cl, pv_ref, mixl_ref, wup_ref, aup_ref, mk_ref,
                 o_lat, o_ctx,
                 s_at0, s_at1, s_rt0, s_rt1, s_bt0, s_bt1, s_kt0, s_kt1, s_bh0, s_bh1, s_kh0, s_kh1, s_v,
                 s_g, s_bonus, s_yf, s_yb, s_w, s_tp, s_gr, s_et, s_state, *, t_lat, t_ctx, p1_chunks):
    C = RWKV_CHUNK
    s_at, s_rt, s_bt, s_kt, s_bh, s_kh = ((s_at0, s_at1), (s_rt0, s_rt1), (s_bt0, s_bt1), (s_kt0, s_kt1),
                                          (s_bh0, s_bh1), (s_kh0, s_kh1))
    s_y = (s_yf, s_yb)
    lo64 = lax.broadcasted_iota(jnp.int32, (C, LANES), 1) < HALF
    bd_ones = mk_ref[MK_BD]

    def pvrow(i):
        return pv_ref[pl.ds(i, 1), :]

    def prep_tile(xs, tseq, it, row0, ch0):
        xr, xk, xv, xg, xl = xs
        R = min(PREP_ROWS, tseq)
        s = it * R
        lo = lax.broadcasted_iota(jnp.int32, (R, LANES), 1) < HALF
        rows = lax.broadcasted_iota(jnp.int32, (R, LANES), 0)

        def shifted(ref, m0, m1):
            cur = ref[0, s:s + R, :]
            prev = ref[0, s - 1:s - 1 + R, :] if s > 0 else jnp.where(rows == 0, 0.0, pltpu.roll(cur, 1, axis=0))
            nxt = (ref[0, s + 1:s + 1 + R, :] if s + R < tseq
                   else jnp.where(rows == R - 1, 0.0, pltpu.roll(cur, R - 1, axis=0)))
            return cur + m0 * (prev - cur) + m1 * (nxt - cur)

        r = shifted(xr, pvrow(PV_MR0), pvrow(PV_MR1))
        k = shifted(xk, pvrow(PV_MK0), pvrow(PV_MK1))
        v = shifted(xv, pvrow(PV_MV0), pvrow(PV_MV1))
        g = shifted(xg, pvrow(PV_MG0), pvrow(PV_MG1))
        lor = shifted(xl, mixl_ref[0:1, :], mixl_ref[1:2, :])
        th = jnp.tanh(lor)
        kkf = k * pvrow(PV_KK)
        kk = kkf * lax.rsqrt(_seg_sum(kkf * kkf, lo) + 1e-6)
        ka = pvrow(PV_KA)
        out = slice(row0 + s, row0 + s + R)
        kd_sum = None
        for d in range(2):
            w_pre = pvrow(PV_W00 + d) + _bdot(th, wup_ref[d])
            lw = (-np.exp(-0.5)) * jax.nn.sigmoid(w_pre)
            ag = jax.nn.sigmoid(pvrow(PV_A00 + d) + _bdot(lor, aup_ref[d]))
            kd = k * (1.0 + (ag - 1.0) * ka)
            bdv = kk * ag
            kd_sum = kd if kd_sum is None else kd_sum + kd
            c_fwd = _chunk_cumsum(lw, False, C)
            tot = jnp.concatenate([jnp.broadcast_to(c_fwd[(c + 1) * C - 1:(c + 1) * C, :], (C, LANES))
                                   for c in range(R // C)], axis=0)
            c_bwd = tot - c_fwd + lw
            cw, rest = (c_fwd, c_bwd - lw) if d == 0 else (c_bwd, c_fwd - lw)
            e_in = jnp.exp(-cw)
            e_out = jnp.exp(rest)
            s_at[d][out, :] = (-kk * jnp.exp(cw - lw)).astype(BF16)
            s_rt[d][out, :] = (r * jnp.exp(cw)).astype(BF16)
            s_bt[d][out, :] = (bdv * e_in).astype(BF16)
            s_kt[d][out, :] = (kd * e_in).astype(BF16)
            s_bh[d][out, :] = (bdv * e_out).astype(BF16)
            s_kh[d][out, :] = (kd * e_out).astype(BF16)
            for c in range(R // C):
                last = c * C + (C - 1 if d == 0 else 0)
                n = ch0 + it * (R // C) + c
                s_et[d, n:n + 1, :] = jnp.exp(cw[last:last + 1, :])
        s_v[out, :] = v.astype(BF16)
        s_g[out, :] = g
        s_bonus[out, :] = _seg_sum(r * kd_sum * pvrow(PV_RK), lo) * v

    def prep_stages(xs, tseq, tiles, row0, ch0):
        for it in tiles:
            prep_tile(xs, tseq, it, row0, ch0)
            yield

    def p1_stages(probs, row0, ch0):
        ds = [d for _, d in probs]
        rows = [pl.ds(pl.multiple_of(row0 + n * C, C), C) for n, _ in probs]
        a2s, gms = [], []
        for (n, d), rw in zip(probs, rows):
            at, rt = s_at[d][rw, :], s_rt[d][rw, :]
            zero = jnp.zeros_like(at)
            a2 = jnp.concatenate([jnp.where(lo64, at, zero), jnp.where(lo64, zero, at)], axis=0)
            l1 = jnp.concatenate([a2, jnp.where(lo64, rt, zero), jnp.where(lo64, zero, rt)], axis=0)
            a2s.append(a2)
            gms.append(_bdot_nt(l1, jnp.concatenate([s_bt[d][rw, :], s_kt[d][rw, :]], axis=0)))
        yield
        ps, a_list = [], []
        for (n, d), rw, gm in zip(probs, rows, gms):
            base = d * MK_PER_DIR
            ga = gm[0:2 * C]
            v = s_v[rw, :]
            ps.append(_bdot(ga * mk_ref[base + MK_P], jnp.concatenate([v, v], axis=0)))
            s_gr[d, ch0 + n] = (gm[2 * C:4 * C] * mk_ref[base + MK_Y]).astype(BF16)
            ga_sw = jnp.concatenate([ga[0:C], pltpu.roll(ga[C:2 * C], HALF, axis=1)], axis=0)
            a_list.append(ga_sw * mk_ref[base + MK_ABD])
        yield
        xs = [mk_ref[MK_EYE] + a * mk_ref[d * MK_PER_DIR + MK_L1] for a, d in zip(a_list, ds)]
        for li in range(N_LV):
            t1 = [_bdot(x, a * mk_ref[d * MK_PER_DIR + MK_LV + li]) for x, a, d in zip(xs, a_list, ds)]
            yield
            xs = [x + _bdot(t, x) for x, t in zip(xs, t1)]
            yield
        for (n, d), x, a2, p in zip(probs, xs, a2s, ps):
            wtp = _bdot(x, jnp.concatenate([a2.astype(F32), p], axis=1))
            s_w[d, ch0 + n] = wtp[:, 0:LANES].astype(BF16)
            s_tp[d, ch0 + n] = jnp.where(lo64, wtp[0:C, LANES:2 * LANES], wtp[C:2 * C, LANES:2 * LANES])
        yield

    def p2_stages(steps, row0, ch0):
        for ns0 in steps:
            ns = [ch0 + n for n in ns0]
            rows = [pl.ds(pl.multiple_of(row0 + ns0[d] * C, C), C) for d in range(2)]
            st = [s_state[d] for d in range(2)]
            xs = [_bdot_nt(jnp.concatenate([s_w[d, ns[d]], s_rt[d][rows[d], :]], axis=0), st[d]) for d in range(2)]
            yield
            uv = [jnp.concatenate([(xs[d][0:C] + xs[d][C:2 * C] + s_tp[d, ns[d]]).astype(BF16), s_v[rows[d], :]], axis=0)
                  for d in range(2)]
            yb = [_bdot(s_gr[d, ns[d]], uv[d]) for d in range(2)]
            for d in range(2):
                s_y[d][rows[d], :] = xs[d][2 * C:3 * C] + jnp.where(lo64, yb[d][0:C], yb[d][C:2 * C])
                bk = jnp.concatenate([s_bh[d][rows[d], :], s_kh[d][rows[d], :]], axis=0)
                s_state[d] = st[d] * s_et[d, pl.ds(ns[d], 1), :] + bd_ones * _bdot_tn(uv[d], bk)
            yield

    def groups(tseq):
        nch = tseq // C
        per = min(p1_chunks, nch)
        probs = lambda g: [(g * per + j, 0) for j in range(per)] + [(nch - 1 - (g * per + j), 1) for j in range(per)]
        steps = lambda g: [(g * per + j, nch - 1 - (g * per + j)) for j in range(per)]
        return nch // per, per, probs, steps

    def epilogue_tile(o_ref, tseq, it, row0):
        R = min(PREP_ROWS, tseq)
        lo = lax.broadcasted_iota(jnp.int32, (R, LANES), 1) < HALF
        rows = slice(row0 + it * R, row0 + (it + 1) * R)
        y = s_yf[rows, :] + s_yb[rows, :]
        mu = _seg_sum(y, lo) * (1.0 / HALF)
        yc = y - mu
        var = _seg_sum(yc * yc, lo) * (1.0 / HALF)
        yn = yc * lax.rsqrt(var + RWKV_GN_EPS) * pvrow(PV_GNG) + pvrow(PV_GNB)
        g = s_g[rows, :]
        o_ref[0, it * R:(it + 1) * R, :] = (yn + s_bonus[rows, :]) * (g * jax.nn.sigmoid(g))

    def epilogue_stages(o_ref, tseq, tiles, row0):
        for it in tiles:
            epilogue_tile(o_ref, tseq, it, row0)
            yield

    lat, ctx = (lr, lk, lv, lg, ll), (cr, ck, cv, cg, cl)
    row_c, ch_c = t_lat, t_lat // C
    s_state[...] = jnp.zeros_like(s_state)
    _interleave(prep_stages(ctx, t_ctx, range(t_ctx // min(PREP_ROWS, t_ctx)), row_c, ch_c))
    ng_c, _, probs_c, steps_c = groups(t_ctx)
    ng, per, probs, steps = groups(t_lat)
    ntile = t_lat // min(PREP_ROWS, t_lat)
    tpg = max(per * C // min(PREP_ROWS, t_lat), 1)
    first = sorted(set(list(range(min(tpg, ntile))) + list(range(max(ntile - tpg, 0), ntile))))
    rest = [t for t in range(ntile) if t not in first]

    def ctx_scan():
        for g in range(ng_c):
            yield from p1_stages(probs_c(g), row_c, ch_c)
            yield from p2_stages(steps_c(g), row_c, ch_c)

    _interleave(prep_stages(lat, t_lat, first, 0, 0), ctx_scan())
    _interleave(prep_stages(lat, t_lat, rest, 0, 0), p1_stages(probs(0), 0, 0),
                epilogue_stages(o_ctx, t_ctx, range(t_ctx // min(PREP_ROWS, t_ctx)), row_c))
    if ng > 1:
        def body(g, carry):
            _interleave(p1_stages(probs(g + 1), 0, 0), p2_stages(steps(g), 0, 0))
            return carry

        lax.fori_loop(0, ng - 1, body, 0)
    done = [t for t in range(ntile) if per * C <= t * min(PREP_ROWS, t_lat) and (t + 1) * min(PREP_ROWS, t_lat) <= (ng - 1) * per * C]
    _interleave(p2_stages(steps(ng - 1), 0, 0), epilogue_stages(o_lat, t_lat, done, 0))
    _interleave(epilogue_stages(o_lat, t_lat, [t for t in range(ntile) if t not in done], 0))


def _col_block_map(b, p, *, off):
    return (b, 0, off + p)


def rwkv_mixer(u_lat, u_ctx, pv, mixl, wup, aup, p1_chunks=8):
    bsz, t_lat, _ = u_lat.shape
    t_ctx = u_ctx.shape[1]
    npair = RWKV_HEADS // 2
    c0 = EV_RWKV_COL // LANES
    nch = (t_lat + t_ctx) // RWKV_CHUNK
    t_all = t_lat + t_ctx
    masks = jnp.asarray(rwkv_masks())

    def seq_specs(t):
        return [pl.BlockSpec((1, t, LANES), functools.partial(_col_block_map, off=c0 + o * npair)) for o in range(4)] + [
            pl.BlockSpec((1, t, LANES), lambda b, p: (b, 0, EV_LORA_COL // LANES))]

    in_specs = seq_specs(t_lat) + seq_specs(t_ctx) + [
        pl.BlockSpec((PV_ROWS, LANES), lambda b, p: (0, p)),
        pl.BlockSpec((8, LANES), lambda b, p: (0, 0)),
        pl.BlockSpec((2, LANES, LANES), lambda b, p: (0, 0, p)),
        pl.BlockSpec((2, LANES, LANES), lambda b, p: (0, 0, p)),
        pl.BlockSpec((N_MASKS, LANES, LANES), lambda b, p: (0, 0, 0)),
    ]
    out_specs = [pl.BlockSpec((1, t_lat, LANES), lambda b, p: (b, 0, p)),
                 pl.BlockSpec((1, t_ctx, LANES), lambda b, p: (b, 0, p))]
    scratch = ([pltpu.VMEM((t_all, LANES), BF16) for _ in range(13)] + [pltpu.VMEM((t_all, LANES), F32) for _ in range(4)]
               + [pltpu.VMEM((2, nch, LANES, LANES), BF16), pltpu.VMEM((2, nch, RWKV_CHUNK, LANES), F32),
                  pltpu.VMEM((2, nch, LANES, LANES), BF16)]
               + [pltpu.VMEM((2, nch, LANES), F32), pltpu.VMEM((2, LANES, LANES), F32)])
    kern = functools.partial(_rwkv_kernel, t_lat=t_lat, t_ctx=t_ctx, p1_chunks=p1_chunks)
    return pl.pallas_call(
        kern,
        grid=(bsz, npair),
        in_specs=in_specs,
        out_specs=out_specs,
        out_shape=[jax.ShapeDtypeStruct((bsz, t_lat, RWKV_WIDTH), F32), jax.ShapeDtypeStruct((bsz, t_ctx, RWKV_WIDTH), F32)],
        scratch_shapes=scratch,
        compiler_params=pltpu.CompilerParams(dimension_semantics=("arbitrary", "arbitrary"),
                                             vmem_limit_bytes=V7X_VMEM_LIMIT_BYTES),
        name="rwkv7_chunked",
    )(*([u_lat] * 5), *([u_ctx] * 5), pv, mixl, wup, aup, masks)


def rwkv_params(mix, w0, w_up, a0, a_up, k_k, k_a, r_k, gn_g, gn_b):
    wd = RWKV_WIDTH
    rows = [mix[0, 0:wd], mix[1, 0:wd], mix[0, wd:2 * wd], mix[1, wd:2 * wd], mix[0, 2 * wd:3 * wd], mix[1, 2 * wd:3 * wd],
            mix[0, 3 * wd:4 * wd], mix[1, 3 * wd:4 * wd], w0[0], w0[1], a0[0], a0[1], k_k, k_a, r_k.reshape(-1), gn_g, gn_b]
    pv = jnp.concatenate([jnp.stack(rows), jnp.zeros((PV_ROWS - len(rows), wd), F32)], 0)
    mixl = jnp.concatenate([mix[:, 4 * wd:], jnp.zeros((6, LANES), F32)], 0)
    z = jnp.zeros((2, HALF, wd), F32)
    wup = jnp.concatenate([w_up, z], 1)
    aup = jnp.concatenate([z, a_up], 1)
    return pv, mixl, wup, aup


GP_CONV_Q, GP_CONV_K, GP_CONV_V = 0, 5, 10
GP_NORM, GP_SC_ALOG, GP_SC_DTB = 15, 16, 17
GP_ROWS = 24
GM_STRICT, GM_INCL, GM_L1, GM_LV = 0, 1, 2, 3
GN_LV = 6
GM_PER_DIR = GM_LV + GN_LV
GM_EYE = 2 * GM_PER_DIR
GN_MASKS = GM_EYE + 1
GDN_P1_CHUNKS = 4


def gdn_masks():
    i = np.arange(GDN_CHUNK)[:, None]
    j = np.arange(GDN_CHUNK)[None, :]
    out = np.zeros((GN_MASKS, GDN_CHUNK, GDN_CHUNK), np.float32)
    for d in range(2):
        base = d * GM_PER_DIR
        out[base + GM_STRICT] = (j < i) if d == 0 else (j > i)
        out[base + GM_INCL] = (j <= i) if d == 0 else (j >= i)
        for li in range(GN_LV + 1):
            s = 1 << li
            blk = (i // (2 * s)) == (j // (2 * s))
            m = blk & (((i & s) != 0) & ((j & s) == 0) if d == 0 else ((i & s) == 0) & ((j & s) != 0))
            out[base + (GM_L1 if li == 0 else GM_LV + li - 1)] = m
    out[GM_EYE] = i == j
    return out


def _gdn_kernel(lq, lk, lv, lz, lbg, cq, ck, cv, cz, cbg, gp_ref, mk_ref, o_lat, o_ctx,
                s_q, s_k, s_kb0, s_kb1, s_kbg0, s_kbg1, s_qg0, s_qg1, s_kg0, s_kg1, s_vb0, s_vb1,
                s_gc0, s_gc1, s_of, s_ob, s_tk, s_tvb, s_a, s_gl, s_state, *, t_lat, t_ctx):
    C = GDN_CHUNK
    s_kb, s_kbg, s_qg, s_kg, s_vb = (s_kb0, s_kb1), (s_kbg0, s_kbg1), (s_qg0, s_qg1), (s_kg0, s_kg1), (s_vb0, s_vb1)
    s_gc, s_o = (s_gc0, s_gc1), (s_of, s_ob)

    def gprow(i):
        return gp_ref[pl.ds(i, 1), :]

    def prep_tile(xs, tseq, it, row0, ch0):
        xq, xk, xv, xbg = xs
        R = min(PREP_ROWS, tseq)
        s = it * R

        def conv_silu(ref, prow0):
            rows = lax.broadcasted_iota(jnp.int32, (R, LANES), 0)
            acc = None
            for j in range(5):
                off = j - 2
                lo_r, hi_r = s + off, s + off + R
                if lo_r < 0:
                    x = jnp.where(rows >= -off, pltpu.roll(ref[0, 0:R, :], -off, axis=0), 0.0)
                elif hi_r > tseq:
                    x = jnp.where(rows < R - off, pltpu.roll(ref[0, tseq - R:tseq, :], R - off, axis=0), 0.0)
                else:
                    x = ref[0, lo_r:hi_r, :]
                term = gprow(prow0 + j) * x
                acc = term if acc is None else acc + term
            return acc * jax.nn.sigmoid(acc)

        qc = conv_silu(xq, GP_CONV_Q)
        kc = conv_silu(xk, GP_CONV_K)
        v = conv_silu(xv, GP_CONV_V)
        q = qc * (lax.rsqrt(jnp.sum(qc * qc, axis=1, keepdims=True) + 1e-6) * (LANES ** -0.5))
        k = kc * lax.rsqrt(jnp.sum(kc * kc, axis=1, keepdims=True) + 1e-6)
        bg = xbg[0, 0, s:s + R, :]
        lane4 = lax.broadcasted_iota(jnp.int32, bg.shape, 1)
        beta4 = jax.nn.sigmoid(bg)
        g4 = -jnp.exp(gprow(GP_SC_ALOG)[:, 0:4]) * jax.nn.softplus(bg + gprow(GP_SC_DTB)[:, 0:4])
        c_fwd = _chunk_cumsum(g4, False, C)
        tot = jnp.concatenate([jnp.broadcast_to(c_fwd[(c + 1) * C - 1:(c + 1) * C, :], (C, 4)) for c in range(R // C)],
                              axis=0)
        c_bwd = tot - c_fwd + g4
        fwd_lane = lane4 == 2
        gc4 = jnp.where(fwd_lane, c_fwd, c_bwd)
        eg4 = jnp.exp(gc4)
        er4 = jnp.exp(jnp.where(fwd_lane, c_bwd, c_fwd) - g4)
        out = slice(row0 + s, row0 + s + R)
        s_q[out, :] = q.astype(BF16)
        s_k[out, :] = k.astype(BF16)
        for d in range(2):
            beta, eg = beta4[:, d:d + 1], eg4[:, 2 + d:3 + d]
            kb = k * beta
            s_kb[d][out, :] = kb.astype(BF16)
            s_kbg[d][out, :] = (kb * eg).astype(BF16)
            s_qg[d][out, :] = (q * eg).astype(BF16)
            s_kg[d][out, :] = (k * er4[:, 2 + d:3 + d]).astype(BF16)
            s_vb[d][out, :] = (v * beta).astype(BF16)
            s_gc[d][out, :] = jnp.broadcast_to(gc4[:, 2 + d:3 + d], (R, LANES))
            for c in range(R // C):
                last = c * C + (C - 1 if d == 0 else 0)
                n = ch0 + it * (R // C) + c
                s_gl[d, n:n + 1, :] = jnp.broadcast_to(eg[last:last + 1, :], (1, LANES))

    def prep_stages(xs, tseq, tiles, row0, ch0):
        for it in tiles:
            prep_tile(xs, tseq, it, row0, ch0)
            yield

    def p1_stages(probs, row0, ch0):
        ds = [d for _, d in probs]
        rows = [pl.ds(pl.multiple_of(row0 + n * C, C), C) for n, _ in probs]
        gms = [_bdot_nt(jnp.concatenate([s_kb[d][rw, :], s_q[rw, :]], axis=0), s_k[rw, :])
               for (n, d), rw in zip(probs, rows)]
        yield
        a_list = []
        for (n, d), rw, gm in zip(probs, rows, gms):
            base = d * GM_PER_DIR
            gc = s_gc[d][rw, :]
            m_incl = mk_ref[base + GM_INCL]
            dec = jnp.exp((gc - gc.T) * m_incl)
            s_a[d, ch0 + n] = (gm[C:2 * C] * dec * m_incl).astype(BF16)
            a_list.append(-(gm[0:C] * dec * mk_ref[base + GM_STRICT]))
        xs = [mk_ref[GM_EYE] + a * mk_ref[d * GM_PER_DIR + GM_L1] for a, d in zip(a_list, ds)]
        for li in range(GN_LV):
            t1 = [_bdot(x, a * mk_ref[d * GM_PER_DIR + GM_LV + li]) for x, a, d in zip(xs, a_list, ds)]
            yield
            xs = [x + _bdot(t, x) for x, t in zip(xs, t1)]
            yield
        for (n, d), rw, x in zip(probs, rows, xs):
            tkv = _bdot(x, jnp.concatenate([s_kbg[d][rw, :], s_vb[d][rw, :]], axis=1))
            s_tk[d, ch0 + n] = tkv[:, 0:LANES].astype(BF16)
            s_tvb[d, ch0 + n] = tkv[:, LANES:2 * LANES]
        yield

    def p2_stages(steps, row0, ch0):
        for ns0 in steps:
            ns = [ch0 + n for n in ns0]
            rows = [pl.ds(pl.multiple_of(row0 + ns0[d] * C, C), C) for d in range(2)]
            st = [s_state[d] for d in range(2)]
            aq = [_bdot(jnp.concatenate([s_tk[d, ns[d]], s_qg[d][rows[d], :]], axis=0), st[d]) for d in range(2)]
            yield
            vnew = [s_tvb[d, ns[d]] - aq[d][0:C] for d in range(2)]
            oo = [aq[d][C:2 * C] + _bdot(s_a[d, ns[d]], vnew[d]) for d in range(2)]
            for d in range(2):
                s_o[d][rows[d], :] = oo[d]
                s_state[d] = st[d] * s_gl[d, pl.ds(ns[d], 1), :] + _bdot_tn(s_kg[d][rows[d], :], vnew[d])
            yield

    def groups(tseq):
        nch = tseq // C
        per = min(GDN_P1_CHUNKS, nch)
        probs = lambda g: [(g * per + j, 0) for j in range(per)] + [(nch - 1 - (g * per + j), 1) for j in range(per)]
        steps = lambda g: [(g * per + j, nch - 1 - (g * per + j)) for j in range(per)]
        return nch // per, per, probs, steps

    def epilogue_tile(o_ref, z_ref, tseq, it, row0):
        R = min(PREP_ROWS, tseq)
        rows = slice(row0 + it * R, row0 + (it + 1) * R)
        o = s_of[rows, :] + s_ob[rows, :]
        o = o * lax.rsqrt(jnp.mean(o * o, axis=1, keepdims=True) + 1e-6) * gprow(GP_NORM)
        z = z_ref[0, it * R:(it + 1) * R, :]
        o_ref[0, it * R:(it + 1) * R, :] = o * (z * jax.nn.sigmoid(z))

    def epilogue_stages(o_ref, z_ref, tseq, tiles, row0):
        for it in tiles:
            epilogue_tile(o_ref, z_ref, tseq, it, row0)
            yield

    lat, ctx = (lq, lk, lv, lbg), (cq, ck, cv, cbg)
    row_c, ch_c = t_lat, t_lat // C
    s_state[...] = jnp.zeros_like(s_state)
    _interleave(prep_stages(ctx, t_ctx, range(t_ctx // min(PREP_ROWS, t_ctx)), row_c, ch_c))
    ng_c, _, probs_c, steps_c = groups(t_ctx)
    ng, per, probs, steps = groups(t_lat)
    ntile = t_lat // min(PREP_ROWS, t_lat)
    tpg = max(per * C // min(PREP_ROWS, t_lat), 1)
    first = sorted(set(list(range(min(tpg, ntile))) + list(range(max(ntile - tpg, 0), ntile))))
    rest = [t for t in range(ntile) if t not in first]

    def ctx_scan():
        for g in range(ng_c):
            yield from p1_stages(probs_c(g), row_c, ch_c)
            yield from p2_stages(steps_c(g), row_c, ch_c)

    _interleave(prep_stages(lat, t_lat, first, 0, 0), ctx_scan())
    _interleave(prep_stages(lat, t_lat, rest, 0, 0), p1_stages(probs(0), 0, 0),
                epilogue_stages(o_ctx, cz, t_ctx, range(t_ctx // min(PREP_ROWS, t_ctx)), row_c))
    if ng > 1:
        def body(g, carry):
            _interleave(p1_stages(probs(g + 1), 0, 0), p2_stages(steps(g), 0, 0))
            return carry

        lax.fori_loop(0, ng - 1, body, 0)
    done = [t for t in range(ntile) if per * C <= t * min(PREP_ROWS, t_lat) and (t + 1) * min(PREP_ROWS, t_lat) <= (ng - 1) * per * C]
    _interleave(p2_stages(steps(ng - 1), 0, 0), epilogue_stages(o_lat, lz, t_lat, done, 0))
    _interleave(epilogue_stages(o_lat, lz, t_lat, [t for t in range(ntile) if t not in done], 0))


def gdn_mixer(u_lat, bg_lat, u_ctx, bg_ctx, gp):
    bsz, t_lat, _ = u_lat.shape
    t_ctx = u_ctx.shape[1]
    nh = GDN_HEADS
    masks = jnp.asarray(gdn_masks())

    def seq_specs(t):
        return [pl.BlockSpec((1, t, LANES), functools.partial(_col_block_map, off=o * nh)) for o in range(4)] + [
            pl.BlockSpec((1, 1, t, 4), lambda b, h: (b, h, 0, 0))]

    in_specs = seq_specs(t_lat) + seq_specs(t_ctx) + [
        pl.BlockSpec((GP_ROWS, LANES), lambda b, h: (0, h)),
        pl.BlockSpec((GN_MASKS, LANES, LANES), lambda b, h: (0, 0, 0)),
    ]
    out_specs = [pl.BlockSpec((1, t_lat, LANES), lambda b, h: (b, 0, h)),
                 pl.BlockSpec((1, t_ctx, LANES), lambda b, h: (b, 0, h))]
    nch = (t_lat + t_ctx) // GDN_CHUNK
    t_all = t_lat + t_ctx
    scratch = ([pltpu.VMEM((t_all, LANES), BF16) for _ in range(12)] + [pltpu.VMEM((t_all, LANES), F32) for _ in range(4)]
               + [pltpu.VMEM((2, nch, LANES, LANES), BF16), pltpu.VMEM((2, nch, LANES, LANES), F32),
                  pltpu.VMEM((2, nch, LANES, LANES), BF16),
                  pltpu.VMEM((2, max(nch, 8), LANES), F32), pltpu.VMEM((2, LANES, LANES), F32)])
    kern = functools.partial(_gdn_kernel, t_lat=t_lat, t_ctx=t_ctx)
    return pl.pallas_call(
        kern,
        grid=(bsz, nh),
        in_specs=in_specs,
        out_specs=out_specs,
        out_shape=[jax.ShapeDtypeStruct((bsz, t_lat, GDN_WIDTH), F32), jax.ShapeDtypeStruct((bsz, t_ctx, GDN_WIDTH), F32)],
        scratch_shapes=scratch,
        compiler_params=pltpu.CompilerParams(dimension_semantics=("arbitrary", "arbitrary"),
                                             vmem_limit_bytes=V7X_VMEM_LIMIT_BYTES),
        name="gdn_chunked",
    )(*([u_lat] * 4), bg_lat, *([u_ctx] * 4), bg_ctx, gp, masks)


def gdn_params(conv_w, a_log, dt_bias, norm_g):
    wd = GDN_WIDTH

    def scalar_row(x):
        return jnp.zeros((GDN_HEADS, LANES), F32).at[:, 2].set(x[0]).at[:, 3].set(x[1]).reshape(-1)

    rows = [conv_w[j, o * wd:(o + 1) * wd] for o in range(3) for j in range(5)]
    rows += [jnp.tile(norm_g, GDN_HEADS), scalar_row(a_log), scalar_row(dt_bias)]
    return jnp.concatenate([jnp.stack(rows), jnp.zeros((GP_ROWS - len(rows), wd), F32)], 0)


def gdn_bg(u_small):
    bsz, t, _ = u_small.shape
    x = u_small.reshape(bsz, t, 2, 2, GDN_HEADS)
    return jnp.transpose(x, (0, 4, 1, 2, 3)).reshape(bsz, GDN_HEADS, t, 4)


OD_Q_COL, OD_GATE_COL, OD_KV_COL = 0, ATT_WIDTH, 2 * ATT_WIDTH
OD_KV_W = 4 * ATT_HEAD
OD_COLS = OD_KV_COL + ATT_KV_HEADS * OD_KV_W


def rope_lane_tables(t):
    pos = np.arange(t)
    inv = ROPE_BASE ** (-np.arange(ROPE_FREQS, dtype=np.float64) / ROPE_FREQS)
    ang_row = (pos // GRID_W)[:, None] * inv
    ang_col = (pos % GRID_W)[:, None] * inv
    ang = np.concatenate([ang_row, ang_row, ang_col, ang_col] * 2, axis=1)
    first = (np.arange(LANES) % (2 * ROPE_FREQS)) < ROPE_FREQS
    cos, sin = np.cos(ang), np.sin(ang)
    return (jnp.asarray(cos, F32), jnp.asarray(np.where(first, -sin, 0.0), F32), jnp.asarray(np.where(first, 0.0, sin), F32))


def _rope(x, cos, sina, sinb):
    n = x.shape[1]
    return x * cos + pltpu.roll(x, n - ROPE_FREQS, axis=1) * sina + pltpu.roll(x, ROPE_FREQS, axis=1) * sinb


def _kv_rope_kernel(u_ref, cos_ref, sina_ref, sinb_ref, o_ref):
    x = u_ref[0]
    cos, sina, sinb = cos_ref[...], sina_ref[...], sinb_ref[...]
    parts = []
    for h in range(ATT_KV_HEADS):
        k2 = x[:, h * OD_KV_W:h * OD_KV_W + LANES]
        parts += [_rope(k2, cos, sina, sinb), x[:, h * OD_KV_W + LANES:(h + 1) * OD_KV_W]]
    o_ref[0] = jnp.concatenate(parts, axis=1).astype(BF16)


def kv_rope(u, tables, tr=512):
    bsz, t, _ = u.shape
    tr = min(tr, t)
    kvw = ATT_KV_HEADS * OD_KV_W
    tab = pl.BlockSpec((tr, LANES), lambda b, i: (i, 0))
    return pl.pallas_call(
        _kv_rope_kernel,
        grid=(bsz, t // tr),
        in_specs=[pl.BlockSpec((1, tr, kvw), lambda b, i: (b, i, OD_KV_COL // kvw)), tab, tab, tab],
        out_specs=pl.BlockSpec((1, tr, kvw), lambda b, i: (b, i, 0)),
        out_shape=jax.ShapeDtypeStruct((bsz, t, kvw), BF16),
        compiler_params=pltpu.CompilerParams(dimension_semantics=("arbitrary", "arbitrary")),
        name="kv_rope",
    )(u, *tables)


def _attn_kernel(sink_ref, *refs, has_local, nb):
    if has_local:
        q_ref, g_ref, kvp_ref, kvc_ref, kvn_ref, cos_ref, sina_ref, sinb_ref, kvx_ref, o_ref = refs
    else:
        q_ref, g_ref, kvx_ref, o_ref = refs
    tq = q_ref.shape[1]
    i = pl.program_id(1)
    q = q_ref[0]
    if has_local:
        rep = ATT_WIDTH // LANES
        cos, sina, sinb = (jnp.concatenate([r[...]] * rep, axis=1) for r in (cos_ref, sina_ref, sinb_ref))
        q = _rope(q, cos, sina, sinb)
    q = q * (ATT_HEAD ** -0.5 * LOG2E)
    lo = lax.broadcasted_iota(jnp.int32, (tq, LANES), 1) < HALF
    if has_local:
        a = lax.broadcasted_iota(jnp.int32, (tq, WINDOW), 0)
        c = lax.broadcasted_iota(jnp.int32, (tq, WINDOW), 1)
        ok_prev = (c >= a) & (i > 0)
        ok_next = (c <= a) & (i < nb - 1)
    outs = []
    for hk in range(ATT_KV_HEADS):
        kcol = slice(hk * OD_KV_W, hk * OD_KV_W + LANES)
        vcol = slice(hk * OD_KV_W + LANES, (hk + 1) * OD_KV_W)
        qs = []
        for pair in range(ATT_GROUP // 2):
            grp = q[:, (hk * (ATT_GROUP // 2) + pair) * LANES:(hk * (ATT_GROUP // 2) + pair + 1) * LANES]
            qs += [jnp.where(lo, grp, 0.0), jnp.where(lo, 0.0, grp)]
        lhs = jnp.concatenate(qs, axis=0).astype(BF16)
        kx, vx = kvx_ref[0, :, kcol].astype(BF16), kvx_ref[0, :, vcol].astype(BF16)
        if has_local:
            kall = jnp.concatenate([kvp_ref[0, :, kcol], kvc_ref[0, :, kcol], kvn_ref[0, :, kcol], kx], axis=0)
            vall = jnp.concatenate([kvp_ref[0, :, vcol], kvc_ref[0, :, vcol], kvn_ref[0, :, vcol], vx], axis=0)
        else:
            kall, vall = kx, vx
        s = _bdot_nt(lhs, kall)
        lov = lax.broadcasted_iota(jnp.int32, vall.shape, 1) < HALF
        zv = jnp.zeros_like(vall)
        v2 = jnp.concatenate([jnp.where(lov, vall, zv), jnp.where(lov, zv, vall)], axis=0)
        ps, inv_l = [], []
        for g in range(ATT_GROUP):
            sg = s[g * tq:(g + 1) * tq]
            if has_local:
                sg = jnp.concatenate([jnp.where(ok_prev, sg[:, 0:tq], NEG_INF), sg[:, tq:2 * tq],
                                      jnp.where(ok_next, sg[:, 2 * tq:3 * tq], NEG_INF), sg[:, 3 * tq:]], axis=1)
            sink = sink_ref[hk, g] * LOG2E
            m = jnp.maximum(jnp.max(sg, axis=1, keepdims=True), sink)
            p = jnp.exp2(sg - m)
            inv_l.append(1.0 / (jnp.sum(p, axis=1, keepdims=True) + jnp.exp2(sink - m)))
            ps.append(p.astype(BF16))
        pv = jnp.dot(jnp.concatenate([jnp.concatenate(ps[2 * pr:2 * pr + 2], axis=1) for pr in range(ATT_GROUP // 2)],
                                     axis=0), v2, preferred_element_type=F32)
        for pr in range(ATT_GROUP // 2):
            outs.append(pv[pr * tq:(pr + 1) * tq] * jnp.where(lo, inv_l[2 * pr], inv_l[2 * pr + 1]))
    g = g_ref[0]
    o_ref[0] = jnp.concatenate(outs, axis=1) * (g * jax.nn.sigmoid(g))


def sink_attention(u, kv_local, tables, u_ctx, sink):
    bsz, t, _ = u.shape
    tq = ATT_BLOCK
    nb = t // tq
    tx = u_ctx.shape[1]
    kvw = ATT_KV_HEADS * OD_KV_W
    has_local = kv_local is not None
    in_specs = [pl.BlockSpec(memory_space=pltpu.SMEM),
                pl.BlockSpec((1, tq, ATT_WIDTH), lambda b, i: (b, i, OD_Q_COL // ATT_WIDTH)),
                pl.BlockSpec((1, tq, ATT_WIDTH), lambda b, i: (b, i, OD_GATE_COL // ATT_WIDTH))]
    args = [sink.reshape(ATT_KV_HEADS, ATT_GROUP), u, u]
    if has_local:
        in_specs += [pl.BlockSpec((1, tq, kvw), lambda b, i: (b, jnp.maximum(i - 1, 0), 0)),
                     pl.BlockSpec((1, tq, kvw), lambda b, i: (b, i, 0)),
                     pl.BlockSpec((1, tq, kvw), lambda b, i: (b, jnp.minimum(i + 1, nb - 1), 0))]
        in_specs += [pl.BlockSpec((tq, LANES), lambda b, i: (i, 0))] * 3
        args += [kv_local] * 3 + list(tables)
    in_specs += [pl.BlockSpec((1, tx, kvw), lambda b, i: (b, 0, OD_KV_COL // kvw))]
    args += [u_ctx]
    return pl.pallas_call(
        functools.partial(_attn_kernel, has_local=has_local, nb=nb),
        grid=(bsz, nb),
        in_specs=in_specs,
        out_specs=pl.BlockSpec((1, tq, ATT_WIDTH), lambda b, i: (b, i, 0)),
        out_shape=jax.ShapeDtypeStruct((bsz, t, ATT_WIDTH), F32),
        compiler_params=pltpu.CompilerParams(dimension_semantics=("arbitrary", "arbitrary"),
                                             vmem_limit_bytes=V7X_VMEM_LIMIT_BYTES),
        name="sink_attention",
    )(*args)


def pack_odd_w_in(w_in):
    nq, nkv = ATT_WIDTH, ATT_KV_HEADS * ATT_HEAD
    parts = [w_in[:, :nq], w_in[:, nq + 2 * nkv:]]
    for h in range(ATT_KV_HEADS):
        k = w_in[:, nq + h * ATT_HEAD:nq + (h + 1) * ATT_HEAD]
        v = w_in[:, nq + nkv + h * ATT_HEAD:nq + nkv + (h + 1) * ATT_HEAD]
        parts += [k, k, v, v]
    return jnp.concatenate(parts, 1).astype(BF16)


def pack_even_w_in(w_in):
    g1 = 4 * GDN_WIDTH
    r0 = GDN_IN
    parts = [w_in[:, :g1], w_in[:, r0:r0 + 4 * RWKV_WIDTH], w_in[:, r0 + 4 * RWKV_WIDTH:], w_in[:, g1:GDN_IN],
             jnp.zeros((D_MODEL, EV_COLS - EV_SMALL_COL - 4 * GDN_HEADS), F32)]
    return jnp.concatenate(parts, 1).astype(BF16)


def kernel(x, c, ctx, c_ctx, mod_w, mod_b, ln_g, ln_b, ev_w_in, ev_w_out, gdn_conv, gdn_a_log, gdn_dt_bias,
           gdn_norm_g, rwkv_mix, rwkv_w0, rwkv_w_up, rwkv_a0, rwkv_a_up, rwkv_k_k, rwkv_k_a, rwkv_r_k,
           rwkv_gn_g, rwkv_gn_b, od_w_in, od_w_out, od_sink):
    bsz, t_lat, d = x.shape
    t_ctx = ctx.shape[1]
    x_lat, x_ctx = x, ctx
    n_mod = -(-(bsz + 1) // 8) * 8
    cc = jax.nn.silu(jnp.concatenate([c, c_ctx[None], jnp.zeros((n_mod - bsz - 1, d), F32)], 0))
    mods = mod_all_layers(cc, mod_w, mod_b)
    for layer in range(DEPTH):
        need_ctx = layer < DEPTH - 1
        i = layer // 2
        mod = mods[layer]
        sh, sc, gt = (mod[:bsz, j * d:(j + 1) * d][:, None, :] for j in range(3))
        sh_c, sc_c, gt_c = (jnp.broadcast_to(mod[bsz, j * d:(j + 1) * d], (bsz, 1, d)) for j in range(3))
        if layer % 2 == 0:
            w_in = pack_even_w_in(ev_w_in[i])
            w_out = ev_w_out[i].astype(BF16)
            u_lat = inproj(x_lat, sc, sh, w_in, 1024, EV_TN)
            u_ctx = inproj(x_ctx.reshape(1, bsz * t_ctx, d), sc_c[:1], sh_c[:1], w_in, 1024, EV_TN).reshape(bsz, t_ctx, -1)
            small = slice(EV_SMALL_COL, EV_SMALL_COL + 4 * GDN_HEADS)
            og_lat, og_ctx = gdn_mixer(u_lat, gdn_bg(u_lat[..., small]), u_ctx, gdn_bg(u_ctx[..., small]),
                                       gdn_params(gdn_conv[i], gdn_a_log[i], gdn_dt_bias[i], gdn_norm_g[i]))
            or_lat, or_ctx = rwkv_mixer(u_lat, u_ctx, *rwkv_params(
                rwkv_mix[i], rwkv_w0[i], rwkv_w_up[i], rwkv_a0[i], rwkv_a_up[i], rwkv_k_k[i], rwkv_k_a[i],
                rwkv_r_k[i], rwkv_gn_g[i], rwkv_gn_b[i]))
            a_lat, a_ctx = [og_lat, or_lat], [og_ctx, or_ctx]
        else:
            w_in = pack_odd_w_in(od_w_in[i])
            w_out = od_w_out[i].astype(BF16)
            u_lat = inproj(x_lat, sc, sh, w_in, 1024, OD_TN)
            u_ctx = inproj(x_ctx.reshape(1, bsz * t_ctx, d), sc_c[:1], sh_c[:1], w_in, 1024, OD_TN).reshape(bsz, t_ctx, -1)
            tables = rope_lane_tables(t_lat)
            o_lat = sink_attention(u_lat, kv_rope(u_lat, tables), tables, u_ctx, od_sink[i])
            o_ctx = sink_attention(u_ctx, None, None, u_ctx, od_sink[i]) if need_ctx else None
            a_lat, a_ctx = [o_lat], [o_ctx]
        x_lat = outproj_ln(a_lat, w_out, x_lat, gt, ln_g[layer], ln_b[layer])
        if need_ctx:
            x_ctx = outproj_ln(a_ctx, w_out, x_ctx, gt_c, ln_g[layer], ln_b[layer])
    return x_lat
```

```python
import functools

import jax
import jax.numpy as jnp
import numpy as np
from jax import lax
from jax.experimental import pallas as pl
from jax.experimental.pallas import tpu as pltpu

F32 = jnp.float32
BF16 = jnp.bfloat16

D_MODEL = 2048
DEPTH = 4
GRID_W = 64
GDN_WIDTH = D_MODEL // 2
GDN_HEADS = 8
GDN_CHUNK = 128
RWKV_WIDTH = D_MODEL // 2
RWKV_HEADS = 16
RWKV_CHUNK = 64
RWKV_GN_EPS = 64e-5
ATT_HEAD = 64
ATT_Q_HEADS = D_MODEL // ATT_HEAD
ATT_KV_HEADS = ATT_Q_HEADS // 8
ATT_GROUP = ATT_Q_HEADS // ATT_KV_HEADS
ATT_WIDTH = ATT_Q_HEADS * ATT_HEAD
WINDOW = 128
ATT_BLOCK = 128
ROPE_BASE = 10000.0
ROPE_FREQS = ATT_HEAD // 4
NEG_INF = -1e30
LOG2E = 1.4426950408889634
DEEPNORM_ALPHA = (2 * DEPTH) ** 0.25
LN_EPS = 1e-5
GDN_IN = 4 * GDN_WIDTH + 4 * GDN_HEADS

LANES = 128
PREP_ROWS = 256
HALF = 64
V7X_VMEM_LIMIT_BYTES = 56 * 1024 * 1024
EV_GDN_COL = 0
EV_RWKV_COL = 4 * GDN_WIDTH
EV_LORA_COL = EV_RWKV_COL + 4 * RWKV_WIDTH
EV_SMALL_COL = EV_LORA_COL + LANES
EV_COLS = EV_SMALL_COL + LANES
EV_TN = 1408
OD_TN = 1280


def _mod_kernel(a_ref, w_ref, b_ref, o_ref):
    o_ref[0] = jnp.dot(a_ref[...].astype(BF16), w_ref[0].astype(BF16), preferred_element_type=F32) + b_ref[0]


def mod_all_layers(a, w, b, tn=512):
    m, k = a.shape
    nl, _, n = w.shape
    return pl.pallas_call(
        _mod_kernel,
        grid=(nl, n // tn),
        in_specs=[pl.BlockSpec((m, k), lambda l, j: (0, 0)), pl.BlockSpec((1, k, tn), lambda l, j: (l, 0, j)),
                  pl.BlockSpec((1, 1, tn), lambda l, j: (l, 0, j))],
        out_specs=pl.BlockSpec((1, m, tn), lambda l, j: (l, 0, j)),
        out_shape=jax.ShapeDtypeStruct((nl, m, n), F32),
        compiler_params=pltpu.CompilerParams(dimension_semantics=("arbitrary", "arbitrary"),
                                             vmem_limit_bytes=V7X_VMEM_LIMIT_BYTES),
        name="mod_matmul",
    )(a, w, b.reshape(nl, 1, n))


def _inproj_kernel(x_ref, sc_ref, sh_ref, w_ref, o_ref, h_ref):
    @pl.when(pl.program_id(2) == 0)
    def _():
        h_ref[...] = (x_ref[0] * (1.0 + sc_ref[0]) + sh_ref[0]).astype(BF16)

    o_ref[0] = jnp.dot(h_ref[...], w_ref[...], preferred_element_type=F32)


def inproj(x, sc, sh, w, tm, tn):
    bsz, t, d = x.shape
    n = w.shape[1]
    tm = min(tm, t)
    assert t % tm == 0 and n % tn == 0
    return pl.pallas_call(
        _inproj_kernel,
        grid=(bsz, t // tm, n // tn),
        in_specs=[pl.BlockSpec((1, tm, d), lambda b, i, j: (b, i, 0)),
                  pl.BlockSpec((1, 1, d), lambda b, i, j: (b, 0, 0)),
                  pl.BlockSpec((1, 1, d), lambda b, i, j: (b, 0, 0)),
                  pl.BlockSpec((d, tn), lambda b, i, j: (0, j))],
        out_specs=pl.BlockSpec((1, tm, tn), lambda b, i, j: (b, i, j)),
        out_shape=jax.ShapeDtypeStruct((bsz, t, n), F32),
        scratch_shapes=[pltpu.VMEM((tm, d), BF16)],
        compiler_params=pltpu.CompilerParams(dimension_semantics=("arbitrary", "arbitrary", "arbitrary"),
                                             vmem_limit_bytes=V7X_VMEM_LIMIT_BYTES),
        name="inproj",
    )(x, sc, sh, w)


def _outproj_kernel(*refs, n_a):
    a_refs, (w_ref, x_ref, gt_ref, g_ref, b_ref, o_ref) = refs[:n_a], refs[n_a:]
    y = None
    k0 = 0
    for a_ref in a_refs:
        kw = a_ref.shape[2]
        term = jnp.dot(a_ref[0].astype(BF16), w_ref[k0:k0 + kw, :], preferred_element_type=F32)
        y = term if y is None else y + term
        k0 += kw
    z = DEEPNORM_ALPHA * x_ref[0] + gt_ref[0] * y
    mu = jnp.mean(z, axis=1, keepdims=True)
    zc = z - mu
    var = jnp.mean(zc * zc, axis=1, keepdims=True)
    o_ref[0] = zc * lax.rsqrt(var + LN_EPS) * g_ref[...] + b_ref[...]


def outproj_ln(a_list, w, x, gt, ln_g, ln_b, tm=512):
    bsz, t, d = x.shape
    tm = min(tm, t)
    in_specs = [pl.BlockSpec((1, tm, a.shape[2]), lambda b, i: (b, i, 0)) for a in a_list] + [
        pl.BlockSpec(w.shape, lambda b, i: (0, 0)),
        pl.BlockSpec((1, tm, d), lambda b, i: (b, i, 0)),
        pl.BlockSpec((1, 1, d), lambda b, i: (b, 0, 0)),
        pl.BlockSpec((1, d), lambda b, i: (0, 0)),
        pl.BlockSpec((1, d), lambda b, i: (0, 0))]
    return pl.pallas_call(
        functools.partial(_outproj_kernel, n_a=len(a_list)),
        grid=(bsz, t // tm),
        in_specs=in_specs,
        out_specs=pl.BlockSpec((1, tm, d), lambda b, i: (b, i, 0)),
        out_shape=jax.ShapeDtypeStruct((bsz, t, d), F32),
        compiler_params=pltpu.CompilerParams(dimension_semantics=("arbitrary", "arbitrary"),
                                             vmem_limit_bytes=V7X_VMEM_LIMIT_BYTES),
        name="outproj_ln",
    )(*a_list, w, x, gt, ln_g.reshape(1, d), ln_b.reshape(1, d))


def _bdot(a, b):
    return jnp.dot(a.astype(BF16), b.astype(BF16), preferred_element_type=F32)


def _bdot_nt(a, b):
    return lax.dot_general(a.astype(BF16), b.astype(BF16), (((1,), (1,)), ((), ())), preferred_element_type=F32)


def _bdot_tn(a, b):
    return lax.dot_general(a.astype(BF16), b.astype(BF16), (((0,), (0,)), ((), ())), preferred_element_type=F32)


(PV_MR0, PV_MR1, PV_MK0, PV_MK1, PV_MV0, PV_MV1, PV_MG0, PV_MG1, PV_W00, PV_W01, PV_A00, PV_A01,
 PV_KK, PV_KA, PV_RK, PV_GNG, PV_GNB) = range(17)
PV_ROWS = 24
MK_ABD, MK_P, MK_Y, MK_L1 = 0, 1, 2, 3
MK_LV = 4
N_LV = 5
MK_PER_DIR = MK_LV + N_LV
MK_EYE = 2 * MK_PER_DIR
MK_BD = MK_EYE + 1
N_MASKS = MK_BD + 1


def rwkv_masks():
    ri = np.arange(128)[:, None]
    ci = np.arange(128)[None, :]
    i, j = ri & 63, ci & 63
    same = (ri < 64) == (ci < 64)
    out = np.zeros((N_MASKS, 128, 128), np.float32)
    for d in range(2):
        strict = (j < i) if d == 0 else (j > i)
        incl = (j <= i) if d == 0 else (j >= i)
        base = d * MK_PER_DIR
        out[base + MK_ABD] = strict & same
        out[base + MK_P] = strict & (ci >= 64)
        out[base + MK_Y] = incl
        for li, s in enumerate((1, 2, 4, 8, 16, 32)):
            blk = (i // (2 * s)) == (j // (2 * s))
            if d == 0:
                m = blk & ((i & s) != 0) & ((j & s) == 0)
            else:
                m = blk & ((i & s) == 0) & ((j & s) != 0)
            out[base + (MK_L1 if li == 0 else MK_LV + li - 1)] = m & same
    out[MK_EYE] = ri == ci
    out[MK_BD] = same
    return out


def _seg_sum(x, lo):
    s0 = jnp.sum(jnp.where(lo, x, 0.0), axis=1, keepdims=True)
    s1 = jnp.sum(jnp.where(lo, 0.0, x), axis=1, keepdims=True)
    return jnp.where(lo, s0, s1)


def _chunk_cumsum(x, chunk):
    rowm = lax.broadcasted_iota(jnp.int32, x.shape, 0) & (chunk - 1)
    s = 1
    while s < chunk:
        x = x + jnp.where(rowm >= s, pltpu.roll(x, s, axis=0), 0.0)
        s *= 2
    return x


def _interleave(*gens):
    live = list(gens)
    while live:
        for g in list(live):
            try:
                next(g)
            except StopIteration:
                live.remove(g)


def _rwkv_kernel(lr, lk, lv, lg, ll, cr, ck, cv, cg, cl, pv_ref, mixl_ref, wup_ref, aup_ref, mk_ref,
                 o_lat, o_ctx,
                 s_at0, s_at1, s_rt0, s_rt1, s_bt0, s_bt1, s_kt0, s_kt1, s_bh0, s_bh1, s_kh0, s_kh1, s_v,
                 s_g, s_bonus, s_yf, s_yb, s_w, s_tp, s_gr, s_et, s_state, *, t_lat, t_ctx, p1_chunks):
    C = RWKV_CHUNK
    s_at, s_rt, s_bt, s_kt, s_bh, s_kh = ((s_at0, s_at1), (s_rt0, s_rt1), (s_bt0, s_bt1), (s_kt0, s_kt1),
                                          (s_bh0, s_bh1), (s_kh0, s_kh1))
    s_y = (s_yf, s_yb)
    lo64 = lax.broadcasted_iota(jnp.int32, (C, LANES), 1) < HALF
    bd_ones = mk_ref[MK_BD]

    def pvrow(i):
        return pv_ref[pl.ds(i, 1), :]

    def prep_tile(xs, tseq, it, row0, ch0):
        xr, xk, xv, xg, xl = xs
        R = min(PREP_ROWS, tseq)
        s = it * R
        lo = lax.broadcasted_iota(jnp.int32, (R, LANES), 1) < HALF
        rows = lax.broadcasted_iota(jnp.int32, (R, LANES), 0)

        def shifted(ref, m0, m1):
            cur = ref[0, s:s + R, :]
            prev = ref[0, s - 1:s - 1 + R, :] if s > 0 else jnp.where(rows == 0, 0.0, pltpu.roll(cur, 1, axis=0))
            nxt = (ref[0, s + 1:s + 1 + R, :] if s + R < tseq
                   else jnp.where(rows == R - 1, 0.0, pltpu.roll(cur, R - 1, axis=0)))
            return cur + m0 * (prev - cur) + m1 * (nxt - cur)

        r = shifted(xr, pvrow(PV_MR0), pvrow(PV_MR1))
        k = shifted(xk, pvrow(PV_MK0), pvrow(PV_MK1))
        v = shifted(xv, pvrow(PV_MV0), pvrow(PV_MV1))
        g = shifted(xg, pvrow(PV_MG0), pvrow(PV_MG1))
        lor = shifted(xl, mixl_ref[0:1, :], mixl_ref[1:2, :])
        th = jnp.tanh(lor)
        kkf = k * pvrow(PV_KK)
        kk = kkf * lax.rsqrt(_seg_sum(kkf * kkf, lo) + 1e-6)
        ka = pvrow(PV_KA)
        out = slice(row0 + s, row0 + s + R)
        kd_sum = None
        for d in range(2):
            w_pre = pvrow(PV_W00 + d) + _bdot(th, wup_ref[d])
            lw = (-np.exp(-0.5)) * jax.nn.sigmoid(w_pre)
            ag = jax.nn.sigmoid(pvrow(PV_A00 + d) + _bdot(lor, aup_ref[d]))
            kd = k * (1.0 + (ag - 1.0) * ka)
            bdv = kk * ag
            kd_sum = kd if kd_sum is None else kd_sum + kd
            c_fwd = _chunk_cumsum(lw, C)
            tot = jnp.concatenate([jnp.broadcast_to(c_fwd[(c + 1) * C - 1:(c + 1) * C, :], (C, LANES))
                                   for c in range(R // C)], axis=0)
            c_bwd = tot - c_fwd + lw
            cw, rest = (c_fwd, c_bwd - lw) if d == 0 else (c_bwd, c_fwd - lw)
            e_in = jnp.exp(-cw)
            e_out = jnp.exp(rest)
            s_at[d][out, :] = (-kk * jnp.exp(cw - lw)).astype(BF16)
            s_rt[d][out, :] = (r * jnp.exp(cw)).astype(BF16)
            s_bt[d][out, :] = (bdv * e_in).astype(BF16)
            s_kt[d][out, :] = (kd * e_in).astype(BF16)
            s_bh[d][out, :] = (bdv * e_out).astype(BF16)
            s_kh[d][out, :] = (kd * e_out).astype(BF16)
            for c in range(R // C):
                last = c * C + (C - 1 if d == 0 else 0)
                n = ch0 + it * (R // C) + c
                s_et[d, n:n + 1, :] = jnp.exp(cw[last:last + 1, :])
        s_v[out, :] = v.astype(BF16)
        s_g[out, :] = g
        s_bonus[out, :] = _seg_sum(r * kd_sum * pvrow(PV_RK), lo) * v

    def prep_stages(xs, tseq, tiles, row0, ch0):
        for it in tiles:
            prep_tile(xs, tseq, it, row0, ch0)
            yield

    def p1_stages(probs, row0, ch0):
        ds = [d for _, d in probs]
        rows = [pl.ds(pl.multiple_of(row0 + n * C, C), C) for n, _ in probs]
        a2s, gms = [], []
        for (n, d), rw in zip(probs, rows):
            at, rt = s_at[d][rw, :], s_rt[d][rw, :]
            zero = jnp.zeros_like(at)
            a2 = jnp.concatenate([jnp.where(lo64, at, zero), jnp.where(lo64, zero, at)], axis=0)
            l1 = jnp.concatenate([a2, jnp.where(lo64, rt, zero), jnp.where(lo64, zero, rt)], axis=0)
            a2s.append(a2)
            gms.append(_bdot_nt(l1, jnp.concatenate([s_bt[d][rw, :], s_kt[d][rw, :]], axis=0)))
        yield
        ps, a_list = [], []
        for (n, d), rw, gm in zip(probs, rows, gms):
            base = d * MK_PER_DIR
            ga = gm[0:2 * C]
            v = s_v[rw, :]
            ps.append(_bdot(ga * mk_ref[base + MK_P], jnp.concatenate([v, v], axis=0)))
            s_gr[d, ch0 + n] = (gm[2 * C:4 * C] * mk_ref[base + MK_Y]).astype(BF16)
            ga_sw = jnp.concatenate([ga[0:C], pltpu.roll(ga[C:2 * C], HALF, axis=1)], axis=0)
            a_list.append(ga_sw * mk_ref[base + MK_ABD])
        yield
        xs = [mk_ref[MK_EYE] + a * mk_ref[d * MK_PER_DIR + MK_L1] for a, d in zip(a_list, ds)]
        for li in range(N_LV):
            t1 = [_bdot(x, a * mk_ref[d * MK_PER_DIR + MK_LV + li]) for x, a, d in zip(xs, a_list, ds)]
            yield
            xs = [x + _bdot(t, x) for x, t in zip(xs, t1)]
            yield
        for (n, d), x, a2, p in zip(probs, xs, a2s, ps):
            wtp = _bdot(x, jnp.concatenate([a2.astype(F32), p], axis=1))
            s_w[d, ch0 + n] = wtp[:, 0:LANES].astype(BF16)
            s_tp[d, ch0 + n] = jnp.where(lo64, wtp[0:C, LANES:2 * LANES], wtp[C:2 * C, LANES:2 * LANES])
        yield

    def p2_stages(steps, row0, ch0):
        for ns0 in steps:
            ns = [ch0 + n for n in ns0]
            rows = [pl.ds(pl.multiple_of(row0 + ns0[d] * C, C), C) for d in range(2)]
            st = [s_state[d] for d in range(2)]
            xs = [_bdot_nt(jnp.concatenate([s_w[d, ns[d]], s_rt[d][rows[d], :]], axis=0), st[d]) for d in range(2)]
            yield
            uv = [jnp.concatenate([(xs[d][0:C] + xs[d][C:2 * C] + s_tp[d, ns[d]]).astype(BF16), s_v[rows[d], :]], axis=0)
                  for d in range(2)]
            yb = [_bdot(s_gr[d, ns[d]], uv[d]) for d in range(2)]
            for d in range(2):
                s_y[d][rows[d], :] = xs[d][2 * C:3 * C] + jnp.where(lo64, yb[d][0:C], yb[d][C:2 * C])
                bk = jnp.concatenate([s_bh[d][rows[d], :], s_kh[d][rows[d], :]], axis=0)
                s_state[d] = st[d] * s_et[d, pl.ds(ns[d], 1), :] + bd_ones * _bdot_tn(uv[d], bk)
            yield

    def groups(tseq):
        nch = tseq // C
        per = min(p1_chunks, nch)
        probs = lambda g: [(g * per + j, 0) for j in range(per)] + [(nch - 1 - (g * per + j), 1) for j in range(per)]
        steps = lambda g: [(g * per + j, nch - 1 - (g * per + j)) for j in range(per)]
        return nch // per, per, probs, steps

    def epilogue_tile(o_ref, tseq, it, row0):
        R = min(PREP_ROWS, tseq)
        lo = lax.broadcasted_iota(jnp.int32, (R, LANES), 1) < HALF
        rows = slice(row0 + it * R, row0 + (it + 1) * R)
        y = s_yf[rows, :] + s_yb[rows, :]
        mu = _seg_sum(y, lo) * (1.0 / HALF)
        yc = y - mu
        var = _seg_sum(yc * yc, lo) * (1.0 / HALF)
        yn = yc * lax.rsqrt(var + RWKV_GN_EPS) * pvrow(PV_GNG) + pvrow(PV_GNB)
        g = s_g[rows, :]
        o_ref[0, it * R:(it + 1) * R, :] = (yn + s_bonus[rows, :]) * (g * jax.nn.sigmoid(g))

    def epilogue_stages(o_ref, tseq, tiles, row0):
        for it in tiles:
            epilogue_tile(o_ref, tseq, it, row0)
            yield

    lat, ctx = (lr, lk, lv, lg, ll), (cr, ck, cv, cg, cl)
    row_c, ch_c = t_lat, t_lat // C
    s_state[...] = jnp.zeros_like(s_state)
    _interleave(prep_stages(ctx, t_ctx, range(t_ctx // min(PREP_ROWS, t_ctx)), row_c, ch_c))
    ng_c, _, probs_c, steps_c = groups(t_ctx)
    ng, per, probs, steps = groups(t_lat)
    ntile = t_lat // min(PREP_ROWS, t_lat)
    tpg = max(per * C // min(PREP_ROWS, t_lat), 1)
    first = sorted(set(list(range(min(tpg, ntile))) + list(range(max(ntile - tpg, 0), ntile))))
    rest = [t for t in range(ntile) if t not in first]

    def ctx_scan():
        for g in range(ng_c):
            yield from p1_stages(probs_c(g), row_c, ch_c)
            yield from p2_stages(steps_c(g), row_c, ch_c)

    _interleave(prep_stages(lat, t_lat, first, 0, 0), ctx_scan())
    _interleave(prep_stages(lat, t_lat, rest, 0, 0), p1_stages(probs(0), 0, 0),
                epilogue_stages(o_ctx, t_ctx, range(t_ctx // min(PREP_ROWS, t_ctx)), row_c))
    if ng > 1:
        def body(g, carry):
            _interleave(p1_stages(probs(g + 1), 0, 0), p2_stages(steps(g), 0, 0))
            return carry

        lax.fori_loop(0, ng - 1, body, 0)
    done = [t for t in range(ntile) if per * C <= t * min(PREP_ROWS, t_lat) and (t + 1) * min(PREP_ROWS, t_lat) <= (ng - 1) * per * C]
    _interleave(p2_stages(steps(ng - 1), 0, 0), epilogue_stages(o_lat, t_lat, done, 0))
    _interleave(epilogue_stages(o_lat, t_lat, [t for t in range(ntile) if t not in done], 0))


def _col_block_map(b, p, *, off):
    return (b, 0, off + p)


def rwkv_mixer(u_lat, u_ctx, pv, mixl, wup, aup, p1_chunks=8):
    bsz, t_lat, _ = u_lat.shape
    t_ctx = u_ctx.shape[1]
    npair = RWKV_HEADS // 2
    c0 = EV_RWKV_COL // LANES
    nch = (t_lat + t_ctx) // RWKV_CHUNK
    t_all = t_lat + t_ctx
    masks = jnp.asarray(rwkv_masks())

    def seq_specs(t):
        return [pl.BlockSpec((1, t, LANES), functools.partial(_col_block_map, off=c0 + o * npair)) for o in range(4)] + [
            pl.BlockSpec((1, t, LANES), lambda b, p: (b, 0, EV_LORA_COL // LANES))]

    in_specs = seq_specs(t_lat) + seq_specs(t_ctx) + [
        pl.BlockSpec((PV_ROWS, LANES), lambda b, p: (0, p)),
        pl.BlockSpec((8, LANES), lambda b, p: (0, 0)),
        pl.BlockSpec((2, LANES, LANES), lambda b, p: (0, 0, p)),
        pl.BlockSpec((2, LANES, LANES), lambda b, p: (0, 0, p)),
        pl.BlockSpec((N_MASKS, LANES, LANES), lambda b, p: (0, 0, 0)),
    ]
    out_specs = [pl.BlockSpec((1, t_lat, LANES), lambda b, p: (b, 0, p)),
                 pl.BlockSpec((1, t_ctx, LANES), lambda b, p: (b, 0, p))]
    scratch = ([pltpu.VMEM((t_all, LANES), BF16) for _ in range(13)] + [pltpu.VMEM((t_all, LANES), F32) for _ in range(4)]
               + [pltpu.VMEM((2, nch, LANES, LANES), BF16), pltpu.VMEM((2, nch, RWKV_CHUNK, LANES), F32),
                  pltpu.VMEM((2, nch, LANES, LANES), BF16)]
               + [pltpu.VMEM((2, nch, LANES), F32), pltpu.VMEM((2, LANES, LANES), F32)])
    kern = functools.partial(_rwkv_kernel, t_lat=t_lat, t_ctx=t_ctx, p1_chunks=p1_chunks)
    return pl.pallas_call(
        kern,
        grid=(bsz, npair),
        in_specs=in_specs,
        out_specs=out_specs,
        out_shape=[jax.ShapeDtypeStruct((bsz, t_lat, RWKV_WIDTH), F32), jax.ShapeDtypeStruct((bsz, t_ctx, RWKV_WIDTH), F32)],
        scratch_shapes=scratch,
        compiler_params=pltpu.CompilerParams(dimension_semantics=("arbitrary", "arbitrary"),
                                             vmem_limit_bytes=V7X_VMEM_LIMIT_BYTES),
        name="rwkv7_chunked",
    )(*([u_lat] * 5), *([u_ctx] * 5), pv, mixl, wup, aup, masks)


def rwkv_params(mix, w0, w_up, a0, a_up, k_k, k_a, r_k, gn_g, gn_b):
    wd = RWKV_WIDTH
    rows = [mix[0, 0:wd], mix[1, 0:wd], mix[0, wd:2 * wd], mix[1, wd:2 * wd], mix[0, 2 * wd:3 * wd], mix[1, 2 * wd:3 * wd],
            mix[0, 3 * wd:4 * wd], mix[1, 3 * wd:4 * wd], w0[0], w0[1], a0[0], a0[1], k_k, k_a, r_k.reshape(-1), gn_g, gn_b]
    pv = jnp.concatenate([jnp.stack(rows), jnp.zeros((PV_ROWS - len(rows), wd), F32)], 0)
    mixl = jnp.concatenate([mix[:, 4 * wd:], jnp.zeros((6, LANES), F32)], 0)
    z = jnp.zeros((2, HALF, wd), F32)
    wup = jnp.concatenate([w_up, z], 1)
    aup = jnp.concatenate([z, a_up], 1)
    return pv, mixl, wup, aup


GP_CONV_Q, GP_CONV_K, GP_CONV_V = 0, 5, 10
GP_NORM, GP_SC_ALOG, GP_SC_DTB = 15, 16, 17
GP_ROWS = 24
GM_STRICT, GM_INCL, GM_L1, GM_LV = 0, 1, 2, 3
GN_LV = 6
GM_PER_DIR = GM_LV + GN_LV
GM_EYE = 2 * GM_PER_DIR
GN_MASKS = GM_EYE + 1
GDN_P1_CHUNKS = 4


def gdn_masks():
    i = np.arange(GDN_CHUNK)[:, None]
    j = np.arange(GDN_CHUNK)[None, :]
    out = np.zeros((GN_MASKS, GDN_CHUNK, GDN_CHUNK), np.float32)
    for d in range(2):
        base = d * GM_PER_DIR
        out[base + GM_STRICT] = (j < i) if d == 0 else (j > i)
        out[base + GM_INCL] = (j <= i) if d == 0 else (j >= i)
        for li in range(GN_LV + 1):
            s = 1 << li
            blk = (i // (2 * s)) == (j // (2 * s))
            m = blk & (((i & s) != 0) & ((j & s) == 0) if d == 0 else ((i & s) == 0) & ((j & s) != 0))
            out[base + (GM_L1 if li == 0 else GM_LV + li - 1)] = m
    out[GM_EYE] = i == j
    return out


def _gdn_kernel(lq, lk, lv, lz, lbg, cq, ck, cv, cz, cbg, gp_ref, mk_ref, o_lat, o_ctx,
                s_q, s_k, s_kb0, s_kb1, s_kbg0, s_kbg1, s_qg0, s_qg1, s_kg0, s_kg1, s_vb0, s_vb1,
                s_gc0, s_gc1, s_of, s_ob, s_tk, s_tvb, s_a, s_gl, s_state, *, t_lat, t_ctx):
    C = GDN_CHUNK
    s_kb, s_kbg, s_qg, s_kg, s_vb = (s_kb0, s_kb1), (s_kbg0, s_kbg1), (s_qg0, s_qg1), (s_kg0, s_kg1), (s_vb0, s_vb1)
    s_gc, s_o = (s_gc0, s_gc1), (s_of, s_ob)

    def gprow(i):
        return gp_ref[pl.ds(i, 1), :]

    def prep_tile(xs, tseq, it, row0, ch0):
        xq, xk, xv, xbg = xs
        R = min(PREP_ROWS, tseq)
        s = it * R

        def conv_silu(ref, prow0):
            rows = lax.broadcasted_iota(jnp.int32, (R, LANES), 0)
            acc = None
            for j in range(5):
                off = j - 2
                lo_r, hi_r = s + off, s + off + R
                if lo_r < 0:
                    x = jnp.where(rows >= -off, pltpu.roll(ref[0, 0:R, :], -off, axis=0), 0.0)
                elif hi_r > tseq:
                    x = jnp.where(rows < R - off, pltpu.roll(ref[0, tseq - R:tseq, :], R - off, axis=0), 0.0)
                else:
                    x = ref[0, lo_r:hi_r, :]
                term = gprow(prow0 + j) * x
                acc = term if acc is None else acc + term
            return acc * jax.nn.sigmoid(acc)

        qc = conv_silu(xq, GP_CONV_Q)
        kc = conv_silu(xk, GP_CONV_K)
        v = conv_silu(xv, GP_CONV_V)
        q = qc * (lax.rsqrt(jnp.sum(qc * qc, axis=1, keepdims=True) + 1e-6) * (LANES ** -0.5))
        k = kc * lax.rsqrt(jnp.sum(kc * kc, axis=1, keepdims=True) + 1e-6)
        bg = xbg[0, 0, s:s + R, :]
        lane4 = lax.broadcasted_iota(jnp.int32, bg.shape, 1)
        beta4 = jax.nn.sigmoid(bg)
        g4 = -jnp.exp(gprow(GP_SC_ALOG)[:, 0:4]) * jax.nn.softplus(bg + gprow(GP_SC_DTB)[:, 0:4])
        c_fwd = _chunk_cumsum(g4, C)
        tot = jnp.concatenate([jnp.broadcast_to(c_fwd[(c + 1) * C - 1:(c + 1) * C, :], (C, 4)) for c in range(R // C)],
                              axis=0)
        c_bwd = tot - c_fwd + g4
        fwd_lane = lane4 == 2
        gc4 = jnp.where(fwd_lane, c_fwd, c_bwd)
        eg4 = jnp.exp(gc4)
        er4 = jnp.exp(jnp.where(fwd_lane, c_bwd, c_fwd) - g4)
        out = slice(row0 + s, row0 + s + R)
        s_q[out, :] = q.astype(BF16)
        s_k[out, :] = k.astype(BF16)
        for d in range(2):
            beta, eg = beta4[:, d:d + 1], eg4[:, 2 + d:3 + d]
            kb = k * beta
            s_kb[d][out, :] = kb.astype(BF16)
            s_kbg[d][out, :] = (kb * eg).astype(BF16)
            s_qg[d][out, :] = (q * eg).astype(BF16)
            s_kg[d][out, :] = (k * er4[:, 2 + d:3 + d]).astype(BF16)
            s_vb[d][out, :] = (v * beta).astype(BF16)
            s_gc[d][out, :] = jnp.broadcast_to(gc4[:, 2 + d:3 + d], (R, LANES))
            for c in range(R // C):
                last = c * C + (C - 1 if d == 0 else 0)
                n = ch0 + it * (R // C) + c
                s_gl[d, n:n + 1, :] = jnp.broadcast_to(eg[last:last + 1, :], (1, LANES))

    def prep_stages(xs, tseq, tiles, row0, ch0):
        for it in tiles:
            prep_tile(xs, tseq, it, row0, ch0)
            yield

    def p1_stages(probs, row0, ch0):
        ds = [d for _, d in probs]
        rows = [pl.ds(pl.multiple_of(row0 + n * C, C), C) for n, _ in probs]
        gms = [_bdot_nt(jnp.concatenate([s_kb[d][rw, :], s_q[rw, :]], axis=0), s_k[rw, :])
               for (n, d), rw in zip(probs, rows)]
        yield
        a_list = []
        for (n, d), rw, gm in zip(probs, rows, gms):
            base = d * GM_PER_DIR
            gc = s_gc[d][rw, :]
            m_incl = mk_ref[base + GM_INCL]
            dec = jnp.exp((gc - gc.T) * m_incl)
            s_a[d, ch0 + n] = (gm[C:2 * C] * dec * m_incl).astype(BF16)
            a_list.append(-(gm[0:C] * dec * mk_ref[base + GM_STRICT]))
        xs = [mk_ref[GM_EYE] + a * mk_ref[d * GM_PER_DIR + GM_L1] for a, d in zip(a_list, ds)]
        for li in range(GN_LV):
            t1 = [_bdot(x, a * mk_ref[d * GM_PER_DIR + GM_LV + li]) for x, a, d in zip(xs, a_list, ds)]
            yield
            xs = [x + _bdot(t, x) for x, t in zip(xs, t1)]
            yield
        for (n, d), rw, x in zip(probs, rows, xs):
            tkv = _bdot(x, jnp.concatenate([s_kbg[d][rw, :], s_vb[d][rw, :]], axis=1))
            s_tk[d, ch0 + n] = tkv[:, 0:LANES].astype(BF16)
            s_tvb[d, ch0 + n] = tkv[:, LANES:2 * LANES]
        yield

    def p2_stages(steps, row0, ch0):
        for ns0 in steps:
            ns = [ch0 + n for n in ns0]
            rows = [pl.ds(pl.multiple_of(row0 + ns0[d] * C, C), C) for d in range(2)]
            st = [s_state[d] for d in range(2)]
            aq = [_bdot(jnp.concatenate([s_tk[d, ns[d]], s_qg[d][rows[d], :]], axis=0), st[d]) for d in range(2)]
            yield
            vnew = [s_tvb[d, ns[d]] - aq[d][0:C] for d in range(2)]
            oo = [aq[d][C:2 * C] + _bdot(s_a[d, ns[d]], vnew[d]) for d in range(2)]
            for d in range(2):
                s_o[d][rows[d], :] = oo[d]
                s_state[d] = st[d] * s_gl[d, pl.ds(ns[d], 1), :] + _bdot_tn(s_kg[d][rows[d], :], vnew[d])
            yield

    def groups(tseq):
        nch = tseq // C
        per = min(GDN_P1_CHUNKS, nch)
        probs = lambda g: [(g * per + j, 0) for j in range(per)] + [(nch - 1 - (g * per + j), 1) for j in range(per)]
        steps = lambda g: [(g * per + j, nch - 1 - (g * per + j)) for j in range(per)]
        return nch // per, per, probs, steps

    def epilogue_tile(o_ref, z_ref, tseq, it, row0):
        R = min(PREP_ROWS, tseq)
        rows = slice(row0 + it * R, row0 + (it + 1) * R)
        o = s_of[rows, :] + s_ob[rows, :]
        o = o * lax.rsqrt(jnp.mean(o * o, axis=1, keepdims=True) + 1e-6) * gprow(GP_NORM)
        z = z_ref[0, it * R:(it + 1) * R, :]
        o_ref[0, it * R:(it + 1) * R, :] = o * (z * jax.nn.sigmoid(z))

    def epilogue_stages(o_ref, z_ref, tseq, tiles, row0):
        for it in tiles:
            epilogue_tile(o_ref, z_ref, tseq, it, row0)
            yield

    lat, ctx = (lq, lk, lv, lbg), (cq, ck, cv, cbg)
    row_c, ch_c = t_lat, t_lat // C
    s_state[...] = jnp.zeros_like(s_state)
    _interleave(prep_stages(ctx, t_ctx, range(t_ctx // min(PREP_ROWS, t_ctx)), row_c, ch_c))
    ng_c, _, probs_c, steps_c = groups(t_ctx)
    ng, per, probs, steps = groups(t_lat)
    ntile = t_lat // min(PREP_ROWS, t_lat)
    tpg = max(per * C // min(PREP_ROWS, t_lat), 1)
    first = sorted(set(list(range(min(tpg, ntile))) + list(range(max(ntile - tpg, 0), ntile))))
    rest = [t for t in range(ntile) if t not in first]

    def ctx_scan():
        for g in range(ng_c):
            yield from p1_stages(probs_c(g), row_c, ch_c)
            yield from p2_stages(steps_c(g), row_c, ch_c)

    _interleave(prep_stages(lat, t_lat, first, 0, 0), ctx_scan())
    _interleave(prep_stages(lat, t_lat, rest, 0, 0), p1_stages(probs(0), 0, 0),
                epilogue_stages(o_ctx, cz, t_ctx, range(t_ctx // min(PREP_ROWS, t_ctx)), row_c))
    if ng > 1:
        def body(g, carry):
            _interleave(p1_stages(probs(g + 1), 0, 0), p2_stages(steps(g), 0, 0))
            return carry

        lax.fori_loop(0, ng - 1, body, 0)
    done = [t for t in range(ntile) if per * C <= t * min(PREP_ROWS, t_lat) and (t + 1) * min(PREP_ROWS, t_lat) <= (ng - 1) * per * C]
    _interleave(p2_stages(steps(ng - 1), 0, 0), epilogue_stages(o_lat, lz, t_lat, done, 0))
    _interleave(epilogue_stages(o_lat, lz, t_lat, [t for t in range(ntile) if t not in done], 0))


def gdn_mixer(u_lat, bg_lat, u_ctx, bg_ctx, gp):
    bsz, t_lat, _ = u_lat.shape
    t_ctx = u_ctx.shape[1]
    nh = GDN_HEADS
    masks = jnp.asarray(gdn_masks())

    def seq_specs(t):
        return [pl.BlockSpec((1, t, LANES), functools.partial(_col_block_map, off=o * nh)) for o in range(4)] + [
            pl.BlockSpec((1, 1, t, 4), lambda b, h: (b, h, 0, 0))]

    in_specs = seq_specs(t_lat) + seq_specs(t_ctx) + [
        pl.BlockSpec((GP_ROWS, LANES), lambda b, h: (0, h)),
        pl.BlockSpec((GN_MASKS, LANES, LANES), lambda b, h: (0, 0, 0)),
    ]
    out_specs = [pl.BlockSpec((1, t_lat, LANES), lambda b, h: (b, 0, h)),
                 pl.BlockSpec((1, t_ctx, LANES), lambda b, h: (b, 0, h))]
    nch = (t_lat + t_ctx) // GDN_CHUNK
    t_all = t_lat + t_ctx
    scratch = ([pltpu.VMEM((t_all, LANES), BF16) for _ in range(12)] + [pltpu.VMEM((t_all, LANES), F32) for _ in range(4)]
               + [pltpu.VMEM((2, nch, LANES, LANES), BF16), pltpu.VMEM((2, nch, LANES, LANES), F32),
                  pltpu.VMEM((2, nch, LANES, LANES), BF16),
                  pltpu.VMEM((2, max(nch, 8), LANES), F32), pltpu.VMEM((2, LANES, LANES), F32)])
    kern = functools.partial(_gdn_kernel, t_lat=t_lat, t_ctx=t_ctx)
    return pl.pallas_call(
        kern,
        grid=(bsz, nh),
        in_specs=in_specs,
        out_specs=out_specs,
        out_shape=[jax.ShapeDtypeStruct((bsz, t_lat, GDN_WIDTH), F32), jax.ShapeDtypeStruct((bsz, t_ctx, GDN_WIDTH), F32)],
        scratch_shapes=scratch,
        compiler_params=pltpu.CompilerParams(dimension_semantics=("arbitrary", "arbitrary"),
                                             vmem_limit_bytes=V7X_VMEM_LIMIT_BYTES),
        name="gdn_chunked",
    )(*([u_lat] * 4), bg_lat, *([u_ctx] * 4), bg_ctx, gp, masks)


def gdn_params(conv_w, a_log, dt_bias, norm_g):
    wd = GDN_WIDTH

    def scalar_row(x):
        return jnp.zeros((GDN_HEADS, LANES), F32).at[:, 2].set(x[0]).at[:, 3].set(x[1]).reshape(-1)

    rows = [conv_w[j, o * wd:(o + 1) * wd] for o in range(3) for j in range(5)]
    rows += [jnp.tile(norm_g, GDN_HEADS), scalar_row(a_log), scalar_row(dt_bias)]
    return jnp.concatenate([jnp.stack(rows), jnp.zeros((GP_ROWS - len(rows), wd), F32)], 0)


def gdn_bg(u_small):
    bsz, t, _ = u_small.shape
    x = u_small.reshape(bsz, t, 2, 2, GDN_HEADS)
    return jnp.transpose(x, (0, 4, 1, 2, 3)).reshape(bsz, GDN_HEADS, t, 4)


OD_Q_COL, OD_GATE_COL, OD_KV_COL = 0, ATT_WIDTH, 2 * ATT_WIDTH
OD_KV_W = 4 * ATT_HEAD
OD_COLS = OD_KV_COL + ATT_KV_HEADS * OD_KV_W


def rope_lane_tables(t):
    pos = np.arange(t)
    inv = ROPE_BASE ** (-np.arange(ROPE_FREQS, dtype=np.float64) / ROPE_FREQS)
    ang_row = (pos // GRID_W)[:, None] * inv
    ang_col = (pos % GRID_W)[:, None] * inv
    ang = np.concatenate([ang_row, ang_row, ang_col, ang_col] * 2, axis=1)
    first = (np.arange(LANES) % (2 * ROPE_FREQS)) < ROPE_FREQS
    cos, sin = np.cos(ang), np.sin(ang)
    return (jnp.asarray(cos, F32), jnp.asarray(np.where(first, -sin, 0.0), F32), jnp.asarray(np.where(first, 0.0, sin), F32))


def _rope(x, cos, sina, sinb):
    n = x.shape[1]
    return x * cos + pltpu.roll(x, n - ROPE_FREQS, axis=1) * sina + pltpu.roll(x, ROPE_FREQS, axis=1) * sinb


def _kv_rope_kernel(u_ref, cos_ref, sina_ref, sinb_ref, o_ref):
    x = u_ref[0]
    cos, sina, sinb = cos_ref[...], sina_ref[...], sinb_ref[...]
    parts = []
    for h in range(ATT_KV_HEADS):
        k2 = x[:, h * OD_KV_W:h * OD_KV_W + LANES]
        parts += [_rope(k2, cos, sina, sinb), x[:, h * OD_KV_W + LANES:(h + 1) * OD_KV_W]]
    o_ref[0] = jnp.concatenate(parts, axis=1).astype(BF16)


def kv_rope(u, tables, tr=512):
    bsz, t, _ = u.shape
    tr = min(tr, t)
    kvw = ATT_KV_HEADS * OD_KV_W
    tab = pl.BlockSpec((tr, LANES), lambda b, i: (i, 0))
    return pl.pallas_call(
        _kv_rope_kernel,
        grid=(bsz, t // tr),
        in_specs=[pl.BlockSpec((1, tr, kvw), lambda b, i: (b, i, OD_KV_COL // kvw)), tab, tab, tab],
        out_specs=pl.BlockSpec((1, tr, kvw), lambda b, i: (b, i, 0)),
        out_shape=jax.ShapeDtypeStruct((bsz, t, kvw), BF16),
        compiler_params=pltpu.CompilerParams(dimension_semantics=("arbitrary", "arbitrary")),
        name="kv_rope",
    )(u, *tables)


def _attn_kernel(sink_ref, *refs, has_local, nb):
    if has_local:
        q_ref, g_ref, kvp_ref, kvc_ref, kvn_ref, cos_ref, sina_ref, sinb_ref, kvx_ref, o_ref = refs
    else:
        q_ref, g_ref, kvx_ref, o_ref = refs
    tq = q_ref.shape[1]
    i = pl.program_id(1)
    q = q_ref[0]
    if has_local:
        rep = ATT_WIDTH // LANES
        cos, sina, sinb = (jnp.concatenate([r[...]] * rep, axis=1) for r in (cos_ref, sina_ref, sinb_ref))
        q = _rope(q, cos, sina, sinb)
    q = q * (ATT_HEAD ** -0.5 * LOG2E)
    lo = lax.broadcasted_iota(jnp.int32, (tq, LANES), 1) < HALF
    if has_local:
        a = lax.broadcasted_iota(jnp.int32, (tq, WINDOW), 0)
        c = lax.broadcasted_iota(jnp.int32, (tq, WINDOW), 1)
        ok_prev = (c >= a) & (i > 0)
        ok_next = (c <= a) & (i < nb - 1)
    outs = []
    for hk in range(ATT_KV_HEADS):
        kcol = slice(hk * OD_KV_W, hk * OD_KV_W + LANES)
        vcol = slice(hk * OD_KV_W + LANES, (hk + 1) * OD_KV_W)
        qs = []
        for pair in range(ATT_GROUP // 2):
            grp = q[:, (hk * (ATT_GROUP // 2) + pair) * LANES:(hk * (ATT_GROUP // 2) + pair + 1) * LANES]
            qs += [jnp.where(lo, grp, 0.0), jnp.where(lo, 0.0, grp)]
        lhs = jnp.concatenate(qs, axis=0).astype(BF16)
        kx, vx = kvx_ref[0, :, kcol].astype(BF16), kvx_ref[0, :, vcol].astype(BF16)
        if has_local:
            kall = jnp.concatenate([kvp_ref[0, :, kcol], kvc_ref[0, :, kcol], kvn_ref[0, :, kcol], kx], axis=0)
            vall = jnp.concatenate([kvp_ref[0, :, vcol], kvc_ref[0, :, vcol], kvn_ref[0, :, vcol], vx], axis=0)
        else:
            kall, vall = kx, vx
        s = _bdot_nt(lhs, kall)
        lov = lax.broadcasted_iota(jnp.int32, vall.shape, 1) < HALF
        zv = jnp.zeros_like(vall)
        v2 = jnp.concatenate([jnp.where(lov, vall, zv), jnp.where(lov, zv, vall)], axis=0)
        ps, inv_l = [], []
        for g in range(ATT_GROUP):
            sg = s[g * tq:(g + 1) * tq]
            if has_local:
                sg = jnp.concatenate([jnp.where(ok_prev, sg[:, 0:tq], NEG_INF), sg[:, tq:2 * tq],
                                      jnp.where(ok_next, sg[:, 2 * tq:3 * tq], NEG_INF), sg[:, 3 * tq:]], axis=1)
            sink = sink_ref[hk, g] * LOG2E
            m = jnp.maximum(jnp.max(sg, axis=1, keepdims=True), sink)
            p = jnp.exp2(sg - m)
            inv_l.append(1.0 / (jnp.sum(p, axis=1, keepdims=True) + jnp.exp2(sink - m)))
            ps.append(p.astype(BF16))
        pv = jnp.dot(jnp.concatenate([jnp.concatenate(ps[2 * pr:2 * pr + 2], axis=1) for pr in range(ATT_GROUP // 2)],
                                     axis=0), v2, preferred_element_type=F32)
        for pr in range(ATT_GROUP // 2):
            outs.append(pv[pr * tq:(pr + 1) * tq] * jnp.where(lo, inv_l[2 * pr], inv_l[2 * pr + 1]))
    g = g_ref[0]
    o_ref[0] = jnp.concatenate(outs, axis=1) * (g * jax.nn.sigmoid(g))


def sink_attention(u, kv_local, tables, u_ctx, sink):
    bsz, t, _ = u.shape
    tq = ATT_BLOCK
    nb = t // tq
    tx = u_ctx.shape[1]
    kvw = ATT_KV_HEADS * OD_KV_W
    has_local = kv_local is not None
    in_specs = [pl.BlockSpec(memory_space=pltpu.SMEM),
                pl.BlockSpec((1, tq, ATT_WIDTH), lambda b, i: (b, i, OD_Q_COL // ATT_WIDTH)),
                pl.BlockSpec((1, tq, ATT_WIDTH), lambda b, i: (b, i, OD_GATE_COL // ATT_WIDTH))]
    args = [sink.reshape(ATT_KV_HEADS, ATT_GROUP), u, u]
    if has_local:
        in_specs += [pl.BlockSpec((1, tq, kvw), lambda b, i: (b, jnp.maximum(i - 1, 0), 0)),
                     pl.BlockSpec((1, tq, kvw), lambda b, i: (b, i, 0)),
                     pl.BlockSpec((1, tq, kvw), lambda b, i: (b, jnp.minimum(i + 1, nb - 1), 0))]
        in_specs += [pl.BlockSpec((tq, LANES), lambda b, i: (i, 0))] * 3
        args += [kv_local] * 3 + list(tables)
    in_specs += [pl.BlockSpec((1, tx, kvw), lambda b, i: (b, 0, OD_KV_COL // kvw))]
    args += [u_ctx]
    return pl.pallas_call(
        functools.partial(_attn_kernel, has_local=has_local, nb=nb),
        grid=(bsz, nb),
        in_specs=in_specs,
        out_specs=pl.BlockSpec((1, tq, ATT_WIDTH), lambda b, i: (b, i, 0)),
        out_shape=jax.ShapeDtypeStruct((bsz, t, ATT_WIDTH), F32),
        compiler_params=pltpu.CompilerParams(dimension_semantics=("arbitrary", "arbitrary"),
                                             vmem_limit_bytes=V7X_VMEM_LIMIT_BYTES),
        name="sink_attention",
    )(*args)


def pack_odd_w_in(w_in):
    nq, nkv = ATT_WIDTH, ATT_KV_HEADS * ATT_HEAD
    parts = [w_in[:, :nq], w_in[:, nq + 2 * nkv:]]
    for h in range(ATT_KV_HEADS):
        k = w_in[:, nq + h * ATT_HEAD:nq + (h + 1) * ATT_HEAD]
        v = w_in[:, nq + nkv + h * ATT_HEAD:nq + nkv + (h + 1) * ATT_HEAD]
        parts += [k, k, v, v]
    return jnp.concatenate(parts, 1).astype(BF16)


def pack_even_w_in(w_in):
    g1 = 4 * GDN_WIDTH
    r0 = GDN_IN
    parts = [w_in[:, :g1], w_in[:, r0:r0 + 4 * RWKV_WIDTH], w_in[:, r0 + 4 * RWKV_WIDTH:], w_in[:, g1:GDN_IN],
             jnp.zeros((D_MODEL, EV_COLS - EV_SMALL_COL - 4 * GDN_HEADS), F32)]
    return jnp.concatenate(parts, 1).astype(BF16)


def kernel(x, c, ctx, c_ctx, mod_w, mod_b, ln_g, ln_b, ev_w_in, ev_w_out, gdn_conv, gdn_a_log, gdn_dt_bias,
           gdn_norm_g, rwkv_mix, rwkv_w0, rwkv_w_up, rwkv_a0, rwkv_a_up, rwkv_k_k, rwkv_k_a, rwkv_r_k,
           rwkv_gn_g, rwkv_gn_b, od_w_in, od_w_out, od_sink):
    bsz, t_lat, d = x.shape
    t_ctx = ctx.shape[1]
    x_lat, x_ctx = x, ctx
    n_mod = -(-(bsz + 1) // 8) * 8
    cc = jax.nn.silu(jnp.concatenate([c, c_ctx[None], jnp.zeros((n_mod - bsz - 1, d), F32)], 0))
    mods = mod_all_layers(cc, mod_w, mod_b)
    for layer in range(DEPTH):
        need_ctx = layer < DEPTH - 1
        i = layer // 2
        mod = mods[layer]
        sh, sc, gt = (mod[:bsz, j * d:(j + 1) * d][:, None, :] for j in range(3))
        sh_c, sc_c, gt_c = (jnp.broadcast_to(mod[bsz, j * d:(j + 1) * d], (bsz, 1, d)) for j in range(3))
        if layer % 2 == 0:
            w_in = pack_even_w_in(ev_w_in[i])
            w_out = ev_w_out[i].astype(BF16)
            u_lat = inproj(x_lat, sc, sh, w_in, 1024, EV_TN)
            u_ctx = inproj(x_ctx.reshape(1, bsz * t_ctx, d), sc_c[:1], sh_c[:1], w_in, 1024, EV_TN).reshape(bsz, t_ctx, -1)
            small = slice(EV_SMALL_COL, EV_SMALL_COL + 4 * GDN_HEADS)
            og_lat, og_ctx = gdn_mixer(u_lat, gdn_bg(u_lat[..., small]), u_ctx, gdn_bg(u_ctx[..., small]),
                                       gdn_params(gdn_conv[i], gdn_a_log[i], gdn_dt_bias[i], gdn_norm_g[i]))
            or_lat, or_ctx = rwkv_mixer(u_lat, u_ctx, *rwkv_params(
                rwkv_mix[i], rwkv_w0[i], rwkv_w_up[i], rwkv_a0[i], rwkv_a_up[i], rwkv_k_k[i], rwkv_k_a[i],
                rwkv_r_k[i], rwkv_gn_g[i], rwkv_gn_b[i]))
            a_lat, a_ctx = [og_lat, or_lat], [og_ctx, or_ctx]
        else:
            w_in = pack_odd_w_in(od_w_in[i])
            w_out = od_w_out[i].astype(BF16)
            u_lat = inproj(x_lat, sc, sh, w_in, 1024, OD_TN)
            u_ctx = inproj(x_ctx.reshape(1, bsz * t_ctx, d), sc_c[:1], sh_c[:1], w_in, 1024, OD_TN).reshape(bsz, t_ctx, -1)
            tables = rope_lane_tables(t_lat)
            o_lat = sink_attention(u_lat, kv_rope(u_lat, tables), tables, u_ctx, od_sink[i])
            o_ctx = sink_attention(u_ctx, None, None, u_ctx, od_sink[i]) if need_ctx else None
            a_lat, a_ctx = [o_lat], [o_ctx]
        x_lat = outproj_ln(a_lat, w_out, x_lat, gt, ln_g[layer], ln_b[layer])
        if need_ctx:
            x_ctx = outproj_ln(a_ctx, w_out, x_ctx, gt_c, ln_g[layer], ln_b[layer])
    return x_lat
```

```python
import functools

import jax
import jax.numpy as jnp
import numpy as np
from jax import lax
from jax.experimental import pallas as pl
from jax.experimental.pallas import tpu as pltpu

F32 = jnp.float32
BF16 = jnp.bfloat16

D_MODEL = 2048
DEPTH = 4
GRID_W = 64
GDN_WIDTH = D_MODEL // 2
GDN_HEADS = 8
GDN_CHUNK = 128
RWKV_WIDTH = D_MODEL // 2
RWKV_HEADS = 16
RWKV_CHUNK = 64
RWKV_GN_EPS = 64e-5
ATT_HEAD = 64
ATT_Q_HEADS = D_MODEL // ATT_HEAD
ATT_KV_HEADS = ATT_Q_HEADS // 8
ATT_GROUP = ATT_Q_HEADS // ATT_KV_HEADS
ATT_WIDTH = ATT_Q_HEADS * ATT_HEAD
WINDOW = 128
ATT_BLOCK = 128
ROPE_BASE = 10000.0
ROPE_FREQS = ATT_HEAD // 4
NEG_INF = -1e30
LOG2E = 1.4426950408889634
DEEPNORM_ALPHA = (2 * DEPTH) ** 0.25
LN_EPS = 1e-5
GDN_IN = 4 * GDN_WIDTH + 4 * GDN_HEADS

LANES = 128
PREP_ROWS = 256
HALF = 64
V7X_VMEM_LIMIT_BYTES = 56 * 1024 * 1024
EV_GDN_COL = 0
EV_RWKV_COL = 4 * GDN_WIDTH
EV_LORA_COL = EV_RWKV_COL + 4 * RWKV_WIDTH
EV_SMALL_COL = EV_LORA_COL + LANES
EV_COLS = EV_SMALL_COL + LANES
EV_TN = 1408
OD_TN = 1280


def _mod_kernel(a_ref, w_ref, b_ref, o_ref):
    o_ref[0] = jnp.dot(a_ref[...].astype(BF16), w_ref[0].astype(BF16), preferred_element_type=F32) + b_ref[0]


def mod_all_layers(a, w, b, tn=512):
    m, k = a.shape
    nl, _, n = w.shape
    return pl.pallas_call(
        _mod_kernel,
        grid=(nl, n // tn),
        in_specs=[pl.BlockSpec((m, k), lambda l, j: (0, 0)), pl.BlockSpec((1, k, tn), lambda l, j: (l, 0, j)),
                  pl.BlockSpec((1, 1, tn), lambda l, j: (l, 0, j))],
        out_specs=pl.BlockSpec((1, m, tn), lambda l, j: (l, 0, j)),
        out_shape=jax.ShapeDtypeStruct((nl, m, n), F32),
        compiler_params=pltpu.CompilerParams(dimension_semantics=("arbitrary", "arbitrary"),
                                             vmem_limit_bytes=V7X_VMEM_LIMIT_BYTES),
        name="mod_matmul",
    )(a, w, b.reshape(nl, 1, n))


def _inproj_kernel(x_ref, sc_ref, sh_ref, w_ref, o_ref, h_ref):
    @pl.when(pl.program_id(2) == 0)
    def _():
        h_ref[...] = (x_ref[0] * (1.0 + sc_ref[0]) + sh_ref[0]).astype(BF16)

    o_ref[0] = jnp.dot(h_ref[...], w_ref[...], preferred_element_type=F32)


def inproj(x, sc, sh, w, tm, tn):
    bsz, t, d = x.shape
    n = w.shape[1]
    tm = min(tm, t)
    assert t % tm == 0 and n % tn == 0
    return pl.pallas_call(
        _inproj_kernel,
        grid=(bsz, t // tm, n // tn),
        in_specs=[pl.BlockSpec((1, tm, d), lambda b, i, j: (b, i, 0)),
                  pl.BlockSpec((1, 1, d), lambda b, i, j: (b, 0, 0)),
                  pl.BlockSpec((1, 1, d), lambda b, i, j: (b, 0, 0)),
                  pl.BlockSpec((d, tn), lambda b, i, j: (0, j))],
        out_specs=pl.BlockSpec((1, tm, tn), lambda b, i, j: (b, i, j)),
        out_shape=jax.ShapeDtypeStruct((bsz, t, n), F32),
        scratch_shapes=[pltpu.VMEM((tm, d), BF16)],
        compiler_params=pltpu.CompilerParams(dimension_semantics=("arbitrary", "arbitrary", "arbitrary"),
                                             vmem_limit_bytes=V7X_VMEM_LIMIT_BYTES),
        name="inproj",
    )(x, sc, sh, w)


def _outproj_kernel(*refs, n_a):
    a_refs, (w_ref, x_ref, gt_ref, g_ref, b_ref, o_ref) = refs[:n_a], refs[n_a:]
    y = None
    k0 = 0
    for a_ref in a_refs:
        kw = a_ref.shape[2]
        term = jnp.dot(a_ref[0].astype(BF16), w_ref[k0:k0 + kw, :], preferred_element_type=F32)
        y = term if y is None else y + term
        k0 += kw
    z = DEEPNORM_ALPHA * x_ref[0] + gt_ref[0] * y
    mu = jnp.mean(z, axis=1, keepdims=True)
    zc = z - mu
    var = jnp.mean(zc * zc, axis=1, keepdims=True)
    o_ref[0] = zc * lax.rsqrt(var + LN_EPS) * g_ref[...] + b_ref[...]


def outproj_ln(a_list, w, x, gt, ln_g, ln_b, tm=512):
    bsz, t, d = x.shape
    tm = min(tm, t)
    in_specs = [pl.BlockSpec((1, tm, a.shape[2]), lambda b, i: (b, i, 0)) for a in a_list] + [
        pl.BlockSpec(w.shape, lambda b, i: (0, 0)),
        pl.BlockSpec((1, tm, d), lambda b, i: (b, i, 0)),
        pl.BlockSpec((1, 1, d), lambda b, i: (b, 0, 0)),
        pl.BlockSpec((1, d), lambda b, i: (0, 0)),
        pl.BlockSpec((1, d), lambda b, i: (0, 0))]
    return pl.pallas_call(
        functools.partial(_outproj_kernel, n_a=len(a_list)),
        grid=(bsz, t // tm),
        in_specs=in_specs,
        out_specs=pl.BlockSpec((1, tm, d), lambda b, i: (b, i, 0)),
        out_shape=jax.ShapeDtypeStruct((bsz, t, d), F32),
        compiler_params=pltpu.CompilerParams(dimension_semantics=("arbitrary", "arbitrary"),
                                             vmem_limit_bytes=V7X_VMEM_LIMIT_BYTES),
        name="outproj_ln",
    )(*a_list, w, x, gt, ln_g.reshape(1, d), ln_b.reshape(1, d))


def _bdot(a, b):
    return jnp.dot(a.astype(BF16), b.astype(BF16), preferred_element_type=F32)


def _bdot_nt(a, b):
    return lax.dot_general(a.astype(BF16), b.astype(BF16), (((1,), (1,)), ((), ())), preferred_element_type=F32)


def _bdot_tn(a, b):
    return lax.dot_general(a.astype(BF16), b.astype(BF16), (((0,), (0,)), ((), ())), preferred_element_type=F32)


(PV_MR0, PV_MR1, PV_MK0, PV_MK1, PV_MV0, PV_MV1, PV_MG0, PV_MG1, PV_W00, PV_W01, PV_A00, PV_A01,
 PV_KK, PV_KA, PV_RK, PV_GNG, PV_GNB) = range(17)
PV_ROWS = 24
MK_ABD, MK_P, MK_Y, MK_L1 = 0, 1, 2, 3
MK_LV = 4
N_LV = 5
MK_PER_DIR = MK_LV + N_LV
MK_EYE = 2 * MK_PER_DIR
MK_BD = MK_EYE + 1
N_MASKS = MK_BD + 1


def rwkv_masks():
    ri = np.arange(128)[:, None]
    ci = np.arange(128)[None, :]
    i, j = ri & 63, ci & 63
    same = (ri < 64) == (ci < 64)
    out = np.zeros((N_MASKS, 128, 128), np.float32)
    for d in range(2):
        strict = (j < i) if d == 0 else (j > i)
        incl = (j <= i) if d == 0 else (j >= i)
        base = d * MK_PER_DIR
        out[base + MK_ABD] = strict & same
        out[base + MK_P] = strict & (ci >= 64)
        out[base + MK_Y] = incl
        for li, s in enumerate((1, 2, 4, 8, 16, 32)):
            blk = (i // (2 * s)) == (j // (2 * s))
            if d == 0:
                m = blk & ((i & s) != 0) & ((j & s) == 0)
            else:
                m = blk & ((i & s) == 0) & ((j & s) != 0)
            out[base + (MK_L1 if li == 0 else MK_LV + li - 1)] = m & same
    out[MK_EYE] = ri == ci
    out[MK_BD] = same
    return out


def _seg_sum(x, lo):
    s0 = jnp.sum(jnp.where(lo, x, 0.0), axis=1, keepdims=True)
    s1 = jnp.sum(jnp.where(lo, 0.0, x), axis=1, keepdims=True)
    return jnp.where(lo, s0, s1)


def _chunk_cumsum(x, chunk):
    rowm = lax.broadcasted_iota(jnp.int32, x.shape, 0) & (chunk - 1)
    s = 1
    while s < chunk:
        x = x + jnp.where(rowm >= s, pltpu.roll(x, s, axis=0), 0.0)
        s *= 2
    return x


def _interleave(*gens):
    live = list(gens)
    while live:
        for g in list(live):
            try:
                next(g)
            except StopIteration:
                live.remove(g)


def _rwkv_kernel(lr, lk, lv, lg, ll, cr, ck, cv, cg, cl, pv_ref, mixl_ref, wup_ref, aup_ref, mk_ref,
                 o_lat, o_ctx,
                 s_at0, s_at1, s_rt0, s_rt1, s_bt0, s_bt1, s_kt0, s_kt1, s_bh0, s_bh1, s_kh0, s_kh1, s_v,
                 s_g, s_bonus, s_yf, s_yb, s_w, s_tp, s_gr, s_et, s_state, *, t_lat, t_ctx, p1_chunks):
    C = RWKV_CHUNK
    s_at, s_rt, s_bt, s_kt, s_bh, s_kh = ((s_at0, s_at1), (s_rt0, s_rt1), (s_bt0, s_bt1), (s_kt0, s_kt1),
                                          (s_bh0, s_bh1), (s_kh0, s_kh1))
    s_y = (s_yf, s_yb)
    lo64 = lax.broadcasted_iota(jnp.int32, (C, LANES), 1) < HALF
    bd_ones = mk_ref[MK_BD]

    def pvrow(i):
        return pv_ref[pl.ds(i, 1), :]

    def prep_tile(xs, tseq, it, row0, ch0):
        xr, xk, xv, xg, xl = xs
        R = min(PREP_ROWS, tseq)
        s = it * R
        lo = lax.broadcasted_iota(jnp.int32, (R, LANES), 1) < HALF
        rows = lax.broadcasted_iota(jnp.int32, (R, LANES), 0)

        def shifted(ref, m0, m1):
            cur = ref[0, s:s + R, :]
            prev = ref[0, s - 1:s - 1 + R, :] if s > 0 else jnp.where(rows == 0, 0.0, pltpu.roll(cur, 1, axis=0))
            nxt = (ref[0, s + 1:s + 1 + R, :] if s + R < tseq
                   else jnp.where(rows == R - 1, 0.0, pltpu.roll(cur, R - 1, axis=0)))
            return cur + m0 * (prev - cur) + m1 * (nxt - cur)

        r = shifted(xr, pvrow(PV_MR0), pvrow(PV_MR1))
        k = shifted(xk, pvrow(PV_MK0), pvrow(PV_MK1))
        v = shifted(xv, pvrow(PV_MV0), pvrow(PV_MV1))
        g = shifted(xg, pvrow(PV_MG0), pvrow(PV_MG1))
        lor = shifted(xl, mixl_ref[0:1, :], mixl_ref[1:2, :])
        th = jnp.tanh(lor)
        kkf = k * pvrow(PV_KK)
        kk = kkf * lax.rsqrt(_seg_sum(kkf * kkf, lo) + 1e-6)
        ka = pvrow(PV_KA)
        out = slice(row0 + s, row0 + s + R)
        kd_sum = None
        for d in range(2):
            w_pre = pvrow(PV_W00 + d) + _bdot(th, wup_ref[d])
            lw = (-np.exp(-0.5)) * jax.nn.sigmoid(w_pre)
            ag = jax.nn.sigmoid(pvrow(PV_A00 + d) + _bdot(lor, aup_ref[d]))
            kd = k * (1.0 + (ag - 1.0) * ka)
            bdv = kk * ag
            kd_sum = kd if kd_sum is None else kd_sum + kd
            c_fwd = _chunk_cumsum(lw, C)
            tot = jnp.concatenate([jnp.broadcast_to(c_fwd[(c + 1) * C - 1:(c + 1) * C, :], (C, LANES))
                                   for c in range(R // C)], axis=0)
            c_bwd = tot - c_fwd + lw
            cw, rest = (c_fwd, c_bwd - lw) if d == 0 else (c_bwd, c_fwd - lw)
            e_in = jnp.exp(-cw)
            e_out = jnp.exp(rest)
            s_at[d][out, :] = (-kk * jnp.exp(cw - lw)).astype(BF16)
            s_rt[d][out, :] = (r * jnp.exp(cw)).astype(BF16)
            s_bt[d][out, :] = (bdv * e_in).astype(BF16)
            s_kt[d][out, :] = (kd * e_in).astype(BF16)
            s_bh[d][out, :] = (bdv * e_out).astype(BF16)
            s_kh[d][out, :] = (kd * e_out).astype(BF16)
            for c in range(R // C):
                last = c * C + (C - 1 if d == 0 else 0)
                n = ch0 + it * (R // C) + c
                s_et[d, n:n + 1, :] = jnp.exp(cw[last:last + 1, :])
        s_v[out, :] = v.astype(BF16)
        s_g[out, :] = g
        s_bonus[out, :] = _seg_sum(r * kd_sum * pvrow(PV_RK), lo) * v

    def prep_stages(xs, tseq, tiles, row0, ch0):
        for it in tiles:
            prep_tile(xs, tseq, it, row0, ch0)
            yield

    def p1_stages(probs, row0, ch0):
        ds = [d for _, d in probs]
        rows = [pl.ds(pl.multiple_of(row0 + n * C, C), C) for n, _ in probs]
        a2s, gms = [], []
        for (n, d), rw in zip(probs, rows):
            at, rt = s_at[d][rw, :], s_rt[d][rw, :]
            zero = jnp.zeros_like(at)
            a2 = jnp.concatenate([jnp.where(lo64, at, zero), jnp.where(lo64, zero, at)], axis=0)
            l1 = jnp.concatenate([a2, jnp.where(lo64, rt, zero), jnp.where(lo64, zero, rt)], axis=0)
            a2s.append(a2)
            gms.append(_bdot_nt(l1, jnp.concatenate([s_bt[d][rw, :], s_kt[d][rw, :]], axis=0)))
        yield
        ps, a_list = [], []
        for (n, d), rw, gm in zip(probs, rows, gms):
            base = d * MK_PER_DIR
            ga = gm[0:2 * C]
            v = s_v[rw, :]
            ps.append(_bdot(ga * mk_ref[base + MK_P], jnp.concatenate([v, v], axis=0)))
            s_gr[d, ch0 + n] = (gm[2 * C:4 * C] * mk_ref[base + MK_Y]).astype(BF16)
            ga_sw = jnp.concatenate([ga[0:C], pltpu.roll(ga[C:2 * C], HALF, axis=1)], axis=0)
            a_list.append(ga_sw * mk_ref[base + MK_ABD])
        yield
        xs = [mk_ref[MK_EYE] + a * mk_ref[d * MK_PER_DIR + MK_L1] for a, d in zip(a_list, ds)]
        for li in range(N_LV):
            t1 = [_bdot(x, a * mk_ref[d * MK_PER_DIR + MK_LV + li]) for x, a, d in zip(xs, a_list, ds)]
            yield
            xs = [x + _bdot(t, x) for x, t in zip(xs, t1)]
            yield
        for (n, d), x, a2, p in zip(probs, xs, a2s, ps):
            wtp = _bdot(x, jnp.concatenate([a2.astype(F32), p], axis=1))
            s_w[d, ch0 + n] = wtp[:, 0:LANES].astype(BF16)
            s_tp[d, ch0 + n] = jnp.where(lo64, wtp[0:C, LANES:2 * LANES], wtp[C:2 * C, LANES:2 * LANES])
        yield

    def p2_stages(steps, row0, ch0):
        for ns0 in steps:
            ns = [ch0 + n for n in ns0]
            rows = [pl.ds(pl.multiple_of(row0 + ns0[d] * C, C), C) for d in range(2)]
            st = [s_state[d] for d in range(2)]
            xs = [_bdot_nt(jnp.concatenate([s_w[d, ns[d]], s_rt[d][rows[d], :]], axis=0), st[d]) for d in range(2)]
            yield
            uv = [jnp.concatenate([(xs[d][0:C] + xs[d][C:2 * C] + s_tp[d, ns[d]]).astype(BF16), s_v[rows[d], :]], axis=0)
                  for d in range(2)]
            yb = [_bdot(s_gr[d, ns[d]], uv[d]) for d in range(2)]
            for d in range(2):
                s_y[d][rows[d], :] = xs[d][2 * C:3 * C] + jnp.where(lo64, yb[d][0:C], yb[d][C:2 * C])
                bk = jnp.concatenate([s_bh[d][rows[d], :], s_kh[d][rows[d], :]], axis=0)
                s_state[d] = st[d] * s_et[d, pl.ds(ns[d], 1), :] + bd_ones * _bdot_tn(uv[d], bk)
            yield

    def groups(tseq):
        nch = tseq // C
        per = min(p1_chunks, nch)
        probs = lambda g: [(g * per + j, 0) for j in range(per)] + [(nch - 1 - (g * per + j), 1) for j in range(per)]
        steps = lambda g: [(g * per + j, nch - 1 - (g * per + j)) for j in range(per)]
        return nch // per, per, probs, steps

    def epilogue_tile(o_ref, tseq, it, row0):
        R = min(PREP_ROWS, tseq)
        lo = lax.broadcasted_iota(jnp.int32, (R, LANES), 1) < HALF
        rows = slice(row0 + it * R, row0 + (it + 1) * R)
        y = s_yf[rows, :] + s_yb[rows, :]
        mu = _seg_sum(y, lo) * (1.0 / HALF)
        yc = y - mu
        var = _seg_sum(yc * yc, lo) * (1.0 / HALF)
        yn = yc * lax.rsqrt(var + RWKV_GN_EPS) * pvrow(PV_GNG) + pvrow(PV_GNB)
        g = s_g[rows, :]
        o_ref[0, it * R:(it + 1) * R, :] = (yn + s_bonus[rows, :]) * (g * jax.nn.sigmoid(g))

    def epilogue_stages(o_ref, tseq, tiles, row0):
        for it in tiles:
            epilogue_tile(o_ref, tseq, it, row0)
            yield

    lat, ctx = (lr, lk, lv, lg, ll), (cr, ck, cv, cg, cl)
    row_c, ch_c = t_lat, t_lat // C
    s_state[...] = jnp.zeros_like(s_state)
    _interleave(prep_stages(ctx, t_ctx, range(t_ctx // min(PREP_ROWS, t_ctx)), row_c, ch_c))
    ng_c, _, probs_c, steps_c = groups(t_ctx)
    ng, per, probs, steps = groups(t_lat)
    ntile = t_lat // min(PREP_ROWS, t_lat)
    tpg = max(per * C // min(PREP_ROWS, t_lat), 1)
    first = sorted(set(list(range(min(tpg, ntile))) + list(range(max(ntile - tpg, 0), ntile))))
    rest = [t for t in range(ntile) if t not in first]

    def ctx_scan():
        for g in range(ng_c):
            yield from p1_stages(probs_c(g), row_c, ch_c)
            yield from p2_stages(steps_c(g), row_c, ch_c)

    _interleave(prep_stages(lat, t_lat, first, 0, 0), ctx_scan())
    _interleave(prep_stages(lat, t_lat, rest, 0, 0), p1_stages(probs(0), 0, 0),
                epilogue_stages(o_ctx, t_ctx, range(t_ctx // min(PREP_ROWS, t_ctx)), row_c))
    for g in range(ng - 1):
        _interleave(p1_stages(probs(g + 1), 0, 0), p2_stages(steps(g), 0, 0))
    done = [t for t in range(ntile) if per * C <= t * min(PREP_ROWS, t_lat) and (t + 1) * min(PREP_ROWS, t_lat) <= (ng - 1) * per * C]
    _interleave(p2_stages(steps(ng - 1), 0, 0), epilogue_stages(o_lat, t_lat, done, 0))
    _interleave(epilogue_stages(o_lat, t_lat, [t for t in range(ntile) if t not in done], 0))


def _col_block_map(b, p, *, off):
    return (b, 0, off + p)


def rwkv_mixer(u_lat, u_ctx, pv, mixl, wup, aup, p1_chunks=8):
    bsz, t_lat, _ = u_lat.shape
    t_ctx = u_ctx.shape[1]
    npair = RWKV_HEADS // 2
    c0 = EV_RWKV_COL // LANES
    nch = (t_lat + t_ctx) // RWKV_CHUNK
    t_all = t_lat + t_ctx
    masks = jnp.asarray(rwkv_masks())

    def seq_specs(t):
        return [pl.BlockSpec((1, t, LANES), functools.partial(_col_block_map, off=c0 + o * npair)) for o in range(4)] + [
            pl.BlockSpec((1, t, LANES), lambda b, p: (b, 0, EV_LORA_COL // LANES))]

    in_specs = seq_specs(t_lat) + seq_specs(t_ctx) + [
        pl.BlockSpec((PV_ROWS, LANES), lambda b, p: (0, p)),
        pl.BlockSpec((8, LANES), lambda b, p: (0, 0)),
        pl.BlockSpec((2, LANES, LANES), lambda b, p: (0, 0, p)),
        pl.BlockSpec((2, LANES, LANES), lambda b, p: (0, 0, p)),
        pl.BlockSpec((N_MASKS, LANES, LANES), lambda b, p: (0, 0, 0)),
    ]
    out_specs = [pl.BlockSpec((1, t_lat, LANES), lambda b, p: (b, 0, p)),
                 pl.BlockSpec((1, t_ctx, LANES), lambda b, p: (b, 0, p))]
    scratch = ([pltpu.VMEM((t_all, LANES), BF16) for _ in range(13)] + [pltpu.VMEM((t_all, LANES), F32) for _ in range(4)]
               + [pltpu.VMEM((2, nch, LANES, LANES), BF16), pltpu.VMEM((2, nch, RWKV_CHUNK, LANES), F32),
                  pltpu.VMEM((2, nch, LANES, LANES), BF16)]
               + [pltpu.VMEM((2, nch, LANES), F32), pltpu.VMEM((2, LANES, LANES), F32)])
    kern = functools.partial(_rwkv_kernel, t_lat=t_lat, t_ctx=t_ctx, p1_chunks=p1_chunks)
    return pl.pallas_call(
        kern,
        grid=(bsz, npair),
        in_specs=in_specs,
        out_specs=out_specs,
        out_shape=[jax.ShapeDtypeStruct((bsz, t_lat, RWKV_WIDTH), F32), jax.ShapeDtypeStruct((bsz, t_ctx, RWKV_WIDTH), F32)],
        scratch_shapes=scratch,
        compiler_params=pltpu.CompilerParams(dimension_semantics=("arbitrary", "arbitrary"),
                                             vmem_limit_bytes=V7X_VMEM_LIMIT_BYTES),
        name="rwkv7_chunked",
    )(*([u_lat] * 5), *([u_ctx] * 5), pv, mixl, wup, aup, masks)


def rwkv_params(mix, w0, w_up, a0, a_up, k_k, k_a, r_k, gn_g, gn_b):
    wd = RWKV_WIDTH
    rows = [mix[0, 0:wd], mix[1, 0:wd], mix[0, wd:2 * wd], mix[1, wd:2 * wd], mix[0, 2 * wd:3 * wd], mix[1, 2 * wd:3 * wd],
            mix[0, 3 * wd:4 * wd], mix[1, 3 * wd:4 * wd], w0[0], w0[1], a0[0], a0[1], k_k, k_a, r_k.reshape(-1), gn_g, gn_b]
    pv = jnp.concatenate([jnp.stack(rows), jnp.zeros((PV_ROWS - len(rows), wd), F32)], 0)
    mixl = jnp.concatenate([mix[:, 4 * wd:], jnp.zeros((6, LANES), F32)], 0)
    z = jnp.zeros((2, HALF, wd), F32)
    wup = jnp.concatenate([w_up, z], 1)
    aup = jnp.concatenate([z, a_up], 1)
    return pv, mixl, wup, aup


GP_CONV_Q, GP_CONV_K, GP_CONV_V = 0, 5, 10
GP_NORM, GP_SC_ALOG, GP_SC_DTB = 15, 16, 17
GP_ROWS = 24
GM_STRICT, GM_INCL, GM_L1, GM_LV = 0, 1, 2, 3
GN_LV = 6
GM_PER_DIR = GM_LV + GN_LV
GM_EYE = 2 * GM_PER_DIR
GN_MASKS = GM_EYE + 1
GDN_P1_CHUNKS = 4


def gdn_masks():
    i = np.arange(GDN_CHUNK)[:, None]
    j = np.arange(GDN_CHUNK)[None, :]
    out = np.zeros((GN_MASKS, GDN_CHUNK, GDN_CHUNK), np.float32)
    for d in range(2):
        base = d * GM_PER_DIR
        out[base + GM_STRICT] = (j < i) if d == 0 else (j > i)
        out[base + GM_INCL] = (j <= i) if d == 0 else (j >= i)
        for li in range(GN_LV + 1):
            s = 1 << li
            blk = (i // (2 * s)) == (j // (2 * s))
            m = blk & (((i & s) != 0) & ((j & s) == 0) if d == 0 else ((i & s) == 0) & ((j & s) != 0))
            out[base + (GM_L1 if li == 0 else GM_LV + li - 1)] = m
    out[GM_EYE] = i == j
    return out


def _gdn_kernel(lq, lk, lv, lz, lbg, cq, ck, cv, cz, cbg, gp_ref, mk_ref, o_lat, o_ctx,
                s_q, s_k, s_kb0, s_kb1, s_kbg0, s_kbg1, s_qg0, s_qg1, s_kg0, s_kg1, s_vb0, s_vb1,
                s_gc0, s_gc1, s_of, s_ob, s_tk, s_tvb, s_a, s_gl, s_state, *, t_lat, t_ctx):
    C = GDN_CHUNK
    s_kb, s_kbg, s_qg, s_kg, s_vb = (s_kb0, s_kb1), (s_kbg0, s_kbg1), (s_qg0, s_qg1), (s_kg0, s_kg1), (s_vb0, s_vb1)
    s_gc, s_o = (s_gc0, s_gc1), (s_of, s_ob)

    def gprow(i):
        return gp_ref[pl.ds(i, 1), :]

    def prep_tile(xs, tseq, it, row0, ch0):
        xq, xk, xv, xbg = xs
        R = min(PREP_ROWS, tseq)
        s = it * R

        def conv_silu(ref, prow0):
            rows = lax.broadcasted_iota(jnp.int32, (R, LANES), 0)
            acc = None
            for j in range(5):
                off = j - 2
                lo_r, hi_r = s + off, s + off + R
                if lo_r < 0:
                    x = jnp.where(rows >= -off, pltpu.roll(ref[0, 0:R, :], -off, axis=0), 0.0)
                elif hi_r > tseq:
                    x = jnp.where(rows < R - off, pltpu.roll(ref[0, tseq - R:tseq, :], R - off, axis=0), 0.0)
                else:
                    x = ref[0, lo_r:hi_r, :]
                term = gprow(prow0 + j) * x
                acc = term if acc is None else acc + term
            return acc * jax.nn.sigmoid(acc)

        qc = conv_silu(xq, GP_CONV_Q)
        kc = conv_silu(xk, GP_CONV_K)
        v = conv_silu(xv, GP_CONV_V)
        q = qc * (lax.rsqrt(jnp.sum(qc * qc, axis=1, keepdims=True) + 1e-6) * (LANES ** -0.5))
        k = kc * lax.rsqrt(jnp.sum(kc * kc, axis=1, keepdims=True) + 1e-6)
        bg = xbg[0, 0, s:s + R, :]
        lane4 = lax.broadcasted_iota(jnp.int32, bg.shape, 1)
        beta4 = jax.nn.sigmoid(bg)
        g4 = -jnp.exp(gprow(GP_SC_ALOG)[:, 0:4]) * jax.nn.softplus(bg + gprow(GP_SC_DTB)[:, 0:4])
        c_fwd = _chunk_cumsum(g4, C)
        tot = jnp.concatenate([jnp.broadcast_to(c_fwd[(c + 1) * C - 1:(c + 1) * C, :], (C, 4)) for c in range(R // C)],
                              axis=0)
        c_bwd = tot - c_fwd + g4
        fwd_lane = lane4 == 2
        gc4 = jnp.where(fwd_lane, c_fwd, c_bwd)
        eg4 = jnp.exp(gc4)
        er4 = jnp.exp(jnp.where(fwd_lane, c_bwd, c_fwd) - g4)
        out = slice(row0 + s, row0 + s + R)
        s_q[out, :] = q.astype(BF16)
        s_k[out, :] = k.astype(BF16)
        for d in range(2):
            beta, eg = beta4[:, d:d + 1], eg4[:, 2 + d:3 + d]
            kb = k * beta
            s_kb[d][out, :] = kb.astype(BF16)
            s_kbg[d][out, :] = (kb * eg).astype(BF16)
            s_qg[d][out, :] = (q * eg).astype(BF16)
            s_kg[d][out, :] = (k * er4[:, 2 + d:3 + d]).astype(BF16)
            s_vb[d][out, :] = (v * beta).astype(BF16)
            s_gc[d][out, :] = jnp.broadcast_to(gc4[:, 2 + d:3 + d], (R, LANES))
            for c in range(R // C):
                last = c * C + (C - 1 if d == 0 else 0)
                n = ch0 + it * (R // C) + c
                s_gl[d, n:n + 1, :] = jnp.broadcast_to(eg[last:last + 1, :], (1, LANES))

    def prep_stages(xs, tseq, tiles, row0, ch0):
        for it in tiles:
            prep_tile(xs, tseq, it, row0, ch0)
            yield

    def p1_stages(probs, row0, ch0):
        ds = [d for _, d in probs]
        rows = [pl.ds(pl.multiple_of(row0 + n * C, C), C) for n, _ in probs]
        gms = [_bdot_nt(jnp.concatenate([s_kb[d][rw, :], s_q[rw, :]], axis=0), s_k[rw, :])
               for (n, d), rw in zip(probs, rows)]
        yield
        a_list = []
        for (n, d), rw, gm in zip(probs, rows, gms):
            base = d * GM_PER_DIR
            gc = s_gc[d][rw, :]
            m_incl = mk_ref[base + GM_INCL]
            dec = jnp.exp((gc - gc.T) * m_incl)
            s_a[d, ch0 + n] = (gm[C:2 * C] * dec * m_incl).astype(BF16)
            a_list.append(-(gm[0:C] * dec * mk_ref[base + GM_STRICT]))
        xs = [mk_ref[GM_EYE] + a * mk_ref[d * GM_PER_DIR + GM_L1] for a, d in zip(a_list, ds)]
        for li in range(GN_LV):
            t1 = [_bdot(x, a * mk_ref[d * GM_PER_DIR + GM_LV + li]) for x, a, d in zip(xs, a_list, ds)]
            yield
            xs = [x + _bdot(t, x) for x, t in zip(xs, t1)]
            yield
        for (n, d), rw, x in zip(probs, rows, xs):
            tkv = _bdot(x, jnp.concatenate([s_kbg[d][rw, :], s_vb[d][rw, :]], axis=1))
            s_tk[d, ch0 + n] = tkv[:, 0:LANES].astype(BF16)
            s_tvb[d, ch0 + n] = tkv[:, LANES:2 * LANES]
        yield

    def p2_stages(steps, row0, ch0):
        for ns0 in steps:
            ns = [ch0 + n for n in ns0]
            rows = [pl.ds(pl.multiple_of(row0 + ns0[d] * C, C), C) for d in range(2)]
            st = [s_state[d] for d in range(2)]
            aq = [_bdot(jnp.concatenate([s_tk[d, ns[d]], s_qg[d][rows[d], :]], axis=0), st[d]) for d in range(2)]
            yield
            vnew = [s_tvb[d, ns[d]] - aq[d][0:C] for d in range(2)]
            oo = [aq[d][C:2 * C] + _bdot(s_a[d, ns[d]], vnew[d]) for d in range(2)]
            for d in range(2):
                s_o[d][rows[d], :] = oo[d]
                s_state[d] = st[d] * s_gl[d, pl.ds(ns[d], 1), :] + _bdot_tn(s_kg[d][rows[d], :], vnew[d])
            yield

    def groups(tseq):
        nch = tseq // C
        per = min(GDN_P1_CHUNKS, nch)
        probs = lambda g: [(g * per + j, 0) for j in range(per)] + [(nch - 1 - (g * per + j), 1) for j in range(per)]
        steps = lambda g: [(g * per + j, nch - 1 - (g * per + j)) for j in range(per)]
        return nch // per, per, probs, steps

    def epilogue_tile(o_ref, z_ref, tseq, it, row0):
        R = min(PREP_ROWS, tseq)
        rows = slice(row0 + it * R, row0 + (it + 1) * R)
        o = s_of[rows, :] + s_ob[rows, :]
        o = o * lax.rsqrt(jnp.mean(o * o, axis=1, keepdims=True) + 1e-6) * gprow(GP_NORM)
        z = z_ref[0, it * R:(it + 1) * R, :]
        o_ref[0, it * R:(it + 1) * R, :] = o * (z * jax.nn.sigmoid(z))

    def epilogue_stages(o_ref, z_ref, tseq, tiles, row0):
        for it in tiles:
            epilogue_tile(o_ref, z_ref, tseq, it, row0)
            yield

    lat, ctx = (lq, lk, lv, lbg), (cq, ck, cv, cbg)
    row_c, ch_c = t_lat, t_lat // C
    s_state[...] = jnp.zeros_like(s_state)
    _interleave(prep_stages(ctx, t_ctx, range(t_ctx // min(PREP_ROWS, t_ctx)), row_c, ch_c))
    ng_c, _, probs_c, steps_c = groups(t_ctx)
    ng, per, probs, steps = groups(t_lat)
    ntile = t_lat // min(PREP_ROWS, t_lat)
    tpg = max(per * C // min(PREP_ROWS, t_lat), 1)
    first = sorted(set(list(range(min(tpg, ntile))) + list(range(max(ntile - tpg, 0), ntile))))
    rest = [t for t in range(ntile) if t not in first]

    def ctx_scan():
        for g in range(ng_c):
            yield from p1_stages(probs_c(g), row_c, ch_c)
            yield from p2_stages(steps_c(g), row_c, ch_c)

    _interleave(prep_stages(lat, t_lat, first, 0, 0), ctx_scan())
    _interleave(prep_stages(lat, t_lat, rest, 0, 0), p1_stages(probs(0), 0, 0),
                epilogue_stages(o_ctx, cz, t_ctx, range(t_ctx // min(PREP_ROWS, t_ctx)), row_c))
    if ng > 1:
        def body(g, carry):
            _interleave(p1_stages(probs(g + 1), 0, 0), p2_stages(steps(g), 0, 0))
            return carry

        lax.fori_loop(0, ng - 1, body, 0)
    done = [t for t in range(ntile) if per * C <= t * min(PREP_ROWS, t_lat) and (t + 1) * min(PREP_ROWS, t_lat) <= (ng - 1) * per * C]
    _interleave(p2_stages(steps(ng - 1), 0, 0), epilogue_stages(o_lat, lz, t_lat, done, 0))
    _interleave(epilogue_stages(o_lat, lz, t_lat, [t for t in range(ntile) if t not in done], 0))


def gdn_mixer(u_lat, bg_lat, u_ctx, bg_ctx, gp):
    bsz, t_lat, _ = u_lat.shape
    t_ctx = u_ctx.shape[1]
    nh = GDN_HEADS
    masks = jnp.asarray(gdn_masks())

    def seq_specs(t):
        return [pl.BlockSpec((1, t, LANES), functools.partial(_col_block_map, off=o * nh)) for o in range(4)] + [
            pl.BlockSpec((1, 1, t, 4), lambda b, h: (b, h, 0, 0))]

    in_specs = seq_specs(t_lat) + seq_specs(t_ctx) + [
        pl.BlockSpec((GP_ROWS, LANES), lambda b, h: (0, h)),
        pl.BlockSpec((GN_MASKS, LANES, LANES), lambda b, h: (0, 0, 0)),
    ]
    out_specs = [pl.BlockSpec((1, t_lat, LANES), lambda b, h: (b, 0, h)),
                 pl.BlockSpec((1, t_ctx, LANES), lambda b, h: (b, 0, h))]
    nch = (t_lat + t_ctx) // GDN_CHUNK
    t_all = t_lat + t_ctx
    scratch = ([pltpu.VMEM((t_all, LANES), BF16) for _ in range(12)] + [pltpu.VMEM((t_all, LANES), F32) for _ in range(4)]
               + [pltpu.VMEM((2, nch, LANES, LANES), BF16), pltpu.VMEM((2, nch, LANES, LANES), F32),
                  pltpu.VMEM((2, nch, LANES, LANES), BF16),
                  pltpu.VMEM((2, max(nch, 8), LANES), F32), pltpu.VMEM((2, LANES, LANES), F32)])
    kern = functools.partial(_gdn_kernel, t_lat=t_lat, t_ctx=t_ctx)
    return pl.pallas_call(
        kern,
        grid=(bsz, nh),
        in_specs=in_specs,
        out_specs=out_specs,
        out_shape=[jax.ShapeDtypeStruct((bsz, t_lat, GDN_WIDTH), F32), jax.ShapeDtypeStruct((bsz, t_ctx, GDN_WIDTH), F32)],
        scratch_shapes=scratch,
        compiler_params=pltpu.CompilerParams(dimension_semantics=("arbitrary", "arbitrary"),
                                             vmem_limit_bytes=V7X_VMEM_LIMIT_BYTES),
        name="gdn_chunked",
    )(*([u_lat] * 4), bg_lat, *([u_ctx] * 4), bg_ctx, gp, masks)


def gdn_params(conv_w, a_log, dt_bias, norm_g):
    wd = GDN_WIDTH

    def scalar_row(x):
        return jnp.zeros((GDN_HEADS, LANES), F32).at[:, 2].set(x[0]).at[:, 3].set(x[1]).reshape(-1)

    rows = [conv_w[j, o * wd:(o + 1) * wd] for o in range(3) for j in range(5)]
    rows += [jnp.tile(norm_g, GDN_HEADS), scalar_row(a_log), scalar_row(dt_bias)]
    return jnp.concatenate([jnp.stack(rows), jnp.zeros((GP_ROWS - len(rows), wd), F32)], 0)


def gdn_bg(u_small):
    bsz, t, _ = u_small.shape
    x = u_small.reshape(bsz, t, 2, 2, GDN_HEADS)
    return jnp.transpose(x, (0, 4, 1, 2, 3)).reshape(bsz, GDN_HEADS, t, 4)


OD_Q_COL, OD_GATE_COL, OD_KV_COL = 0, ATT_WIDTH, 2 * ATT_WIDTH
OD_KV_W = 4 * ATT_HEAD
OD_COLS = OD_KV_COL + ATT_KV_HEADS * OD_KV_W


def rope_lane_tables(t):
    pos = np.arange(t)
    inv = ROPE_BASE ** (-np.arange(ROPE_FREQS, dtype=np.float64) / ROPE_FREQS)
    ang_row = (pos // GRID_W)[:, None] * inv
    ang_col = (pos % GRID_W)[:, None] * inv
    ang = np.concatenate([ang_row, ang_row, ang_col, ang_col] * 2, axis=1)
    first = (np.arange(LANES) % (2 * ROPE_FREQS)) < ROPE_FREQS
    cos, sin = np.cos(ang), np.sin(ang)
    return (jnp.asarray(cos, F32), jnp.asarray(np.where(first, -sin, 0.0), F32), jnp.asarray(np.where(first, 0.0, sin), F32))


def _rope(x, cos, sina, sinb):
    n = x.shape[1]
    return x * cos + pltpu.roll(x, n - ROPE_FREQS, axis=1) * sina + pltpu.roll(x, ROPE_FREQS, axis=1) * sinb


def _kv_rope_kernel(u_ref, cos_ref, sina_ref, sinb_ref, o_ref):
    x = u_ref[0]
    cos, sina, sinb = cos_ref[...], sina_ref[...], sinb_ref[...]
    parts = []
    for h in range(ATT_KV_HEADS):
        k2 = x[:, h * OD_KV_W:h * OD_KV_W + LANES]
        parts += [_rope(k2, cos, sina, sinb), x[:, h * OD_KV_W + LANES:(h + 1) * OD_KV_W]]
    o_ref[0] = jnp.concatenate(parts, axis=1).astype(BF16)


def kv_rope(u, tables, tr=512):
    bsz, t, _ = u.shape
    tr = min(tr, t)
    kvw = ATT_KV_HEADS * OD_KV_W
    tab = pl.BlockSpec((tr, LANES), lambda b, i: (i, 0))
    return pl.pallas_call(
        _kv_rope_kernel,
        grid=(bsz, t // tr),
        in_specs=[pl.BlockSpec((1, tr, kvw), lambda b, i: (b, i, OD_KV_COL // kvw)), tab, tab, tab],
        out_specs=pl.BlockSpec((1, tr, kvw), lambda b, i: (b, i, 0)),
        out_shape=jax.ShapeDtypeStruct((bsz, t, kvw), BF16),
        compiler_params=pltpu.CompilerParams(dimension_semantics=("arbitrary", "arbitrary")),
        name="kv_rope",
    )(u, *tables)


def _attn_kernel(sink_ref, *refs, has_local, nb):
    if has_local:
        q_ref, g_ref, kvp_ref, kvc_ref, kvn_ref, cos_ref, sina_ref, sinb_ref, kvx_ref, o_ref = refs
    else:
        q_ref, g_ref, kvx_ref, o_ref = refs
    tq = q_ref.shape[1]
    i = pl.program_id(1)
    q = q_ref[0]
    if has_local:
        rep = ATT_WIDTH // LANES
        cos, sina, sinb = (jnp.concatenate([r[...]] * rep, axis=1) for r in (cos_ref, sina_ref, sinb_ref))
        q = _rope(q, cos, sina, sinb)
    q = q * (ATT_HEAD ** -0.5 * LOG2E)
    lo = lax.broadcasted_iota(jnp.int32, (tq, LANES), 1) < HALF
    if has_local:
        a = lax.broadcasted_iota(jnp.int32, (tq, WINDOW), 0)
        c = lax.broadcasted_iota(jnp.int32, (tq, WINDOW), 1)
        ok_prev = (c >= a) & (i > 0)
        ok_next = (c <= a) & (i < nb - 1)
    outs = []
    for hk in range(ATT_KV_HEADS):
        kcol = slice(hk * OD_KV_W, hk * OD_KV_W + LANES)
        vcol = slice(hk * OD_KV_W + LANES, (hk + 1) * OD_KV_W)
        qs = []
        for pair in range(ATT_GROUP // 2):
            grp = q[:, (hk * (ATT_GROUP // 2) + pair) * LANES:(hk * (ATT_GROUP // 2) + pair + 1) * LANES]
            qs += [jnp.where(lo, grp, 0.0), jnp.where(lo, 0.0, grp)]
        lhs = jnp.concatenate(qs, axis=0).astype(BF16)
        kx, vx = kvx_ref[0, :, kcol].astype(BF16), kvx_ref[0, :, vcol].astype(BF16)
        if has_local:
            kall = jnp.concatenate([kvp_ref[0, :, kcol], kvc_ref[0, :, kcol], kvn_ref[0, :, kcol], kx], axis=0)
            vall = jnp.concatenate([kvp_ref[0, :, vcol], kvc_ref[0, :, vcol], kvn_ref[0, :, vcol], vx], axis=0)
        else:
            kall, vall = kx, vx
        s = _bdot_nt(lhs, kall)
        lov = lax.broadcasted_iota(jnp.int32, vall.shape, 1) < HALF
        zv = jnp.zeros_like(vall)
        v2 = jnp.concatenate([jnp.where(lov, vall, zv), jnp.where(lov, zv, vall)], axis=0)
        ps, inv_l = [], []
        for g in range(ATT_GROUP):
            sg = s[g * tq:(g + 1) * tq]
            if has_local:
                sg = jnp.concatenate([jnp.where(ok_prev, sg[:, 0:tq], NEG_INF), sg[:, tq:2 * tq],
                                      jnp.where(ok_next, sg[:, 2 * tq:3 * tq], NEG_INF), sg[:, 3 * tq:]], axis=1)
            sink = sink_ref[hk, g] * LOG2E
            m = jnp.maximum(jnp.max(sg, axis=1, keepdims=True), sink)
            p = jnp.exp2(sg - m)
            inv_l.append(1.0 / (jnp.sum(p, axis=1, keepdims=True) + jnp.exp2(sink - m)))
            ps.append(p.astype(BF16))
        pv = jnp.dot(jnp.concatenate([jnp.concatenate(ps[2 * pr:2 * pr + 2], axis=1) for pr in range(ATT_GROUP // 2)],
                                     axis=0), v2, preferred_element_type=F32)
        for pr in range(ATT_GROUP // 2):
            outs.append(pv[pr * tq:(pr + 1) * tq] * jnp.where(lo, inv_l[2 * pr], inv_l[2 * pr + 1]))
    g = g_ref[0]
    o_ref[0] = jnp.concatenate(outs, axis=1) * (g * jax.nn.sigmoid(g))


def sink_attention(u, kv_local, tables, u_ctx, sink):
    bsz, t, _ = u.shape
    tq = ATT_BLOCK
    nb = t // tq
    tx = u_ctx.shape[1]
    kvw = ATT_KV_HEADS * OD_KV_W
    has_local = kv_local is not None
    in_specs = [pl.BlockSpec(memory_space=pltpu.SMEM),
                pl.BlockSpec((1, tq, ATT_WIDTH), lambda b, i: (b, i, OD_Q_COL // ATT_WIDTH)),
                pl.BlockSpec((1, tq, ATT_WIDTH), lambda b, i: (b, i, OD_GATE_COL // ATT_WIDTH))]
    args = [sink.reshape(ATT_KV_HEADS, ATT_GROUP), u, u]
    if has_local:
        in_specs += [pl.BlockSpec((1, tq, kvw), lambda b, i: (b, jnp.maximum(i - 1, 0), 0)),
                     pl.BlockSpec((1, tq, kvw), lambda b, i: (b, i, 0)),
                     pl.BlockSpec((1, tq, kvw), lambda b, i: (b, jnp.minimum(i + 1, nb - 1), 0))]
        in_specs += [pl.BlockSpec((tq, LANES), lambda b, i: (i, 0))] * 3
        args += [kv_local] * 3 + list(tables)
    in_specs += [pl.BlockSpec((1, tx, kvw), lambda b, i: (b, 0, OD_KV_COL // kvw))]
    args += [u_ctx]
    return pl.pallas_call(
        functools.partial(_attn_kernel, has_local=has_local, nb=nb),
        grid=(bsz, nb),
        in_specs=in_specs,
        out_specs=pl.BlockSpec((1, tq, ATT_WIDTH), lambda b, i: (b, i, 0)),
        out_shape=jax.ShapeDtypeStruct((bsz, t, ATT_WIDTH), F32),
        compiler_params=pltpu.CompilerParams(dimension_semantics=("arbitrary", "arbitrary"),
                                             vmem_limit_bytes=V7X_VMEM_LIMIT_BYTES),
        name="sink_attention",
    )(*args)


def pack_odd_w_in(w_in):
    nq, nkv = ATT_WIDTH, ATT_KV_HEADS * ATT_HEAD
    parts = [w_in[:, :nq], w_in[:, nq + 2 * nkv:]]
    for h in range(ATT_KV_HEADS):
        k = w_in[:, nq + h * ATT_HEAD:nq + (h + 1) * ATT_HEAD]
        v = w_in[:, nq + nkv + h * ATT_HEAD:nq + nkv + (h + 1) * ATT_HEAD]
        parts += [k, k, v, v]
    return jnp.concatenate(parts, 1).astype(BF16)


def pack_even_w_in(w_in):
    g1 = 4 * GDN_WIDTH
    r0 = GDN_IN
    parts = [w_in[:, :g1], w_in[:, r0:r0 + 4 * RWKV_WIDTH], w_in[:, r0 + 4 * RWKV_WIDTH:], w_in[:, g1:GDN_IN],
             jnp.zeros((D_MODEL, EV_COLS - EV_SMALL_COL - 4 * GDN_HEADS), F32)]
    return jnp.concatenate(parts, 1).astype(BF16)


def kernel(x, c, ctx, c_ctx, mod_w, mod_b, ln_g, ln_b, ev_w_in, ev_w_out, gdn_conv, gdn_a_log, gdn_dt_bias,
           gdn_norm_g, rwkv_mix, rwkv_w0, rwkv_w_up, rwkv_a0, rwkv_a_up, rwkv_k_k, rwkv_k_a, rwkv_r_k,
           rwkv_gn_g, rwkv_gn_b, od_w_in, od_w_out, od_sink):
    bsz, t_lat, d = x.shape
    t_ctx = ctx.shape[1]
    x_lat, x_ctx = x, ctx
    n_mod = -(-(bsz + 1) // 8) * 8
    cc = jax.nn.silu(jnp.concatenate([c, c_ctx[None], jnp.zeros((n_mod - bsz - 1, d), F32)], 0))
    mods = mod_all_layers(cc, mod_w, mod_b)
    for layer in range(DEPTH):
        need_ctx = layer < DEPTH - 1
        i = layer // 2
        mod = mods[layer]
        sh, sc, gt = (mod[:bsz, j * d:(j + 1) * d][:, None, :] for j in range(3))
        sh_c, sc_c, gt_c = (jnp.broadcast_to(mod[bsz, j * d:(j + 1) * d], (bsz, 1, d)) for j in range(3))
        if layer % 2 == 0:
            w_in = pack_even_w_in(ev_w_in[i])
            w_out = ev_w_out[i].astype(BF16)
            u_lat = inproj(x_lat, sc, sh, w_in, 1024, EV_TN)
            u_ctx = inproj(x_ctx.reshape(1, bsz * t_ctx, d), sc_c[:1], sh_c[:1], w_in, 1024, EV_TN).reshape(bsz, t_ctx, -1)
            small = slice(EV_SMALL_COL, EV_SMALL_COL + 4 * GDN_HEADS)
            og_lat, og_ctx = gdn_mixer(u_lat, gdn_bg(u_lat[..., small]), u_ctx, gdn_bg(u_ctx[..., small]),
                                       gdn_params(gdn_conv[i], gdn_a_log[i], gdn_dt_bias[i], gdn_norm_g[i]))
            or_lat, or_ctx = rwkv_mixer(u_lat, u_ctx, *rwkv_params(
                rwkv_mix[i], rwkv_w0[i], rwkv_w_up[i], rwkv_a0[i], rwkv_a_up[i], rwkv_k_k[i], rwkv_k_a[i],
                rwkv_r_k[i], rwkv_gn_g[i], rwkv_gn_b[i]))
            a_lat, a_ctx = [og_lat, or_lat], [og_ctx, or_ctx]
        else:
            w_in = pack_odd_w_in(od_w_in[i])
            w_out = od_w_out[i].astype(BF16)
            u_lat = inproj(x_lat, sc, sh, w_in, 1024, OD_TN)
            u_ctx = inproj(x_ctx.reshape(1, bsz * t_ctx, d), sc_c[:1], sh_c[:1], w_in, 1024, OD_TN).reshape(bsz, t_ctx, -1)
            tables = rope_lane_tables(t_lat)
            o_lat = sink_attention(u_lat, kv_rope(u_lat, tables), tables, u_ctx, od_sink[i])
            o_ctx = sink_attention(u_ctx, None, None, u_ctx, od_sink[i]) if need_ctx else None
            a_lat, a_ctx = [o_lat], [o_ctx]
        x_lat = outproj_ln(a_lat, w_out, x_lat, gt, ln_g[layer], ln_b[layer])
        if need_ctx:
            x_ctx = outproj_ln(a_ctx, w_out, x_ctx, gt_c, ln_g[layer], ln_b[layer])
    return x_lat
```

```python
import functools

import jax
import jax.numpy as jnp
import numpy as np
from jax import lax
from jax.experimental import pallas as pl
from jax.experimental.pallas import tpu as pltpu

F32 = jnp.float32
BF16 = jnp.bfloat16

D_MODEL = 2048
DEPTH = 4
GRID_W = 64
GDN_WIDTH = D_MODEL // 2
GDN_HEADS = 8
GDN_CHUNK = 128
RWKV_WIDTH = D_MODEL // 2
RWKV_HEADS = 16
RWKV_CHUNK = 64
RWKV_GN_EPS = 64e-5
ATT_HEAD = 64
ATT_Q_HEADS = D_MODEL // ATT_HEAD
ATT_KV_HEADS = ATT_Q_HEADS // 8
ATT_GROUP = ATT_Q_HEADS // ATT_KV_HEADS
ATT_WIDTH = ATT_Q_HEADS * ATT_HEAD
WINDOW = 128
ATT_BLOCK = 128
ROPE_BASE = 10000.0
ROPE_FREQS = ATT_HEAD // 4
NEG_INF = -1e30
LOG2E = 1.4426950408889634
DEEPNORM_ALPHA = (2 * DEPTH) ** 0.25
LN_EPS = 1e-5
GDN_IN = 4 * GDN_WIDTH + 4 * GDN_HEADS

LANES = 128
PREP_ROWS = 256
HALF = 64
V7X_VMEM_LIMIT_BYTES = 56 * 1024 * 1024
EV_GDN_COL = 0
EV_RWKV_COL = 4 * GDN_WIDTH
EV_LORA_COL = EV_RWKV_COL + 4 * RWKV_WIDTH
EV_SMALL_COL = EV_LORA_COL + LANES
EV_COLS = EV_SMALL_COL + LANES
EV_TN = 1408
OD_TN = 1280


def _mod_kernel(a_ref, w_ref, b_ref, o_ref):
    o_ref[0] = jnp.dot(a_ref[...].astype(BF16), w_ref[0].astype(BF16), preferred_element_type=F32) + b_ref[0]


def mod_all_layers(a, w, b, tn=512):
    m, k = a.shape
    nl, _, n = w.shape
    return pl.pallas_call(
        _mod_kernel,
        grid=(nl, n // tn),
        in_specs=[pl.BlockSpec((m, k), lambda l, j: (0, 0)), pl.BlockSpec((1, k, tn), lambda l, j: (l, 0, j)),
                  pl.BlockSpec((1, 1, tn), lambda l, j: (l, 0, j))],
        out_specs=pl.BlockSpec((1, m, tn), lambda l, j: (l, 0, j)),
        out_shape=jax.ShapeDtypeStruct((nl, m, n), F32),
        compiler_params=pltpu.CompilerParams(dimension_semantics=("arbitrary", "arbitrary"),
                                             vmem_limit_bytes=V7X_VMEM_LIMIT_BYTES),
        name="mod_matmul",
    )(a, w, b.reshape(nl, 1, n))


def _inproj_kernel(x_ref, sc_ref, sh_ref, w_ref, o_ref, h_ref):
    @pl.when(pl.program_id(2) == 0)
    def _():
        h_ref[...] = (x_ref[0] * (1.0 + sc_ref[0]) + sh_ref[0]).astype(BF16)

    o_ref[0] = jnp.dot(h_ref[...], w_ref[...], preferred_element_type=F32)


def inproj(x, sc, sh, w, tm, tn):
    bsz, t, d = x.shape
    n = w.shape[1]
    tm = min(tm, t)
    assert t % tm == 0 and n % tn == 0
    return pl.pallas_call(
        _inproj_kernel,
        grid=(bsz, t // tm, n // tn),
        in_specs=[pl.BlockSpec((1, tm, d), lambda b, i, j: (b, i, 0)),
                  pl.BlockSpec((1, 1, d), lambda b, i, j: (b, 0, 0)),
                  pl.BlockSpec((1, 1, d), lambda b, i, j: (b, 0, 0)),
                  pl.BlockSpec((d, tn), lambda b, i, j: (0, j))],
        out_specs=pl.BlockSpec((1, tm, tn), lambda b, i, j: (b, i, j)),
        out_shape=jax.ShapeDtypeStruct((bsz, t, n), F32),
        scratch_shapes=[pltpu.VMEM((tm, d), BF16)],
        compiler_params=pltpu.CompilerParams(dimension_semantics=("arbitrary", "arbitrary", "arbitrary"),
                                             vmem_limit_bytes=V7X_VMEM_LIMIT_BYTES),
        name="inproj",
    )(x, sc, sh, w)


def _outproj_kernel(*refs, n_a):
    a_refs, (w_ref, x_ref, gt_ref, g_ref, b_ref, o_ref) = refs[:n_a], refs[n_a:]
    y = None
    k0 = 0
    for a_ref in a_refs:
        kw = a_ref.shape[2]
        term = jnp.dot(a_ref[0].astype(BF16), w_ref[k0:k0 + kw, :], preferred_element_type=F32)
        y = term if y is None else y + term
        k0 += kw
    z = DEEPNORM_ALPHA * x_ref[0] + gt_ref[0] * y
    mu = jnp.mean(z, axis=1, keepdims=True)
    zc = z - mu
    var = jnp.mean(zc * zc, axis=1, keepdims=True)
    o_ref[0] = zc * lax.rsqrt(var + LN_EPS) * g_ref[...] + b_ref[...]


def outproj_ln(a_list, w, x, gt, ln_g, ln_b, tm=512):
    bsz, t, d = x.shape
    tm = min(tm, t)
    in_specs = [pl.BlockSpec((1, tm, a.shape[2]), lambda b, i: (b, i, 0)) for a in a_list] + [
        pl.BlockSpec(w.shape, lambda b, i: (0, 0)),
        pl.BlockSpec((1, tm, d), lambda b, i: (b, i, 0)),
        pl.BlockSpec((1, 1, d), lambda b, i: (b, 0, 0)),
        pl.BlockSpec((1, d), lambda b, i: (0, 0)),
        pl.BlockSpec((1, d), lambda b, i: (0, 0))]
    return pl.pallas_call(
        functools.partial(_outproj_kernel, n_a=len(a_list)),
        grid=(bsz, t // tm),
        in_specs=in_specs,
        out_specs=pl.BlockSpec((1, tm, d), lambda b, i: (b, i, 0)),
        out_shape=jax.ShapeDtypeStruct((bsz, t, d), F32),
        compiler_params=pltpu.CompilerParams(dimension_semantics=("arbitrary", "arbitrary"),
                                             vmem_limit_bytes=V7X_VMEM_LIMIT_BYTES),
        name="outproj_ln",
    )(*a_list, w, x, gt, ln_g.reshape(1, d), ln_b.reshape(1, d))


def _bdot(a, b):
    return jnp.dot(a.astype(BF16), b.astype(BF16), preferred_element_type=F32)


def _bdot_nt(a, b):
    return lax.dot_general(a.astype(BF16), b.astype(BF16), (((1,), (1,)), ((), ())), preferred_element_type=F32)


def _bdot_tn(a, b):
    return lax.dot_general(a.astype(BF16), b.astype(BF16), (((0,), (0,)), ((), ())), preferred_element_type=F32)


(PV_MR0, PV_MR1, PV_MK0, PV_MK1, PV_MV0, PV_MV1, PV_MG0, PV_MG1, PV_W00, PV_W01, PV_A00, PV_A01,
 PV_KK, PV_KA, PV_RK, PV_GNG, PV_GNB) = range(17)
PV_ROWS = 24
MK_ABD, MK_P, MK_Y, MK_L1 = 0, 1, 2, 3
MK_LV = 4
N_LV = 5
MK_PER_DIR = MK_LV + N_LV
MK_EYE = 2 * MK_PER_DIR
MK_BD = MK_EYE + 1
N_MASKS = MK_BD + 1


def rwkv_masks():
    ri = np.arange(128)[:, None]
    ci = np.arange(128)[None, :]
    i, j = ri & 63, ci & 63
    same = (ri < 64) == (ci < 64)
    out = np.zeros((N_MASKS, 128, 128), np.float32)
    for d in range(2):
        strict = (j < i) if d == 0 else (j > i)
        incl = (j <= i) if d == 0 else (j >= i)
        base = d * MK_PER_DIR
        out[base + MK_ABD] = strict & same
        out[base + MK_P] = strict & (ci >= 64)
        out[base + MK_Y] = incl
        for li, s in enumerate((1, 2, 4, 8, 16, 32)):
            blk = (i // (2 * s)) == (j // (2 * s))
            if d == 0:
                m = blk & ((i & s) != 0) & ((j & s) == 0)
            else:
                m = blk & ((i & s) == 0) & ((j & s) != 0)
            out[base + (MK_L1 if li == 0 else MK_LV + li - 1)] = m & same
    out[MK_EYE] = ri == ci
    out[MK_BD] = same
    return out


def _seg_sum(x, lo):
    s0 = jnp.sum(jnp.where(lo, x, 0.0), axis=1, keepdims=True)
    s1 = jnp.sum(jnp.where(lo, 0.0, x), axis=1, keepdims=True)
    return jnp.where(lo, s0, s1)


def _chunk_cumsum(x, chunk):
    rowm = lax.broadcasted_iota(jnp.int32, x.shape, 0) & (chunk - 1)
    s = 1
    while s < chunk:
        x = x + jnp.where(rowm >= s, pltpu.roll(x, s, axis=0), 0.0)
        s *= 2
    return x


def _interleave(*gens):
    live = list(gens)
    while live:
        for g in list(live):
            try:
                next(g)
            except StopIteration:
                live.remove(g)


def _rwkv_kernel(lr, lk, lv, lg, ll, cr, ck, cv, cg, cl, pv_ref, mixl_ref, wup_ref, aup_ref, mk_ref,
                 o_lat, o_ctx,
                 s_at0, s_at1, s_rt0, s_rt1, s_bt0, s_bt1, s_kt0, s_kt1, s_bh0, s_bh1, s_kh0, s_kh1, s_v,
                 s_g, s_bonus, s_yf, s_yb, s_w, s_tp, s_gr, s_et, s_state, *, t_lat, t_ctx, p1_chunks):
    C = RWKV_CHUNK
    s_at, s_rt, s_bt, s_kt, s_bh, s_kh = ((s_at0, s_at1), (s_rt0, s_rt1), (s_bt0, s_bt1), (s_kt0, s_kt1),
                                          (s_bh0, s_bh1), (s_kh0, s_kh1))
    s_y = (s_yf, s_yb)
    lo64 = lax.broadcasted_iota(jnp.int32, (C, LANES), 1) < HALF
    bd_ones = mk_ref[MK_BD]

    def pvrow(i):
        return pv_ref[pl.ds(i, 1), :]

    def prep_tile(xs, tseq, it, row0, ch0):
        xr, xk, xv, xg, xl = xs
        R = min(PREP_ROWS, tseq)
        s = it * R
        lo = lax.broadcasted_iota(jnp.int32, (R, LANES), 1) < HALF
        rows = lax.broadcasted_iota(jnp.int32, (R, LANES), 0)

        def shifted(ref, m0, m1):
            cur = ref[0, s:s + R, :]
            prev = ref[0, s - 1:s - 1 + R, :] if s > 0 else jnp.where(rows == 0, 0.0, pltpu.roll(cur, 1, axis=0))
            nxt = (ref[0, s + 1:s + 1 + R, :] if s + R < tseq
                   else jnp.where(rows == R - 1, 0.0, pltpu.roll(cur, R - 1, axis=0)))
            return cur + m0 * (prev - cur) + m1 * (nxt - cur)

        r = shifted(xr, pvrow(PV_MR0), pvrow(PV_MR1))
        k = shifted(xk, pvrow(PV_MK0), pvrow(PV_MK1))
        v = shifted(xv, pvrow(PV_MV0), pvrow(PV_MV1))
        g = shifted(xg, pvrow(PV_MG0), pvrow(PV_MG1))
        lor = shifted(xl, mixl_ref[0:1, :], mixl_ref[1:2, :])
        th = jnp.tanh(lor)
        kkf = k * pvrow(PV_KK)
        kk = kkf * lax.rsqrt(_seg_sum(kkf * kkf, lo) + 1e-6)
        ka = pvrow(PV_KA)
        out = slice(row0 + s, row0 + s + R)
        kd_sum = None
        for d in range(2):
            w_pre = pvrow(PV_W00 + d) + _bdot(th, wup_ref[d])
            lw = (-np.exp(-0.5)) * jax.nn.sigmoid(w_pre)
            ag = jax.nn.sigmoid(pvrow(PV_A00 + d) + _bdot(lor, aup_ref[d]))
            kd = k * (1.0 + (ag - 1.0) * ka)
            bdv = kk * ag
            kd_sum = kd if kd_sum is None else kd_sum + kd
            c_fwd = _chunk_cumsum(lw, C)
            tot = jnp.concatenate([jnp.broadcast_to(c_fwd[(c + 1) * C - 1:(c + 1) * C, :], (C, LANES))
                                   for c in range(R // C)], axis=0)
            c_bwd = tot - c_fwd + lw
            cw, rest = (c_fwd, c_bwd - lw) if d == 0 else (c_bwd, c_fwd - lw)
            e_in = jnp.exp(-cw)
            e_out = jnp.exp(rest)
            s_at[d][out, :] = (-kk * jnp.exp(cw - lw)).astype(BF16)
            s_rt[d][out, :] = (r * jnp.exp(cw)).astype(BF16)
            s_bt[d][out, :] = (bdv * e_in).astype(BF16)
            s_kt[d][out, :] = (kd * e_in).astype(BF16)
            s_bh[d][out, :] = (bdv * e_out).astype(BF16)
            s_kh[d][out, :] = (kd * e_out).astype(BF16)
            for c in range(R // C):
                last = c * C + (C - 1 if d == 0 else 0)
                n = ch0 + it * (R // C) + c
                s_et[d, n:n + 1, :] = jnp.exp(cw[last:last + 1, :])
        s_v[out, :] = v.astype(BF16)
        s_g[out, :] = g
        s_bonus[out, :] = _seg_sum(r * kd_sum * pvrow(PV_RK), lo) * v

    def prep_stages(xs, tseq, tiles, row0, ch0):
        for it in tiles:
            prep_tile(xs, tseq, it, row0, ch0)
            yield

    def p1_stages(probs, row0, ch0):
        ds = [d for _, d in probs]
        rows = [pl.ds(pl.multiple_of(row0 + n * C, C), C) for n, _ in probs]
        a2s, gms = [], []
        for (n, d), rw in zip(probs, rows):
            at, rt = s_at[d][rw, :], s_rt[d][rw, :]
            zero = jnp.zeros_like(at)
            a2 = jnp.concatenate([jnp.where(lo64, at, zero), jnp.where(lo64, zero, at)], axis=0)
            l1 = jnp.concatenate([a2, jnp.where(lo64, rt, zero), jnp.where(lo64, zero, rt)], axis=0)
            a2s.append(a2)
            gms.append(_bdot_nt(l1, jnp.concatenate([s_bt[d][rw, :], s_kt[d][rw, :]], axis=0)))
        yield
        ps, a_list = [], []
        for (n, d), rw, gm in zip(probs, rows, gms):
            base = d * MK_PER_DIR
            ga = gm[0:2 * C]
            v = s_v[rw, :]
            ps.append(_bdot(ga * mk_ref[base + MK_P], jnp.concatenate([v, v], axis=0)))
            s_gr[d, ch0 + n] = (gm[2 * C:4 * C] * mk_ref[base + MK_Y]).astype(BF16)
            ga_sw = jnp.concatenate([ga[0:C], pltpu.roll(ga[C:2 * C], HALF, axis=1)], axis=0)
            a_list.append(ga_sw * mk_ref[base + MK_ABD])
        yield
        xs = [mk_ref[MK_EYE] + a * mk_ref[d * MK_PER_DIR + MK_L1] for a, d in zip(a_list, ds)]
        for li in range(N_LV):
            t1 = [_bdot(x, a * mk_ref[d * MK_PER_DIR + MK_LV + li]) for x, a, d in zip(xs, a_list, ds)]
            yield
            xs = [x + _bdot(t, x) for x, t in zip(xs, t1)]
            yield
        for (n, d), x, a2, p in zip(probs, xs, a2s, ps):
            wtp = _bdot(x, jnp.concatenate([a2.astype(F32), p], axis=1))
            s_w[d, ch0 + n] = wtp[:, 0:LANES].astype(BF16)
            s_tp[d, ch0 + n] = jnp.where(lo64, wtp[0:C, LANES:2 * LANES], wtp[C:2 * C, LANES:2 * LANES])
        yield

    def p2_stages(steps, row0, ch0):
        for ns0 in steps:
            ns = [ch0 + n for n in ns0]
            rows = [pl.ds(pl.multiple_of(row0 + ns0[d] * C, C), C) for d in range(2)]
            st = [s_state[d] for d in range(2)]
            xs = [_bdot_nt(jnp.concatenate([s_w[d, ns[d]], s_rt[d][rows[d], :]], axis=0), st[d]) for d in range(2)]
            yield
            uv = [jnp.concatenate([(xs[d][0:C] + xs[d][C:2 * C] + s_tp[d, ns[d]]).astype(BF16), s_v[rows[d], :]], axis=0)
                  for d in range(2)]
            yb = [_bdot(s_gr[d, ns[d]], uv[d]) for d in range(2)]
            for d in range(2):
                s_y[d][rows[d], :] = xs[d][2 * C:3 * C] + jnp.where(lo64, yb[d][0:C], yb[d][C:2 * C])
                bk = jnp.concatenate([s_bh[d][rows[d], :], s_kh[d][rows[d], :]], axis=0)
                s_state[d] = st[d] * s_et[d, pl.ds(ns[d], 1), :] + bd_ones * _bdot_tn(uv[d], bk)
            yield

    def groups(tseq):
        nch = tseq // C
        per = min(p1_chunks, nch)
        probs = lambda g: [(g * per + j, 0) for j in range(per)] + [(nch - 1 - (g * per + j), 1) for j in range(per)]
        steps = lambda g: [(g * per + j, nch - 1 - (g * per + j)) for j in range(per)]
        return nch // per, per, probs, steps

    def epilogue_tile(o_ref, tseq, it, row0):
        R = min(PREP_ROWS, tseq)
        lo = lax.broadcasted_iota(jnp.int32, (R, LANES), 1) < HALF
        rows = slice(row0 + it * R, row0 + (it + 1) * R)
        y = s_yf[rows, :] + s_yb[rows, :]
        mu = _seg_sum(y, lo) * (1.0 / HALF)
        yc = y - mu
        var = _seg_sum(yc * yc, lo) * (1.0 / HALF)
        yn = yc * lax.rsqrt(var + RWKV_GN_EPS) * pvrow(PV_GNG) + pvrow(PV_GNB)
        g = s_g[rows, :]
        o_ref[0, it * R:(it + 1) * R, :] = (yn + s_bonus[rows, :]) * (g * jax.nn.sigmoid(g))

    def epilogue_stages(o_ref, tseq, tiles, row0):
        for it in tiles:
            epilogue_tile(o_ref, tseq, it, row0)
            yield

    lat, ctx = (lr, lk, lv, lg, ll), (cr, ck, cv, cg, cl)
    row_c, ch_c = t_lat, t_lat // C
    s_state[...] = jnp.zeros_like(s_state)
    _interleave(prep_stages(ctx, t_ctx, range(t_ctx // min(PREP_ROWS, t_ctx)), row_c, ch_c))
    ng_c, _, probs_c, steps_c = groups(t_ctx)
    ng, per, probs, steps = groups(t_lat)
    ntile = t_lat // min(PREP_ROWS, t_lat)
    tpg = max(per * C // min(PREP_ROWS, t_lat), 1)
    first = sorted(set(list(range(min(tpg, ntile))) + list(range(max(ntile - tpg, 0), ntile))))
    rest = [t for t in range(ntile) if t not in first]

    def ctx_scan():
        for g in range(ng_c):
            yield from p1_stages(probs_c(g), row_c, ch_c)
            yield from p2_stages(steps_c(g), row_c, ch_c)

    _interleave(prep_stages(lat, t_lat, first, 0, 0), ctx_scan())
    _interleave(prep_stages(lat, t_lat, rest, 0, 0), p1_stages(probs(0), 0, 0),
                epilogue_stages(o_ctx, t_ctx, range(t_ctx // min(PREP_ROWS, t_ctx)), row_c))
    for g in range(ng - 1):
        _interleave(p1_stages(probs(g + 1), 0, 0), p2_stages(steps(g), 0, 0))
    done = [t for t in range(ntile) if per * C <= t * min(PREP_ROWS, t_lat) and (t + 1) * min(PREP_ROWS, t_lat) <= (ng - 1) * per * C]
    _interleave(p2_stages(steps(ng - 1), 0, 0), epilogue_stages(o_lat, t_lat, done, 0))
    _interleave(epilogue_stages(o_lat, t_lat, [t for t in range(ntile) if t not in done], 0))


def _col_block_map(b, p, *, off):
    return (b, 0, off + p)


def rwkv_mixer(u_lat, u_ctx, pv, mixl, wup, aup, p1_chunks=8):
    bsz, t_lat, _ = u_lat.shape
    t_ctx = u_ctx.shape[1]
    npair = RWKV_HEADS // 2
    c0 = EV_RWKV_COL // LANES
    nch = (t_lat + t_ctx) // RWKV_CHUNK
    t_all = t_lat + t_ctx
    masks = jnp.asarray(rwkv_masks())

    def seq_specs(t):
        return [pl.BlockSpec((1, t, LANES), functools.partial(_col_block_map, off=c0 + o * npair)) for o in range(4)] + [
            pl.BlockSpec((1, t, LANES), lambda b, p: (b, 0, EV_LORA_COL // LANES))]

    in_specs = seq_specs(t_lat) + seq_specs(t_ctx) + [
        pl.BlockSpec((PV_ROWS, LANES), lambda b, p: (0, p)),
        pl.BlockSpec((8, LANES), lambda b, p: (0, 0)),
        pl.BlockSpec((2, LANES, LANES), lambda b, p: (0, 0, p)),
        pl.BlockSpec((2, LANES, LANES), lambda b, p: (0, 0, p)),
        pl.BlockSpec((N_MASKS, LANES, LANES), lambda b, p: (0, 0, 0)),
    ]
    out_specs = [pl.BlockSpec((1, t_lat, LANES), lambda b, p: (b, 0, p)),
                 pl.BlockSpec((1, t_ctx, LANES), lambda b, p: (b, 0, p))]
    scratch = ([pltpu.VMEM((t_all, LANES), BF16) for _ in range(13)] + [pltpu.VMEM((t_all, LANES), F32) for _ in range(4)]
               + [pltpu.VMEM((2, nch, LANES, LANES), BF16), pltpu.VMEM((2, nch, RWKV_CHUNK, LANES), F32),
                  pltpu.VMEM((2, nch, LANES, LANES), BF16)]
               + [pltpu.VMEM((2, nch, LANES), F32), pltpu.VMEM((2, LANES, LANES), F32)])
    kern = functools.partial(_rwkv_kernel, t_lat=t_lat, t_ctx=t_ctx, p1_chunks=p1_chunks)
    return pl.pallas_call(
        kern,
        grid=(bsz, npair),
        in_specs=in_specs,
        out_specs=out_specs,
        out_shape=[jax.ShapeDtypeStruct((bsz, t_lat, RWKV_WIDTH), F32), jax.ShapeDtypeStruct((bsz, t_ctx, RWKV_WIDTH), F32)],
        scratch_shapes=scratch,
        compiler_params=pltpu.CompilerParams(dimension_semantics=("arbitrary", "arbitrary"),
                                             vmem_limit_bytes=V7X_VMEM_LIMIT_BYTES),
        name="rwkv7_chunked",
    )(*([u_lat] * 5), *([u_ctx] * 5), pv, mixl, wup, aup, masks)


def rwkv_params(mix, w0, w_up, a0, a_up, k_k, k_a, r_k, gn_g, gn_b):
    wd = RWKV_WIDTH
    rows = [mix[0, 0:wd], mix[1, 0:wd], mix[0, wd:2 * wd], mix[1, wd:2 * wd], mix[0, 2 * wd:3 * wd], mix[1, 2 * wd:3 * wd],
            mix[0, 3 * wd:4 * wd], mix[1, 3 * wd:4 * wd], w0[0], w0[1], a0[0], a0[1], k_k, k_a, r_k.reshape(-1), gn_g, gn_b]
    pv = jnp.concatenate([jnp.stack(rows), jnp.zeros((PV_ROWS - len(rows), wd), F32)], 0)
    mixl = jnp.concatenate([mix[:, 4 * wd:], jnp.zeros((6, LANES), F32)], 0)
    z = jnp.zeros((2, HALF, wd), F32)
    wup = jnp.concatenate([w_up, z], 1)
    aup = jnp.concatenate([z, a_up], 1)
    return pv, mixl, wup, aup


GP_CONV_Q, GP_CONV_K, GP_CONV_V = 0, 5, 10
GP_NORM, GP_SC_ALOG, GP_SC_DTB = 15, 16, 17
GP_ROWS = 24
GM_STRICT, GM_INCL, GM_L1, GM_LV = 0, 1, 2, 3
GN_LV = 6
GM_PER_DIR = GM_LV + GN_LV
GM_EYE = 2 * GM_PER_DIR
GN_MASKS = GM_EYE + 1
GDN_P1_CHUNKS = 4


def gdn_masks():
    i = np.arange(GDN_CHUNK)[:, None]
    j = np.arange(GDN_CHUNK)[None, :]
    out = np.zeros((GN_MASKS, GDN_CHUNK, GDN_CHUNK), np.float32)
    for d in range(2):
        base = d * GM_PER_DIR
        out[base + GM_STRICT] = (j < i) if d == 0 else (j > i)
        out[base + GM_INCL] = (j <= i) if d == 0 else (j >= i)
        for li in range(GN_LV + 1):
            s = 1 << li
            blk = (i // (2 * s)) == (j // (2 * s))
            m = blk & (((i & s) != 0) & ((j & s) == 0) if d == 0 else ((i & s) == 0) & ((j & s) != 0))
            out[base + (GM_L1 if li == 0 else GM_LV + li - 1)] = m
    out[GM_EYE] = i == j
    return out


def _gdn_kernel(lq, lk, lv, lz, lbg, cq, ck, cv, cz, cbg, gp_ref, mk_ref, o_lat, o_ctx,
                s_q, s_k, s_kb0, s_kb1, s_kbg0, s_kbg1, s_qg0, s_qg1, s_kg0, s_kg1, s_vb0, s_vb1,
                s_gc0, s_gc1, s_of, s_ob, s_tk, s_tvb, s_a, s_gl, s_state, *, t_lat, t_ctx):
    C = GDN_CHUNK
    s_kb, s_kbg, s_qg, s_kg, s_vb = (s_kb0, s_kb1), (s_kbg0, s_kbg1), (s_qg0, s_qg1), (s_kg0, s_kg1), (s_vb0, s_vb1)
    s_gc, s_o = (s_gc0, s_gc1), (s_of, s_ob)

    def gprow(i):
        return gp_ref[pl.ds(i, 1), :]

    def prep_tile(xs, tseq, it, row0, ch0):
        xq, xk, xv, xbg = xs
        R = min(PREP_ROWS, tseq)
        s = it * R

        def conv_silu(ref, prow0):
            rows = lax.broadcasted_iota(jnp.int32, (R, LANES), 0)
            acc = None
            for j in range(5):
                off = j - 2
                lo_r, hi_r = s + off, s + off + R
                if lo_r < 0:
                    x = jnp.where(rows >= -off, pltpu.roll(ref[0, 0:R, :], -off, axis=0), 0.0)
                elif hi_r > tseq:
                    x = jnp.where(rows < R - off, pltpu.roll(ref[0, tseq - R:tseq, :], R - off, axis=0), 0.0)
                else:
                    x = ref[0, lo_r:hi_r, :]
                term = gprow(prow0 + j) * x
                acc = term if acc is None else acc + term
            return acc * jax.nn.sigmoid(acc)

        qc = conv_silu(xq, GP_CONV_Q)
        kc = conv_silu(xk, GP_CONV_K)
        v = conv_silu(xv, GP_CONV_V)
        q = qc * (lax.rsqrt(jnp.sum(qc * qc, axis=1, keepdims=True) + 1e-6) * (LANES ** -0.5))
        k = kc * lax.rsqrt(jnp.sum(kc * kc, axis=1, keepdims=True) + 1e-6)
        bg = xbg[0, 0, s:s + R, :]
        lane4 = lax.broadcasted_iota(jnp.int32, bg.shape, 1)
        beta4 = jax.nn.sigmoid(bg)
        g4 = -jnp.exp(gprow(GP_SC_ALOG)[:, 0:4]) * jax.nn.softplus(bg + gprow(GP_SC_DTB)[:, 0:4])
        c_fwd = _chunk_cumsum(g4, C)
        tot = jnp.concatenate([jnp.broadcast_to(c_fwd[(c + 1) * C - 1:(c + 1) * C, :], (C, 4)) for c in range(R // C)],
                              axis=0)
        c_bwd = tot - c_fwd + g4
        fwd_lane = lane4 == 2
        gc4 = jnp.where(fwd_lane, c_fwd, c_bwd)
        eg4 = jnp.exp(gc4)
        er4 = jnp.exp(jnp.where(fwd_lane, c_bwd, c_fwd) - g4)
        out = slice(row0 + s, row0 + s + R)
        s_q[out, :] = q.astype(BF16)
        s_k[out, :] = k.astype(BF16)
        for d in range(2):
            beta, eg = beta4[:, d:d + 1], eg4[:, 2 + d:3 + d]
            kb = k * beta
            s_kb[d][out, :] = kb.astype(BF16)
            s_kbg[d][out, :] = (kb * eg).astype(BF16)
            s_qg[d][out, :] = (q * eg).astype(BF16)
            s_kg[d][out, :] = (k * er4[:, 2 + d:3 + d]).astype(BF16)
            s_vb[d][out, :] = (v * beta).astype(BF16)
            s_gc[d][out, :] = jnp.broadcast_to(gc4[:, 2 + d:3 + d], (R, LANES))
            for c in range(R // C):
                last = c * C + (C - 1 if d == 0 else 0)
                n = ch0 + it * (R // C) + c
                s_gl[d, n:n + 1, :] = jnp.broadcast_to(eg[last:last + 1, :], (1, LANES))

    def prep_stages(xs, tseq, tiles, row0, ch0):
        for it in tiles:
            prep_tile(xs, tseq, it, row0, ch0)
            yield

    def p1_stages(probs, row0, ch0):
        ds = [d for _, d in probs]
        rows = [pl.ds(pl.multiple_of(row0 + n * C, C), C) for n, _ in probs]
        gms = [_bdot_nt(jnp.concatenate([s_kb[d][rw, :], s_q[rw, :]], axis=0), s_k[rw, :])
               for (n, d), rw in zip(probs, rows)]
        yield
        a_list = []
        for (n, d), rw, gm in zip(probs, rows, gms):
            base = d * GM_PER_DIR
            gc = s_gc[d][rw, :]
            m_incl = mk_ref[base + GM_INCL]
            dec = jnp.exp((gc - gc.T) * m_incl)
            s_a[d, ch0 + n] = (gm[C:2 * C] * dec * m_incl).astype(BF16)
            a_list.append(-(gm[0:C] * dec * mk_ref[base + GM_STRICT]))
        xs = [mk_ref[GM_EYE] + a * mk_ref[d * GM_PER_DIR + GM_L1] for a, d in zip(a_list, ds)]
        for li in range(GN_LV):
            t1 = [_bdot(x, a * mk_ref[d * GM_PER_DIR + GM_LV + li]) for x, a, d in zip(xs, a_list, ds)]
            yield
            xs = [x + _bdot(t, x) for x, t in zip(xs, t1)]
            yield
        for (n, d), rw, x in zip(probs, rows, xs):
            tkv = _bdot(x, jnp.concatenate([s_kbg[d][rw, :], s_vb[d][rw, :]], axis=1))
            s_tk[d, ch0 + n] = tkv[:, 0:LANES].astype(BF16)
            s_tvb[d, ch0 + n] = tkv[:, LANES:2 * LANES]
        yield

    def p2_stages(steps, row0, ch0):
        for ns0 in steps:
            ns = [ch0 + n for n in ns0]
            rows = [pl.ds(pl.multiple_of(row0 + ns0[d] * C, C), C) for d in range(2)]
            st = [s_state[d] for d in range(2)]
            aq = [_bdot(jnp.concatenate([s_tk[d, ns[d]], s_qg[d][rows[d], :]], axis=0), st[d]) for d in range(2)]
            yield
            vnew = [s_tvb[d, ns[d]] - aq[d][0:C] for d in range(2)]
            oo = [aq[d][C:2 * C] + _bdot(s_a[d, ns[d]], vnew[d]) for d in range(2)]
            for d in range(2):
                s_o[d][rows[d], :] = oo[d]
                s_state[d] = st[d] * s_gl[d, pl.ds(ns[d], 1), :] + _bdot_tn(s_kg[d][rows[d], :], vnew[d])
            yield

    def groups(tseq):
        nch = tseq // C
        per = min(GDN_P1_CHUNKS, nch)
        probs = lambda g: [(g * per + j, 0) for j in range(per)] + [(nch - 1 - (g * per + j), 1) for j in range(per)]
        steps = lambda g: [(g * per + j, nch - 1 - (g * per + j)) for j in range(per)]
        return nch // per, per, probs, steps

    def epilogue_tile(o_ref, z_ref, tseq, it, row0):
        R = min(PREP_ROWS, tseq)
        rows = slice(row0 + it * R, row0 + (it + 1) * R)
        o = s_of[rows, :] + s_ob[rows, :]
        o = o * lax.rsqrt(jnp.mean(o * o, axis=1, keepdims=True) + 1e-6) * gprow(GP_NORM)
        z = z_ref[0, it * R:(it + 1) * R, :]
        o_ref[0, it * R:(it + 1) * R, :] = o * (z * jax.nn.sigmoid(z))

    def epilogue_stages(o_ref, z_ref, tseq, tiles, row0):
        for it in tiles:
            epilogue_tile(o_ref, z_ref, tseq, it, row0)
            yield

    lat, ctx = (lq, lk, lv, lbg), (cq, ck, cv, cbg)
    row_c, ch_c = t_lat, t_lat // C
    s_state[...] = jnp.zeros_like(s_state)
    _interleave(prep_stages(ctx, t_ctx, range(t_ctx // min(PREP_ROWS, t_ctx)), row_c, ch_c))
    ng_c, _, probs_c, steps_c = groups(t_ctx)
    ng, per, probs, steps = groups(t_lat)
    ntile = t_lat // min(PREP_ROWS, t_lat)
    tpg = max(per * C // min(PREP_ROWS, t_lat), 1)
    first = sorted(set(list(range(min(tpg, ntile))) + list(range(max(ntile - tpg, 0), ntile))))
    rest = [t for t in range(ntile) if t not in first]

    def ctx_scan():
        for g in range(ng_c):
            yield from p1_stages(probs_c(g), row_c, ch_c)
            yield from p2_stages(steps_c(g), row_c, ch_c)

    _interleave(prep_stages(lat, t_lat, first, 0, 0), ctx_scan())
    _interleave(prep_stages(lat, t_lat, rest, 0, 0), p1_stages(probs(0), 0, 0),
                epilogue_stages(o_ctx, cz, t_ctx, range(t_ctx // min(PREP_ROWS, t_ctx)), row_c))
    if ng > 1:
        def body(g, carry):
            _interleave(p1_stages(probs(g + 1), 0, 0), p2_stages(steps(g), 0, 0))
            return carry

        lax.fori_loop(0, ng - 1, body, 0)
    done = [t for t in range(ntile) if per * C <= t * min(PREP_ROWS, t_lat) and (t + 1) * min(PREP_ROWS, t_lat) <= (ng - 1) * per * C]
    _interleave(p2_stages(steps(ng - 1), 0, 0), epilogue_stages(o_lat, lz, t_lat, done, 0))
    _interleave(epilogue_stages(o_lat, lz, t_lat, [t for t in range(ntile) if t not in done], 0))


def gdn_mixer(u_lat, bg_lat, u_ctx, bg_ctx, gp):
    bsz, t_lat, _ = u_lat.shape
    t_ctx = u_ctx.shape[1]
    nh = GDN_HEADS
    masks = jnp.asarray(gdn_masks())

    def seq_specs(t):
        return [pl.BlockSpec((1, t, LANES), functools.partial(_col_block_map, off=o * nh)) for o in range(4)] + [
            pl.BlockSpec((1, 1, t, 4), lambda b, h: (b, h, 0, 0))]

    in_specs = seq_specs(t_lat) + seq_specs(t_ctx) + [
        pl.BlockSpec((GP_ROWS, LANES), lambda b, h: (0, h)),
        pl.BlockSpec((GN_MASKS, LANES, LANES), lambda b, h: (0, 0, 0)),
    ]
    out_specs = [pl.BlockSpec((1, t_lat, LANES), lambda b, h: (b, 0, h)),
                 pl.BlockSpec((1, t_ctx, LANES), lambda b, h: (b, 0, h))]
    nch = (t_lat + t_ctx) // GDN_CHUNK
    t_all = t_lat + t_ctx
    scratch = ([pltpu.VMEM((t_all, LANES), BF16) for _ in range(12)] + [pltpu.VMEM((t_all, LANES), F32) for _ in range(4)]
               + [pltpu.VMEM((2, nch, LANES, LANES), BF16), pltpu.VMEM((2, nch, LANES, LANES), F32),
                  pltpu.VMEM((2, nch, LANES, LANES), BF16),
                  pltpu.VMEM((2, max(nch, 8), LANES), F32), pltpu.VMEM((2, LANES, LANES), F32)])
    kern = functools.partial(_gdn_kernel, t_lat=t_lat, t_ctx=t_ctx)
    return pl.pallas_call(
        kern,
        grid=(bsz, nh),
        in_specs=in_specs,
        out_specs=out_specs,
        out_shape=[jax.ShapeDtypeStruct((bsz, t_lat, GDN_WIDTH), F32), jax.ShapeDtypeStruct((bsz, t_ctx, GDN_WIDTH), F32)],
        scratch_shapes=scratch,
        compiler_params=pltpu.CompilerParams(dimension_semantics=("arbitrary", "arbitrary"),
                                             vmem_limit_bytes=V7X_VMEM_LIMIT_BYTES),
        name="gdn_chunked",
    )(*([u_lat] * 4), bg_lat, *([u_ctx] * 4), bg_ctx, gp, masks)


def gdn_params(conv_w, a_log, dt_bias, norm_g):
    wd = GDN_WIDTH

    def scalar_row(x):
        return jnp.zeros((GDN_HEADS, LANES), F32).at[:, 2].set(x[0]).at[:, 3].set(x[1]).reshape(-1)

    rows = [conv_w[j, o * wd:(o + 1) * wd] for o in range(3) for j in range(5)]
    rows += [jnp.tile(norm_g, GDN_HEADS), scalar_row(a_log), scalar_row(dt_bias)]
    return jnp.concatenate([jnp.stack(rows), jnp.zeros((GP_ROWS - len(rows), wd), F32)], 0)


def gdn_bg(u_small):
    bsz, t, _ = u_small.shape
    x = u_small.reshape(bsz, t, 2, 2, GDN_HEADS)
    return jnp.transpose(x, (0, 4, 1, 2, 3)).reshape(bsz, GDN_HEADS, t, 4)


OD_Q_COL, OD_GATE_COL, OD_KV_COL = 0, ATT_WIDTH, 2 * ATT_WIDTH
OD_KV_W = 4 * ATT_HEAD
OD_COLS = OD_KV_COL + ATT_KV_HEADS * OD_KV_W


def rope_lane_tables(t):
    pos = np.arange(t)
    inv = ROPE_BASE ** (-np.arange(ROPE_FREQS, dtype=np.float64) / ROPE_FREQS)
    ang_row = (pos // GRID_W)[:, None] * inv
    ang_col = (pos % GRID_W)[:, None] * inv
    ang = np.concatenate([ang_row, ang_row, ang_col, ang_col] * 2, axis=1)
    first = (np.arange(LANES) % (2 * ROPE_FREQS)) < ROPE_FREQS
    cos, sin = np.cos(ang), np.sin(ang)
    return (jnp.asarray(cos, F32), jnp.asarray(np.where(first, -sin, 0.0), F32), jnp.asarray(np.where(first, 0.0, sin), F32))


def _rope(x, cos, sina, sinb):
    n = x.shape[1]
    return x * cos + pltpu.roll(x, n - ROPE_FREQS, axis=1) * sina + pltpu.roll(x, ROPE_FREQS, axis=1) * sinb


def _kv_rope_kernel(u_ref, cos_ref, sina_ref, sinb_ref, o_ref):
    x = u_ref[0]
    cos, sina, sinb = cos_ref[...], sina_ref[...], sinb_ref[...]
    parts = []
    for h in range(ATT_KV_HEADS):
        k2 = x[:, h * OD_KV_W:h * OD_KV_W + LANES]
        parts += [_rope(k2, cos, sina, sinb), x[:, h * OD_KV_W + LANES:(h + 1) * OD_KV_W]]
    o_ref[0] = jnp.concatenate(parts, axis=1).astype(BF16)


def kv_rope(u, tables, tr=512):
    bsz, t, _ = u.shape
    tr = min(tr, t)
    kvw = ATT_KV_HEADS * OD_KV_W
    tab = pl.BlockSpec((tr, LANES), lambda b, i: (i, 0))
    return pl.pallas_call(
        _kv_rope_kernel,
        grid=(bsz, t // tr),
        in_specs=[pl.BlockSpec((1, tr, kvw), lambda b, i: (b, i, OD_KV_COL // kvw)), tab, tab, tab],
        out_specs=pl.BlockSpec((1, tr, kvw), lambda b, i: (b, i, 0)),
        out_shape=jax.ShapeDtypeStruct((bsz, t, kvw), BF16),
        compiler_params=pltpu.CompilerParams(dimension_semantics=("arbitrary", "arbitrary")),
        name="kv_rope",
    )(u, *tables)


def _attn_kernel(sink_ref, *refs, has_local, nb):
    if has_local:
        q_ref, g_ref, kvp_ref, kvc_ref, kvn_ref, cos_ref, sina_ref, sinb_ref, kvx_ref, o_ref = refs
    else:
        q_ref, g_ref, kvx_ref, o_ref = refs
    tq = q_ref.shape[1]
    i = pl.program_id(1)
    q = q_ref[0]
    if has_local:
        rep = ATT_WIDTH // LANES
        cos, sina, sinb = (jnp.concatenate([r[...]] * rep, axis=1) for r in (cos_ref, sina_ref, sinb_ref))
        q = _rope(q, cos, sina, sinb)
    q = q * (ATT_HEAD ** -0.5 * LOG2E)
    lo = lax.broadcasted_iota(jnp.int32, (tq, LANES), 1) < HALF
    if has_local:
        a = lax.broadcasted_iota(jnp.int32, (tq, WINDOW), 0)
        c = lax.broadcasted_iota(jnp.int32, (tq, WINDOW), 1)
        ok_prev = (c >= a) & (i > 0)
        ok_next = (c <= a) & (i < nb - 1)
    def scores(hk):
        kcol = slice(hk * OD_KV_W, hk * OD_KV_W + LANES)
        vcol = slice(hk * OD_KV_W + LANES, (hk + 1) * OD_KV_W)
        qs = []
        for pair in range(ATT_GROUP // 2):
            grp = q[:, (hk * (ATT_GROUP // 2) + pair) * LANES:(hk * (ATT_GROUP // 2) + pair + 1) * LANES]
            qs += [jnp.where(lo, grp, 0.0), jnp.where(lo, 0.0, grp)]
        lhs = jnp.concatenate(qs, axis=0).astype(BF16)
        kx, vx = kvx_ref[0, :, kcol].astype(BF16), kvx_ref[0, :, vcol].astype(BF16)
        if has_local:
            kall = jnp.concatenate([kvp_ref[0, :, kcol], kvc_ref[0, :, kcol], kvn_ref[0, :, kcol], kx], axis=0)
            vall = jnp.concatenate([kvp_ref[0, :, vcol], kvc_ref[0, :, vcol], kvn_ref[0, :, vcol], vx], axis=0)
        else:
            kall, vall = kx, vx
        return _bdot_nt(lhs, kall), vall

    outs = []
    nxt = scores(0)
    for hk in range(ATT_KV_HEADS):
        s, vall = nxt
        if hk + 1 < ATT_KV_HEADS:
            nxt = scores(hk + 1)
        lov = lax.broadcasted_iota(jnp.int32, vall.shape, 1) < HALF
        zv = jnp.zeros_like(vall)
        v2 = jnp.concatenate([jnp.where(lov, vall, zv), jnp.where(lov, zv, vall)], axis=0)
        ps, inv_l = [], []
        for g in range(ATT_GROUP):
            sg = s[g * tq:(g + 1) * tq]
            if has_local:
                sg = jnp.concatenate([jnp.where(ok_prev, sg[:, 0:tq], NEG_INF), sg[:, tq:2 * tq],
                                      jnp.where(ok_next, sg[:, 2 * tq:3 * tq], NEG_INF), sg[:, 3 * tq:]], axis=1)
            sink = sink_ref[hk, g] * LOG2E
            m = jnp.maximum(jnp.max(sg, axis=1, keepdims=True), sink)
            p = jnp.exp2(sg - m)
            inv_l.append(1.0 / (jnp.sum(p, axis=1, keepdims=True) + jnp.exp2(sink - m)))
            ps.append(p.astype(BF16))
        pv = jnp.dot(jnp.concatenate([jnp.concatenate(ps[2 * pr:2 * pr + 2], axis=1) for pr in range(ATT_GROUP // 2)],
                                     axis=0), v2, preferred_element_type=F32)
        for pr in range(ATT_GROUP // 2):
            outs.append(pv[pr * tq:(pr + 1) * tq] * jnp.where(lo, inv_l[2 * pr], inv_l[2 * pr + 1]))
    g = g_ref[0]
    o_ref[0] = jnp.concatenate(outs, axis=1) * (g * jax.nn.sigmoid(g))


def sink_attention(u, kv_local, tables, u_ctx, sink):
    bsz, t, _ = u.shape
    tq = ATT_BLOCK
    nb = t // tq
    tx = u_ctx.shape[1]
    kvw = ATT_KV_HEADS * OD_KV_W
    has_local = kv_local is not None
    in_specs = [pl.BlockSpec(memory_space=pltpu.SMEM),
                pl.BlockSpec((1, tq, ATT_WIDTH), lambda b, i: (b, i, OD_Q_COL // ATT_WIDTH)),
                pl.BlockSpec((1, tq, ATT_WIDTH), lambda b, i: (b, i, OD_GATE_COL // ATT_WIDTH))]
    args = [sink.reshape(ATT_KV_HEADS, ATT_GROUP), u, u]
    if has_local:
        in_specs += [pl.BlockSpec((1, tq, kvw), lambda b, i: (b, jnp.maximum(i - 1, 0), 0)),
                     pl.BlockSpec((1, tq, kvw), lambda b, i: (b, i, 0)),
                     pl.BlockSpec((1, tq, kvw), lambda b, i: (b, jnp.minimum(i + 1, nb - 1), 0))]
        in_specs += [pl.BlockSpec((tq, LANES), lambda b, i: (i, 0))] * 3
        args += [kv_local] * 3 + list(tables)
    in_specs += [pl.BlockSpec((1, tx, kvw), lambda b, i: (b, 0, OD_KV_COL // kvw))]
    args += [u_ctx]
    return pl.pallas_call(
        functools.partial(_attn_kernel, has_local=has_local, nb=nb),
        grid=(bsz, nb),
        in_specs=in_specs,
        out_specs=pl.BlockSpec((1, tq, ATT_WIDTH), lambda b, i: (b, i, 0)),
        out_shape=jax.ShapeDtypeStruct((bsz, t, ATT_WIDTH), F32),
        compiler_params=pltpu.CompilerParams(dimension_semantics=("arbitrary", "arbitrary"),
                                             vmem_limit_bytes=V7X_VMEM_LIMIT_BYTES),
        name="sink_attention",
    )(*args)


def pack_odd_w_in(w_in):
    nq, nkv = ATT_WIDTH, ATT_KV_HEADS * ATT_HEAD
    parts = [w_in[:, :nq], w_in[:, nq + 2 * nkv:]]
    for h in range(ATT_KV_HEADS):
        k = w_in[:, nq + h * ATT_HEAD:nq + (h + 1) * ATT_HEAD]
        v = w_in[:, nq + nkv + h * ATT_HEAD:nq + nkv + (h + 1) * ATT_HEAD]
        parts += [k, k, v, v]
    return jnp.concatenate(parts, 1).astype(BF16)


def pack_even_w_in(w_in):
    g1 = 4 * GDN_WIDTH
    r0 = GDN_IN
    parts = [w_in[:, :g1], w_in[:, r0:r0 + 4 * RWKV_WIDTH], w_in[:, r0 + 4 * RWKV_WIDTH:], w_in[:, g1:GDN_IN],
             jnp.zeros((D_MODEL, EV_COLS - EV_SMALL_COL - 4 * GDN_HEADS), F32)]
    return jnp.concatenate(parts, 1).astype(BF16)


def kernel(x, c, ctx, c_ctx, mod_w, mod_b, ln_g, ln_b, ev_w_in, ev_w_out, gdn_conv, gdn_a_log, gdn_dt_bias,
           gdn_norm_g, rwkv_mix, rwkv_w0, rwkv_w_up, rwkv_a0, rwkv_a_up, rwkv_k_k, rwkv_k_a, rwkv_r_k,
           rwkv_gn_g, rwkv_gn_b, od_w_in, od_w_out, od_sink):
    bsz, t_lat, d = x.shape
    t_ctx = ctx.shape[1]
    x_lat, x_ctx = x, ctx
    n_mod = -(-(bsz + 1) // 8) * 8
    cc = jax.nn.silu(jnp.concatenate([c, c_ctx[None], jnp.zeros((n_mod - bsz - 1, d), F32)], 0))
    mods = mod_all_layers(cc, mod_w, mod_b)
    for layer in range(DEPTH):
        need_ctx = layer < DEPTH - 1
        i = layer // 2
        mod = mods[layer]
        sh, sc, gt = (mod[:bsz, j * d:(j + 1) * d][:, None, :] for j in range(3))
        sh_c, sc_c, gt_c = (jnp.broadcast_to(mod[bsz, j * d:(j + 1) * d], (bsz, 1, d)) for j in range(3))
        if layer % 2 == 0:
            w_in = pack_even_w_in(ev_w_in[i])
            w_out = ev_w_out[i].astype(BF16)
            u_lat = inproj(x_lat, sc, sh, w_in, 1024, EV_TN)
            u_ctx = inproj(x_ctx.reshape(1, bsz * t_ctx, d), sc_c[:1], sh_c[:1], w_in, 1024, EV_TN).reshape(bsz, t_ctx, -1)
            small = slice(EV_SMALL_COL, EV_SMALL_COL + 4 * GDN_HEADS)
            og_lat, og_ctx = gdn_mixer(u_lat, gdn_bg(u_lat[..., small]), u_ctx, gdn_bg(u_ctx[..., small]),
                                       gdn_params(gdn_conv[i], gdn_a_log[i], gdn_dt_bias[i], gdn_norm_g[i]))
            or_lat, or_ctx = rwkv_mixer(u_lat, u_ctx, *rwkv_params(
                rwkv_mix[i], rwkv_w0[i], rwkv_w_up[i], rwkv_a0[i], rwkv_a_up[i], rwkv_k_k[i], rwkv_k_a[i],
                rwkv_r_k[i], rwkv_gn_g[i], rwkv_gn_b[i]))
            a_lat, a_ctx = [og_lat, or_lat], [og_ctx, or_ctx]
        else:
            w_in = pack_odd_w_in(od_w_in[i])
            w_out = od_w_out[i].astype(BF16)
            u_lat = inproj(x_lat, sc, sh, w_in, 1024, OD_TN)
            u_ctx = inproj(x_ctx.reshape(1, bsz * t_ctx, d), sc_c[:1], sh_c[:1], w_in, 1024, OD_TN).reshape(bsz, t_ctx, -1)
            tables = rope_lane_tables(t_lat)
            o_lat = sink_attention(u_lat, kv_rope(u_lat, tables), tables, u_ctx, od_sink[i])
            o_ctx = sink_attention(u_ctx, None, None, u_ctx, od_sink[i]) if need_ctx else None
            a_lat, a_ctx = [o_lat], [o_ctx]
        x_lat = outproj_ln(a_lat, w_out, x_lat, gt, ln_g[layer], ln_b[layer])
        if need_ctx:
            x_ctx = outproj_ln(a_ctx, w_out, x_ctx, gt_c, ln_g[layer], ln_b[layer])
    return x_lat
```

```python
import functools

import jax
import jax.numpy as jnp
import numpy as np
from jax import lax
from jax.experimental import pallas as pl
from jax.experimental.pallas import tpu as pltpu

F32 = jnp.float32
BF16 = jnp.bfloat16

D_MODEL = 2048
DEPTH = 4
GRID_W = 64
GDN_WIDTH = D_MODEL // 2
GDN_HEADS = 8
GDN_CHUNK = 128
RWKV_WIDTH = D_MODEL // 2
RWKV_HEADS = 16
RWKV_CHUNK = 64
RWKV_GN_EPS = 64e-5
ATT_HEAD = 64
ATT_Q_HEADS = D_MODEL // ATT_HEAD
ATT_KV_HEADS = ATT_Q_HEADS // 8
ATT_GROUP = ATT_Q_HEADS // ATT_KV_HEADS
ATT_WIDTH = ATT_Q_HEADS * ATT_HEAD
WINDOW = 128
ATT_BLOCK = 128
ROPE_BASE = 10000.0
ROPE_FREQS = ATT_HEAD // 4
NEG_INF = -1e30
LOG2E = 1.4426950408889634
DEEPNORM_ALPHA = (2 * DEPTH) ** 0.25
LN_EPS = 1e-5
GDN_IN = 4 * GDN_WIDTH + 4 * GDN_HEADS

LANES = 128
PREP_ROWS = 256
HALF = 64
V7X_VMEM_LIMIT_BYTES = 56 * 1024 * 1024
EV_GDN_COL = 0
EV_RWKV_COL = 4 * GDN_WIDTH
EV_LORA_COL = EV_RWKV_COL + 4 * RWKV_WIDTH
EV_SMALL_COL = EV_LORA_COL + LANES
EV_COLS = EV_SMALL_COL + LANES
EV_TN = 1408
OD_TN = 1280


def _mod_kernel(a_ref, w_ref, b_ref, o_ref):
    o_ref[0] = jnp.dot(a_ref[...].astype(BF16), w_ref[0].astype(BF16), preferred_element_type=F32) + b_ref[0]


def mod_all_layers(a, w, b, tn=512):
    m, k = a.shape
    nl, _, n = w.shape
    return pl.pallas_call(
        _mod_kernel,
        grid=(nl, n // tn),
        in_specs=[pl.BlockSpec((m, k), lambda l, j: (0, 0)), pl.BlockSpec((1, k, tn), lambda l, j: (l, 0, j)),
                  pl.BlockSpec((1, 1, tn), lambda l, j: (l, 0, j))],
        out_specs=pl.BlockSpec((1, m, tn), lambda l, j: (l, 0, j)),
        out_shape=jax.ShapeDtypeStruct((nl, m, n), F32),
        compiler_params=pltpu.CompilerParams(dimension_semantics=("arbitrary", "arbitrary"),
                                             vmem_limit_bytes=V7X_VMEM_LIMIT_BYTES),
        name="mod_matmul",
    )(a, w, b.reshape(nl, 1, n))


def _inproj_kernel(x_ref, sc_ref, sh_ref, w_ref, o_ref, h_ref):
    @pl.when(pl.program_id(2) == 0)
    def _():
        h_ref[...] = (x_ref[0] * (1.0 + sc_ref[0]) + sh_ref[0]).astype(BF16)

    o_ref[0] = jnp.dot(h_ref[...], w_ref[...], preferred_element_type=F32)


def inproj(x, sc, sh, w, tm, tn):
    bsz, t, d = x.shape
    n = w.shape[1]
    tm = min(tm, t)
    assert t % tm == 0 and n % tn == 0
    return pl.pallas_call(
        _inproj_kernel,
        grid=(bsz, t // tm, n // tn),
        in_specs=[pl.BlockSpec((1, tm, d), lambda b, i, j: (b, i, 0)),
                  pl.BlockSpec((1, 1, d), lambda b, i, j: (b, 0, 0)),
                  pl.BlockSpec((1, 1, d), lambda b, i, j: (b, 0, 0)),
                  pl.BlockSpec((d, tn), lambda b, i, j: (0, j))],
        out_specs=pl.BlockSpec((1, tm, tn), lambda b, i, j: (b, i, j)),
        out_shape=jax.ShapeDtypeStruct((bsz, t, n), F32),
        scratch_shapes=[pltpu.VMEM((tm, d), BF16)],
        compiler_params=pltpu.CompilerParams(dimension_semantics=("arbitrary", "arbitrary", "arbitrary"),
                                             vmem_limit_bytes=V7X_VMEM_LIMIT_BYTES),
        name="inproj",
    )(x, sc, sh, w)


def _outproj_kernel(*refs, n_a):
    a_refs, (w_ref, x_ref, gt_ref, g_ref, b_ref, o_ref) = refs[:n_a], refs[n_a:]
    y = None
    k0 = 0
    for a_ref in a_refs:
        kw = a_ref.shape[2]
        term = jnp.dot(a_ref[0].astype(BF16), w_ref[k0:k0 + kw, :], preferred_element_type=F32)
        y = term if y is None else y + term
        k0 += kw
    z = DEEPNORM_ALPHA * x_ref[0] + gt_ref[0] * y
    mu = jnp.mean(z, axis=1, keepdims=True)
    zc = z - mu
    var = jnp.mean(zc * zc, axis=1, keepdims=True)
    o_ref[0] = zc * lax.rsqrt(var + LN_EPS) * g_ref[...] + b_ref[...]


def outproj_ln(a_list, w, x, gt, ln_g, ln_b, tm=512):
    bsz, t, d = x.shape
    tm = min(tm, t)
    in_specs = [pl.BlockSpec((1, tm, a.shape[2]), lambda b, i: (b, i, 0)) for a in a_list] + [
        pl.BlockSpec(w.shape, lambda b, i: (0, 0)),
        pl.BlockSpec((1, tm, d), lambda b, i: (b, i, 0)),
        pl.BlockSpec((1, 1, d), lambda b, i: (b, 0, 0)),
        pl.BlockSpec((1, d), lambda b, i: (0, 0)),
        pl.BlockSpec((1, d), lambda b, i: (0, 0))]
    return pl.pallas_call(
        functools.partial(_outproj_kernel, n_a=len(a_list)),
        grid=(bsz, t // tm),
        in_specs=in_specs,
        out_specs=pl.BlockSpec((1, tm, d), lambda b, i: (b, i, 0)),
        out_shape=jax.ShapeDtypeStruct((bsz, t, d), F32),
        compiler_params=pltpu.CompilerParams(dimension_semantics=("arbitrary", "arbitrary"),
                                             vmem_limit_bytes=V7X_VMEM_LIMIT_BYTES),
        name="outproj_ln",
    )(*a_list, w, x, gt, ln_g.reshape(1, d), ln_b.reshape(1, d))


def _bdot(a, b):
    return jnp.dot(a.astype(BF16), b.astype(BF16), preferred_element_type=F32)


def _bdot_nt(a, b):
    return lax.dot_general(a.astype(BF16), b.astype(BF16), (((1,), (1,)), ((), ())), preferred_element_type=F32)


def _bdot_tn(a, b):
    return lax.dot_general(a.astype(BF16), b.astype(BF16), (((0,), (0,)), ((), ())), preferred_element_type=F32)


(PV_MR0, PV_MR1, PV_MK0, PV_MK1, PV_MV0, PV_MV1, PV_MG0, PV_MG1, PV_W00, PV_W01, PV_A00, PV_A01,
 PV_KK, PV_KA, PV_RK, PV_GNG, PV_GNB) = range(17)
PV_ROWS = 24
MK_ABD, MK_P, MK_Y, MK_L1 = 0, 1, 2, 3
MK_LV = 4
N_LV = 5
MK_PER_DIR = MK_LV + N_LV
MK_EYE = 2 * MK_PER_DIR
MK_BD = MK_EYE + 1
N_MASKS = MK_BD + 1


def rwkv_masks():
    ri = np.arange(128)[:, None]
    ci = np.arange(128)[None, :]
    i, j = ri & 63, ci & 63
    same = (ri < 64) == (ci < 64)
    out = np.zeros((N_MASKS, 128, 128), np.float32)
    for d in range(2):
        strict = (j < i) if d == 0 else (j > i)
        incl = (j <= i) if d == 0 else (j >= i)
        base = d * MK_PER_DIR
        out[base + MK_ABD] = strict & same
        out[base + MK_P] = strict & (ci >= 64)
        out[base + MK_Y] = incl
        for li, s in enumerate((1, 2, 4, 8, 16, 32)):
            blk = (i // (2 * s)) == (j // (2 * s))
            if d == 0:
                m = blk & ((i & s) != 0) & ((j & s) == 0)
            else:
                m = blk & ((i & s) == 0) & ((j & s) != 0)
            out[base + (MK_L1 if li == 0 else MK_LV + li - 1)] = m & same
    out[MK_EYE] = ri == ci
    out[MK_BD] = same
    return out


def _seg_sum(x, lo):
    s0 = jnp.sum(jnp.where(lo, x, 0.0), axis=1, keepdims=True)
    s1 = jnp.sum(jnp.where(lo, 0.0, x), axis=1, keepdims=True)
    return jnp.where(lo, s0, s1)


def _chunk_cumsum(x, chunk):
    rowm = lax.broadcasted_iota(jnp.int32, x.shape, 0) & (chunk - 1)
    s = 1
    while s < chunk:
        x = x + jnp.where(rowm >= s, pltpu.roll(x, s, axis=0), 0.0)
        s *= 2
    return x


def _interleave(*gens):
    live = list(gens)
    while live:
        for g in list(live):
            try:
                next(g)
            except StopIteration:
                live.remove(g)


def _rwkv_kernel(lr, lk, lv, lg, ll, cr, ck, cv, cg, cl, pv_ref, mixl_ref, wup_ref, aup_ref, mk_ref,
                 o_lat, o_ctx,
                 s_at0, s_at1, s_rt0, s_rt1, s_bt0, s_bt1, s_kt0, s_kt1, s_bh0, s_bh1, s_kh0, s_kh1, s_v,
                 s_g, s_bonus, s_yf, s_yb, s_w, s_tp, s_gr, s_et, s_state, *, t_lat, t_ctx, p1_chunks):
    C = RWKV_CHUNK
    s_at, s_rt, s_bt, s_kt, s_bh, s_kh = ((s_at0, s_at1), (s_rt0, s_rt1), (s_bt0, s_bt1), (s_kt0, s_kt1),
                                          (s_bh0, s_bh1), (s_kh0, s_kh1))
    s_y = (s_yf, s_yb)
    lo64 = lax.broadcasted_iota(jnp.int32, (C, LANES), 1) < HALF
    bd_ones = mk_ref[MK_BD]

    def pvrow(i):
        return pv_ref[pl.ds(i, 1), :]

    def prep_tile(xs, tseq, it, row0, ch0):
        xr, xk, xv, xg, xl = xs
        R = min(PREP_ROWS, tseq)
        s = it * R
        lo = lax.broadcasted_iota(jnp.int32, (R, LANES), 1) < HALF
        rows = lax.broadcasted_iota(jnp.int32, (R, LANES), 0)

        def shifted(ref, m0, m1):
            cur = ref[0, s:s + R, :]
            prev = ref[0, s - 1:s - 1 + R, :] if s > 0 else jnp.where(rows == 0, 0.0, pltpu.roll(cur, 1, axis=0))
            nxt = (ref[0, s + 1:s + 1 + R, :] if s + R < tseq
                   else jnp.where(rows == R - 1, 0.0, pltpu.roll(cur, R - 1, axis=0)))
            return cur + m0 * (prev - cur) + m1 * (nxt - cur)

        r = shifted(xr, pvrow(PV_MR0), pvrow(PV_MR1))
        k = shifted(xk, pvrow(PV_MK0), pvrow(PV_MK1))
        v = shifted(xv, pvrow(PV_MV0), pvrow(PV_MV1))
        g = shifted(xg, pvrow(PV_MG0), pvrow(PV_MG1))
        lor = shifted(xl, mixl_ref[0:1, :], mixl_ref[1:2, :])
        th = jnp.tanh(lor)
        kkf = k * pvrow(PV_KK)
        kk = kkf * lax.rsqrt(_seg_sum(kkf * kkf, lo) + 1e-6)
        ka = pvrow(PV_KA)
        out = slice(row0 + s, row0 + s + R)
        kd_sum = None
        for d in range(2):
            w_pre = pvrow(PV_W00 + d) + _bdot(th, wup_ref[d])
            lw = (-np.exp(-0.5)) * jax.nn.sigmoid(w_pre)
            ag = jax.nn.sigmoid(pvrow(PV_A00 + d) + _bdot(lor, aup_ref[d]))
            kd = k * (1.0 + (ag - 1.0) * ka)
            bdv = kk * ag
            kd_sum = kd if kd_sum is None else kd_sum + kd
            c_fwd = _chunk_cumsum(lw, C)
            tot = jnp.concatenate([jnp.broadcast_to(c_fwd[(c + 1) * C - 1:(c + 1) * C, :], (C, LANES))
                                   for c in range(R // C)], axis=0)
            c_bwd = tot - c_fwd + lw
            cw, rest = (c_fwd, c_bwd - lw) if d == 0 else (c_bwd, c_fwd - lw)
            e_in = jnp.exp(-cw)
            e_out = jnp.exp(rest)
            s_at[d][out, :] = (-kk * jnp.exp(cw - lw)).astype(BF16)
            s_rt[d][out, :] = (r * jnp.exp(cw)).astype(BF16)
            s_bt[d][out, :] = (bdv * e_in).astype(BF16)
            s_kt[d][out, :] = (kd * e_in).astype(BF16)
            s_bh[d][out, :] = (bdv * e_out).astype(BF16)
            s_kh[d][out, :] = (kd * e_out).astype(BF16)
            for c in range(R // C):
                last = c * C + (C - 1 if d == 0 else 0)
                n = ch0 + it * (R // C) + c
                s_et[d, n:n + 1, :] = jnp.exp(cw[last:last + 1, :])
        s_v[out, :] = v.astype(BF16)
        s_g[out, :] = g
        s_bonus[out, :] = _seg_sum(r * kd_sum * pvrow(PV_RK), lo) * v

    def prep_stages(xs, tseq, tiles, row0, ch0):
        for it in tiles:
            prep_tile(xs, tseq, it, row0, ch0)
            yield

    def p1_stages(probs, row0, ch0):
        ds = [d for _, d in probs]
        rows = [pl.ds(pl.multiple_of(row0 + n * C, C), C) for n, _ in probs]
        a2s, gms = [], []
        for (n, d), rw in zip(probs, rows):
            at, rt = s_at[d][rw, :], s_rt[d][rw, :]
            zero = jnp.zeros_like(at)
            a2 = jnp.concatenate([jnp.where(lo64, at, zero), jnp.where(lo64, zero, at)], axis=0)
            l1 = jnp.concatenate([a2, jnp.where(lo64, rt, zero), jnp.where(lo64, zero, rt)], axis=0)
            a2s.append(a2)
            gms.append(_bdot_nt(l1, jnp.concatenate([s_bt[d][rw, :], s_kt[d][rw, :]], axis=0)))
        yield
        ps, a_list = [], []
        for (n, d), rw, gm in zip(probs, rows, gms):
            base = d * MK_PER_DIR
            ga = gm[0:2 * C]
            v = s_v[rw, :]
            ps.append(_bdot(ga * mk_ref[base + MK_P], jnp.concatenate([v, v], axis=0)))
            s_gr[d, ch0 + n] = (gm[2 * C:4 * C] * mk_ref[base + MK_Y]).astype(BF16)
            ga_sw = jnp.concatenate([ga[0:C], pltpu.roll(ga[C:2 * C], HALF, axis=1)], axis=0)
            a_list.append(ga_sw * mk_ref[base + MK_ABD])
        yield
        xs = [mk_ref[MK_EYE] + a * mk_ref[d * MK_PER_DIR + MK_L1] for a, d in zip(a_list, ds)]
        for li in range(N_LV):
            t1 = [_bdot(x, a * mk_ref[d * MK_PER_DIR + MK_LV + li]) for x, a, d in zip(xs, a_list, ds)]
            yield
            xs = [x + _bdot(t, x) for x, t in zip(xs, t1)]
            yield
        for (n, d), x, a2, p in zip(probs, xs, a2s, ps):
            wtp = _bdot(x, jnp.concatenate([a2.astype(F32), p], axis=1))
            s_w[d, ch0 + n] = wtp[:, 0:LANES].astype(BF16)
            s_tp[d, ch0 + n] = jnp.where(lo64, wtp[0:C, LANES:2 * LANES], wtp[C:2 * C, LANES:2 * LANES])
        yield

    def p2_stages(steps, row0, ch0):
        for ns0 in steps:
            ns = [ch0 + n for n in ns0]
            rows = [pl.ds(pl.multiple_of(row0 + ns0[d] * C, C), C) for d in range(2)]
            st = [s_state[d] for d in range(2)]
            xs = [_bdot_nt(jnp.concatenate([s_w[d, ns[d]], s_rt[d][rows[d], :]], axis=0), st[d]) for d in range(2)]
            yield
            uv = [jnp.concatenate([(xs[d][0:C] + xs[d][C:2 * C] + s_tp[d, ns[d]]).astype(BF16), s_v[rows[d], :]], axis=0)
                  for d in range(2)]
            yb = [_bdot(s_gr[d, ns[d]], uv[d]) for d in range(2)]
            for d in range(2):
                s_y[d][rows[d], :] = xs[d][2 * C:3 * C] + jnp.where(lo64, yb[d][0:C], yb[d][C:2 * C])
                bk = jnp.concatenate([s_bh[d][rows[d], :], s_kh[d][rows[d], :]], axis=0)
                s_state[d] = st[d] * s_et[d, pl.ds(ns[d], 1), :] + bd_ones * _bdot_tn(uv[d], bk)
            yield

    def groups(tseq):
        nch = tseq // C
        per = min(p1_chunks, nch)
        probs = lambda g: [(g * per + j, 0) for j in range(per)] + [(nch - 1 - (g * per + j), 1) for j in range(per)]
        steps = lambda g: [(g * per + j, nch - 1 - (g * per + j)) for j in range(per)]
        return nch // per, per, probs, steps

    def epilogue_tile(o_ref, tseq, it, row0):
        R = min(PREP_ROWS, tseq)
        lo = lax.broadcasted_iota(jnp.int32, (R, LANES), 1) < HALF
        rows = slice(row0 + it * R, row0 + (it + 1) * R)
        y = s_yf[rows, :] + s_yb[rows, :]
        mu = _seg_sum(y, lo) * (1.0 / HALF)
        yc = y - mu
        var = _seg_sum(yc * yc, lo) * (1.0 / HALF)
        yn = yc * lax.rsqrt(var + RWKV_GN_EPS) * pvrow(PV_GNG) + pvrow(PV_GNB)
        g = s_g[rows, :]
        o_ref[0, it * R:(it + 1) * R, :] = ((yn + s_bonus[rows, :]) * (g * jax.nn.sigmoid(g))).astype(BF16)

    def epilogue_stages(o_ref, tseq, tiles, row0):
        for it in tiles:
            epilogue_tile(o_ref, tseq, it, row0)
            yield

    lat, ctx = (lr, lk, lv, lg, ll), (cr, ck, cv, cg, cl)
    row_c, ch_c = t_lat, t_lat // C
    s_state[...] = jnp.zeros_like(s_state)
    _interleave(prep_stages(ctx, t_ctx, range(t_ctx // min(PREP_ROWS, t_ctx)), row_c, ch_c))
    ng_c, _, probs_c, steps_c = groups(t_ctx)
    ng, per, probs, steps = groups(t_lat)
    ntile = t_lat // min(PREP_ROWS, t_lat)
    tpg = max(per * C // min(PREP_ROWS, t_lat), 1)
    first = sorted(set(list(range(min(tpg, ntile))) + list(range(max(ntile - tpg, 0), ntile))))
    rest = [t for t in range(ntile) if t not in first]

    def ctx_scan():
        for g in range(ng_c):
            yield from p1_stages(probs_c(g), row_c, ch_c)
            yield from p2_stages(steps_c(g), row_c, ch_c)

    _interleave(prep_stages(lat, t_lat, first, 0, 0), ctx_scan())
    _interleave(prep_stages(lat, t_lat, rest, 0, 0), p1_stages(probs(0), 0, 0),
                epilogue_stages(o_ctx, t_ctx, range(t_ctx // min(PREP_ROWS, t_ctx)), row_c))
    for g in range(ng - 1):
        _interleave(p1_stages(probs(g + 1), 0, 0), p2_stages(steps(g), 0, 0))
    done = [t for t in range(ntile) if per * C <= t * min(PREP_ROWS, t_lat) and (t + 1) * min(PREP_ROWS, t_lat) <= (ng - 1) * per * C]
    _interleave(p2_stages(steps(ng - 1), 0, 0), epilogue_stages(o_lat, t_lat, done, 0))
    _interleave(epilogue_stages(o_lat, t_lat, [t for t in range(ntile) if t not in done], 0))


def _col_block_map(b, p, *, off):
    return (b, 0, off + p)


def rwkv_mixer(u_lat, u_ctx, pv, mixl, wup, aup, p1_chunks=8):
    bsz, t_lat, _ = u_lat.shape
    t_ctx = u_ctx.shape[1]
    npair = RWKV_HEADS // 2
    c0 = EV_RWKV_COL // LANES
    nch = (t_lat + t_ctx) // RWKV_CHUNK
    t_all = t_lat + t_ctx
    masks = jnp.asarray(rwkv_masks())

    def seq_specs(t):
        return [pl.BlockSpec((1, t, LANES), functools.partial(_col_block_map, off=c0 + o * npair)) for o in range(4)] + [
            pl.BlockSpec((1, t, LANES), lambda b, p: (b, 0, EV_LORA_COL // LANES))]

    in_specs = seq_specs(t_lat) + seq_specs(t_ctx) + [
        pl.BlockSpec((PV_ROWS, LANES), lambda b, p: (0, p)),
        pl.BlockSpec((8, LANES), lambda b, p: (0, 0)),
        pl.BlockSpec((2, LANES, LANES), lambda b, p: (0, 0, p)),
        pl.BlockSpec((2, LANES, LANES), lambda b, p: (0, 0, p)),
        pl.BlockSpec((N_MASKS, LANES, LANES), lambda b, p: (0, 0, 0)),
    ]
    out_specs = [pl.BlockSpec((1, t_lat, LANES), lambda b, p: (b, 0, p)),
                 pl.BlockSpec((1, t_ctx, LANES), lambda b, p: (b, 0, p))]
    scratch = ([pltpu.VMEM((t_all, LANES), BF16) for _ in range(13)] + [pltpu.VMEM((t_all, LANES), F32) for _ in range(4)]
               + [pltpu.VMEM((2, nch, LANES, LANES), BF16), pltpu.VMEM((2, nch, RWKV_CHUNK, LANES), F32),
                  pltpu.VMEM((2, nch, LANES, LANES), BF16)]
               + [pltpu.VMEM((2, nch, LANES), F32), pltpu.VMEM((2, LANES, LANES), F32)])
    kern = functools.partial(_rwkv_kernel, t_lat=t_lat, t_ctx=t_ctx, p1_chunks=p1_chunks)
    return pl.pallas_call(
        kern,
        grid=(bsz, npair),
        in_specs=in_specs,
        out_specs=out_specs,
        out_shape=[jax.ShapeDtypeStruct((bsz, t_lat, RWKV_WIDTH), BF16), jax.ShapeDtypeStruct((bsz, t_ctx, RWKV_WIDTH), BF16)],
        scratch_shapes=scratch,
        compiler_params=pltpu.CompilerParams(dimension_semantics=("arbitrary", "arbitrary"),
                                             vmem_limit_bytes=V7X_VMEM_LIMIT_BYTES),
        name="rwkv7_chunked",
    )(*([u_lat] * 5), *([u_ctx] * 5), pv, mixl, wup, aup, masks)


def rwkv_params(mix, w0, w_up, a0, a_up, k_k, k_a, r_k, gn_g, gn_b):
    wd = RWKV_WIDTH
    rows = [mix[0, 0:wd], mix[1, 0:wd], mix[0, wd:2 * wd], mix[1, wd:2 * wd], mix[0, 2 * wd:3 * wd], mix[1, 2 * wd:3 * wd],
            mix[0, 3 * wd:4 * wd], mix[1, 3 * wd:4 * wd], w0[0], w0[1], a0[0], a0[1], k_k, k_a, r_k.reshape(-1), gn_g, gn_b]
    pv = jnp.concatenate([jnp.stack(rows), jnp.zeros((PV_ROWS - len(rows), wd), F32)], 0)
    mixl = jnp.concatenate([mix[:, 4 * wd:], jnp.zeros((6, LANES), F32)], 0)
    z = jnp.zeros((2, HALF, wd), F32)
    wup = jnp.concatenate([w_up, z], 1)
    aup = jnp.concatenate([z, a_up], 1)
    return pv, mixl, wup, aup


GP_CONV_Q, GP_CONV_K, GP_CONV_V = 0, 5, 10
GP_NORM, GP_SC_ALOG, GP_SC_DTB = 15, 16, 17
GP_ROWS = 24
GM_STRICT, GM_INCL, GM_L1, GM_LV = 0, 1, 2, 3
GN_LV = 6
GM_PER_DIR = GM_LV + GN_LV
GM_EYE = 2 * GM_PER_DIR
GN_MASKS = GM_EYE + 1
GDN_P1_CHUNKS = 4


def gdn_masks():
    i = np.arange(GDN_CHUNK)[:, None]
    j = np.arange(GDN_CHUNK)[None, :]
    out = np.zeros((GN_MASKS, GDN_CHUNK, GDN_CHUNK), np.float32)
    for d in range(2):
        base = d * GM_PER_DIR
        out[base + GM_STRICT] = (j < i) if d == 0 else (j > i)
        out[base + GM_INCL] = (j <= i) if d == 0 else (j >= i)
        for li in range(GN_LV + 1):
            s = 1 << li
            blk = (i // (2 * s)) == (j // (2 * s))
            m = blk & (((i & s) != 0) & ((j & s) == 0) if d == 0 else ((i & s) == 0) & ((j & s) != 0))
            out[base + (GM_L1 if li == 0 else GM_LV + li - 1)] = m
    out[GM_EYE] = i == j
    return out


def _gdn_kernel(lq, lk, lv, lz, lbg, cq, ck, cv, cz, cbg, gp_ref, mk_ref, o_lat, o_ctx,
                s_q, s_k, s_kb0, s_kb1, s_kbg0, s_kbg1, s_qg0, s_qg1, s_kg0, s_kg1, s_vb0, s_vb1,
                s_gc0, s_gc1, s_of, s_ob, s_tk, s_tvb, s_a, s_gl, s_state, *, t_lat, t_ctx):
    C = GDN_CHUNK
    s_kb, s_kbg, s_qg, s_kg, s_vb = (s_kb0, s_kb1), (s_kbg0, s_kbg1), (s_qg0, s_qg1), (s_kg0, s_kg1), (s_vb0, s_vb1)
    s_gc, s_o = (s_gc0, s_gc1), (s_of, s_ob)

    def gprow(i):
        return gp_ref[pl.ds(i, 1), :]

    def prep_tile(xs, tseq, it, row0, ch0):
        xq, xk, xv, xbg = xs
        R = min(PREP_ROWS, tseq)
        s = it * R

        def conv_silu(ref, prow0):
            rows = lax.broadcasted_iota(jnp.int32, (R, LANES), 0)
            acc = None
            for j in range(5):
                off = j - 2
                lo_r, hi_r = s + off, s + off + R
                if lo_r < 0:
                    x = jnp.where(rows >= -off, pltpu.roll(ref[0, 0:R, :], -off, axis=0), 0.0)
                elif hi_r > tseq:
                    x = jnp.where(rows < R - off, pltpu.roll(ref[0, tseq - R:tseq, :], R - off, axis=0), 0.0)
                else:
                    x = ref[0, lo_r:hi_r, :]
                term = gprow(prow0 + j) * x
                acc = term if acc is None else acc + term
            return acc * jax.nn.sigmoid(acc)

        qc = conv_silu(xq, GP_CONV_Q)
        kc = conv_silu(xk, GP_CONV_K)
        v = conv_silu(xv, GP_CONV_V)
        q = qc * (lax.rsqrt(jnp.sum(qc * qc, axis=1, keepdims=True) + 1e-6) * (LANES ** -0.5))
        k = kc * lax.rsqrt(jnp.sum(kc * kc, axis=1, keepdims=True) + 1e-6)
        bg = xbg[0, 0, s:s + R, :]
        lane4 = lax.broadcasted_iota(jnp.int32, bg.shape, 1)
        beta4 = jax.nn.sigmoid(bg)
        g4 = -jnp.exp(gprow(GP_SC_ALOG)[:, 0:4]) * jax.nn.softplus(bg + gprow(GP_SC_DTB)[:, 0:4])
        c_fwd = _chunk_cumsum(g4, C)
        tot = jnp.concatenate([jnp.broadcast_to(c_fwd[(c + 1) * C - 1:(c + 1) * C, :], (C, 4)) for c in range(R // C)],
                              axis=0)
        c_bwd = tot - c_fwd + g4
        fwd_lane = lane4 == 2
        gc4 = jnp.where(fwd_lane, c_fwd, c_bwd)
        eg4 = jnp.exp(gc4)
        er4 = jnp.exp(jnp.where(fwd_lane, c_bwd, c_fwd) - g4)
        out = slice(row0 + s, row0 + s + R)
        s_q[out, :] = q.astype(BF16)
        s_k[out, :] = k.astype(BF16)
        for d in range(2):
            beta, eg = beta4[:, d:d + 1], eg4[:, 2 + d:3 + d]
            kb = k * beta
            s_kb[d][out, :] = kb.astype(BF16)
            s_kbg[d][out, :] = (kb * eg).astype(BF16)
            s_qg[d][out, :] = (q * eg).astype(BF16)
            s_kg[d][out, :] = (k * er4[:, 2 + d:3 + d]).astype(BF16)
            s_vb[d][out, :] = (v * beta).astype(BF16)
            s_gc[d][out, :] = jnp.broadcast_to(gc4[:, 2 + d:3 + d], (R, LANES))
            for c in range(R // C):
                last = c * C + (C - 1 if d == 0 else 0)
                n = ch0 + it * (R // C) + c
                s_gl[d, n:n + 1, :] = jnp.broadcast_to(eg[last:last + 1, :], (1, LANES))

    def prep_stages(xs, tseq, tiles, row0, ch0):
        for it in tiles:
            prep_tile(xs, tseq, it, row0, ch0)
            yield

    def p1_stages(probs, row0, ch0):
        ds = [d for _, d in probs]
        rows = [pl.ds(pl.multiple_of(row0 + n * C, C), C) for n, _ in probs]
        gms = [_bdot_nt(jnp.concatenate([s_kb[d][rw, :], s_q[rw, :]], axis=0), s_k[rw, :])
               for (n, d), rw in zip(probs, rows)]
        yield
        a_list = []
        for (n, d), rw, gm in zip(probs, rows, gms):
            base = d * GM_PER_DIR
            gc = s_gc[d][rw, :]
            m_incl = mk_ref[base + GM_INCL]
            dec = jnp.exp((gc - gc.T) * m_incl)
            s_a[d, ch0 + n] = (gm[C:2 * C] * dec * m_incl).astype(BF16)
            a_list.append(-(gm[0:C] * dec * mk_ref[base + GM_STRICT]))
        xs = [mk_ref[GM_EYE] + a * mk_ref[d * GM_PER_DIR + GM_L1] for a, d in zip(a_list, ds)]
        for li in range(GN_LV):
            t1 = [_bdot(x, a * mk_ref[d * GM_PER_DIR + GM_LV + li]) for x, a, d in zip(xs, a_list, ds)]
            yield
            xs = [x + _bdot(t, x) for x, t in zip(xs, t1)]
            yield
        for (n, d), rw, x in zip(probs, rows, xs):
            tkv = _bdot(x, jnp.concatenate([s_kbg[d][rw, :], s_vb[d][rw, :]], axis=1))
            s_tk[d, ch0 + n] = tkv[:, 0:LANES].astype(BF16)
            s_tvb[d, ch0 + n] = tkv[:, LANES:2 * LANES]
        yield

    def p2_stages(steps, row0, ch0):
        for ns0 in steps:
            ns = [ch0 + n for n in ns0]
            rows = [pl.ds(pl.multiple_of(row0 + ns0[d] * C, C), C) for d in range(2)]
            st = [s_state[d] for d in range(2)]
            aq = [_bdot(jnp.concatenate([s_tk[d, ns[d]], s_qg[d][rows[d], :]], axis=0), st[d]) for d in range(2)]
            yield
            vnew = [s_tvb[d, ns[d]] - aq[d][0:C] for d in range(2)]
            oo = [aq[d][C:2 * C] + _bdot(s_a[d, ns[d]], vnew[d]) for d in range(2)]
            for d in range(2):
                s_o[d][rows[d], :] = oo[d]
                s_state[d] = st[d] * s_gl[d, pl.ds(ns[d], 1), :] + _bdot_tn(s_kg[d][rows[d], :], vnew[d])
            yield

    def groups(tseq):
        nch = tseq // C
        per = min(GDN_P1_CHUNKS, nch)
        probs = lambda g: [(g * per + j, 0) for j in range(per)] + [(nch - 1 - (g * per + j), 1) for j in range(per)]
        steps = lambda g: [(g * per + j, nch - 1 - (g * per + j)) for j in range(per)]
        return nch // per, per, probs, steps

    def epilogue_tile(o_ref, z_ref, tseq, it, row0):
        R = min(PREP_ROWS, tseq)
        rows = slice(row0 + it * R, row0 + (it + 1) * R)
        o = s_of[rows, :] + s_ob[rows, :]
        o = o * lax.rsqrt(jnp.mean(o * o, axis=1, keepdims=True) + 1e-6) * gprow(GP_NORM)
        z = z_ref[0, it * R:(it + 1) * R, :]
        o_ref[0, it * R:(it + 1) * R, :] = (o * (z * jax.nn.sigmoid(z))).astype(BF16)

    def epilogue_stages(o_ref, z_ref, tseq, tiles, row0):
        for it in tiles:
            epilogue_tile(o_ref, z_ref, tseq, it, row0)
            yield

    lat, ctx = (lq, lk, lv, lbg), (cq, ck, cv, cbg)
    row_c, ch_c = t_lat, t_lat // C
    s_state[...] = jnp.zeros_like(s_state)
    _interleave(prep_stages(ctx, t_ctx, range(t_ctx // min(PREP_ROWS, t_ctx)), row_c, ch_c))
    ng_c, _, probs_c, steps_c = groups(t_ctx)
    ng, per, probs, steps = groups(t_lat)
    ntile = t_lat // min(PREP_ROWS, t_lat)
    tpg = max(per * C // min(PREP_ROWS, t_lat), 1)
    first = sorted(set(list(range(min(tpg, ntile))) + list(range(max(ntile - tpg, 0), ntile))))
    rest = [t for t in range(ntile) if t not in first]

    def ctx_scan():
        for g in range(ng_c):
            yield from p1_stages(probs_c(g), row_c, ch_c)
            yield from p2_stages(steps_c(g), row_c, ch_c)

    _interleave(prep_stages(lat, t_lat, first, 0, 0), ctx_scan())
    _interleave(prep_stages(lat, t_lat, rest, 0, 0), p1_stages(probs(0), 0, 0),
                epilogue_stages(o_ctx, cz, t_ctx, range(t_ctx // min(PREP_ROWS, t_ctx)), row_c))
    if ng > 1:
        def body(g, carry):
            _interleave(p1_stages(probs(g + 1), 0, 0), p2_stages(steps(g), 0, 0))
            return carry

        lax.fori_loop(0, ng - 1, body, 0)
    done = [t for t in range(ntile) if per * C <= t * min(PREP_ROWS, t_lat) and (t + 1) * min(PREP_ROWS, t_lat) <= (ng - 1) * per * C]
    _interleave(p2_stages(steps(ng - 1), 0, 0), epilogue_stages(o_lat, lz, t_lat, done, 0))
    _interleave(epilogue_stages(o_lat, lz, t_lat, [t for t in range(ntile) if t not in done], 0))


def gdn_mixer(u_lat, bg_lat, u_ctx, bg_ctx, gp):
    bsz, t_lat, _ = u_lat.shape
    t_ctx = u_ctx.shape[1]
    nh = GDN_HEADS
    masks = jnp.asarray(gdn_masks())

    def seq_specs(t):
        return [pl.BlockSpec((1, t, LANES), functools.partial(_col_block_map, off=o * nh)) for o in range(4)] + [
            pl.BlockSpec((1, 1, t, 4), lambda b, h: (b, h, 0, 0))]

    in_specs = seq_specs(t_lat) + seq_specs(t_ctx) + [
        pl.BlockSpec((GP_ROWS, LANES), lambda b, h: (0, h)),
        pl.BlockSpec((GN_MASKS, LANES, LANES), lambda b, h: (0, 0, 0)),
    ]
    out_specs = [pl.BlockSpec((1, t_lat, LANES), lambda b, h: (b, 0, h)),
                 pl.BlockSpec((1, t_ctx, LANES), lambda b, h: (b, 0, h))]
    nch = (t_lat + t_ctx) // GDN_CHUNK
    t_all = t_lat + t_ctx
    scratch = ([pltpu.VMEM((t_all, LANES), BF16) for _ in range(12)] + [pltpu.VMEM((t_all, LANES), F32) for _ in range(4)]
               + [pltpu.VMEM((2, nch, LANES, LANES), BF16), pltpu.VMEM((2, nch, LANES, LANES), F32),
                  pltpu.VMEM((2, nch, LANES, LANES), BF16),
                  pltpu.VMEM((2, max(nch, 8), LANES), F32), pltpu.VMEM((2, LANES, LANES), F32)])
    kern = functools.partial(_gdn_kernel, t_lat=t_lat, t_ctx=t_ctx)
    return pl.pallas_call(
        kern,
        grid=(bsz, nh),
        in_specs=in_specs,
        out_specs=out_specs,
        out_shape=[jax.ShapeDtypeStruct((bsz, t_lat, GDN_WIDTH), BF16), jax.ShapeDtypeStruct((bsz, t_ctx, GDN_WIDTH), BF16)],
        scratch_shapes=scratch,
        compiler_params=pltpu.CompilerParams(dimension_semantics=("arbitrary", "arbitrary"),
                                             vmem_limit_bytes=V7X_VMEM_LIMIT_BYTES),
        name="gdn_chunked",
    )(*([u_lat] * 4), bg_lat, *([u_ctx] * 4), bg_ctx, gp, masks)


def gdn_params(conv_w, a_log, dt_bias, norm_g):
    wd = GDN_WIDTH

    def scalar_row(x):
        return jnp.zeros((GDN_HEADS, LANES), F32).at[:, 2].set(x[0]).at[:, 3].set(x[1]).reshape(-1)

    rows = [conv_w[j, o * wd:(o + 1) * wd] for o in range(3) for j in range(5)]
    rows += [jnp.tile(norm_g, GDN_HEADS), scalar_row(a_log), scalar_row(dt_bias)]
    return jnp.concatenate([jnp.stack(rows), jnp.zeros((GP_ROWS - len(rows), wd), F32)], 0)


def gdn_bg(u_small):
    bsz, t, _ = u_small.shape
    x = u_small.reshape(bsz, t, 2, 2, GDN_HEADS)
    return jnp.transpose(x, (0, 4, 1, 2, 3)).reshape(bsz, GDN_HEADS, t, 4)


OD_Q_COL, OD_GATE_COL, OD_KV_COL = 0, ATT_WIDTH, 2 * ATT_WIDTH
OD_KV_W = 4 * ATT_HEAD
OD_COLS = OD_KV_COL + ATT_KV_HEADS * OD_KV_W


def rope_lane_tables(t):
    pos = np.arange(t)
    inv = ROPE_BASE ** (-np.arange(ROPE_FREQS, dtype=np.float64) / ROPE_FREQS)
    ang_row = (pos // GRID_W)[:, None] * inv
    ang_col = (pos % GRID_W)[:, None] * inv
    ang = np.concatenate([ang_row, ang_row, ang_col, ang_col] * 2, axis=1)
    first = (np.arange(LANES) % (2 * ROPE_FREQS)) < ROPE_FREQS
    cos, sin = np.cos(ang), np.sin(ang)
    return (jnp.asarray(cos, F32), jnp.asarray(np.where(first, -sin, 0.0), F32), jnp.asarray(np.where(first, 0.0, sin), F32))


def _rope(x, cos, sina, sinb):
    n = x.shape[1]
    return x * cos + pltpu.roll(x, n - ROPE_FREQS, axis=1) * sina + pltpu.roll(x, ROPE_FREQS, axis=1) * sinb


def _kv_rope_kernel(u_ref, cos_ref, sina_ref, sinb_ref, o_ref):
    x = u_ref[0]
    cos, sina, sinb = cos_ref[...], sina_ref[...], sinb_ref[...]
    parts = []
    for h in range(ATT_KV_HEADS):
        k2 = x[:, h * OD_KV_W:h * OD_KV_W + LANES]
        parts += [_rope(k2, cos, sina, sinb), x[:, h * OD_KV_W + LANES:(h + 1) * OD_KV_W]]
    o_ref[0] = jnp.concatenate(parts, axis=1).astype(BF16)


def kv_rope(u, tables, tr=512):
    bsz, t, _ = u.shape
    tr = min(tr, t)
    kvw = ATT_KV_HEADS * OD_KV_W
    tab = pl.BlockSpec((tr, LANES), lambda b, i: (i, 0))
    return pl.pallas_call(
        _kv_rope_kernel,
        grid=(bsz, t // tr),
        in_specs=[pl.BlockSpec((1, tr, kvw), lambda b, i: (b, i, OD_KV_COL // kvw)), tab, tab, tab],
        out_specs=pl.BlockSpec((1, tr, kvw), lambda b, i: (b, i, 0)),
        out_shape=jax.ShapeDtypeStruct((bsz, t, kvw), BF16),
        compiler_params=pltpu.CompilerParams(dimension_semantics=("arbitrary", "arbitrary")),
        name="kv_rope",
    )(u, *tables)


def _attn_kernel(sink_ref, *refs, has_local, nb):
    if has_local:
        q_ref, g_ref, kvp_ref, kvc_ref, kvn_ref, cos_ref, sina_ref, sinb_ref, kvx_ref, o_ref = refs
    else:
        q_ref, g_ref, kvx_ref, o_ref = refs
    tq = q_ref.shape[1]
    i = pl.program_id(1)
    q = q_ref[0]
    if has_local:
        rep = ATT_WIDTH // LANES
        cos, sina, sinb = (jnp.concatenate([r[...]] * rep, axis=1) for r in (cos_ref, sina_ref, sinb_ref))
        q = _rope(q, cos, sina, sinb)
    q = q * (ATT_HEAD ** -0.5 * LOG2E)
    lo = lax.broadcasted_iota(jnp.int32, (tq, LANES), 1) < HALF
    if has_local:
        a = lax.broadcasted_iota(jnp.int32, (tq, WINDOW), 0)
        c = lax.broadcasted_iota(jnp.int32, (tq, WINDOW), 1)
        ok_prev = (c >= a) & (i > 0)
        ok_next = (c <= a) & (i < nb - 1)
    def scores(hk):
        kcol = slice(hk * OD_KV_W, hk * OD_KV_W + LANES)
        vcol = slice(hk * OD_KV_W + LANES, (hk + 1) * OD_KV_W)
        qs = []
        for pair in range(ATT_GROUP // 2):
            grp = q[:, (hk * (ATT_GROUP // 2) + pair) * LANES:(hk * (ATT_GROUP // 2) + pair + 1) * LANES]
            qs += [jnp.where(lo, grp, 0.0), jnp.where(lo, 0.0, grp)]
        lhs = jnp.concatenate(qs, axis=0).astype(BF16)
        kx, vx = kvx_ref[0, :, kcol].astype(BF16), kvx_ref[0, :, vcol].astype(BF16)
        if has_local:
            kall = jnp.concatenate([kvp_ref[0, :, kcol], kvc_ref[0, :, kcol], kvn_ref[0, :, kcol], kx], axis=0)
            vall = jnp.concatenate([kvp_ref[0, :, vcol], kvc_ref[0, :, vcol], kvn_ref[0, :, vcol], vx], axis=0)
        else:
            kall, vall = kx, vx
        return _bdot_nt(lhs, kall), vall

    outs = []
    nxt = scores(0)
    for hk in range(ATT_KV_HEADS):
        s, vall = nxt
        if hk + 1 < ATT_KV_HEADS:
            nxt = scores(hk + 1)
        lov = lax.broadcasted_iota(jnp.int32, vall.shape, 1) < HALF
        zv = jnp.zeros_like(vall)
        v2 = jnp.concatenate([jnp.where(lov, vall, zv), jnp.where(lov, zv, vall)], axis=0)
        ps, inv_l = [], []
        for g in range(ATT_GROUP):
            sg = s[g * tq:(g + 1) * tq]
            if has_local:
                sg = jnp.concatenate([jnp.where(ok_prev, sg[:, 0:tq], NEG_INF), sg[:, tq:2 * tq],
                                      jnp.where(ok_next, sg[:, 2 * tq:3 * tq], NEG_INF), sg[:, 3 * tq:]], axis=1)
            sink = sink_ref[hk, g] * LOG2E
            m = jnp.maximum(jnp.max(sg, axis=1, keepdims=True), sink)
            p = jnp.exp2(sg - m)
            inv_l.append(1.0 / (jnp.sum(p, axis=1, keepdims=True) + jnp.exp2(sink - m)))
            ps.append(p.astype(BF16))
        pv = jnp.dot(jnp.concatenate([jnp.concatenate(ps[2 * pr:2 * pr + 2], axis=1) for pr in range(ATT_GROUP // 2)],
                                     axis=0), v2, preferred_element_type=F32)
        for pr in range(ATT_GROUP // 2):
            outs.append(pv[pr * tq:(pr + 1) * tq] * jnp.where(lo, inv_l[2 * pr], inv_l[2 * pr + 1]))
    g = g_ref[0]
    o_ref[0] = (jnp.concatenate(outs, axis=1) * (g * jax.nn.sigmoid(g))).astype(BF16)


def sink_attention(u, kv_local, tables, u_ctx, sink):
    bsz, t, _ = u.shape
    tq = ATT_BLOCK
    nb = t // tq
    tx = u_ctx.shape[1]
    kvw = ATT_KV_HEADS * OD_KV_W
    has_local = kv_local is not None
    in_specs = [pl.BlockSpec(memory_space=pltpu.SMEM),
                pl.BlockSpec((1, tq, ATT_WIDTH), lambda b, i: (b, i, OD_Q_COL // ATT_WIDTH)),
                pl.BlockSpec((1, tq, ATT_WIDTH), lambda b, i: (b, i, OD_GATE_COL // ATT_WIDTH))]
    args = [sink.reshape(ATT_KV_HEADS, ATT_GROUP), u, u]
    if has_local:
        in_specs += [pl.BlockSpec((1, tq, kvw), lambda b, i: (b, jnp.maximum(i - 1, 0), 0)),
                     pl.BlockSpec((1, tq, kvw), lambda b, i: (b, i, 0)),
                     pl.BlockSpec((1, tq, kvw), lambda b, i: (b, jnp.minimum(i + 1, nb - 1), 0))]
        in_specs += [pl.BlockSpec((tq, LANES), lambda b, i: (i, 0))] * 3
        args += [kv_local] * 3 + list(tables)
    in_specs += [pl.BlockSpec((1, tx, kvw), lambda b, i: (b, 0, OD_KV_COL // kvw))]
    args += [u_ctx]
    return pl.pallas_call(
        functools.partial(_attn_kernel, has_local=has_local, nb=nb),
        grid=(bsz, nb),
        in_specs=in_specs,
        out_specs=pl.BlockSpec((1, tq, ATT_WIDTH), lambda b, i: (b, i, 0)),
        out_shape=jax.ShapeDtypeStruct((bsz, t, ATT_WIDTH), BF16),
        compiler_params=pltpu.CompilerParams(dimension_semantics=("arbitrary", "arbitrary"),
                                             vmem_limit_bytes=V7X_VMEM_LIMIT_BYTES),
        name="sink_attention",
    )(*args)


def pack_odd_w_in(w_in):
    nq, nkv = ATT_WIDTH, ATT_KV_HEADS * ATT_HEAD
    parts = [w_in[:, :nq], w_in[:, nq + 2 * nkv:]]
    for h in range(ATT_KV_HEADS):
        k = w_in[:, nq + h * ATT_HEAD:nq + (h + 1) * ATT_HEAD]
        v = w_in[:, nq + nkv + h * ATT_HEAD:nq + nkv + (h + 1) * ATT_HEAD]
        parts += [k, k, v, v]
    return jnp.concatenate(parts, 1).astype(BF16)


def pack_even_w_in(w_in):
    g1 = 4 * GDN_WIDTH
    r0 = GDN_IN
    parts = [w_in[:, :g1], w_in[:, r0:r0 + 4 * RWKV_WIDTH], w_in[:, r0 + 4 * RWKV_WIDTH:], w_in[:, g1:GDN_IN],
             jnp.zeros((D_MODEL, EV_COLS - EV_SMALL_COL - 4 * GDN_HEADS), F32)]
    return jnp.concatenate(parts, 1).astype(BF16)


def kernel(x, c, ctx, c_ctx, mod_w, mod_b, ln_g, ln_b, ev_w_in, ev_w_out, gdn_conv, gdn_a_log, gdn_dt_bias,
           gdn_norm_g, rwkv_mix, rwkv_w0, rwkv_w_up, rwkv_a0, rwkv_a_up, rwkv_k_k, rwkv_k_a, rwkv_r_k,
           rwkv_gn_g, rwkv_gn_b, od_w_in, od_w_out, od_sink):
    bsz, t_lat, d = x.shape
    t_ctx = ctx.shape[1]
    x_lat, x_ctx = x, ctx
    n_mod = -(-(bsz + 1) // 8) * 8
    cc = jax.nn.silu(jnp.concatenate([c, c_ctx[None], jnp.zeros((n_mod - bsz - 1, d), F32)], 0))
    mods = mod_all_layers(cc, mod_w, mod_b)
    for layer in range(DEPTH):
        need_ctx = layer < DEPTH - 1
        i = layer // 2
        mod = mods[layer]
        sh, sc, gt = (mod[:bsz, j * d:(j + 1) * d][:, None, :] for j in range(3))
        sh_c, sc_c, gt_c = (jnp.broadcast_to(mod[bsz, j * d:(j + 1) * d], (bsz, 1, d)) for j in range(3))
        if layer % 2 == 0:
            w_in = pack_even_w_in(ev_w_in[i])
            w_out = ev_w_out[i].astype(BF16)
            u_lat = inproj(x_lat, sc, sh, w_in, 1024, EV_TN)
            u_ctx = inproj(x_ctx.reshape(1, bsz * t_ctx, d), sc_c[:1], sh_c[:1], w_in, 1024, EV_TN).reshape(bsz, t_ctx, -1)
            small = slice(EV_SMALL_COL, EV_SMALL_COL + 4 * GDN_HEADS)
            og_lat, og_ctx = gdn_mixer(u_lat, gdn_bg(u_lat[..., small]), u_ctx, gdn_bg(u_ctx[..., small]),
                                       gdn_params(gdn_conv[i], gdn_a_log[i], gdn_dt_bias[i], gdn_norm_g[i]))
            or_lat, or_ctx = rwkv_mixer(u_lat, u_ctx, *rwkv_params(
                rwkv_mix[i], rwkv_w0[i], rwkv_w_up[i], rwkv_a0[i], rwkv_a_up[i], rwkv_k_k[i], rwkv_k_a[i],
                rwkv_r_k[i], rwkv_gn_g[i], rwkv_gn_b[i]))
            a_lat, a_ctx = [og_lat, or_lat], [og_ctx, or_ctx]
        else:
            w_in = pack_odd_w_in(od_w_in[i])
            w_out = od_w_out[i].astype(BF16)
            u_lat = inproj(x_lat, sc, sh, w_in, 1024, OD_TN)
            u_ctx = inproj(x_ctx.reshape(1, bsz * t_ctx, d), sc_c[:1], sh_c[:1], w_in, 1024, OD_TN).reshape(bsz, t_ctx, -1)
            tables = rope_lane_tables(t_lat)
            o_lat = sink_attention(u_lat, kv_rope(u_lat, tables), tables, u_ctx, od_sink[i])
            o_ctx = sink_attention(u_ctx, None, None, u_ctx, od_sink[i]) if need_ctx else None
            a_lat, a_ctx = [o_lat], [o_ctx]
        x_lat = outproj_ln(a_lat, w_out, x_lat, gt, ln_g[layer], ln_b[layer])
        if need_ctx:
            x_ctx = outproj_ln(a_ctx, w_out, x_ctx, gt_c, ln_g[layer], ln_b[layer])
    return x_lat
```

```python
import functools

import jax
import jax.numpy as jnp
import numpy as np
from jax import lax
from jax.experimental import pallas as pl
from jax.experimental.pallas import tpu as pltpu

F32 = jnp.float32
BF16 = jnp.bfloat16

D_MODEL = 2048
DEPTH = 4
GRID_W = 64
GDN_WIDTH = D_MODEL // 2
GDN_HEADS = 8
GDN_CHUNK = 128
RWKV_WIDTH = D_MODEL // 2
RWKV_HEADS = 16
RWKV_CHUNK = 64
RWKV_GN_EPS = 64e-5
ATT_HEAD = 64
ATT_Q_HEADS = D_MODEL // ATT_HEAD
ATT_KV_HEADS = ATT_Q_HEADS // 8
ATT_GROUP = ATT_Q_HEADS // ATT_KV_HEADS
ATT_WIDTH = ATT_Q_HEADS * ATT_HEAD
WINDOW = 128
ATT_BLOCK = 128
ROPE_BASE = 10000.0
ROPE_FREQS = ATT_HEAD // 4
NEG_INF = -1e30
LOG2E = 1.4426950408889634
DEEPNORM_ALPHA = (2 * DEPTH) ** 0.25
LN_EPS = 1e-5
GDN_IN = 4 * GDN_WIDTH + 4 * GDN_HEADS

LANES = 128
PREP_ROWS = 256
HALF = 64
V7X_VMEM_LIMIT_BYTES = 56 * 1024 * 1024
EV_GDN_COL = 0
EV_RWKV_COL = 4 * GDN_WIDTH
EV_LORA_COL = EV_RWKV_COL + 4 * RWKV_WIDTH
EV_SMALL_COL = EV_LORA_COL + LANES
EV_COLS = EV_SMALL_COL + LANES
EV_TN = 1408
OD_TN = 1280


def _mod_kernel(a_ref, w_ref, b_ref, o_ref):
    o_ref[0] = jnp.dot(a_ref[...].astype(BF16), w_ref[0].astype(BF16), preferred_element_type=F32) + b_ref[0]


def mod_all_layers(a, w, b, tn=512):
    m, k = a.shape
    nl, _, n = w.shape
    return pl.pallas_call(
        _mod_kernel,
        grid=(nl, n // tn),
        in_specs=[pl.BlockSpec((m, k), lambda l, j: (0, 0)), pl.BlockSpec((1, k, tn), lambda l, j: (l, 0, j)),
                  pl.BlockSpec((1, 1, tn), lambda l, j: (l, 0, j))],
        out_specs=pl.BlockSpec((1, m, tn), lambda l, j: (l, 0, j)),
        out_shape=jax.ShapeDtypeStruct((nl, m, n), F32),
        compiler_params=pltpu.CompilerParams(dimension_semantics=("arbitrary", "arbitrary"),
                                             vmem_limit_bytes=V7X_VMEM_LIMIT_BYTES),
        name="mod_matmul",
    )(a, w, b.reshape(nl, 1, n))


def _inproj_kernel(x_ref, sc_ref, sh_ref, w_ref, o_ref, h_ref):
    @pl.when(pl.program_id(2) == 0)
    def _():
        h_ref[...] = (x_ref[0] * (1.0 + sc_ref[0]) + sh_ref[0]).astype(BF16)

    o_ref[0] = jnp.dot(h_ref[...], w_ref[...], preferred_element_type=F32)


def inproj(x, sc, sh, w, tm, tn):
    bsz, t, d = x.shape
    n = w.shape[1]
    tm = min(tm, t)
    assert t % tm == 0 and n % tn == 0
    return pl.pallas_call(
        _inproj_kernel,
        grid=(bsz, t // tm, n // tn),
        in_specs=[pl.BlockSpec((1, tm, d), lambda b, i, j: (b, i, 0)),
                  pl.BlockSpec((1, 1, d), lambda b, i, j: (b, 0, 0)),
                  pl.BlockSpec((1, 1, d), lambda b, i, j: (b, 0, 0)),
                  pl.BlockSpec((d, tn), lambda b, i, j: (0, j))],
        out_specs=pl.BlockSpec((1, tm, tn), lambda b, i, j: (b, i, j)),
        out_shape=jax.ShapeDtypeStruct((bsz, t, n), F32),
        scratch_shapes=[pltpu.VMEM((tm, d), BF16)],
        compiler_params=pltpu.CompilerParams(dimension_semantics=("arbitrary", "arbitrary", "arbitrary"),
                                             vmem_limit_bytes=V7X_VMEM_LIMIT_BYTES),
        name="inproj",
    )(x, sc, sh, w)


def _outproj_kernel(*refs, n_a):
    a_refs, (w_ref, x_ref, gt_ref, g_ref, b_ref, o_ref) = refs[:n_a], refs[n_a:]
    tm = x_ref.shape[1]
    halves = [slice(0, tm // 2), slice(tm // 2, tm)] if tm % 16 == 0 else [slice(0, tm)]
    ys = []
    for rows in halves:
        y = None
        k0 = 0
        for a_ref in a_refs:
            kw = a_ref.shape[2]
            term = jnp.dot(a_ref[0, rows, :].astype(BF16), w_ref[k0:k0 + kw, :], preferred_element_type=F32)
            y = term if y is None else y + term
            k0 += kw
        ys.append(y)
    for rows, y in zip(halves, ys):
        z = DEEPNORM_ALPHA * x_ref[0, rows, :] + gt_ref[0] * y
        mu = jnp.mean(z, axis=1, keepdims=True)
        zc = z - mu
        var = jnp.mean(zc * zc, axis=1, keepdims=True)
        o_ref[0, rows, :] = zc * lax.rsqrt(var + LN_EPS) * g_ref[...] + b_ref[...]


def outproj_ln(a_list, w, x, gt, ln_g, ln_b, tm=512):
    bsz, t, d = x.shape
    tm = min(tm, t)
    in_specs = [pl.BlockSpec((1, tm, a.shape[2]), lambda b, i: (b, i, 0)) for a in a_list] + [
        pl.BlockSpec(w.shape, lambda b, i: (0, 0)),
        pl.BlockSpec((1, tm, d), lambda b, i: (b, i, 0)),
        pl.BlockSpec((1, 1, d), lambda b, i: (b, 0, 0)),
        pl.BlockSpec((1, d), lambda b, i: (0, 0)),
        pl.BlockSpec((1, d), lambda b, i: (0, 0))]
    return pl.pallas_call(
        functools.partial(_outproj_kernel, n_a=len(a_list)),
        grid=(bsz, t // tm),
        in_specs=in_specs,
        out_specs=pl.BlockSpec((1, tm, d), lambda b, i: (b, i, 0)),
        out_shape=jax.ShapeDtypeStruct((bsz, t, d), F32),
        compiler_params=pltpu.CompilerParams(dimension_semantics=("arbitrary", "arbitrary"),
                                             vmem_limit_bytes=V7X_VMEM_LIMIT_BYTES),
        name="outproj_ln",
    )(*a_list, w, x, gt, ln_g.reshape(1, d), ln_b.reshape(1, d))


def _bdot(a, b):
    return jnp.dot(a.astype(BF16), b.astype(BF16), preferred_element_type=F32)


def _bdot_nt(a, b):
    return lax.dot_general(a.astype(BF16), b.astype(BF16), (((1,), (1,)), ((), ())), preferred_element_type=F32)


def _bdot_tn(a, b):
    return lax.dot_general(a.astype(BF16), b.astype(BF16), (((0,), (0,)), ((), ())), preferred_element_type=F32)


(PV_MR0, PV_MR1, PV_MK0, PV_MK1, PV_MV0, PV_MV1, PV_MG0, PV_MG1, PV_W00, PV_W01, PV_A00, PV_A01,
 PV_KK, PV_KA, PV_RK, PV_GNG, PV_GNB) = range(17)
PV_ROWS = 24
MK_ABD, MK_P, MK_Y, MK_L1 = 0, 1, 2, 3
MK_LV = 4
N_LV = 5
MK_PER_DIR = MK_LV + N_LV
MK_EYE = 2 * MK_PER_DIR
MK_BD = MK_EYE + 1
N_MASKS = MK_BD + 1


def rwkv_masks():
    ri = np.arange(128)[:, None]
    ci = np.arange(128)[None, :]
    i, j = ri & 63, ci & 63
    same = (ri < 64) == (ci < 64)
    out = np.zeros((N_MASKS, 128, 128), np.float32)
    for d in range(2):
        strict = (j < i) if d == 0 else (j > i)
        incl = (j <= i) if d == 0 else (j >= i)
        base = d * MK_PER_DIR
        out[base + MK_ABD] = strict & same
        out[base + MK_P] = strict & (ci >= 64)
        out[base + MK_Y] = incl
        for li, s in enumerate((1, 2, 4, 8, 16, 32)):
            blk = (i // (2 * s)) == (j // (2 * s))
            if d == 0:
                m = blk & ((i & s) != 0) & ((j & s) == 0)
            else:
                m = blk & ((i & s) == 0) & ((j & s) != 0)
            out[base + (MK_L1 if li == 0 else MK_LV + li - 1)] = m & same
    out[MK_EYE] = ri == ci
    out[MK_BD] = same
    return out


def _seg_sum(x, lo):
    s0 = jnp.sum(jnp.where(lo, x, 0.0), axis=1, keepdims=True)
    s1 = jnp.sum(jnp.where(lo, 0.0, x), axis=1, keepdims=True)
    return jnp.where(lo, s0, s1)


def _chunk_cumsum(x, chunk):
    rowm = lax.broadcasted_iota(jnp.int32, x.shape, 0) & (chunk - 1)
    s = 1
    while s < chunk:
        x = x + jnp.where(rowm >= s, pltpu.roll(x, s, axis=0), 0.0)
        s *= 2
    return x


def _interleave(*gens):
    live = list(gens)
    while live:
        for g in list(live):
            try:
                next(g)
            except StopIteration:
                live.remove(g)


def _rwkv_kernel(lr, lk, lv, lg, ll, cr, ck, cv, cg, cl, pv_ref, mixl_ref, wup_ref, aup_ref, mk_ref,
                 o_lat, o_ctx,
                 s_at0, s_at1, s_rt0, s_rt1, s_bt0, s_bt1, s_kt0, s_kt1, s_bh0, s_bh1, s_kh0, s_kh1, s_v,
                 s_g, s_bonus, s_yf, s_yb, s_w, s_tp, s_gr, s_et, s_state, *, t_lat, t_ctx, p1_chunks):
    C = RWKV_CHUNK
    s_at, s_rt, s_bt, s_kt, s_bh, s_kh = ((s_at0, s_at1), (s_rt0, s_rt1), (s_bt0, s_bt1), (s_kt0, s_kt1),
                                          (s_bh0, s_bh1), (s_kh0, s_kh1))
    s_y = (s_yf, s_yb)
    lo64 = lax.broadcasted_iota(jnp.int32, (C, LANES), 1) < HALF
    bd_ones = mk_ref[MK_BD]

    def pvrow(i):
        return pv_ref[pl.ds(i, 1), :]

    def prep_tile(xs, tseq, it, row0, ch0):
        xr, xk, xv, xg, xl = xs
        R = min(PREP_ROWS, tseq)
        s = it * R
        lo = lax.broadcasted_iota(jnp.int32, (R, LANES), 1) < HALF
        rows = lax.broadcasted_iota(jnp.int32, (R, LANES), 0)

        def shifted(ref, m0, m1):
            cur = ref[0, s:s + R, :]
            prev = ref[0, s - 1:s - 1 + R, :] if s > 0 else jnp.where(rows == 0, 0.0, pltpu.roll(cur, 1, axis=0))
            nxt = (ref[0, s + 1:s + 1 + R, :] if s + R < tseq
                   else jnp.where(rows == R - 1, 0.0, pltpu.roll(cur, R - 1, axis=0)))
            return cur + m0 * (prev - cur) + m1 * (nxt - cur)

        r = shifted(xr, pvrow(PV_MR0), pvrow(PV_MR1))
        k = shifted(xk, pvrow(PV_MK0), pvrow(PV_MK1))
        v = shifted(xv, pvrow(PV_MV0), pvrow(PV_MV1))
        g = shifted(xg, pvrow(PV_MG0), pvrow(PV_MG1))
        lor = shifted(xl, mixl_ref[0:1, :], mixl_ref[1:2, :])
        th = jnp.tanh(lor)
        kkf = k * pvrow(PV_KK)
        kk = kkf * lax.rsqrt(_seg_sum(kkf * kkf, lo) + 1e-6)
        ka = pvrow(PV_KA)
        out = slice(row0 + s, row0 + s + R)
        kd_sum = None
        for d in range(2):
            w_pre = pvrow(PV_W00 + d) + _bdot(th, wup_ref[d])
            lw = (-np.exp(-0.5)) * jax.nn.sigmoid(w_pre)
            ag = jax.nn.sigmoid(pvrow(PV_A00 + d) + _bdot(lor, aup_ref[d]))
            kd = k * (1.0 + (ag - 1.0) * ka)
            bdv = kk * ag
            kd_sum = kd if kd_sum is None else kd_sum + kd
            c_fwd = _chunk_cumsum(lw, C)
            tot = jnp.concatenate([jnp.broadcast_to(c_fwd[(c + 1) * C - 1:(c + 1) * C, :], (C, LANES))
                                   for c in range(R // C)], axis=0)
            c_bwd = tot - c_fwd + lw
            cw, rest = (c_fwd, c_bwd - lw) if d == 0 else (c_bwd, c_fwd - lw)
            e_in = jnp.exp(-cw)
            e_out = jnp.exp(rest)
            s_at[d][out, :] = (-kk * jnp.exp(cw - lw)).astype(BF16)
            s_rt[d][out, :] = (r * jnp.exp(cw)).astype(BF16)
            s_bt[d][out, :] = (bdv * e_in).astype(BF16)
            s_kt[d][out, :] = (kd * e_in).astype(BF16)
            s_bh[d][out, :] = (bdv * e_out).astype(BF16)
            s_kh[d][out, :] = (kd * e_out).astype(BF16)
            for c in range(R // C):
                last = c * C + (C - 1 if d == 0 else 0)
                n = ch0 + it * (R // C) + c
                s_et[d, n:n + 1, :] = jnp.exp(cw[last:last + 1, :])
        s_v[out, :] = v.astype(BF16)
        s_g[out, :] = g
        s_bonus[out, :] = _seg_sum(r * kd_sum * pvrow(PV_RK), lo) * v

    def prep_stages(xs, tseq, tiles, row0, ch0):
        for it in tiles:
            prep_tile(xs, tseq, it, row0, ch0)
            yield

    def p1_stages(probs, row0, ch0):
        ds = [d for _, d in probs]
        rows = [pl.ds(pl.multiple_of(row0 + n * C, C), C) for n, _ in probs]
        a2s, gms = [], []
        for (n, d), rw in zip(probs, rows):
            at, rt = s_at[d][rw, :], s_rt[d][rw, :]
            zero = jnp.zeros_like(at)
            a2 = jnp.concatenate([jnp.where(lo64, at, zero), jnp.where(lo64, zero, at)], axis=0)
            l1 = jnp.concatenate([a2, jnp.where(lo64, rt, zero), jnp.where(lo64, zero, rt)], axis=0)
            a2s.append(a2)
            gms.append(_bdot_nt(l1, jnp.concatenate([s_bt[d][rw, :], s_kt[d][rw, :]], axis=0)))
        yield
        ps, a_list = [], []
        for (n, d), rw, gm in zip(probs, rows, gms):
            base = d * MK_PER_DIR
            ga = gm[0:2 * C]
            v = s_v[rw, :]
            ps.append(_bdot(ga * mk_ref[base + MK_P], jnp.concatenate([v, v], axis=0)))
            s_gr[d, ch0 + n] = (gm[2 * C:4 * C] * mk_ref[base + MK_Y]).astype(BF16)
            ga_sw = jnp.concatenate([ga[0:C], pltpu.roll(ga[C:2 * C], HALF, axis=1)], axis=0)
            a_list.append(ga_sw * mk_ref[base + MK_ABD])
        yield
        xs = [mk_ref[MK_EYE] + a * mk_ref[d * MK_PER_DIR + MK_L1] for a, d in zip(a_list, ds)]
        for li in range(N_LV):
            t1 = [_bdot(x, a * mk_ref[d * MK_PER_DIR + MK_LV + li]) for x, a, d in zip(xs, a_list, ds)]
            yield
            xs = [x + _bdot(t, x) for x, t in zip(xs, t1)]
            yield
        for (n, d), x, a2, p in zip(probs, xs, a2s, ps):
            wtp = _bdot(x, jnp.concatenate([a2.astype(F32), p], axis=1))
            s_w[d, ch0 + n] = wtp[:, 0:LANES].astype(BF16)
            s_tp[d, ch0 + n] = jnp.where(lo64, wtp[0:C, LANES:2 * LANES], wtp[C:2 * C, LANES:2 * LANES])
        yield

    def p2_stages(steps, row0, ch0):
        for ns0 in steps:
            ns = [ch0 + n for n in ns0]
            rows = [pl.ds(pl.multiple_of(row0 + ns0[d] * C, C), C) for d in range(2)]
            st = [s_state[d] for d in range(2)]
            xs = [_bdot_nt(jnp.concatenate([s_w[d, ns[d]], s_rt[d][rows[d], :]], axis=0), st[d]) for d in range(2)]
            yield
            uv = [jnp.concatenate([(xs[d][0:C] + xs[d][C:2 * C] + s_tp[d, ns[d]]).astype(BF16), s_v[rows[d], :]], axis=0)
                  for d in range(2)]
            yb = [_bdot(s_gr[d, ns[d]], uv[d]) for d in range(2)]
            for d in range(2):
                s_y[d][rows[d], :] = xs[d][2 * C:3 * C] + jnp.where(lo64, yb[d][0:C], yb[d][C:2 * C])
                bk = jnp.concatenate([s_bh[d][rows[d], :], s_kh[d][rows[d], :]], axis=0)
                s_state[d] = st[d] * s_et[d, pl.ds(ns[d], 1), :] + bd_ones * _bdot_tn(uv[d], bk)
            yield

    def groups(tseq):
        nch = tseq // C
        per = min(p1_chunks, nch)
        probs = lambda g: [(g * per + j, 0) for j in range(per)] + [(nch - 1 - (g * per + j), 1) for j in range(per)]
        steps = lambda g: [(g * per + j, nch - 1 - (g * per + j)) for j in range(per)]
        return nch // per, per, probs, steps

    def epilogue_tile(o_ref, tseq, it, row0):
        R = min(PREP_ROWS, tseq)
        lo = lax.broadcasted_iota(jnp.int32, (R, LANES), 1) < HALF
        rows = slice(row0 + it * R, row0 + (it + 1) * R)
        y = s_yf[rows, :] + s_yb[rows, :]
        mu = _seg_sum(y, lo) * (1.0 / HALF)
        yc = y - mu
        var = _seg_sum(yc * yc, lo) * (1.0 / HALF)
        yn = yc * lax.rsqrt(var + RWKV_GN_EPS) * pvrow(PV_GNG) + pvrow(PV_GNB)
        g = s_g[rows, :]
        o_ref[0, it * R:(it + 1) * R, :] = ((yn + s_bonus[rows, :]) * (g * jax.nn.sigmoid(g))).astype(BF16)

    def epilogue_stages(o_ref, tseq, tiles, row0):
        for it in tiles:
            epilogue_tile(o_ref, tseq, it, row0)
            yield

    lat, ctx = (lr, lk, lv, lg, ll), (cr, ck, cv, cg, cl)
    row_c, ch_c = t_lat, t_lat // C
    s_state[...] = jnp.zeros_like(s_state)
    _interleave(prep_stages(ctx, t_ctx, range(t_ctx // min(PREP_ROWS, t_ctx)), row_c, ch_c))
    ng_c, _, probs_c, steps_c = groups(t_ctx)
    ng, per, probs, steps = groups(t_lat)
    ntile = t_lat // min(PREP_ROWS, t_lat)
    tpg = max(per * C // min(PREP_ROWS, t_lat), 1)
    first = sorted(set(list(range(min(tpg, ntile))) + list(range(max(ntile - tpg, 0), ntile))))
    rest = [t for t in range(ntile) if t not in first]

    def ctx_scan():
        for g in range(ng_c):
            yield from p1_stages(probs_c(g), row_c, ch_c)
            yield from p2_stages(steps_c(g), row_c, ch_c)

    _interleave(prep_stages(lat, t_lat, first, 0, 0), ctx_scan())
    _interleave(prep_stages(lat, t_lat, rest, 0, 0), p1_stages(probs(0), 0, 0),
                epilogue_stages(o_ctx, t_ctx, range(t_ctx // min(PREP_ROWS, t_ctx)), row_c))
    for g in range(ng - 1):
        _interleave(p1_stages(probs(g + 1), 0, 0), p2_stages(steps(g), 0, 0))
    done = [t for t in range(ntile) if per * C <= t * min(PREP_ROWS, t_lat) and (t + 1) * min(PREP_ROWS, t_lat) <= (ng - 1) * per * C]
    _interleave(p2_stages(steps(ng - 1), 0, 0), epilogue_stages(o_lat, t_lat, done, 0))
    _interleave(epilogue_stages(o_lat, t_lat, [t for t in range(ntile) if t not in done], 0))


def _col_block_map(b, p, *, off):
    return (b, 0, off + p)


def rwkv_mixer(u_lat, u_ctx, pv, mixl, wup, aup, p1_chunks=8):
    bsz, t_lat, _ = u_lat.shape
    t_ctx = u_ctx.shape[1]
    npair = RWKV_HEADS // 2
    c0 = EV_RWKV_COL // LANES
    nch = (t_lat + t_ctx) // RWKV_CHUNK
    t_all = t_lat + t_ctx
    masks = jnp.asarray(rwkv_masks())

    def seq_specs(t):
        return [pl.BlockSpec((1, t, LANES), functools.partial(_col_block_map, off=c0 + o * npair)) for o in range(4)] + [
            pl.BlockSpec((1, t, LANES), lambda b, p: (b, 0, EV_LORA_COL // LANES))]

    in_specs = seq_specs(t_lat) + seq_specs(t_ctx) + [
        pl.BlockSpec((PV_ROWS, LANES), lambda b, p: (0, p)),
        pl.BlockSpec((8, LANES), lambda b, p: (0, 0)),
        pl.BlockSpec((2, LANES, LANES), lambda b, p: (0, 0, p)),
        pl.BlockSpec((2, LANES, LANES), lambda b, p: (0, 0, p)),
        pl.BlockSpec((N_MASKS, LANES, LANES), lambda b, p: (0, 0, 0)),
    ]
    out_specs = [pl.BlockSpec((1, t_lat, LANES), lambda b, p: (b, 0, p)),
                 pl.BlockSpec((1, t_ctx, LANES), lambda b, p: (b, 0, p))]
    scratch = ([pltpu.VMEM((t_all, LANES), BF16) for _ in range(13)] + [pltpu.VMEM((t_all, LANES), F32) for _ in range(4)]
               + [pltpu.VMEM((2, nch, LANES, LANES), BF16), pltpu.VMEM((2, nch, RWKV_CHUNK, LANES), F32),
                  pltpu.VMEM((2, nch, LANES, LANES), BF16)]
               + [pltpu.VMEM((2, nch, LANES), F32), pltpu.VMEM((2, LANES, LANES), F32)])
    kern = functools.partial(_rwkv_kernel, t_lat=t_lat, t_ctx=t_ctx, p1_chunks=p1_chunks)
    return pl.pallas_call(
        kern,
        grid=(bsz, npair),
        in_specs=in_specs,
        out_specs=out_specs,
        out_shape=[jax.ShapeDtypeStruct((bsz, t_lat, RWKV_WIDTH), BF16), jax.ShapeDtypeStruct((bsz, t_ctx, RWKV_WIDTH), BF16)],
        scratch_shapes=scratch,
        compiler_params=pltpu.CompilerParams(dimension_semantics=("arbitrary", "arbitrary"),
                                             vmem_limit_bytes=V7X_VMEM_LIMIT_BYTES),
        name="rwkv7_chunked",
    )(*([u_lat] * 5), *([u_ctx] * 5), pv, mixl, wup, aup, masks)


def rwkv_params(mix, w0, w_up, a0, a_up, k_k, k_a, r_k, gn_g, gn_b):
    wd = RWKV_WIDTH
    rows = [mix[0, 0:wd], mix[1, 0:wd], mix[0, wd:2 * wd], mix[1, wd:2 * wd], mix[0, 2 * wd:3 * wd], mix[1, 2 * wd:3 * wd],
            mix[0, 3 * wd:4 * wd], mix[1, 3 * wd:4 * wd], w0[0], w0[1], a0[0], a0[1], k_k, k_a, r_k.reshape(-1), gn_g, gn_b]
    pv = jnp.concatenate([jnp.stack(rows), jnp.zeros((PV_ROWS - len(rows), wd), F32)], 0)
    mixl = jnp.concatenate([mix[:, 4 * wd:], jnp.zeros((6, LANES), F32)], 0)
    z = jnp.zeros((2, HALF, wd), F32)
    wup = jnp.concatenate([w_up, z], 1)
    aup = jnp.concatenate([z, a_up], 1)
    return pv, mixl, wup, aup


GP_CONV_Q, GP_CONV_K, GP_CONV_V = 0, 5, 10
GP_NORM, GP_SC_ALOG, GP_SC_DTB = 15, 16, 17
GP_ROWS = 24
GM_STRICT, GM_INCL, GM_L1, GM_LV = 0, 1, 2, 3
GN_LV = 6
GM_PER_DIR = GM_LV + GN_LV
GM_EYE = 2 * GM_PER_DIR
GN_MASKS = GM_EYE + 1
GDN_P1_CHUNKS = 4


def gdn_masks():
    i = np.arange(GDN_CHUNK)[:, None]
    j = np.arange(GDN_CHUNK)[None, :]
    out = np.zeros((GN_MASKS, GDN_CHUNK, GDN_CHUNK), np.float32)
    for d in range(2):
        base = d * GM_PER_DIR
        out[base + GM_STRICT] = (j < i) if d == 0 else (j > i)
        out[base + GM_INCL] = (j <= i) if d == 0 else (j >= i)
        for li in range(GN_LV + 1):
            s = 1 << li
            blk = (i // (2 * s)) == (j // (2 * s))
            m = blk & (((i & s) != 0) & ((j & s) == 0) if d == 0 else ((i & s) == 0) & ((j & s) != 0))
            out[base + (GM_L1 if li == 0 else GM_LV + li - 1)] = m
    out[GM_EYE] = i == j
    return out


def _gdn_kernel(lq, lk, lv, lz, lbg, cq, ck, cv, cz, cbg, gp_ref, mk_ref, o_lat, o_ctx,
                s_q, s_k, s_kb0, s_kb1, s_kbg0, s_kbg1, s_qg0, s_qg1, s_kg0, s_kg1, s_vb0, s_vb1,
                s_gc0, s_gc1, s_of, s_ob, s_tk, s_tvb, s_a, s_gl, s_state, *, t_lat, t_ctx):
    C = GDN_CHUNK
    s_kb, s_kbg, s_qg, s_kg, s_vb = (s_kb0, s_kb1), (s_kbg0, s_kbg1), (s_qg0, s_qg1), (s_kg0, s_kg1), (s_vb0, s_vb1)
    s_gc, s_o = (s_gc0, s_gc1), (s_of, s_ob)

    def gprow(i):
        return gp_ref[pl.ds(i, 1), :]

    def prep_tile(xs, tseq, it, row0, ch0):
        xq, xk, xv, xbg = xs
        R = min(PREP_ROWS, tseq)
        s = it * R

        def conv_silu(ref, prow0):
            rows = lax.broadcasted_iota(jnp.int32, (R, LANES), 0)
            acc = None
            for j in range(5):
                off = j - 2
                lo_r, hi_r = s + off, s + off + R
                if lo_r < 0:
                    x = jnp.where(rows >= -off, pltpu.roll(ref[0, 0:R, :], -off, axis=0), 0.0)
                elif hi_r > tseq:
                    x = jnp.where(rows < R - off, pltpu.roll(ref[0, tseq - R:tseq, :], R - off, axis=0), 0.0)
                else:
                    x = ref[0, lo_r:hi_r, :]
                term = gprow(prow0 + j) * x
                acc = term if acc is None else acc + term
            return acc * jax.nn.sigmoid(acc)

        qc = conv_silu(xq, GP_CONV_Q)
        kc = conv_silu(xk, GP_CONV_K)
        v = conv_silu(xv, GP_CONV_V)
        q = qc * (lax.rsqrt(jnp.sum(qc * qc, axis=1, keepdims=True) + 1e-6) * (LANES ** -0.5))
        k = kc * lax.rsqrt(jnp.sum(kc * kc, axis=1, keepdims=True) + 1e-6)
        bg = xbg[0, 0, s:s + R, :]
        lane4 = lax.broadcasted_iota(jnp.int32, bg.shape, 1)
        beta4 = jax.nn.sigmoid(bg)
        g4 = -jnp.exp(gprow(GP_SC_ALOG)[:, 0:4]) * jax.nn.softplus(bg + gprow(GP_SC_DTB)[:, 0:4])
        c_fwd = _chunk_cumsum(g4, C)
        tot = jnp.concatenate([jnp.broadcast_to(c_fwd[(c + 1) * C - 1:(c + 1) * C, :], (C, 4)) for c in range(R // C)],
                              axis=0)
        c_bwd = tot - c_fwd + g4
        fwd_lane = lane4 == 2
        gc4 = jnp.where(fwd_lane, c_fwd, c_bwd)
        eg4 = jnp.exp(gc4)
        er4 = jnp.exp(jnp.where(fwd_lane, c_bwd, c_fwd) - g4)
        out = slice(row0 + s, row0 + s + R)
        s_q[out, :] = q.astype(BF16)
        s_k[out, :] = k.astype(BF16)
        for d in range(2):
            beta, eg = beta4[:, d:d + 1], eg4[:, 2 + d:3 + d]
            kb = k * beta
            s_kb[d][out, :] = kb.astype(BF16)
            s_kbg[d][out, :] = (kb * eg).astype(BF16)
            s_qg[d][out, :] = (q * eg).astype(BF16)
            s_kg[d][out, :] = (k * er4[:, 2 + d:3 + d]).astype(BF16)
            s_vb[d][out, :] = (v * beta).astype(BF16)
            s_gc[d][out, :] = jnp.broadcast_to(gc4[:, 2 + d:3 + d], (R, LANES))
            for c in range(R // C):
                last = c * C + (C - 1 if d == 0 else 0)
                n = ch0 + it * (R // C) + c
                s_gl[d, n:n + 1, :] = jnp.broadcast_to(eg[last:last + 1, :], (1, LANES))

    def prep_stages(xs, tseq, tiles, row0, ch0):
        for it in tiles:
            prep_tile(xs, tseq, it, row0, ch0)
            yield

    def p1_stages(probs, row0, ch0):
        ds = [d for _, d in probs]
        rows = [pl.ds(pl.multiple_of(row0 + n * C, C), C) for n, _ in probs]
        gms = [_bdot_nt(jnp.concatenate([s_kb[d][rw, :], s_q[rw, :]], axis=0), s_k[rw, :])
               for (n, d), rw in zip(probs, rows)]
        yield
        a_list = []
        for (n, d), rw, gm in zip(probs, rows, gms):
            base = d * GM_PER_DIR
            gc = s_gc[d][rw, :]
            m_incl = mk_ref[base + GM_INCL]
            dec = jnp.exp((gc - gc.T) * m_incl)
            s_a[d, ch0 + n] = (gm[C:2 * C] * dec * m_incl).astype(BF16)
            a_list.append(-(gm[0:C] * dec * mk_ref[base + GM_STRICT]))
        xs = [mk_ref[GM_EYE] + a * mk_ref[d * GM_PER_DIR + GM_L1] for a, d in zip(a_list, ds)]
        for li in range(GN_LV):
            t1 = [_bdot(x, a * mk_ref[d * GM_PER_DIR + GM_LV + li]) for x, a, d in zip(xs, a_list, ds)]
            yield
            xs = [x + _bdot(t, x) for x, t in zip(xs, t1)]
            yield
        for (n, d), rw, x in zip(probs, rows, xs):
            tkv = _bdot(x, jnp.concatenate([s_kbg[d][rw, :], s_vb[d][rw, :]], axis=1))
            s_tk[d, ch0 + n] = tkv[:, 0:LANES].astype(BF16)
            s_tvb[d, ch0 + n] = tkv[:, LANES:2 * LANES]
        yield

    def p2_stages(steps, row0, ch0):
        for ns0 in steps:
            ns = [ch0 + n for n in ns0]
            rows = [pl.ds(pl.multiple_of(row0 + ns0[d] * C, C), C) for d in range(2)]
            st = [s_state[d] for d in range(2)]
            aq = [_bdot(jnp.concatenate([s_tk[d, ns[d]], s_qg[d][rows[d], :]], axis=0), st[d]) for d in range(2)]
            yield
            vnew = [s_tvb[d, ns[d]] - aq[d][0:C] for d in range(2)]
            oo = [aq[d][C:2 * C] + _bdot(s_a[d, ns[d]], vnew[d]) for d in range(2)]
            for d in range(2):
                s_o[d][rows[d], :] = oo[d]
                s_state[d] = st[d] * s_gl[d, pl.ds(ns[d], 1), :] + _bdot_tn(s_kg[d][rows[d], :], vnew[d])
            yield

    def groups(tseq):
        nch = tseq // C
        per = min(GDN_P1_CHUNKS, nch)
        probs = lambda g: [(g * per + j, 0) for j in range(per)] + [(nch - 1 - (g * per + j), 1) for j in range(per)]
        steps = lambda g: [(g * per + j, nch - 1 - (g * per + j)) for j in range(per)]
        return nch // per, per, probs, steps

    def epilogue_tile(o_ref, z_ref, tseq, it, row0):
        R = min(PREP_ROWS, tseq)
        rows = slice(row0 + it * R, row0 + (it + 1) * R)
        o = s_of[rows, :] + s_ob[rows, :]
        o = o * lax.rsqrt(jnp.mean(o * o, axis=1, keepdims=True) + 1e-6) * gprow(GP_NORM)
        z = z_ref[0, it * R:(it + 1) * R, :]
        o_ref[0, it * R:(it + 1) * R, :] = (o * (z * jax.nn.sigmoid(z))).astype(BF16)

    def epilogue_stages(o_ref, z_ref, tseq, tiles, row0):
        for it in tiles:
            epilogue_tile(o_ref, z_ref, tseq, it, row0)
            yield

    lat, ctx = (lq, lk, lv, lbg), (cq, ck, cv, cbg)
    row_c, ch_c = t_lat, t_lat // C
    s_state[...] = jnp.zeros_like(s_state)
    _interleave(prep_stages(ctx, t_ctx, range(t_ctx // min(PREP_ROWS, t_ctx)), row_c, ch_c))
    ng_c, _, probs_c, steps_c = groups(t_ctx)
    ng, per, probs, steps = groups(t_lat)
    ntile = t_lat // min(PREP_ROWS, t_lat)
    tpg = max(per * C // min(PREP_ROWS, t_lat), 1)
    first = sorted(set(list(range(min(tpg, ntile))) + list(range(max(ntile - tpg, 0), ntile))))
    rest = [t for t in range(ntile) if t not in first]

    def ctx_scan():
        for g in range(ng_c):
            yield from p1_stages(probs_c(g), row_c, ch_c)
            yield from p2_stages(steps_c(g), row_c, ch_c)

    _interleave(prep_stages(lat, t_lat, first, 0, 0), ctx_scan())
    _interleave(prep_stages(lat, t_lat, rest, 0, 0), p1_stages(probs(0), 0, 0),
                epilogue_stages(o_ctx, cz, t_ctx, range(t_ctx // min(PREP_ROWS, t_ctx)), row_c))
    if ng > 1:
        def body(g, carry):
            _interleave(p1_stages(probs(g + 1), 0, 0), p2_stages(steps(g), 0, 0))
            return carry

        lax.fori_loop(0, ng - 1, body, 0)
    done = [t for t in range(ntile) if per * C <= t * min(PREP_ROWS, t_lat) and (t + 1) * min(PREP_ROWS, t_lat) <= (ng - 1) * per * C]
    _interleave(p2_stages(steps(ng - 1), 0, 0), epilogue_stages(o_lat, lz, t_lat, done, 0))
    _interleave(epilogue_stages(o_lat, lz, t_lat, [t for t in range(ntile) if t not in done], 0))


def gdn_mixer(u_lat, bg_lat, u_ctx, bg_ctx, gp):
    bsz, t_lat, _ = u_lat.shape
    t_ctx = u_ctx.shape[1]
    nh = GDN_HEADS
    masks = jnp.asarray(gdn_masks())

    def seq_specs(t):
        return [pl.BlockSpec((1, t, LANES), functools.partial(_col_block_map, off=o * nh)) for o in range(4)] + [
            pl.BlockSpec((1, 1, t, 4), lambda b, h: (b, h, 0, 0))]

    in_specs = seq_specs(t_lat) + seq_specs(t_ctx) + [
        pl.BlockSpec((GP_ROWS, LANES), lambda b, h: (0, h)),
        pl.BlockSpec((GN_MASKS, LANES, LANES), lambda b, h: (0, 0, 0)),
    ]
    out_specs = [pl.BlockSpec((1, t_lat, LANES), lambda b, h: (b, 0, h)),
                 pl.BlockSpec((1, t_ctx, LANES), lambda b, h: (b, 0, h))]
    nch = (t_lat + t_ctx) // GDN_CHUNK
    t_all = t_lat + t_ctx
    scratch = ([pltpu.VMEM((t_all, LANES), BF16) for _ in range(12)] + [pltpu.VMEM((t_all, LANES), F32) for _ in range(4)]
               + [pltpu.VMEM((2, nch, LANES, LANES), BF16), pltpu.VMEM((2, nch, LANES, LANES), F32),
                  pltpu.VMEM((2, nch, LANES, LANES), BF16),
                  pltpu.VMEM((2, max(nch, 8), LANES), F32), pltpu.VMEM((2, LANES, LANES), F32)])
    kern = functools.partial(_gdn_kernel, t_lat=t_lat, t_ctx=t_ctx)
    return pl.pallas_call(
        kern,
        grid=(bsz, nh),
        in_specs=in_specs,
        out_specs=out_specs,
        out_shape=[jax.ShapeDtypeStruct((bsz, t_lat, GDN_WIDTH), BF16), jax.ShapeDtypeStruct((bsz, t_ctx, GDN_WIDTH), BF16)],
        scratch_shapes=scratch,
        compiler_params=pltpu.CompilerParams(dimension_semantics=("arbitrary", "arbitrary"),
                                             vmem_limit_bytes=V7X_VMEM_LIMIT_BYTES),
        name="gdn_chunked",
    )(*([u_lat] * 4), bg_lat, *([u_ctx] * 4), bg_ctx, gp, masks)


def gdn_params(conv_w, a_log, dt_bias, norm_g):
    wd = GDN_WIDTH

    def scalar_row(x):
        return jnp.zeros((GDN_HEADS, LANES), F32).at[:, 2].set(x[0]).at[:, 3].set(x[1]).reshape(-1)

    rows = [conv_w[j, o * wd:(o + 1) * wd] for o in range(3) for j in range(5)]
    rows += [jnp.tile(norm_g, GDN_HEADS), scalar_row(a_log), scalar_row(dt_bias)]
    return jnp.concatenate([jnp.stack(rows), jnp.zeros((GP_ROWS - len(rows), wd), F32)], 0)


def gdn_bg(u_small):
    bsz, t, _ = u_small.shape
    x = u_small.reshape(bsz, t, 2, 2, GDN_HEADS)
    return jnp.transpose(x, (0, 4, 1, 2, 3)).reshape(bsz, GDN_HEADS, t, 4)


OD_Q_COL, OD_GATE_COL, OD_KV_COL = 0, ATT_WIDTH, 2 * ATT_WIDTH
OD_KV_W = 4 * ATT_HEAD
OD_COLS = OD_KV_COL + ATT_KV_HEADS * OD_KV_W


def rope_lane_tables(t):
    pos = np.arange(t)
    inv = ROPE_BASE ** (-np.arange(ROPE_FREQS, dtype=np.float64) / ROPE_FREQS)
    ang_row = (pos // GRID_W)[:, None] * inv
    ang_col = (pos % GRID_W)[:, None] * inv
    ang = np.concatenate([ang_row, ang_row, ang_col, ang_col] * 2, axis=1)
    first = (np.arange(LANES) % (2 * ROPE_FREQS)) < ROPE_FREQS
    cos, sin = np.cos(ang), np.sin(ang)
    return (jnp.asarray(cos, F32), jnp.asarray(np.where(first, -sin, 0.0), F32), jnp.asarray(np.where(first, 0.0, sin), F32))


def _rope(x, cos, sina, sinb):
    n = x.shape[1]
    return x * cos + pltpu.roll(x, n - ROPE_FREQS, axis=1) * sina + pltpu.roll(x, ROPE_FREQS, axis=1) * sinb


def _kv_rope_kernel(u_ref, cos_ref, sina_ref, sinb_ref, o_ref):
    x = u_ref[0]
    cos, sina, sinb = cos_ref[...], sina_ref[...], sinb_ref[...]
    parts = []
    for h in range(ATT_KV_HEADS):
        k2 = x[:, h * OD_KV_W:h * OD_KV_W + LANES]
        parts += [_rope(k2, cos, sina, sinb), x[:, h * OD_KV_W + LANES:(h + 1) * OD_KV_W]]
    o_ref[0] = jnp.concatenate(parts, axis=1).astype(BF16)


def kv_rope(u, tables, tr=512):
    bsz, t, _ = u.shape
    tr = min(tr, t)
    kvw = ATT_KV_HEADS * OD_KV_W
    tab = pl.BlockSpec((tr, LANES), lambda b, i: (i, 0))
    return pl.pallas_call(
        _kv_rope_kernel,
        grid=(bsz, t // tr),
        in_specs=[pl.BlockSpec((1, tr, kvw), lambda b, i: (b, i, OD_KV_COL // kvw)), tab, tab, tab],
        out_specs=pl.BlockSpec((1, tr, kvw), lambda b, i: (b, i, 0)),
        out_shape=jax.ShapeDtypeStruct((bsz, t, kvw), BF16),
        compiler_params=pltpu.CompilerParams(dimension_semantics=("arbitrary", "arbitrary")),
        name="kv_rope",
    )(u, *tables)


def _attn_kernel(sink_ref, *refs, has_local, nb):
    if has_local:
        q_ref, g_ref, kvp_ref, kvc_ref, kvn_ref, cos_ref, sina_ref, sinb_ref, kvx_ref, o_ref = refs
    else:
        q_ref, g_ref, kvx_ref, o_ref = refs
    tq = q_ref.shape[1]
    i = pl.program_id(1)
    q = q_ref[0]
    if has_local:
        rep = ATT_WIDTH // LANES
        cos, sina, sinb = (jnp.concatenate([r[...]] * rep, axis=1) for r in (cos_ref, sina_ref, sinb_ref))
        q = _rope(q, cos, sina, sinb)
    q = q * (ATT_HEAD ** -0.5 * LOG2E)
    lo = lax.broadcasted_iota(jnp.int32, (tq, LANES), 1) < HALF
    if has_local:
        a = lax.broadcasted_iota(jnp.int32, (tq, WINDOW), 0)
        c = lax.broadcasted_iota(jnp.int32, (tq, WINDOW), 1)
        ok_prev = (c >= a) & (i > 0)
        ok_next = (c <= a) & (i < nb - 1)
    def scores(hk):
        kcol = slice(hk * OD_KV_W, hk * OD_KV_W + LANES)
        vcol = slice(hk * OD_KV_W + LANES, (hk + 1) * OD_KV_W)
        qs = []
        for pair in range(ATT_GROUP // 2):
            grp = q[:, (hk * (ATT_GROUP // 2) + pair) * LANES:(hk * (ATT_GROUP // 2) + pair + 1) * LANES]
            qs += [jnp.where(lo, grp, 0.0), jnp.where(lo, 0.0, grp)]
        lhs = jnp.concatenate(qs, axis=0).astype(BF16)
        kx, vx = kvx_ref[0, :, kcol].astype(BF16), kvx_ref[0, :, vcol].astype(BF16)
        if has_local:
            kall = jnp.concatenate([kvp_ref[0, :, kcol], kvc_ref[0, :, kcol], kvn_ref[0, :, kcol], kx], axis=0)
            vall = jnp.concatenate([kvp_ref[0, :, vcol], kvc_ref[0, :, vcol], kvn_ref[0, :, vcol], vx], axis=0)
        else:
            kall, vall = kx, vx
        return _bdot_nt(lhs, kall), vall

    outs = []
    nxt = scores(0)
    for hk in range(ATT_KV_HEADS):
        s, vall = nxt
        if hk + 1 < ATT_KV_HEADS:
            nxt = scores(hk + 1)
        lov = lax.broadcasted_iota(jnp.int32, vall.shape, 1) < HALF
        zv = jnp.zeros_like(vall)
        v2 = jnp.concatenate([jnp.where(lov, vall, zv), jnp.where(lov, zv, vall)], axis=0)
        ps, inv_l = [], []
        for g in range(ATT_GROUP):
            sg = s[g * tq:(g + 1) * tq]
            if has_local:
                sg = jnp.concatenate([jnp.where(ok_prev, sg[:, 0:tq], NEG_INF), sg[:, tq:2 * tq],
                                      jnp.where(ok_next, sg[:, 2 * tq:3 * tq], NEG_INF), sg[:, 3 * tq:]], axis=1)
            sink = sink_ref[hk, g] * LOG2E
            m = jnp.maximum(jnp.max(sg, axis=1, keepdims=True), sink)
            p = jnp.exp2(sg - m)
            inv_l.append(1.0 / (jnp.sum(p, axis=1, keepdims=True) + jnp.exp2(sink - m)))
            ps.append(p.astype(BF16))
        pv = jnp.dot(jnp.concatenate([jnp.concatenate(ps[2 * pr:2 * pr + 2], axis=1) for pr in range(ATT_GROUP // 2)],
                                     axis=0), v2, preferred_element_type=F32)
        for pr in range(ATT_GROUP // 2):
            outs.append(pv[pr * tq:(pr + 1) * tq] * jnp.where(lo, inv_l[2 * pr], inv_l[2 * pr + 1]))
    g = g_ref[0]
    o_ref[0] = (jnp.concatenate(outs, axis=1) * (g * jax.nn.sigmoid(g))).astype(BF16)


def sink_attention(u, kv_local, tables, u_ctx, sink):
    bsz, t, _ = u.shape
    tq = ATT_BLOCK
    nb = t // tq
    tx = u_ctx.shape[1]
    kvw = ATT_KV_HEADS * OD_KV_W
    has_local = kv_local is not None
    in_specs = [pl.BlockSpec(memory_space=pltpu.SMEM),
                pl.BlockSpec((1, tq, ATT_WIDTH), lambda b, i: (b, i, OD_Q_COL // ATT_WIDTH)),
                pl.BlockSpec((1, tq, ATT_WIDTH), lambda b, i: (b, i, OD_GATE_COL // ATT_WIDTH))]
    args = [sink.reshape(ATT_KV_HEADS, ATT_GROUP), u, u]
    if has_local:
        in_specs += [pl.BlockSpec((1, tq, kvw), lambda b, i: (b, jnp.maximum(i - 1, 0), 0)),
                     pl.BlockSpec((1, tq, kvw), lambda b, i: (b, i, 0)),
                     pl.BlockSpec((1, tq, kvw), lambda b, i: (b, jnp.minimum(i + 1, nb - 1), 0))]
        in_specs += [pl.BlockSpec((tq, LANES), lambda b, i: (i, 0))] * 3
        args += [kv_local] * 3 + list(tables)
    in_specs += [pl.BlockSpec((1, tx, kvw), lambda b, i: (b, 0, OD_KV_COL // kvw))]
    args += [u_ctx]
    return pl.pallas_call(
        functools.partial(_attn_kernel, has_local=has_local, nb=nb),
        grid=(bsz, nb),
        in_specs=in_specs,
        out_specs=pl.BlockSpec((1, tq, ATT_WIDTH), lambda b, i: (b, i, 0)),
        out_shape=jax.ShapeDtypeStruct((bsz, t, ATT_WIDTH), BF16),
        compiler_params=pltpu.CompilerParams(dimension_semantics=("arbitrary", "arbitrary"),
                                             vmem_limit_bytes=V7X_VMEM_LIMIT_BYTES),
        name="sink_attention",
    )(*args)


def pack_odd_w_in(w_in):
    nq, nkv = ATT_WIDTH, ATT_KV_HEADS * ATT_HEAD
    parts = [w_in[:, :nq], w_in[:, nq + 2 * nkv:]]
    for h in range(ATT_KV_HEADS):
        k = w_in[:, nq + h * ATT_HEAD:nq + (h + 1) * ATT_HEAD]
        v = w_in[:, nq + nkv + h * ATT_HEAD:nq + nkv + (h + 1) * ATT_HEAD]
        parts += [k, k, v, v]
    return jnp.concatenate(parts, 1).astype(BF16)


def pack_even_w_in(w_in):
    g1 = 4 * GDN_WIDTH
    r0 = GDN_IN
    parts = [w_in[:, :g1], w_in[:, r0:r0 + 4 * RWKV_WIDTH], w_in[:, r0 + 4 * RWKV_WIDTH:], w_in[:, g1:GDN_IN],
             jnp.zeros((D_MODEL, EV_COLS - EV_SMALL_COL - 4 * GDN_HEADS), F32)]
    return jnp.concatenate(parts, 1).astype(BF16)


def kernel(x, c, ctx, c_ctx, mod_w, mod_b, ln_g, ln_b, ev_w_in, ev_w_out, gdn_conv, gdn_a_log, gdn_dt_bias,
           gdn_norm_g, rwkv_mix, rwkv_w0, rwkv_w_up, rwkv_a0, rwkv_a_up, rwkv_k_k, rwkv_k_a, rwkv_r_k,
           rwkv_gn_g, rwkv_gn_b, od_w_in, od_w_out, od_sink):
    bsz, t_lat, d = x.shape
    t_ctx = ctx.shape[1]
    x_lat, x_ctx = x, ctx
    n_mod = -(-(bsz + 1) // 8) * 8
    cc = jax.nn.silu(jnp.concatenate([c, c_ctx[None], jnp.zeros((n_mod - bsz - 1, d), F32)], 0))
    mods = mod_all_layers(cc, mod_w, mod_b)
    for layer in range(DEPTH):
        need_ctx = layer < DEPTH - 1
        i = layer // 2
        mod = mods[layer]
        sh, sc, gt = (mod[:bsz, j * d:(j + 1) * d][:, None, :] for j in range(3))
        sh_c, sc_c, gt_c = (jnp.broadcast_to(mod[bsz, j * d:(j + 1) * d], (bsz, 1, d)) for j in range(3))
        if layer % 2 == 0:
            w_in = pack_even_w_in(ev_w_in[i])
            w_out = ev_w_out[i].astype(BF16)
            u_lat = inproj(x_lat, sc, sh, w_in, 1024, EV_TN)
            u_ctx = inproj(x_ctx.reshape(1, bsz * t_ctx, d), sc_c[:1], sh_c[:1], w_in, 1024, EV_TN).reshape(bsz, t_ctx, -1)
            small = slice(EV_SMALL_COL, EV_SMALL_COL + 4 * GDN_HEADS)
            og_lat, og_ctx = gdn_mixer(u_lat, gdn_bg(u_lat[..., small]), u_ctx, gdn_bg(u_ctx[..., small]),
                                       gdn_params(gdn_conv[i], gdn_a_log[i], gdn_dt_bias[i], gdn_norm_g[i]))
            or_lat, or_ctx = rwkv_mixer(u_lat, u_ctx, *rwkv_params(
                rwkv_mix[i], rwkv_w0[i], rwkv_w_up[i], rwkv_a0[i], rwkv_a_up[i], rwkv_k_k[i], rwkv_k_a[i],
                rwkv_r_k[i], rwkv_gn_g[i], rwkv_gn_b[i]))
            a_lat, a_ctx = [og_lat, or_lat], [og_ctx, or_ctx]
        else:
            w_in = pack_odd_w_in(od_w_in[i])
            w_out = od_w_out[i].astype(BF16)
            u_lat = inproj(x_lat, sc, sh, w_in, 1024, OD_TN)
            u_ctx = inproj(x_ctx.reshape(1, bsz * t_ctx, d), sc_c[:1], sh_c[:1], w_in, 1024, OD_TN).reshape(bsz, t_ctx, -1)
            tables = rope_lane_tables(t_lat)
            o_lat = sink_attention(u_lat, kv_rope(u_lat, tables), tables, u_ctx, od_sink[i])
            o_ctx = sink_attention(u_ctx, None, None, u_ctx, od_sink[i]) if need_ctx else None
            a_lat, a_ctx = [o_lat], [o_ctx]
        x_lat = outproj_ln(a_lat, w_out, x_lat, gt, ln_g[layer], ln_b[layer])
        if need_ctx:
            x_ctx = outproj_ln(a_ctx, w_out, x_ctx, gt_c, ln_g[layer], ln_b[layer])
    return x_lat
```

```python
import functools

import jax
import jax.numpy as jnp
import numpy as np
from jax import lax
from jax.experimental import pallas as pl
from jax.experimental.pallas import tpu as pltpu

F32 = jnp.float32
BF16 = jnp.bfloat16

D_MODEL = 2048
DEPTH = 4
GRID_W = 64
GDN_WIDTH = D_MODEL // 2
GDN_HEADS = 8
GDN_CHUNK = 128
RWKV_WIDTH = D_MODEL // 2
RWKV_HEADS = 16
RWKV_CHUNK = 64
RWKV_GN_EPS = 64e-5
ATT_HEAD = 64
ATT_Q_HEADS = D_MODEL // ATT_HEAD
ATT_KV_HEADS = ATT_Q_HEADS // 8
ATT_GROUP = ATT_Q_HEADS // ATT_KV_HEADS
ATT_WIDTH = ATT_Q_HEADS * ATT_HEAD
WINDOW = 128
ATT_BLOCK = 128
ROPE_BASE = 10000.0
ROPE_FREQS = ATT_HEAD // 4
NEG_INF = -1e30
LOG2E = 1.4426950408889634
DEEPNORM_ALPHA = (2 * DEPTH) ** 0.25
LN_EPS = 1e-5
GDN_IN = 4 * GDN_WIDTH + 4 * GDN_HEADS

LANES = 128
PREP_ROWS = 256
HALF = 64
V7X_VMEM_LIMIT_BYTES = 56 * 1024 * 1024
EV_GDN_COL = 0
EV_RWKV_COL = 4 * GDN_WIDTH
EV_LORA_COL = EV_RWKV_COL + 4 * RWKV_WIDTH
EV_SMALL_COL = EV_LORA_COL + LANES
EV_COLS = EV_SMALL_COL + LANES
EV_TN = 1408
OD_TN = 1280


def _mod_kernel(a_ref, w_ref, b_ref, o_ref):
    o_ref[0] = jnp.dot(a_ref[...].astype(BF16), w_ref[0].astype(BF16), preferred_element_type=F32) + b_ref[0]


def mod_all_layers(a, w, b, tn=512):
    m, k = a.shape
    nl, _, n = w.shape
    return pl.pallas_call(
        _mod_kernel,
        grid=(nl, n // tn),
        in_specs=[pl.BlockSpec((m, k), lambda l, j: (0, 0)), pl.BlockSpec((1, k, tn), lambda l, j: (l, 0, j)),
                  pl.BlockSpec((1, 1, tn), lambda l, j: (l, 0, j))],
        out_specs=pl.BlockSpec((1, m, tn), lambda l, j: (l, 0, j)),
        out_shape=jax.ShapeDtypeStruct((nl, m, n), F32),
        compiler_params=pltpu.CompilerParams(dimension_semantics=("arbitrary", "arbitrary"),
                                             vmem_limit_bytes=V7X_VMEM_LIMIT_BYTES),
        name="mod_matmul",
    )(a, w, b.reshape(nl, 1, n))


def _inproj_kernel(x_ref, sc_ref, sh_ref, w_ref, o_ref, h_ref):
    @pl.when(pl.program_id(2) == 0)
    def _():
        h_ref[...] = (x_ref[0] * (1.0 + sc_ref[0]) + sh_ref[0]).astype(BF16)

    o_ref[0] = jnp.dot(h_ref[...], w_ref[...], preferred_element_type=F32)


def inproj(x, sc, sh, w, tm, tn):
    bsz, t, d = x.shape
    n = w.shape[1]
    tm = min(tm, t)
    assert t % tm == 0 and n % tn == 0
    return pl.pallas_call(
        _inproj_kernel,
        grid=(bsz, t // tm, n // tn),
        in_specs=[pl.BlockSpec((1, tm, d), lambda b, i, j: (b, i, 0)),
                  pl.BlockSpec((1, 1, d), lambda b, i, j: (b, 0, 0)),
                  pl.BlockSpec((1, 1, d), lambda b, i, j: (b, 0, 0)),
                  pl.BlockSpec((d, tn), lambda b, i, j: (0, j))],
        out_specs=pl.BlockSpec((1, tm, tn), lambda b, i, j: (b, i, j)),
        out_shape=jax.ShapeDtypeStruct((bsz, t, n), F32),
        scratch_shapes=[pltpu.VMEM((tm, d), BF16)],
        compiler_params=pltpu.CompilerParams(dimension_semantics=("arbitrary", "arbitrary", "arbitrary"),
                                             vmem_limit_bytes=V7X_VMEM_LIMIT_BYTES),
        name="inproj",
    )(x, sc, sh, w)


def _outproj_kernel(*refs, n_a):
    a_refs, (w_ref, x_ref, gt_ref, g_ref, b_ref, o_ref) = refs[:n_a], refs[n_a:]
    tm = x_ref.shape[1]
    halves = [slice(0, tm // 2), slice(tm // 2, tm)] if tm % 16 == 0 else [slice(0, tm)]
    ys = []
    for rows in halves:
        y = None
        k0 = 0
        for a_ref in a_refs:
            kw = a_ref.shape[2]
            term = jnp.dot(a_ref[0, rows, :].astype(BF16), w_ref[k0:k0 + kw, :], preferred_element_type=F32)
            y = term if y is None else y + term
            k0 += kw
        ys.append(y)
    for rows, y in zip(halves, ys):
        z = DEEPNORM_ALPHA * x_ref[0, rows, :] + gt_ref[0] * y
        mu = jnp.mean(z, axis=1, keepdims=True)
        zc = z - mu
        var = jnp.mean(zc * zc, axis=1, keepdims=True)
        o_ref[0, rows, :] = zc * lax.rsqrt(var + LN_EPS) * g_ref[...] + b_ref[...]


def outproj_ln(a_list, w, x, gt, ln_g, ln_b, tm=512):
    bsz, t, d = x.shape
    tm = min(tm, t)
    in_specs = [pl.BlockSpec((1, tm, a.shape[2]), lambda b, i: (b, i, 0)) for a in a_list] + [
        pl.BlockSpec(w.shape, lambda b, i: (0, 0)),
        pl.BlockSpec((1, tm, d), lambda b, i: (b, i, 0)),
        pl.BlockSpec((1, 1, d), lambda b, i: (b, 0, 0)),
        pl.BlockSpec((1, d), lambda b, i: (0, 0)),
        pl.BlockSpec((1, d), lambda b, i: (0, 0))]
    return pl.pallas_call(
        functools.partial(_outproj_kernel, n_a=len(a_list)),
        grid=(bsz, t // tm),
        in_specs=in_specs,
        out_specs=pl.BlockSpec((1, tm, d), lambda b, i: (b, i, 0)),
        out_shape=jax.ShapeDtypeStruct((bsz, t, d), F32),
        compiler_params=pltpu.CompilerParams(dimension_semantics=("arbitrary", "arbitrary"),
                                             vmem_limit_bytes=V7X_VMEM_LIMIT_BYTES),
        name="outproj_ln",
    )(*a_list, w, x, gt, ln_g.reshape(1, d), ln_b.reshape(1, d))


def _bdot(a, b):
    return jnp.dot(a.astype(BF16), b.astype(BF16), preferred_element_type=F32)


def _bdot_nt(a, b):
    return lax.dot_general(a.astype(BF16), b.astype(BF16), (((1,), (1,)), ((), ())), preferred_element_type=F32)


def _bdot_tn(a, b):
    return lax.dot_general(a.astype(BF16), b.astype(BF16), (((0,), (0,)), ((), ())), preferred_element_type=F32)


(PV_MR0, PV_MR1, PV_MK0, PV_MK1, PV_MV0, PV_MV1, PV_MG0, PV_MG1, PV_W00, PV_W01, PV_A00, PV_A01,
 PV_KK, PV_KA, PV_RK, PV_GNG, PV_GNB) = range(17)
PV_ROWS = 24
MK_ABD, MK_P, MK_Y, MK_L1 = 0, 1, 2, 3
MK_LV = 4
N_LV = 5
MK_PER_DIR = MK_LV + N_LV
MK_EYE = 2 * MK_PER_DIR
MK_BD = MK_EYE + 1
N_MASKS = MK_BD + 1


def rwkv_masks():
    ri = np.arange(128)[:, None]
    ci = np.arange(128)[None, :]
    i, j = ri & 63, ci & 63
    same = (ri < 64) == (ci < 64)
    out = np.zeros((N_MASKS, 128, 128), np.float32)
    for d in range(2):
        strict = (j < i) if d == 0 else (j > i)
        incl = (j <= i) if d == 0 else (j >= i)
        base = d * MK_PER_DIR
        out[base + MK_ABD] = strict & same
        out[base + MK_P] = strict & (ci >= 64)
        out[base + MK_Y] = incl
        for li, s in enumerate((1, 2, 4, 8, 16, 32)):
            blk = (i // (2 * s)) == (j // (2 * s))
            if d == 0:
                m = blk & ((i & s) != 0) & ((j & s) == 0)
            else:
                m = blk & ((i & s) == 0) & ((j & s) != 0)
            out[base + (MK_L1 if li == 0 else MK_LV + li - 1)] = m & same
    out[MK_EYE] = ri == ci
    out[MK_BD] = same
    return out


def _seg_sum(x, lo):
    s0 = jnp.sum(jnp.where(lo, x, 0.0), axis=1, keepdims=True)
    s1 = jnp.sum(jnp.where(lo, 0.0, x), axis=1, keepdims=True)
    return jnp.where(lo, s0, s1)


def _chunk_cumsum(x, chunk):
    rowm = lax.broadcasted_iota(jnp.int32, x.shape, 0) & (chunk - 1)
    s = 1
    while s < chunk:
        x = x + jnp.where(rowm >= s, pltpu.roll(x, s, axis=0), 0.0)
        s *= 2
    return x


def _interleave(*gens):
    live = list(gens)
    while live:
        for g in list(live):
            try:
                next(g)
            except StopIteration:
                live.remove(g)


def _rwkv_kernel(lr, lk, lv, lg, ll, cr, ck, cv, cg, cl, pv_ref, mixl_ref, wup_ref, aup_ref, mk_ref,
                 o_lat, o_ctx,
                 s_at0, s_at1, s_rt0, s_rt1, s_bt0, s_bt1, s_kt0, s_kt1, s_bh0, s_bh1, s_kh0, s_kh1, s_v,
                 s_g, s_bonus, s_yf, s_yb, s_w, s_tp, s_gr, s_et, s_state, *, t_lat, t_ctx, p1_chunks):
    C = RWKV_CHUNK
    s_at, s_rt, s_bt, s_kt, s_bh, s_kh = ((s_at0, s_at1), (s_rt0, s_rt1), (s_bt0, s_bt1), (s_kt0, s_kt1),
                                          (s_bh0, s_bh1), (s_kh0, s_kh1))
    s_y = (s_yf, s_yb)
    lo64 = lax.broadcasted_iota(jnp.int32, (C, LANES), 1) < HALF
    bd_ones = mk_ref[MK_BD]

    def pvrow(i):
        return pv_ref[pl.ds(i, 1), :]

    def prep_tile(xs, tseq, it, row0, ch0):
        xr, xk, xv, xg, xl = xs
        R = min(PREP_ROWS, tseq)
        s = it * R
        lo = lax.broadcasted_iota(jnp.int32, (R, LANES), 1) < HALF
        rows = lax.broadcasted_iota(jnp.int32, (R, LANES), 0)

        def shifted(ref, m0, m1):
            cur = ref[0, s:s + R, :]
            prev = ref[0, s - 1:s - 1 + R, :] if s > 0 else jnp.where(rows == 0, 0.0, pltpu.roll(cur, 1, axis=0))
            nxt = (ref[0, s + 1:s + 1 + R, :] if s + R < tseq
                   else jnp.where(rows == R - 1, 0.0, pltpu.roll(cur, R - 1, axis=0)))
            return cur + m0 * (prev - cur) + m1 * (nxt - cur)

        r = shifted(xr, pvrow(PV_MR0), pvrow(PV_MR1))
        k = shifted(xk, pvrow(PV_MK0), pvrow(PV_MK1))
        v = shifted(xv, pvrow(PV_MV0), pvrow(PV_MV1))
        g = shifted(xg, pvrow(PV_MG0), pvrow(PV_MG1))
        lor = shifted(xl, mixl_ref[0:1, :], mixl_ref[1:2, :])
        th = jnp.tanh(lor)
        kkf = k * pvrow(PV_KK)
        kk = kkf * lax.rsqrt(_seg_sum(kkf * kkf, lo) + 1e-6)
        ka = pvrow(PV_KA)
        out = slice(row0 + s, row0 + s + R)
        kd_sum = None
        for d in range(2):
            w_pre = pvrow(PV_W00 + d) + _bdot(th, wup_ref[d])
            lw = (-np.exp(-0.5)) * jax.nn.sigmoid(w_pre)
            ag = jax.nn.sigmoid(pvrow(PV_A00 + d) + _bdot(lor, aup_ref[d]))
            kd = k * (1.0 + (ag - 1.0) * ka)
            bdv = kk * ag
            kd_sum = kd if kd_sum is None else kd_sum + kd
            c_fwd = _chunk_cumsum(lw, C)
            tot = jnp.concatenate([jnp.broadcast_to(c_fwd[(c + 1) * C - 1:(c + 1) * C, :], (C, LANES))
                                   for c in range(R // C)], axis=0)
            c_bwd = tot - c_fwd + lw
            cw, rest = (c_fwd, c_bwd - lw) if d == 0 else (c_bwd, c_fwd - lw)
            e_in = jnp.exp(-cw)
            e_out = jnp.exp(rest)
            s_at[d][out, :] = (-kk * jnp.exp(cw - lw)).astype(BF16)
            s_rt[d][out, :] = (r * jnp.exp(cw)).astype(BF16)
            s_bt[d][out, :] = (bdv * e_in).astype(BF16)
            s_kt[d][out, :] = (kd * e_in).astype(BF16)
            s_bh[d][out, :] = (bdv * e_out).astype(BF16)
            s_kh[d][out, :] = (kd * e_out).astype(BF16)
            for c in range(R // C):
                last = c * C + (C - 1 if d == 0 else 0)
                n = ch0 + it * (R // C) + c
                s_et[d, n:n + 1, :] = jnp.exp(cw[last:last + 1, :])
        s_v[out, :] = v.astype(BF16)
        s_g[out, :] = g
        s_bonus[out, :] = _seg_sum(r * kd_sum * pvrow(PV_RK), lo) * v

    def prep_stages(xs, tseq, tiles, row0, ch0):
        for it in tiles:
            prep_tile(xs, tseq, it, row0, ch0)
            yield

    def p1_stages(probs, row0, ch0):
        ds = [d for _, d in probs]
        rows = [pl.ds(pl.multiple_of(row0 + n * C, C), C) for n, _ in probs]
        a2s, gms = [], []
        for (n, d), rw in zip(probs, rows):
            at, rt = s_at[d][rw, :], s_rt[d][rw, :]
            zero = jnp.zeros_like(at)
            a2 = jnp.concatenate([jnp.where(lo64, at, zero), jnp.where(lo64, zero, at)], axis=0)
            l1 = jnp.concatenate([a2, jnp.where(lo64, rt, zero), jnp.where(lo64, zero, rt)], axis=0)
            a2s.append(a2)
            gms.append(_bdot_nt(l1, jnp.concatenate([s_bt[d][rw, :], s_kt[d][rw, :]], axis=0)))
        yield
        ps, a_list = [], []
        for (n, d), rw, gm in zip(probs, rows, gms):
            base = d * MK_PER_DIR
            ga = gm[0:2 * C]
            v = s_v[rw, :]
            ps.append(_bdot(ga * mk_ref[base + MK_P], jnp.concatenate([v, v], axis=0)))
            s_gr[d, ch0 + n] = (gm[2 * C:4 * C] * mk_ref[base + MK_Y]).astype(BF16)
            ga_sw = jnp.concatenate([ga[0:C], pltpu.roll(ga[C:2 * C], HALF, axis=1)], axis=0)
            a_list.append(ga_sw * mk_ref[base + MK_ABD])
        yield
        xs = [mk_ref[MK_EYE] + a * mk_ref[d * MK_PER_DIR + MK_L1] for a, d in zip(a_list, ds)]
        for li in range(N_LV):
            t1 = [_bdot(x, a * mk_ref[d * MK_PER_DIR + MK_LV + li]) for x, a, d in zip(xs, a_list, ds)]
            yield
            xs = [x + _bdot(t, x) for x, t in zip(xs, t1)]
            yield
        for (n, d), x, a2, p in zip(probs, xs, a2s, ps):
            wtp = _bdot(x, jnp.concatenate([a2.astype(F32), p], axis=1))
            s_w[d, ch0 + n] = wtp[:, 0:LANES].astype(BF16)
            s_tp[d, ch0 + n] = jnp.where(lo64, wtp[0:C, LANES:2 * LANES], wtp[C:2 * C, LANES:2 * LANES])
        yield

    def p2_stages(steps, row0, ch0):
        for ns0 in steps:
            ns = [ch0 + n for n in ns0]
            rows = [pl.ds(pl.multiple_of(row0 + ns0[d] * C, C), C) for d in range(2)]
            st = [s_state[d] for d in range(2)]
            xs = [_bdot_nt(jnp.concatenate([s_w[d, ns[d]], s_rt[d][rows[d], :]], axis=0), st[d]) for d in range(2)]
            yield
            uv = [jnp.concatenate([(xs[d][0:C] + xs[d][C:2 * C] + s_tp[d, ns[d]]).astype(BF16), s_v[rows[d], :]], axis=0)
                  for d in range(2)]
            yb = [_bdot(s_gr[d, ns[d]], uv[d]) for d in range(2)]
            for d in range(2):
                s_y[d][rows[d], :] = xs[d][2 * C:3 * C] + jnp.where(lo64, yb[d][0:C], yb[d][C:2 * C])
                bk = jnp.concatenate([s_bh[d][rows[d], :], s_kh[d][rows[d], :]], axis=0)
                s_state[d] = st[d] * s_et[d, pl.ds(ns[d], 1), :] + bd_ones * _bdot_tn(uv[d], bk)
            yield

    def groups(tseq):
        nch = tseq // C
        per = min(p1_chunks, nch)
        probs = lambda g: [(g * per + j, 0) for j in range(per)] + [(nch - 1 - (g * per + j), 1) for j in range(per)]
        steps = lambda g: [(g * per + j, nch - 1 - (g * per + j)) for j in range(per)]
        return nch // per, per, probs, steps

    def epilogue_tile(o_ref, tseq, it, row0):
        R = min(PREP_ROWS, tseq)
        lo = lax.broadcasted_iota(jnp.int32, (R, LANES), 1) < HALF
        rows = slice(row0 + it * R, row0 + (it + 1) * R)
        y = s_yf[rows, :] + s_yb[rows, :]
        mu = _seg_sum(y, lo) * (1.0 / HALF)
        yc = y - mu
        var = _seg_sum(yc * yc, lo) * (1.0 / HALF)
        yn = yc * lax.rsqrt(var + RWKV_GN_EPS) * pvrow(PV_GNG) + pvrow(PV_GNB)
        g = s_g[rows, :]
        o_ref[0, it * R:(it + 1) * R, :] = ((yn + s_bonus[rows, :]) * (g * jax.nn.sigmoid(g))).astype(BF16)

    def epilogue_stages(o_ref, tseq, tiles, row0):
        for it in tiles:
            epilogue_tile(o_ref, tseq, it, row0)
            yield

    lat, ctx = (lr, lk, lv, lg, ll), (cr, ck, cv, cg, cl)
    row_c, ch_c = t_lat, t_lat // C
    s_state[...] = jnp.zeros_like(s_state)
    _interleave(prep_stages(ctx, t_ctx, range(t_ctx // min(PREP_ROWS, t_ctx)), row_c, ch_c))
    ng_c, _, probs_c, steps_c = groups(t_ctx)
    ng, per, probs, steps = groups(t_lat)
    ntile = t_lat // min(PREP_ROWS, t_lat)
    tpg = max(per * C // min(PREP_ROWS, t_lat), 1)
    first = sorted(set(list(range(min(tpg, ntile))) + list(range(max(ntile - tpg, 0), ntile))))
    rest = [t for t in range(ntile) if t not in first]

    def ctx_scan():
        for g in range(ng_c):
            yield from p1_stages(probs_c(g), row_c, ch_c)
            yield from p2_stages(steps_c(g), row_c, ch_c)

    _interleave(prep_stages(lat, t_lat, first, 0, 0), ctx_scan())
    _interleave(prep_stages(lat, t_lat, rest, 0, 0), p1_stages(probs(0), 0, 0),
                epilogue_stages(o_ctx, t_ctx, range(t_ctx // min(PREP_ROWS, t_ctx)), row_c))
    for g in range(ng - 1):
        _interleave(p1_stages(probs(g + 1), 0, 0), p2_stages(steps(g), 0, 0))
    done = [t for t in range(ntile) if per * C <= t * min(PREP_ROWS, t_lat) and (t + 1) * min(PREP_ROWS, t_lat) <= (ng - 1) * per * C]
    _interleave(p2_stages(steps(ng - 1), 0, 0), epilogue_stages(o_lat, t_lat, done, 0))
    _interleave(epilogue_stages(o_lat, t_lat, [t for t in range(ntile) if t not in done], 0))


def _col_block_map(b, p, *, off):
    return (b, 0, off + p)


def rwkv_mixer(u_lat, u_ctx, pv, mixl, wup, aup, p1_chunks=8):
    bsz, t_lat, _ = u_lat.shape
    t_ctx = u_ctx.shape[1]
    npair = RWKV_HEADS // 2
    c0 = EV_RWKV_COL // LANES
    nch = (t_lat + t_ctx) // RWKV_CHUNK
    t_all = t_lat + t_ctx
    masks = jnp.asarray(rwkv_masks())

    def seq_specs(t):
        return [pl.BlockSpec((1, t, LANES), functools.partial(_col_block_map, off=c0 + o * npair)) for o in range(4)] + [
            pl.BlockSpec((1, t, LANES), lambda b, p: (b, 0, EV_LORA_COL // LANES))]

    in_specs = seq_specs(t_lat) + seq_specs(t_ctx) + [
        pl.BlockSpec((PV_ROWS, LANES), lambda b, p: (0, p)),
        pl.BlockSpec((8, LANES), lambda b, p: (0, 0)),
        pl.BlockSpec((2, LANES, LANES), lambda b, p: (0, 0, p)),
        pl.BlockSpec((2, LANES, LANES), lambda b, p: (0, 0, p)),
        pl.BlockSpec((N_MASKS, LANES, LANES), lambda b, p: (0, 0, 0)),
    ]
    out_specs = [pl.BlockSpec((1, t_lat, LANES), lambda b, p: (b, 0, p)),
                 pl.BlockSpec((1, t_ctx, LANES), lambda b, p: (b, 0, p))]
    scratch = ([pltpu.VMEM((t_all, LANES), BF16) for _ in range(13)] + [pltpu.VMEM((t_all, LANES), F32) for _ in range(4)]
               + [pltpu.VMEM((2, nch, LANES, LANES), BF16), pltpu.VMEM((2, nch, RWKV_CHUNK, LANES), F32),
                  pltpu.VMEM((2, nch, LANES, LANES), BF16)]
               + [pltpu.VMEM((2, nch, LANES), F32), pltpu.VMEM((2, LANES, LANES), F32)])
    kern = functools.partial(_rwkv_kernel, t_lat=t_lat, t_ctx=t_ctx, p1_chunks=p1_chunks)
    return pl.pallas_call(
        kern,
        grid=(bsz, npair),
        in_specs=in_specs,
        out_specs=out_specs,
        out_shape=[jax.ShapeDtypeStruct((bsz, t_lat, RWKV_WIDTH), BF16), jax.ShapeDtypeStruct((bsz, t_ctx, RWKV_WIDTH), BF16)],
        scratch_shapes=scratch,
        compiler_params=pltpu.CompilerParams(dimension_semantics=("arbitrary", "arbitrary"),
                                             vmem_limit_bytes=V7X_VMEM_LIMIT_BYTES),
        name="rwkv7_chunked",
    )(*([u_lat] * 5), *([u_ctx] * 5), pv, mixl, wup, aup, masks)


def rwkv_params(mix, w0, w_up, a0, a_up, k_k, k_a, r_k, gn_g, gn_b):
    wd = RWKV_WIDTH
    rows = [mix[0, 0:wd], mix[1, 0:wd], mix[0, wd:2 * wd], mix[1, wd:2 * wd], mix[0, 2 * wd:3 * wd], mix[1, 2 * wd:3 * wd],
            mix[0, 3 * wd:4 * wd], mix[1, 3 * wd:4 * wd], w0[0], w0[1], a0[0], a0[1], k_k, k_a, r_k.reshape(-1), gn_g, gn_b]
    pv = jnp.concatenate([jnp.stack(rows), jnp.zeros((PV_ROWS - len(rows), wd), F32)], 0)
    mixl = jnp.concatenate([mix[:, 4 * wd:], jnp.zeros((6, LANES), F32)], 0)
    z = jnp.zeros((2, HALF, wd), F32)
    wup = jnp.concatenate([w_up, z], 1)
    aup = jnp.concatenate([z, a_up], 1)
    return pv, mixl, wup, aup


GP_CONV_Q, GP_CONV_K, GP_CONV_V = 0, 5, 10
GP_NORM, GP_SC_ALOG, GP_SC_DTB = 15, 16, 17
GP_ROWS = 24
GM_STRICT, GM_INCL, GM_L1, GM_LV = 0, 1, 2, 3
GN_LV = 6
GM_PER_DIR = GM_LV + GN_LV
GM_EYE = 2 * GM_PER_DIR
GN_MASKS = GM_EYE + 1
GDN_P1_CHUNKS = 4


def gdn_masks():
    i = np.arange(GDN_CHUNK)[:, None]
    j = np.arange(GDN_CHUNK)[None, :]
    out = np.zeros((GN_MASKS, GDN_CHUNK, GDN_CHUNK), np.float32)
    for d in range(2):
        base = d * GM_PER_DIR
        out[base + GM_STRICT] = (j < i) if d == 0 else (j > i)
        out[base + GM_INCL] = (j <= i) if d == 0 else (j >= i)
        for li in range(GN_LV + 1):
            s = 1 << li
            blk = (i // (2 * s)) == (j // (2 * s))
            m = blk & (((i & s) != 0) & ((j & s) == 0) if d == 0 else ((i & s) == 0) & ((j & s) != 0))
            out[base + (GM_L1 if li == 0 else GM_LV + li - 1)] = m
    out[GM_EYE] = i == j
    return out


def _gdn_kernel(lq, lk, lv, lz, lbg, cq, ck, cv, cz, cbg, gp_ref, mk_ref, o_lat, o_ctx,
                s_q, s_k, s_kb0, s_kb1, s_kbg0, s_kbg1, s_qg0, s_qg1, s_kg0, s_kg1, s_vb0, s_vb1,
                s_gc0, s_gc1, s_of, s_ob, s_tk, s_tvb, s_a, s_gl, s_state, *, t_lat, t_ctx):
    C = GDN_CHUNK
    s_kb, s_kbg, s_qg, s_kg, s_vb = (s_kb0, s_kb1), (s_kbg0, s_kbg1), (s_qg0, s_qg1), (s_kg0, s_kg1), (s_vb0, s_vb1)
    s_gc, s_o = (s_gc0, s_gc1), (s_of, s_ob)

    def gprow(i):
        return gp_ref[pl.ds(i, 1), :]

    def prep_tile(xs, tseq, it, row0, ch0):
        xq, xk, xv, xbg = xs
        R = min(PREP_ROWS, tseq)
        s = it * R

        def conv_silu(ref, prow0):
            rows = lax.broadcasted_iota(jnp.int32, (R, LANES), 0)
            acc = None
            for j in range(5):
                off = j - 2
                lo_r, hi_r = s + off, s + off + R
                if lo_r < 0:
                    x = jnp.where(rows >= -off, pltpu.roll(ref[0, 0:R, :], -off, axis=0), 0.0)
                elif hi_r > tseq:
                    x = jnp.where(rows < R - off, pltpu.roll(ref[0, tseq - R:tseq, :], R - off, axis=0), 0.0)
                else:
                    x = ref[0, lo_r:hi_r, :]
                term = gprow(prow0 + j) * x
                acc = term if acc is None else acc + term
            return acc * jax.nn.sigmoid(acc)

        qc = conv_silu(xq, GP_CONV_Q)
        kc = conv_silu(xk, GP_CONV_K)
        v = conv_silu(xv, GP_CONV_V)
        q = qc * (lax.rsqrt(jnp.sum(qc * qc, axis=1, keepdims=True) + 1e-6) * (LANES ** -0.5))
        k = kc * lax.rsqrt(jnp.sum(kc * kc, axis=1, keepdims=True) + 1e-6)
        bg = xbg[0, 0, s:s + R, :]
        lane4 = lax.broadcasted_iota(jnp.int32, bg.shape, 1)
        beta4 = jax.nn.sigmoid(bg)
        g4 = -jnp.exp(gprow(GP_SC_ALOG)[:, 0:4]) * jax.nn.softplus(bg + gprow(GP_SC_DTB)[:, 0:4])
        c_fwd = _chunk_cumsum(g4, C)
        tot = jnp.concatenate([jnp.broadcast_to(c_fwd[(c + 1) * C - 1:(c + 1) * C, :], (C, 4)) for c in range(R // C)],
                              axis=0)
        c_bwd = tot - c_fwd + g4
        fwd_lane = lane4 == 2
        gc4 = jnp.where(fwd_lane, c_fwd, c_bwd)
        eg4 = jnp.exp(gc4)
        er4 = jnp.exp(jnp.where(fwd_lane, c_bwd, c_fwd) - g4)
        out = slice(row0 + s, row0 + s + R)
        s_q[out, :] = q.astype(BF16)
        s_k[out, :] = k.astype(BF16)
        for d in range(2):
            beta, eg = beta4[:, d:d + 1], eg4[:, 2 + d:3 + d]
            kb = k * beta
            s_kb[d][out, :] = kb.astype(BF16)
            s_kbg[d][out, :] = (kb * eg).astype(BF16)
            s_qg[d][out, :] = (q * eg).astype(BF16)
            s_kg[d][out, :] = (k * er4[:, 2 + d:3 + d]).astype(BF16)
            s_vb[d][out, :] = (v * beta).astype(BF16)
            s_gc[d][out, :] = jnp.broadcast_to(gc4[:, 2 + d:3 + d], (R, LANES))
            for c in range(R // C):
                last = c * C + (C - 1 if d == 0 else 0)
                n = ch0 + it * (R // C) + c
                s_gl[d, n:n + 1, :] = jnp.broadcast_to(eg[last:last + 1, :], (1, LANES))

    def prep_stages(xs, tseq, tiles, row0, ch0):
        for it in tiles:
            prep_tile(xs, tseq, it, row0, ch0)
            yield

    def p1_stages(probs, row0, ch0):
        ds = [d for _, d in probs]
        rows = [pl.ds(pl.multiple_of(row0 + n * C, C), C) for n, _ in probs]
        gms = [_bdot_nt(jnp.concatenate([s_kb[d][rw, :], s_q[rw, :]], axis=0), s_k[rw, :])
               for (n, d), rw in zip(probs, rows)]
        yield
        a_list = []
        for (n, d), rw, gm in zip(probs, rows, gms):
            base = d * GM_PER_DIR
            gc = s_gc[d][rw, :]
            m_incl = mk_ref[base + GM_INCL]
            dec = jnp.exp((gc - gc.T) * m_incl)
            s_a[d, ch0 + n] = (gm[C:2 * C] * dec * m_incl).astype(BF16)
            a_list.append(-(gm[0:C] * dec * mk_ref[base + GM_STRICT]))
        xs = [mk_ref[GM_EYE] + a * mk_ref[d * GM_PER_DIR + GM_L1] for a, d in zip(a_list, ds)]
        for li in range(GN_LV):
            t1 = [_bdot(x, a * mk_ref[d * GM_PER_DIR + GM_LV + li]) for x, a, d in zip(xs, a_list, ds)]
            yield
            xs = [x + _bdot(t, x) for x, t in zip(xs, t1)]
            yield
        for (n, d), rw, x in zip(probs, rows, xs):
            tkv = _bdot(x, jnp.concatenate([s_kbg[d][rw, :], s_vb[d][rw, :]], axis=1))
            s_tk[d, ch0 + n] = tkv[:, 0:LANES].astype(BF16)
            s_tvb[d, ch0 + n] = tkv[:, LANES:2 * LANES]
        yield

    def p2_stages(steps, row0, ch0):
        for ns0 in steps:
            ns = [ch0 + n for n in ns0]
            rows = [pl.ds(pl.multiple_of(row0 + ns0[d] * C, C), C) for d in range(2)]
            st = [s_state[d] for d in range(2)]
            aq = [_bdot(jnp.concatenate([s_tk[d, ns[d]], s_qg[d][rows[d], :]], axis=0), st[d]) for d in range(2)]
            yield
            vnew = [s_tvb[d, ns[d]] - aq[d][0:C] for d in range(2)]
            oo = [aq[d][C:2 * C] + _bdot(s_a[d, ns[d]], vnew[d]) for d in range(2)]
            for d in range(2):
                s_o[d][rows[d], :] = oo[d]
                s_state[d] = st[d] * s_gl[d, pl.ds(ns[d], 1), :] + _bdot_tn(s_kg[d][rows[d], :], vnew[d])
            yield

    def groups(tseq):
        nch = tseq // C
        per = min(GDN_P1_CHUNKS, nch)
        probs = lambda g: [(g * per + j, 0) for j in range(per)] + [(nch - 1 - (g * per + j), 1) for j in range(per)]
        steps = lambda g: [(g * per + j, nch - 1 - (g * per + j)) for j in range(per)]
        return nch // per, per, probs, steps

    def epilogue_tile(o_ref, z_ref, tseq, it, row0):
        R = min(PREP_ROWS, tseq)
        rows = slice(row0 + it * R, row0 + (it + 1) * R)
        o = s_of[rows, :] + s_ob[rows, :]
        o = o * lax.rsqrt(jnp.mean(o * o, axis=1, keepdims=True) + 1e-6) * gprow(GP_NORM)
        z = z_ref[0, it * R:(it + 1) * R, :]
        o_ref[0, it * R:(it + 1) * R, :] = (o * (z * jax.nn.sigmoid(z))).astype(BF16)

    def epilogue_stages(o_ref, z_ref, tseq, tiles, row0):
        for it in tiles:
            epilogue_tile(o_ref, z_ref, tseq, it, row0)
            yield

    lat, ctx = (lq, lk, lv, lbg), (cq, ck, cv, cbg)
    row_c, ch_c = t_lat, t_lat // C
    s_state[...] = jnp.zeros_like(s_state)
    _interleave(prep_stages(ctx, t_ctx, range(t_ctx // min(PREP_ROWS, t_ctx)), row_c, ch_c))
    ng_c, _, probs_c, steps_c = groups(t_ctx)
    ng, per, probs, steps = groups(t_lat)
    ntile = t_lat // min(PREP_ROWS, t_lat)
    tpg = max(per * C // min(PREP_ROWS, t_lat), 1)
    first = sorted(set(list(range(min(tpg, ntile))) + list(range(max(ntile - tpg, 0), ntile))))
    rest = [t for t in range(ntile) if t not in first]

    def ctx_scan():
        for g in range(ng_c):
            yield from p1_stages(probs_c(g), row_c, ch_c)
            yield from p2_stages(steps_c(g), row_c, ch_c)

    _interleave(prep_stages(lat, t_lat, first, 0, 0), ctx_scan())
    _interleave(prep_stages(lat, t_lat, rest, 0, 0), p1_stages(probs(0), 0, 0),
                epilogue_stages(o_ctx, cz, t_ctx, range(t_ctx // min(PREP_ROWS, t_ctx)), row_c))
    if ng > 1:
        def body(g, carry):
            _interleave(p1_stages(probs(g + 1), 0, 0), p2_stages(steps(g), 0, 0))
            return carry

        lax.fori_loop(0, ng - 1, body, 0)
    done = [t for t in range(ntile) if per * C <= t * min(PREP_ROWS, t_lat) and (t + 1) * min(PREP_ROWS, t_lat) <= (ng - 1) * per * C]
    _interleave(p2_stages(steps(ng - 1), 0, 0), epilogue_stages(o_lat, lz, t_lat, done, 0))
    _interleave(epilogue_stages(o_lat, lz, t_lat, [t for t in range(ntile) if t not in done], 0))


def gdn_mixer(u_lat, bg_lat, u_ctx, bg_ctx, gp):
    bsz, t_lat, _ = u_lat.shape
    t_ctx = u_ctx.shape[1]
    nh = GDN_HEADS
    masks = jnp.asarray(gdn_masks())

    def seq_specs(t):
        return [pl.BlockSpec((1, t, LANES), functools.partial(_col_block_map, off=o * nh)) for o in range(4)] + [
            pl.BlockSpec((1, 1, t, 4), lambda b, h: (b, h, 0, 0))]

    in_specs = seq_specs(t_lat) + seq_specs(t_ctx) + [
        pl.BlockSpec((GP_ROWS, LANES), lambda b, h: (0, h)),
        pl.BlockSpec((GN_MASKS, LANES, LANES), lambda b, h: (0, 0, 0)),
    ]
    out_specs = [pl.BlockSpec((1, t_lat, LANES), lambda b, h: (b, 0, h)),
                 pl.BlockSpec((1, t_ctx, LANES), lambda b, h: (b, 0, h))]
    nch = (t_lat + t_ctx) // GDN_CHUNK
    t_all = t_lat + t_ctx
    scratch = ([pltpu.VMEM((t_all, LANES), BF16) for _ in range(12)] + [pltpu.VMEM((t_all, LANES), F32) for _ in range(4)]
               + [pltpu.VMEM((2, nch, LANES, LANES), BF16), pltpu.VMEM((2, nch, LANES, LANES), F32),
                  pltpu.VMEM((2, nch, LANES, LANES), BF16),
                  pltpu.VMEM((2, max(nch, 8), LANES), F32), pltpu.VMEM((2, LANES, LANES), F32)])
    kern = functools.partial(_gdn_kernel, t_lat=t_lat, t_ctx=t_ctx)
    return pl.pallas_call(
        kern,
        grid=(bsz, nh),
        in_specs=in_specs,
        out_specs=out_specs,
        out_shape=[jax.ShapeDtypeStruct((bsz, t_lat, GDN_WIDTH), BF16), jax.ShapeDtypeStruct((bsz, t_ctx, GDN_WIDTH), BF16)],
        scratch_shapes=scratch,
        compiler_params=pltpu.CompilerParams(dimension_semantics=("arbitrary", "arbitrary"),
                                             vmem_limit_bytes=V7X_VMEM_LIMIT_BYTES),
        name="gdn_chunked",
    )(*([u_lat] * 4), bg_lat, *([u_ctx] * 4), bg_ctx, gp, masks)


def gdn_params(conv_w, a_log, dt_bias, norm_g):
    wd = GDN_WIDTH

    def scalar_row(x):
        return jnp.zeros((GDN_HEADS, LANES), F32).at[:, 2].set(x[0]).at[:, 3].set(x[1]).reshape(-1)

    rows = [conv_w[j, o * wd:(o + 1) * wd] for o in range(3) for j in range(5)]
    rows += [jnp.tile(norm_g, GDN_HEADS), scalar_row(a_log), scalar_row(dt_bias)]
    return jnp.concatenate([jnp.stack(rows), jnp.zeros((GP_ROWS - len(rows), wd), F32)], 0)


def gdn_bg(u_small):
    bsz, t, _ = u_small.shape
    x = u_small.reshape(bsz, t, 2, 2, GDN_HEADS)
    return jnp.transpose(x, (0, 4, 1, 2, 3)).reshape(bsz, GDN_HEADS, t, 4)


OD_Q_COL, OD_GATE_COL, OD_KV_COL = 0, ATT_WIDTH, 2 * ATT_WIDTH
OD_KV_W = 4 * ATT_HEAD
OD_COLS = OD_KV_COL + ATT_KV_HEADS * OD_KV_W


def rope_lane_tables(t):
    pos = np.arange(t)
    inv = ROPE_BASE ** (-np.arange(ROPE_FREQS, dtype=np.float64) / ROPE_FREQS)
    ang_row = (pos // GRID_W)[:, None] * inv
    ang_col = (pos % GRID_W)[:, None] * inv
    ang = np.concatenate([ang_row, ang_row, ang_col, ang_col] * 2, axis=1)
    first = (np.arange(LANES) % (2 * ROPE_FREQS)) < ROPE_FREQS
    cos, sin = np.cos(ang), np.sin(ang)
    return (jnp.asarray(cos, F32), jnp.asarray(np.where(first, -sin, 0.0), F32), jnp.asarray(np.where(first, 0.0, sin), F32))


def _rope(x, cos, sina, sinb):
    n = x.shape[1]
    return x * cos + pltpu.roll(x, n - ROPE_FREQS, axis=1) * sina + pltpu.roll(x, ROPE_FREQS, axis=1) * sinb


def _kv_rope_kernel(u_ref, cos_ref, sina_ref, sinb_ref, o_ref):
    x = u_ref[0]
    cos, sina, sinb = cos_ref[...], sina_ref[...], sinb_ref[...]
    parts = []
    for h in range(ATT_KV_HEADS):
        k2 = x[:, h * OD_KV_W:h * OD_KV_W + LANES]
        parts += [_rope(k2, cos, sina, sinb), x[:, h * OD_KV_W + LANES:(h + 1) * OD_KV_W]]
    o_ref[0] = jnp.concatenate(parts, axis=1).astype(BF16)


def kv_rope(u, tables, tr=512):
    bsz, t, _ = u.shape
    tr = min(tr, t)
    kvw = ATT_KV_HEADS * OD_KV_W
    tab = pl.BlockSpec((tr, LANES), lambda b, i: (i, 0))
    return pl.pallas_call(
        _kv_rope_kernel,
        grid=(bsz, t // tr),
        in_specs=[pl.BlockSpec((1, tr, kvw), lambda b, i: (b, i, OD_KV_COL // kvw)), tab, tab, tab],
        out_specs=pl.BlockSpec((1, tr, kvw), lambda b, i: (b, i, 0)),
        out_shape=jax.ShapeDtypeStruct((bsz, t, kvw), BF16),
        compiler_params=pltpu.CompilerParams(dimension_semantics=("arbitrary", "arbitrary")),
        name="kv_rope",
    )(u, *tables)


def _attn_kernel(sink_ref, *refs, has_local, nb):
    if has_local:
        q_ref, g_ref, kvp_ref, kvc_ref, kvn_ref, cos_ref, sina_ref, sinb_ref, kvx_ref, o_ref = refs
    else:
        q_ref, g_ref, kvx_ref, o_ref = refs
    tq = q_ref.shape[1]
    i = pl.program_id(1)
    q = q_ref[0]
    if has_local:
        rep = ATT_WIDTH // LANES
        cos, sina, sinb = (jnp.concatenate([r[...]] * rep, axis=1) for r in (cos_ref, sina_ref, sinb_ref))
        q = _rope(q, cos, sina, sinb)
    q = q * (ATT_HEAD ** -0.5 * LOG2E)
    lo = lax.broadcasted_iota(jnp.int32, (tq, LANES), 1) < HALF
    if has_local:
        a = lax.broadcasted_iota(jnp.int32, (tq, WINDOW), 0)
        c = lax.broadcasted_iota(jnp.int32, (tq, WINDOW), 1)
        ok_prev = (c >= a) & (i > 0)
        ok_next = (c <= a) & (i < nb - 1)
    def scores(hk):
        kcol = slice(hk * OD_KV_W, hk * OD_KV_W + LANES)
        vcol = slice(hk * OD_KV_W + LANES, (hk + 1) * OD_KV_W)
        qs = []
        for pair in range(ATT_GROUP // 2):
            grp = q[:, (hk * (ATT_GROUP // 2) + pair) * LANES:(hk * (ATT_GROUP // 2) + pair + 1) * LANES]
            qs += [jnp.where(lo, grp, 0.0), jnp.where(lo, 0.0, grp)]
        lhs = jnp.concatenate(qs, axis=0).astype(BF16)
        kx, vx = kvx_ref[0, :, kcol].astype(BF16), kvx_ref[0, :, vcol].astype(BF16)
        if has_local:
            kall = jnp.concatenate([kvp_ref[0, :, kcol], kvc_ref[0, :, kcol], kvn_ref[0, :, kcol], kx], axis=0)
            vall = jnp.concatenate([kvp_ref[0, :, vcol], kvc_ref[0, :, vcol], kvn_ref[0, :, vcol], vx], axis=0)
        else:
            kall, vall = kx, vx
        return _bdot_nt(lhs, kall), vall

    outs = []
    nxt = scores(0)
    for hk in range(ATT_KV_HEADS):
        s, vall = nxt
        if hk + 1 < ATT_KV_HEADS:
            nxt = scores(hk + 1)
        lov = lax.broadcasted_iota(jnp.int32, vall.shape, 1) < HALF
        zv = jnp.zeros_like(vall)
        v2 = jnp.concatenate([jnp.where(lov, vall, zv), jnp.where(lov, zv, vall)], axis=0)
        ps, inv_l = [], []
        for g in range(ATT_GROUP):
            sg = s[g * tq:(g + 1) * tq]
            if has_local:
                sg = jnp.concatenate([jnp.where(ok_prev, sg[:, 0:tq], NEG_INF), sg[:, tq:2 * tq],
                                      jnp.where(ok_next, sg[:, 2 * tq:3 * tq], NEG_INF), sg[:, 3 * tq:]], axis=1)
            sink = sink_ref[hk, g] * LOG2E
            m = jnp.maximum(jnp.max(sg, axis=1, keepdims=True), sink)
            p = jnp.exp2(sg - m)
            inv_l.append(1.0 / (jnp.sum(p, axis=1, keepdims=True) + jnp.exp2(sink - m)))
            ps.append(p.astype(BF16))
        pv = jnp.dot(jnp.concatenate([jnp.concatenate(ps[2 * pr:2 * pr + 2], axis=1) for pr in range(ATT_GROUP // 2)],
                                     axis=0), v2, preferred_element_type=F32)
        for pr in range(ATT_GROUP // 2):
            outs.append(pv[pr * tq:(pr + 1) * tq] * jnp.where(lo, inv_l[2 * pr], inv_l[2 * pr + 1]))
    g = g_ref[0]
    o_ref[0] = (jnp.concatenate(outs, axis=1) * (g * jax.nn.sigmoid(g))).astype(BF16)


def sink_attention(u, kv_local, tables, u_ctx, sink):
    bsz, t, _ = u.shape
    tq = ATT_BLOCK
    nb = t // tq
    tx = u_ctx.shape[1]
    kvw = ATT_KV_HEADS * OD_KV_W
    has_local = kv_local is not None
    in_specs = [pl.BlockSpec(memory_space=pltpu.SMEM),
                pl.BlockSpec((1, tq, ATT_WIDTH), lambda b, i: (b, i, OD_Q_COL // ATT_WIDTH)),
                pl.BlockSpec((1, tq, ATT_WIDTH), lambda b, i: (b, i, OD_GATE_COL // ATT_WIDTH))]
    args = [sink.reshape(ATT_KV_HEADS, ATT_GROUP), u, u]
    if has_local:
        in_specs += [pl.BlockSpec((1, tq, kvw), lambda b, i: (b, jnp.maximum(i - 1, 0), 0)),
                     pl.BlockSpec((1, tq, kvw), lambda b, i: (b, i, 0)),
                     pl.BlockSpec((1, tq, kvw), lambda b, i: (b, jnp.minimum(i + 1, nb - 1), 0))]
        in_specs += [pl.BlockSpec((tq, LANES), lambda b, i: (i, 0))] * 3
        args += [kv_local] * 3 + list(tables)
    kvx_block = u_ctx.shape[2] // kvw - 1
    in_specs += [pl.BlockSpec((1, tx, kvw), lambda b, i: (b, 0, kvx_block))]
    args += [u_ctx]
    return pl.pallas_call(
        functools.partial(_attn_kernel, has_local=has_local, nb=nb),
        grid=(bsz, nb),
        in_specs=in_specs,
        out_specs=pl.BlockSpec((1, tq, ATT_WIDTH), lambda b, i: (b, i, 0)),
        out_shape=jax.ShapeDtypeStruct((bsz, t, ATT_WIDTH), BF16),
        compiler_params=pltpu.CompilerParams(dimension_semantics=("arbitrary", "arbitrary"),
                                             vmem_limit_bytes=V7X_VMEM_LIMIT_BYTES),
        name="sink_attention",
    )(*args)


def pack_odd_w_in(w_in):
    nq, nkv = ATT_WIDTH, ATT_KV_HEADS * ATT_HEAD
    parts = [w_in[:, :nq], w_in[:, nq + 2 * nkv:]]
    for h in range(ATT_KV_HEADS):
        k = w_in[:, nq + h * ATT_HEAD:nq + (h + 1) * ATT_HEAD]
        v = w_in[:, nq + nkv + h * ATT_HEAD:nq + nkv + (h + 1) * ATT_HEAD]
        parts += [k, k, v, v]
    return jnp.concatenate(parts, 1).astype(BF16)


def pack_even_w_in(w_in):
    g1 = 4 * GDN_WIDTH
    r0 = GDN_IN
    parts = [w_in[:, :g1], w_in[:, r0:r0 + 4 * RWKV_WIDTH], w_in[:, r0 + 4 * RWKV_WIDTH:], w_in[:, g1:GDN_IN],
             jnp.zeros((D_MODEL, EV_COLS - EV_SMALL_COL - 4 * GDN_HEADS), F32)]
    return jnp.concatenate(parts, 1).astype(BF16)


def kernel(x, c, ctx, c_ctx, mod_w, mod_b, ln_g, ln_b, ev_w_in, ev_w_out, gdn_conv, gdn_a_log, gdn_dt_bias,
           gdn_norm_g, rwkv_mix, rwkv_w0, rwkv_w_up, rwkv_a0, rwkv_a_up, rwkv_k_k, rwkv_k_a, rwkv_r_k,
           rwkv_gn_g, rwkv_gn_b, od_w_in, od_w_out, od_sink):
    bsz, t_lat, d = x.shape
    t_ctx = ctx.shape[1]
    x_lat, x_ctx = x, ctx
    n_mod = -(-(bsz + 1) // 8) * 8
    cc = jax.nn.silu(jnp.concatenate([c, c_ctx[None], jnp.zeros((n_mod - bsz - 1, d), F32)], 0))
    mods = mod_all_layers(cc, mod_w, mod_b)
    for layer in range(DEPTH):
        need_ctx = layer < DEPTH - 1
        i = layer // 2
        mod = mods[layer]
        sh, sc, gt = (mod[:bsz, j * d:(j + 1) * d][:, None, :] for j in range(3))
        sh_c, sc_c, gt_c = (jnp.broadcast_to(mod[bsz, j * d:(j + 1) * d], (bsz, 1, d)) for j in range(3))
        if layer % 2 == 0:
            w_in = pack_even_w_in(ev_w_in[i])
            w_out = ev_w_out[i].astype(BF16)
            u_lat = inproj(x_lat, sc, sh, w_in, 1024, EV_TN)
            u_ctx = inproj(x_ctx.reshape(1, bsz * t_ctx, d), sc_c[:1], sh_c[:1], w_in, 1024, EV_TN).reshape(bsz, t_ctx, -1)
            small = slice(EV_SMALL_COL, EV_SMALL_COL + 4 * GDN_HEADS)
            og_lat, og_ctx = gdn_mixer(u_lat, gdn_bg(u_lat[..., small]), u_ctx, gdn_bg(u_ctx[..., small]),
                                       gdn_params(gdn_conv[i], gdn_a_log[i], gdn_dt_bias[i], gdn_norm_g[i]))
            or_lat, or_ctx = rwkv_mixer(u_lat, u_ctx, *rwkv_params(
                rwkv_mix[i], rwkv_w0[i], rwkv_w_up[i], rwkv_a0[i], rwkv_a_up[i], rwkv_k_k[i], rwkv_k_a[i],
                rwkv_r_k[i], rwkv_gn_g[i], rwkv_gn_b[i]))
            a_lat, a_ctx = [og_lat, or_lat], [og_ctx, or_ctx]
        else:
            w_in = pack_odd_w_in(od_w_in[i])
            w_out = od_w_out[i].astype(BF16)
            u_lat = inproj(x_lat, sc, sh, w_in, 1024, OD_TN)
            w_ctx, tn_ctx = (w_in, OD_TN) if need_ctx else (w_in[:, OD_KV_COL:], OD_COLS - OD_KV_COL)
            u_ctx = inproj(x_ctx.reshape(1, bsz * t_ctx, d), sc_c[:1], sh_c[:1], w_ctx, 1024, tn_ctx).reshape(bsz, t_ctx, -1)
            tables = rope_lane_tables(t_lat)
            o_lat = sink_attention(u_lat, kv_rope(u_lat, tables), tables, u_ctx, od_sink[i])
            o_ctx = sink_attention(u_ctx, None, None, u_ctx, od_sink[i]) if need_ctx else None
            a_lat, a_ctx = [o_lat], [o_ctx]
        x_lat = outproj_ln(a_lat, w_out, x_lat, gt, ln_g[layer], ln_b[layer])
        if need_ctx:
            x_ctx = outproj_ln(a_ctx, w_out, x_ctx, gt_c, ln_g[layer], ln_b[layer])
    return x_lat
```
